```python
import math
import jax, jax.numpy as jnp
from jax import lax
import numpy as np

D_MODEL = 1024
BATCH = 2
SEQ = 8192
DEPTH = 1

CHUNK = 64
Q_BLOCK = 128
MEM_LEN = 256
EPS = 1e-6
NEG_INF = -1e30
MLA_HEADS = 8
MLA_Q_LORA = 384
MLA_KV_LORA = 256
MLA_NOPE = 64
MLA_ROPE = 32
MLA_V = 64
ROPE_BASE = 10000.0
DIFF_HEADS = 8
DIFF_DIM = 64
DIFF_WIDTH = DIFF_HEADS * 2 * DIFF_DIM
N_BUCKETS = 32
MAX_DISTANCE = 128
XATTN_HEADS = 4
XATTN_DIM = 128
D_FF = 2816
IN_SPLITS = (MLA_Q_LORA, MLA_KV_LORA, MLA_ROPE, DIFF_WIDTH, DIFF_WIDTH, DIFF_WIDTH, D_MODEL, D_MODEL)
N_IN = sum(IN_SPLITS)

kernel_name = 'hybrid_mla_diffattn_macaron_layer'


def rmsnorm(x, g):
    xf = x.astype(jnp.float32)
    y = xf * lax.rsqrt(jnp.mean(xf * xf, axis=-1, keepdims=True) + EPS)
    return (y * g.astype(jnp.float32)).astype(x.dtype)


def swiglu(x, w_gate, w_up, w_down):
    return (jax.nn.silu(x @ w_gate) * (x @ w_up)) @ w_down


def rope(x, pos):
    half = x.shape[-1] // 2
    freqs = ROPE_BASE ** (-jnp.arange(half, dtype=jnp.float32) / half)
    ang = pos.astype(jnp.float32)[:, None] * freqs[None, :]
    cos = jnp.cos(ang)[:, None, :].astype(x.dtype)
    sin = jnp.sin(ang)[:, None, :].astype(x.dtype)
    x1, x2 = x[..., :half], x[..., half:]
    return jnp.concatenate([x1 * cos - x2 * sin, x1 * sin + x2 * cos], axis=-1)


def t5_bucket(rel):
    half = N_BUCKETS // 2
    max_exact = half // 2
    ret = jnp.where(rel > 0, half, 0)
    n = jnp.abs(rel)
    nf = jnp.maximum(n, 1).astype(jnp.float32)
    large = max_exact + (jnp.log(nf / max_exact) / math.log(MAX_DISTANCE / max_exact) * (half - max_exact)).astype(jnp.int32)
    large = jnp.minimum(large, half - 1)
    return ret + jnp.where(n < max_exact, n, large)


def chunk_mask(q_pos, k_pos):
    return (k_pos // CHUNK)[None, :] <= (q_pos // CHUNK)[:, None]


def to_blocks(t):
    b, s = t.shape[:2]
    return jnp.moveaxis(t.reshape(b, s // Q_BLOCK, Q_BLOCK, *t.shape[2:]), 1, 0)


def from_blocks(t):
    t = jnp.moveaxis(t, 0, 1)
    return t.reshape(t.shape[0], t.shape[1] * t.shape[2], *t.shape[3:])


def mla_attention(q, k, v):
    s_len = q.shape[1]
    k_pos = jnp.arange(s_len)
    q_pos_blocks = k_pos.reshape(-1, Q_BLOCK)
    scale = (MLA_NOPE + MLA_ROPE) ** -0.5

    def one(args):
        qb, qp = args
        logits = jnp.einsum('bqhd,bkhd->bhqk', qb, k).astype(jnp.float32) * scale
        logits = jnp.where(chunk_mask(qp, k_pos), logits, NEG_INF)
        p = jax.nn.softmax(logits, axis=-1).astype(v.dtype)
        return jnp.einsum('bhqk,bkhd->bqhd', p, v)

    return from_blocks(lax.map(one, (to_blocks(q), q_pos_blocks)))


def diff_attention(q1, q2, k1, k2, v, lam, rel_bias):
    s_len = q1.shape[1]
    k_pos = jnp.arange(s_len)
    q_pos_blocks = k_pos.reshape(-1, Q_BLOCK)
    scale = DIFF_DIM ** -0.5
    table = rel_bias.astype(jnp.float32)

    def one(args):
        q1b, q2b, qp = args
        mask = chunk_mask(qp, k_pos)
        bias = jnp.transpose(table[t5_bucket(k_pos[None, :] - qp[:, None])], (2, 0, 1))[None]

        def probs(qb, kk):
            logits = jnp.einsum('bqhd,bkhd->bhqk', qb, kk).astype(jnp.float32) * scale + bias
            return jax.nn.softmax(jnp.where(mask, logits, NEG_INF), axis=-1)

        attn = probs(q1b, k1) - lam * probs(q2b, k2)
        return jnp.einsum('bhqk,bkhd->bqhd', attn.astype(v.dtype), v)

    return from_blocks(lax.map(one, (to_blocks(q1), to_blocks(q2), q_pos_blocks)))


def mla_branch(q_lat, kv_lat, k_rope, pos, q_norm, w_q_up, kv_norm, w_kv_up):
    b, s = q_lat.shape[:2]
    q = (rmsnorm(q_lat, q_norm) @ w_q_up).reshape(b, s, MLA_HEADS, MLA_NOPE + MLA_ROPE)
    q = jnp.concatenate([q[..., :MLA_NOPE], rope(q[..., MLA_NOPE:], pos)], axis=-1)
    kv = (rmsnorm(kv_lat, kv_norm) @ w_kv_up).reshape(b, s, MLA_HEADS, MLA_NOPE + MLA_V)
    k_r = jnp.broadcast_to(rope(k_rope[:, :, None, :], pos), (b, s, MLA_HEADS, MLA_ROPE))
    k = jnp.concatenate([kv[..., :MLA_NOPE], k_r], axis=-1)
    v = kv[..., MLA_NOPE:]
    return mla_attention(q, k, v).reshape(b, s, MLA_HEADS * MLA_V)


def diff_branch(dq, dk, dv, layer, lam_q1, lam_k1, lam_q2, lam_k2, sub_norm, rel_bias):
    b, s = dq.shape[:2]
    q = dq.reshape(b, s, DIFF_HEADS, 2, DIFF_DIM)
    k = dk.reshape(b, s, DIFF_HEADS, 2, DIFF_DIM)
    v = dv.reshape(b, s, DIFF_HEADS, 2 * DIFF_DIM)
    lam_init = 0.8 - 0.6 * math.exp(-0.3 * layer)
    f32 = jnp.float32
    lam = (jnp.exp(jnp.sum(lam_q1.astype(f32) * lam_k1.astype(f32)))
           - jnp.exp(jnp.sum(lam_q2.astype(f32) * lam_k2.astype(f32))) + lam_init)
    o = diff_attention(q[..., 0, :], q[..., 1, :], k[..., 0, :], k[..., 1, :], v, lam, rel_bias)
    o = rmsnorm(o, sub_norm) * (1.0 - lam_init)
    return o.reshape(b, s, DIFF_WIDTH)


def memory_cross_attention(u, m, w_q, w_kv, w_o):
    b, s = u.shape[:2]
    q = (u @ w_q).reshape(b, s, XATTN_HEADS, XATTN_DIM)
    k, v = jnp.split((m @ w_kv).reshape(b, m.shape[1], XATTN_HEADS, 2 * XATTN_DIM), 2, axis=-1)
    logits = jnp.einsum('bqhd,bkhd->bhqk', q, k).astype(jnp.float32) * (XATTN_DIM ** -0.5)
    p = jax.nn.softmax(logits, axis=-1).astype(v.dtype)
    o = jnp.einsum('bhqk,bkhd->bqhd', p, v).reshape(b, s, XATTN_HEADS * XATTN_DIM)
    return o @ w_o


def setup_inputs(seed: int = 0) -> dict:
    key = jax.random.key(seed)
    ks = iter(jax.random.split(key, 40))

    def dense(shape, fan_in):
        return jax.random.normal(next(ks), shape, jnp.float32) * (fan_in ** -0.5)

    def gain(shape):
        return 1.0 + 0.02 * jax.random.normal(next(ks), shape, jnp.float32)

    def small(shape, scale):
        return scale * jax.random.normal(next(ks), shape, jnp.float32)

    L, D = DEPTH, D_MODEL
    return {
        'x': jax.random.normal(next(ks), (BATCH, SEQ, D), jnp.float32),
        'mem': jax.random.normal(next(ks), (BATCH, MEM_LEN, D), jnp.float32),
        'ffn1_norm': gain((L, D)),
        'ffn1_w_gate': dense((L, D, D_FF), D),
        'ffn1_w_up': dense((L, D, D_FF), D),
        'ffn1_w_down': dense((L, D_FF, D), D_FF),
        'mix_norm': gain((L, D)),
        'w_in': dense((L, D, N_IN), D),
        'mla_q_norm': gain((L, MLA_Q_LORA)),
        'mla_w_q_up': dense((L, MLA_Q_LORA, MLA_HEADS * (MLA_NOPE + MLA_ROPE)), MLA_Q_LORA),
        'mla_kv_norm': gain((L, MLA_KV_LORA)),
        'mla_w_kv_up': dense((L, MLA_KV_LORA, MLA_HEADS * (MLA_NOPE + MLA_V)), MLA_KV_LORA),
        'diff_lambda_q1': small((L, DIFF_DIM), 0.1),
        'diff_lambda_k1': small((L, DIFF_DIM), 0.1),
        'diff_lambda_q2': small((L, DIFF_DIM), 0.1),
        'diff_lambda_k2': small((L, DIFF_DIM), 0.1),
        'diff_sub_norm': gain((L, 2 * DIFF_DIM)),
        'rel_bias': small((N_BUCKETS, DIFF_HEADS), 0.5),
        'w_branch_a': dense((L, MLA_HEADS * MLA_V, D), MLA_HEADS * MLA_V),
        'w_branch_b': dense((L, DIFF_WIDTH, D), DIFF_WIDTH),
        'w_out': dense((L, D, D), D),
        'xattn_norm': gain((L, D)),
        'mem_norm': gain((L, D)),
        'xattn_w_q': dense((L, D, XATTN_HEADS * XATTN_DIM), D),
        'xattn_w_kv': dense((L, D, 2 * XATTN_HEADS * XATTN_DIM), D),
        'xattn_w_o': dense((L, XATTN_HEADS * XATTN_DIM, D), XATTN_HEADS * XATTN_DIM),
        'ffn2_norm': gain((L, D)),
        'ffn2_w_gate': dense((L, D, D_FF), D),
        'ffn2_w_up': dense((L, D, D_FF), D),
        'ffn2_w_down': dense((L, D_FF, D), D_FF),
        'final_norm': gain((D,)),
    }


def reference(x, mem, ffn1_norm, ffn1_w_gate, ffn1_w_up, ffn1_w_down, mix_norm, w_in,
              mla_q_norm, mla_w_q_up, mla_kv_norm, mla_w_kv_up,
              diff_lambda_q1, diff_lambda_k1, diff_lambda_q2, diff_lambda_k2, diff_sub_norm,
              rel_bias, w_branch_a, w_branch_b, w_out,
              xattn_norm, mem_norm, xattn_w_q, xattn_w_kv, xattn_w_o,
              ffn2_norm, ffn2_w_gate, ffn2_w_up, ffn2_w_down, final_norm):
    pos = jnp.arange(x.shape[1])
    offsets = np.cumsum(IN_SPLITS)[:-1].tolist()
    h = x
    for l in range(DEPTH):
        h = h + 0.5 * swiglu(rmsnorm(h, ffn1_norm[l]), ffn1_w_gate[l], ffn1_w_up[l], ffn1_w_down[l])
        u = rmsnorm(h, mix_norm[l])
        q_lat, kv_lat, k_rope, dq, dk, dv, gate_a, gate_b = jnp.split(u @ w_in[l], offsets, axis=-1)
        y_a = mla_branch(q_lat, kv_lat, k_rope, pos, mla_q_norm[l], mla_w_q_up[l],
                         mla_kv_norm[l], mla_w_kv_up[l]) @ w_branch_a[l]
        y_b = diff_branch(dq, dk, dv, l, diff_lambda_q1[l], diff_lambda_k1[l], diff_lambda_q2[l],
                          diff_lambda_k2[l], diff_sub_norm[l], rel_bias) @ w_branch_b[l]
        merged = jax.nn.sigmoid(gate_a) * y_a + jax.nn.sigmoid(gate_b) * y_b
        h = h + merged @ w_out[l]
        h = h + memory_cross_attention(rmsnorm(h, xattn_norm[l]), rmsnorm(mem, mem_norm[l]),
                                       xattn_w_q[l], xattn_w_kv[l], xattn_w_o[l])
        h = h + 0.5 * swiglu(rmsnorm(h, ffn2_norm[l]), ffn2_w_gate[l], ffn2_w_up[l], ffn2_w_down[l])
    return rmsnorm(h, final_norm)
```

```python
import functools
import math

import jax
import jax.numpy as jnp
from jax import lax
from jax.experimental import pallas as pl
from jax.experimental.pallas import tpu as pltpu

F32 = jnp.float32
BF16 = jnp.bfloat16

CHUNK = 64
EPS = 1e-6
NEG_INF = -1e30
MLA_HEADS = 8
MLA_Q_LORA = 384
MLA_KV_LORA = 256
MLA_NOPE = 64
MLA_ROPE = 32
MLA_V = 64
ROPE_BASE = 10000.0
DIFF_HEADS = 8
DIFF_DIM = 64
N_BUCKETS = 32
XATTN_HEADS = 4
XATTN_DIM = 128
LOG2E = math.log2(math.e)

LANES = 128
HEAD_SLOT = LANES
VMEM_LIMIT = 56 * 1024 * 1024

TM_FFN = 256
TM_PROJ = 256
TQ = 512
TK = 512


def _rms(x, g):
    return x * lax.rsqrt(jnp.mean(x * x, axis=-1, keepdims=True) + EPS) * g


def _const_spec(shape):
    nd = len(shape)
    return pl.BlockSpec(shape, lambda *_: (0,) * nd)


def _params(n_grid):
    return pltpu.CompilerParams(
        dimension_semantics=("arbitrary",) * n_grid, vmem_limit_bytes=VMEM_LIMIT)


def _ffn_kernel(x_ref, g_ref, wg_ref, wu_ref, wd_ref, fin_ref, o_ref, *, splits, final_norm):
    x = x_ref[...]
    nb = _rms(x, g_ref[...]).astype(BF16)
    y = None
    for lo, hi in splits:
        g = jnp.dot(nb, wg_ref[:, lo:hi], preferred_element_type=F32)
        u = jnp.dot(nb, wu_ref[:, lo:hi], preferred_element_type=F32)
        h = (g * jax.nn.sigmoid(g) * u).astype(BF16)
        part = jnp.dot(h, wd_ref[lo:hi, :], preferred_element_type=F32)
        y = part if y is None else y + part
    out = x + 0.5 * y
    if final_norm:
        out = _rms(out, fin_ref[...])
    o_ref[...] = out


def _ffn(x, norm_g, wg, wu, wd, fin_g, final_norm):
    t, d = x.shape
    dff = wg.shape[1]
    half = (dff // 2 + 255) // 256 * 256
    splits = ((0, half), (half, dff))
    return pl.pallas_call(
        functools.partial(_ffn_kernel, splits=splits, final_norm=final_norm),
        grid=(t // TM_FFN,),
        in_specs=[
            pl.BlockSpec((TM_FFN, d), lambda i: (i, 0)),
            _const_spec((1, d)),
            _const_spec((d, dff)),
            _const_spec((d, dff)),
            _const_spec((dff, d)),
            _const_spec((1, d)),
        ],
        out_specs=pl.BlockSpec((TM_FFN, d), lambda i: (i, 0)),
        out_shape=jax.ShapeDtypeStruct((t, d), F32),
        compiler_params=_params(1),
        name="ffn",
    )(x, norm_g, wg, wu, wd, fin_g)


def _proj_kernel(h_ref, g_ref, wlat_ref, wbig_ref, wgate_ref, qn_ref, wq_ref, kvn_ref, wkv_ref,
                 cq_ref, sq_ref, ck_ref, sk_ref,
                 q_ref, k_ref, v_ref, dq_ref, dk_ref, dv_ref, ga_ref, gb_ref, *, diff_scale):
    d = h_ref.shape[1]
    u = _rms(h_ref[...], g_ref[...]).astype(BF16)

    lat = jnp.dot(u, wlat_ref[...], preferred_element_type=F32)
    q_lat = lat[:, :MLA_Q_LORA]
    kv_lat = lat[:, MLA_Q_LORA:MLA_Q_LORA + MLA_KV_LORA]
    o = MLA_Q_LORA + MLA_KV_LORA
    kr_a = lat[:, o:o + LANES]
    kr_b = lat[:, o + LANES:o + 2 * LANES]

    qn = _rms(q_lat, qn_ref[...]).astype(BF16)
    qq = jnp.dot(qn, wq_ref[...], preferred_element_type=F32)
    kvn = _rms(kv_lat, kvn_ref[...]).astype(BF16)
    kk = jnp.dot(kvn, wkv_ref[...], preferred_element_type=F32)

    cq, sq = cq_ref[...], sq_ref[...]
    kr = kr_a * ck_ref[...] + kr_b * sk_ref[...]
    nh = MLA_HEADS * HEAD_SLOT
    for hd in range(MLA_HEADS):
        sl = slice(hd * HEAD_SLOT, (hd + 1) * HEAD_SLOT)
        sl2 = slice(nh + hd * HEAD_SLOT, nh + (hd + 1) * HEAD_SLOT)
        q_ref[:, sl] = (qq[:, sl] * cq + qq[:, sl2] * sq).astype(BF16)
        k_ref[:, sl] = (kk[:, sl] + kr).astype(BF16)
    v_ref[...] = kk[:, nh:].astype(BF16)

    big = jnp.dot(u, wbig_ref[...], preferred_element_type=F32)
    dq_ref[...] = (big[:, :d] * diff_scale).astype(BF16)
    dk_ref[...] = big[:, d:2 * d].astype(BF16)
    dv_ref[...] = big[:, 2 * d:].astype(BF16)

    gates = jax.nn.sigmoid(jnp.dot(u, wgate_ref[...], preferred_element_type=F32))
    ga_ref[...] = gates[:, :d].astype(BF16)
    gb_ref[...] = gates[:, d:].astype(BF16)


def _proj(h, g, wlat, wbig, wgate, qn, wq, kvn, wkv, tabs, seq):
    t, d = h.shape
    nseq = seq // TM_PROJ
    tok = lambda i: (i, 0)
    pos = lambda i: (i % nseq, 0)
    outs = [jax.ShapeDtypeStruct((t, d), BF16)] * 8
    return pl.pallas_call(
        functools.partial(_proj_kernel, diff_scale=DIFF_DIM ** -0.5 * LOG2E),
        grid=(t // TM_PROJ,),
        in_specs=[
            pl.BlockSpec((TM_PROJ, d), tok),
            _const_spec(g.shape), _const_spec(wlat.shape), _const_spec(wbig.shape),
            _const_spec(wgate.shape), _const_spec(qn.shape), _const_spec(wq.shape),
            _const_spec(kvn.shape), _const_spec(wkv.shape),
        ] + [pl.BlockSpec((TM_PROJ, LANES), pos)] * 4,
        out_specs=[pl.BlockSpec((TM_PROJ, d), tok)] * 8,
        out_shape=outs,
        compiler_params=_params(1),
        name="in_proj",
    )(h, g, wlat, wbig, wgate, qn, wq, kvn, wkv, *tabs)


def _t5_bucket(rel):
    half = N_BUCKETS // 2
    max_exact = half // 2
    n = jnp.abs(rel)
    n2 = n * n
    large = max_exact
    for k in range(1, half - max_exact):
        large = large + (n2 >= (max_exact * max_exact) << k).astype(jnp.int32)
    return jnp.where(rel > 0, half, 0) + jnp.where(n < max_exact, n, large)


def _bias_kernel(tab_ref, o_ref):
    hd = pl.program_id(0)
    r = lax.broadcasted_iota(jnp.int32, (TQ, TK), 0)
    c = lax.broadcasted_iota(jnp.int32, (TQ, TK), 1)
    for delta in range(2):
        bucket = _t5_bucket(c - r - delta * TK)
        val = jnp.zeros((TQ, TK), F32)
        for b in range(N_BUCKETS):
            val = jnp.where(bucket == b, tab_ref[b, hd], val)
        val = val * LOG2E
        if delta == 0:
            val = jnp.where(c // CHUNK <= r // CHUNK, val, NEG_INF)
        o_ref[0, delta] = val


def _bias_tiles(rel_bias):
    return pl.pallas_call(
        _bias_kernel,
        grid=(DIFF_HEADS,),
        in_specs=[pl.BlockSpec(memory_space=pltpu.SMEM)],
        out_specs=pl.BlockSpec((1, 2, TQ, TK), lambda h: (h, 0, 0, 0)),
        out_shape=jax.ShapeDtypeStruct((DIFF_HEADS, 2, TQ, TK), F32),
        compiler_params=_params(1),
        name="bias_tiles",
    )(rel_bias)


def _lane_fold(x, op):
    out = x[:, :LANES]
    for c in range(1, x.shape[1] // LANES):
        out = op(out, x[:, c * LANES:(c + 1) * LANES])
    return out


def _flash_step(q, k, v, bias, m_ref, l_ref, acc_ref):
    s = lax.dot_general(q, k, (((1,), (1,)), ((), ())), preferred_element_type=F32)
    if bias is not None:
        s = s + bias
    m_prev = m_ref[...]
    m_cur = jnp.max(_lane_fold(s, jnp.maximum), axis=1, keepdims=True)
    m_new = jnp.maximum(m_prev, m_cur)
    alpha = jnp.exp2(m_prev - m_new)
    p = jnp.exp2(s - jnp.tile(m_new, (1, s.shape[1] // LANES)))
    l_ref[...] = alpha * l_ref[...] + _lane_fold(p, jnp.add)
    acc_ref[...] = alpha * acc_ref[...] + jnp.dot(p.astype(BF16), v, preferred_element_type=F32)
    m_ref[...] = m_new


def _flash_init(m_ref, l_ref, acc_ref):
    m_ref[...] = jnp.full(m_ref.shape, NEG_INF, F32)
    l_ref[...] = jnp.zeros(l_ref.shape, F32)
    acc_ref[...] = jnp.zeros(acc_ref.shape, F32)


def _flash_out(l_ref, acc_ref):
    return acc_ref[...] / jnp.sum(l_ref[...], axis=1, keepdims=True)


def _kv_tile(ref, j):
    return ref[0, pl.ds(pl.multiple_of(j * TK, TK), TK), :]


def _mla_kernel(q_ref, k_ref, v_ref, mask_ref, o_ref, m_ref, l_ref, acc_ref):
    i = pl.program_id(2)
    q = q_ref[0]
    _flash_init(m_ref, l_ref, acc_ref)

    def body(j, carry):
        _flash_step(q, _kv_tile(k_ref, j), _kv_tile(v_ref, j), None, m_ref, l_ref, acc_ref)
        return carry

    lax.fori_loop(0, i, body, 0)
    _flash_step(q, _kv_tile(k_ref, i), _kv_tile(v_ref, i), mask_ref[...], m_ref, l_ref, acc_ref)
    o_ref[0] = _flash_out(l_ref, acc_ref).astype(BF16)


def _mla_attention(q, k, v, mask):
    b, s, w = q.shape
    heads = w // HEAD_SLOT
    qspec = pl.BlockSpec((1, TQ, HEAD_SLOT), lambda bi, h, i: (bi, i, h))
    kvspec = pl.BlockSpec((1, s, HEAD_SLOT), lambda bi, h, i: (bi, 0, h))
    return pl.pallas_call(
        _mla_kernel,
        grid=(b, heads, s // TQ),
        in_specs=[qspec, kvspec, kvspec, _const_spec(mask.shape)],
        out_specs=qspec,
        out_shape=jax.ShapeDtypeStruct((b, s, w), BF16),
        scratch_shapes=[pltpu.VMEM((TQ, LANES), F32)] * 3,
        compiler_params=_params(3),
        name="mla_attn",
    )(q, k, v, mask)


def _diff_kernel(q_ref, k_ref, v_ref, bias_ref, far_ref, lq1_ref, lk1_ref, lq2_ref, lk2_ref, sub_ref,
                 o_ref, m1_ref, l1_ref, a1_ref, m2_ref, l2_ref, a2_ref, *, lam_init):
    hd = pl.program_id(1)
    i = pl.program_id(2)
    q = q_ref[0]
    lane = lax.broadcasted_iota(jnp.int32, q.shape, 1)
    q1 = jnp.where(lane < DIFF_DIM, q, jnp.zeros_like(q))
    q2 = jnp.where(lane >= DIFF_DIM, q, jnp.zeros_like(q))
    _flash_init(m1_ref, l1_ref, a1_ref)
    _flash_init(m2_ref, l2_ref, a2_ref)
    far = far_ref[hd]

    def step(j, bias):
        k = _kv_tile(k_ref, j)
        v = _kv_tile(v_ref, j)
        _flash_step(q1, k, v, bias, m1_ref, l1_ref, a1_ref)
        _flash_step(q2, k, v, bias, m2_ref, l2_ref, a2_ref)

    def body(j, carry):
        step(j, far)
        return carry

    lax.fori_loop(0, i - 1, body, 0)

    @pl.when(i > 0)
    def _():
        step(i - 1, bias_ref[0, 1])

    step(i, bias_ref[0, 0])

    lam = (jnp.exp(jnp.sum(lq1_ref[...] * lk1_ref[...], axis=1, keepdims=True))
           - jnp.exp(jnp.sum(lq2_ref[...] * lk2_ref[...], axis=1, keepdims=True)) + lam_init)
    o = _flash_out(l1_ref, a1_ref) - lam * _flash_out(l2_ref, a2_ref)
    o_ref[0] = (_rms(o, sub_ref[...]) * (1.0 - lam_init)).astype(BF16)


def _diff_attention(q, k, v, bias, far, lq1, lk1, lq2, lk2, sub, lam_init):
    b, s, w = q.shape
    heads = w // HEAD_SLOT
    qspec = pl.BlockSpec((1, TQ, HEAD_SLOT), lambda bi, h, i: (bi, i, h))
    kvspec = pl.BlockSpec((1, s, HEAD_SLOT), lambda bi, h, i: (bi, 0, h))
    vec = _const_spec(lq1.shape)
    return pl.pallas_call(
        functools.partial(_diff_kernel, lam_init=lam_init),
        grid=(b, heads, s // TQ),
        in_specs=[qspec, kvspec, kvspec,
                  pl.BlockSpec((1, 2, TQ, TK), lambda bi, h, i: (h, 0, 0, 0)),
                  pl.BlockSpec(memory_space=pltpu.SMEM),
                  vec, vec, vec, vec, _const_spec(sub.shape)],
        out_specs=qspec,
        out_shape=jax.ShapeDtypeStruct((b, s, w), BF16),
        scratch_shapes=[pltpu.VMEM((TQ, LANES), F32)] * 6,
        compiler_params=_params(3),
        name="diff_attn",
    )(q, k, v, bias, far, lq1, lk1, lq2, lk2, sub)


def _merge_kernel(h_ref, oa_ref, ob_ref, ga_ref, gb_ref, wa_ref, wb_ref, wo_ref, o_ref):
    ya = jnp.dot(oa_ref[...], wa_ref[...], preferred_element_type=F32)
    yb = jnp.dot(ob_ref[...], wb_ref[...], preferred_element_type=F32)
    merged = ga_ref[...].astype(F32) * ya + gb_ref[...].astype(F32) * yb
    o_ref[...] = h_ref[...] + jnp.dot(merged.astype(BF16), wo_ref[...], preferred_element_type=F32)


def _merge(h, oa, ob, ga, gb, wa, wb, wo):
    t, d = h.shape
    tok = pl.BlockSpec((TM_PROJ, d), lambda i: (i, 0))
    return pl.pallas_call(
        _merge_kernel,
        grid=(t // TM_PROJ,),
        in_specs=[tok] * 5 + [_const_spec(wa.shape), _const_spec(wb.shape), _const_spec(wo.shape)],
        out_specs=tok,
        out_shape=jax.ShapeDtypeStruct((t, d), F32),
        compiler_params=_params(1),
        name="merge",
    )(h, oa, ob, ga, gb, wa, wb, wo)


def _memkv_kernel(m_ref, g_ref, w_ref, o_ref):
    mn = _rms(m_ref[...], g_ref[...]).astype(BF16)
    o_ref[...] = jnp.dot(mn, w_ref[...], preferred_element_type=F32).astype(BF16)


def _memkv(mem2d, g, w):
    rows, d = mem2d.shape
    return pl.pallas_call(
        _memkv_kernel,
        grid=(1,),
        in_specs=[_const_spec(mem2d.shape), _const_spec(g.shape), _const_spec(w.shape)],
        out_specs=_const_spec((rows, w.shape[1])),
        out_shape=jax.ShapeDtypeStruct((rows, w.shape[1]), BF16),
        compiler_params=_params(1),
        name="mem_kv",
    )(mem2d, g, w)


def _xattn_kernel(h_ref, g_ref, wq_ref, kv_ref, wo_ref, o_ref, *, scale):
    h = h_ref[...]
    u = _rms(h, g_ref[...]).astype(BF16)
    q = (jnp.dot(u, wq_ref[...], preferred_element_type=F32) * scale).astype(BF16)
    kv = kv_ref[...]
    outs = []
    for hd in range(XATTN_HEADS):
        qh = q[:, hd * XATTN_DIM:(hd + 1) * XATTN_DIM]
        kh = kv[:, 2 * hd * XATTN_DIM:(2 * hd + 1) * XATTN_DIM]
        vh = kv[:, (2 * hd + 1) * XATTN_DIM:(2 * hd + 2) * XATTN_DIM]
        s = lax.dot_general(qh, kh, (((1,), (1,)), ((), ())), preferred_element_type=F32)
        m = jnp.max(s, axis=1, keepdims=True)
        p = jnp.exp2(s - m)
        l = jnp.sum(p, axis=1, keepdims=True)
        oh = jnp.dot(p.astype(BF16), vh, preferred_element_type=F32) / l
        outs.append(oh.astype(BF16))
    o = jnp.concatenate(outs, axis=1)
    o_ref[...] = h + jnp.dot(o, wo_ref[...], preferred_element_type=F32)


def _xattn(h, g, wq, kvmem, wo, seq, mem_len):
    t, d = h.shape
    nseq = seq // TM_PROJ
    tok = pl.BlockSpec((TM_PROJ, d), lambda i: (i, 0))
    return pl.pallas_call(
        functools.partial(_xattn_kernel, scale=XATTN_DIM ** -0.5 * LOG2E),
        grid=(t // TM_PROJ,),
        in_specs=[tok, _const_spec(g.shape), _const_spec(wq.shape),
                  pl.BlockSpec((mem_len, kvmem.shape[1]), lambda i: (i // nseq, 0)),
                  _const_spec(wo.shape)],
        out_specs=tok,
        out_shape=jax.ShapeDtypeStruct((t, d), F32),
        compiler_params=_params(1),
        name="xattn",
    )(h, g, wq, kvmem, wo)


def _pad_heads(w, heads, width):
    k = w.shape[0]
    w = w.reshape(k, heads, width)
    return jnp.pad(w, ((0, 0), (0, 0), (0, HEAD_SLOT - width))).reshape(k, heads * HEAD_SLOT)


def _rope_tables(seq):
    half = MLA_ROPE // 2
    pos = jnp.arange(seq)
    freqs = ROPE_BASE ** (-jnp.arange(half, dtype=F32) / half)
    ang = pos.astype(F32)[:, None] * freqs[None, :]
    cos, sin = jnp.cos(ang), jnp.sin(ang)
    ones = jnp.ones((seq, MLA_NOPE), F32)
    z_nope = jnp.zeros((seq, MLA_NOPE), F32)
    z_pad = jnp.zeros((seq, HEAD_SLOT - MLA_NOPE - MLA_ROPE), F32)
    c_rope = jnp.concatenate([cos, cos], axis=1)
    s_rope = jnp.concatenate([-sin, sin], axis=1)
    qscale = (MLA_NOPE + MLA_ROPE) ** -0.5 * LOG2E
    cq = jnp.concatenate([ones, c_rope, z_pad], axis=1) * qscale
    sq = jnp.concatenate([z_nope, s_rope, z_pad], axis=1) * qscale
    ck = jnp.concatenate([z_nope, c_rope, z_pad], axis=1)
    sk = jnp.concatenate([z_nope, s_rope, z_pad], axis=1)
    return cq, sq, ck, sk


def _swap_halves(w):
    half = w.shape[-1] // 2
    return jnp.concatenate([w[..., half:], w[..., :half]], axis=-1)


def kernel(x, mem, ffn1_norm, ffn1_w_gate, ffn1_w_up, ffn1_w_down, mix_norm, w_in, mla_q_norm, mla_w_q_up, mla_kv_norm, mla_w_kv_up, diff_lambda_q1, diff_lambda_k1, diff_lambda_q2, diff_lambda_k2, diff_sub_norm, rel_bias, w_branch_a, w_branch_b, w_out, xattn_norm, mem_norm, xattn_w_q, xattn_w_kv, xattn_w_o, ffn2_norm, ffn2_w_gate, ffn2_w_up, ffn2_w_down, final_norm):
    b, s, d = x.shape
    depth = ffn1_norm.shape[0]
    mem_len = mem.shape[1]
    t = b * s
    bf = lambda a: a.astype(BF16)
    row = lambda a: a.reshape(1, -1)

    cq, sq, ck, sk = _rope_tables(s)
    r = jnp.arange(TQ)[:, None]
    c = jnp.arange(TK)[None, :]
    mla_mask = jnp.where(c // CHUNK <= r // CHUNK, 0.0, NEG_INF).astype(F32)
    bias = _bias_tiles(rel_bias.astype(F32))
    far = rel_bias.astype(F32)[N_BUCKETS // 2 - 1] * LOG2E

    h = x.reshape(t, d)
    for l in range(depth):
        h = _ffn(h, row(ffn1_norm[l]), bf(ffn1_w_gate[l]), bf(ffn1_w_up[l]), bf(ffn1_w_down[l]),
                 row(final_norm), final_norm=False)

        w = w_in[l]
        o_kv = MLA_Q_LORA
        o_kr = o_kv + MLA_KV_LORA
        o_dq = o_kr + MLA_ROPE
        w_kr = w[:, o_kr:o_dq]
        pad_l = jnp.zeros((d, MLA_NOPE), F32)
        pad_r = jnp.zeros((d, HEAD_SLOT - MLA_NOPE - MLA_ROPE), F32)
        wlat = jnp.concatenate([w[:, :o_kr], pad_l, w_kr, pad_r, pad_l, _swap_halves(w_kr), pad_r], axis=1)
        wbig = w[:, o_dq:o_dq + 3 * d]
        wgate = w[:, o_dq + 3 * d:]

        qk_w = MLA_NOPE + MLA_ROPE
        wq3 = mla_w_q_up[l].reshape(MLA_Q_LORA, MLA_HEADS, qk_w)
        wq_sw = jnp.concatenate([jnp.zeros((MLA_Q_LORA, MLA_HEADS, MLA_NOPE), F32),
                                 _swap_halves(wq3[..., MLA_NOPE:])], axis=-1)
        wq = jnp.concatenate([_pad_heads(mla_w_q_up[l], MLA_HEADS, qk_w),
                              _pad_heads(wq_sw.reshape(MLA_Q_LORA, -1), MLA_HEADS, qk_w)], axis=1)
        wkv3 = mla_w_kv_up[l].reshape(MLA_KV_LORA, MLA_HEADS, MLA_NOPE + MLA_V)
        wkv = jnp.concatenate([_pad_heads(wkv3[..., :MLA_NOPE].reshape(MLA_KV_LORA, -1), MLA_HEADS, MLA_NOPE),
                               _pad_heads(wkv3[..., MLA_NOPE:].reshape(MLA_KV_LORA, -1), MLA_HEADS, MLA_V)], axis=1)

        q, k, v, dq, dk, dv, ga, gb = _proj(
            h, row(mix_norm[l]), bf(wlat), bf(wbig), bf(wgate), row(mla_q_norm[l]), bf(wq),
            row(mla_kv_norm[l]), bf(wkv), (cq, sq, ck, sk), s)

        sh = lambda a: a.reshape(b, s, d)
        oa = _mla_attention(sh(q), sh(k), sh(v), mla_mask)
        lam_init = 0.8 - 0.6 * math.exp(-0.3 * l)
        ob = _diff_attention(sh(dq), sh(dk), sh(dv), bias, far,
                             row(diff_lambda_q1[l]), row(diff_lambda_k1[l]),
                             row(diff_lambda_q2[l]), row(diff_lambda_k2[l]),
                             row(diff_sub_norm[l]), lam_init)

        wa = w_branch_a[l].reshape(MLA_HEADS, MLA_V, d)
        wa = jnp.pad(wa, ((0, 0), (0, HEAD_SLOT - MLA_V), (0, 0))).reshape(MLA_HEADS * HEAD_SLOT, d)
        h = _merge(h, oa.reshape(t, d), ob.reshape(t, d), ga, gb, bf(wa), bf(w_branch_b[l]), bf(w_out[l]))

        kvmem = _memkv(mem.reshape(b * mem_len, d), row(mem_norm[l]), bf(xattn_w_kv[l]))
        h = _xattn(h, row(xattn_norm[l]), bf(xattn_w_q[l]), kvmem, bf(xattn_w_o[l]), s, mem_len)

        last = l == depth - 1
        h = _ffn(h, row(ffn2_norm[l]), bf(ffn2_w_gate[l]), bf(ffn2_w_up[l]), bf(ffn2_w_down[l]),
                 row(final_norm), final_norm=last)
    if depth == 0:
        h = _rms(h, final_norm)
    return h.reshape(b, s, d)
```

```python
import functools
import math

import jax
import jax.numpy as jnp
from jax import lax
from jax.experimental import pallas as pl
from jax.experimental.pallas import tpu as pltpu

F32 = jnp.float32
BF16 = jnp.bfloat16

CHUNK = 64
EPS = 1e-6
NEG_INF = -1e30
MLA_HEADS = 8
MLA_Q_LORA = 384
MLA_KV_LORA = 256
MLA_NOPE = 64
MLA_ROPE = 32
MLA_V = 64
ROPE_BASE = 10000.0
DIFF_HEADS = 8
DIFF_DIM = 64
N_BUCKETS = 32
XATTN_HEADS = 4
XATTN_DIM = 128
LOG2E = math.log2(math.e)

LANES = 128
HEAD_SLOT = LANES
VMEM_LIMIT = 56 * 1024 * 1024

TM_FFN = 256
TM_PROJ = 256
TQ = 512
TK = 512


def _rms(x, g):
    return x * lax.rsqrt(jnp.mean(x * x, axis=-1, keepdims=True) + EPS) * g


def _const_spec(shape):
    nd = len(shape)
    return pl.BlockSpec(shape, lambda *_: (0,) * nd)


def _params(n_grid):
    return pltpu.CompilerParams(
        dimension_semantics=("arbitrary",) * n_grid, vmem_limit_bytes=VMEM_LIMIT)


def _ffn_kernel(x_ref, g_ref, wg_ref, wu_ref, wd_ref, fin_ref, o_ref, *, splits, final_norm):
    x = x_ref[...]
    nb = _rms(x, g_ref[...]).astype(BF16)
    y = None
    for lo, hi in splits:
        g = jnp.dot(nb, wg_ref[:, lo:hi], preferred_element_type=F32)
        u = jnp.dot(nb, wu_ref[:, lo:hi], preferred_element_type=F32)
        h = (g * jax.nn.sigmoid(g) * u).astype(BF16)
        part = jnp.dot(h, wd_ref[lo:hi, :], preferred_element_type=F32)
        y = part if y is None else y + part
    out = x + 0.5 * y
    if final_norm:
        out = _rms(out, fin_ref[...])
    o_ref[...] = out


def _ffn(x, norm_g, wg, wu, wd, fin_g, final_norm):
    t, d = x.shape
    dff = wg.shape[1]
    half = (dff // 2 + 255) // 256 * 256
    splits = ((0, half), (half, dff))
    return pl.pallas_call(
        functools.partial(_ffn_kernel, splits=splits, final_norm=final_norm),
        grid=(t // TM_FFN,),
        in_specs=[
            pl.BlockSpec((TM_FFN, d), lambda i: (i, 0)),
            _const_spec((1, d)),
            _const_spec((d, dff)),
            _const_spec((d, dff)),
            _const_spec((dff, d)),
            _const_spec((1, d)),
        ],
        out_specs=pl.BlockSpec((TM_FFN, d), lambda i: (i, 0)),
        out_shape=jax.ShapeDtypeStruct((t, d), F32),
        compiler_params=_params(1),
        name="ffn",
    )(x, norm_g, wg, wu, wd, fin_g)


def _proj_kernel(h_ref, g_ref, wlat_ref, wbig_ref, wgate_ref, qn_ref, wq_ref, kvn_ref, wkv_ref,
                 cq_ref, sq_ref, ck_ref, sk_ref,
                 q_ref, k_ref, v_ref, dq_ref, dk_ref, dv_ref, ga_ref, gb_ref, *, diff_scale):
    d = h_ref.shape[1]
    u = _rms(h_ref[...], g_ref[...]).astype(BF16)

    lat = jnp.dot(u, wlat_ref[...], preferred_element_type=F32)
    q_lat = lat[:, :MLA_Q_LORA]
    kv_lat = lat[:, MLA_Q_LORA:MLA_Q_LORA + MLA_KV_LORA]
    o = MLA_Q_LORA + MLA_KV_LORA
    kr_a = lat[:, o:o + LANES]
    kr_b = lat[:, o + LANES:o + 2 * LANES]

    qn = _rms(q_lat, qn_ref[...]).astype(BF16)
    qq = jnp.dot(qn, wq_ref[...], preferred_element_type=F32)
    kvn = _rms(kv_lat, kvn_ref[...]).astype(BF16)
    kk = jnp.dot(kvn, wkv_ref[...], preferred_element_type=F32)

    cq, sq = cq_ref[...], sq_ref[...]
    kr = kr_a * ck_ref[...] + kr_b * sk_ref[...]
    nh = MLA_HEADS * HEAD_SLOT
    for hd in range(MLA_HEADS):
        sl = slice(hd * HEAD_SLOT, (hd + 1) * HEAD_SLOT)
        sl2 = slice(nh + hd * HEAD_SLOT, nh + (hd + 1) * HEAD_SLOT)
        q_ref[:, sl] = (qq[:, sl] * cq + qq[:, sl2] * sq).astype(BF16)
        k_ref[:, sl] = (kk[:, sl] + kr).astype(BF16)
    slot_lane = lax.broadcasted_iota(jnp.int32, (1, nh), 1) % HEAD_SLOT
    v_ref[...] = (kk[:, nh:] + jnp.where(slot_lane == MLA_V, 1.0, 0.0)).astype(BF16)

    big = jnp.dot(u, wbig_ref[...], preferred_element_type=F32)
    dq_ref[...] = (big[:, :d] * diff_scale).astype(BF16)
    dk_ref[...] = big[:, d:2 * d].astype(BF16)
    dv_ref[...] = big[:, 2 * d:].astype(BF16)

    gates = jax.nn.sigmoid(jnp.dot(u, wgate_ref[...], preferred_element_type=F32))
    ga_ref[...] = gates[:, :d].astype(BF16)
    gb_ref[...] = gates[:, d:].astype(BF16)


def _proj(h, g, wlat, wbig, wgate, qn, wq, kvn, wkv, tabs, seq):
    t, d = h.shape
    nseq = seq // TM_PROJ
    tok = lambda i: (i, 0)
    pos = lambda i: (i % nseq, 0)
    outs = [jax.ShapeDtypeStruct((t, d), BF16)] * 8
    return pl.pallas_call(
        functools.partial(_proj_kernel, diff_scale=DIFF_DIM ** -0.5 * LOG2E),
        grid=(t // TM_PROJ,),
        in_specs=[
            pl.BlockSpec((TM_PROJ, d), tok),
            _const_spec(g.shape), _const_spec(wlat.shape), _const_spec(wbig.shape),
            _const_spec(wgate.shape), _const_spec(qn.shape), _const_spec(wq.shape),
            _const_spec(kvn.shape), _const_spec(wkv.shape),
        ] + [pl.BlockSpec((TM_PROJ, LANES), pos)] * 4,
        out_specs=[pl.BlockSpec((TM_PROJ, d), tok)] * 8,
        out_shape=outs,
        compiler_params=_params(1),
        name="in_proj",
    )(h, g, wlat, wbig, wgate, qn, wq, kvn, wkv, *tabs)


def _t5_bucket(rel):
    half = N_BUCKETS // 2
    max_exact = half // 2
    n = jnp.abs(rel)
    n2 = n * n
    large = max_exact
    for k in range(1, half - max_exact):
        large = large + (n2 >= (max_exact * max_exact) << k).astype(jnp.int32)
    return jnp.where(rel > 0, half, 0) + jnp.where(n < max_exact, n, large)


def _bias_kernel(tab_ref, o_ref):
    hd = pl.program_id(0)
    r = lax.broadcasted_iota(jnp.int32, (TQ, TK), 0)
    c = lax.broadcasted_iota(jnp.int32, (TQ, TK), 1)
    for delta in range(2):
        bucket = _t5_bucket(c - r - delta * TK)
        val = jnp.zeros((TQ, TK), F32)
        for b in range(N_BUCKETS):
            val = jnp.where(bucket == b, tab_ref[b, hd], val)
        val = (val - tab_ref[N_BUCKETS // 2 - 1, hd]) * LOG2E
        if delta == 0:
            val = jnp.where(c // CHUNK <= r // CHUNK, val, NEG_INF)
        o_ref[0, delta] = val


def _bias_tiles(rel_bias):
    return pl.pallas_call(
        _bias_kernel,
        grid=(DIFF_HEADS,),
        in_specs=[pl.BlockSpec(memory_space=pltpu.SMEM)],
        out_specs=pl.BlockSpec((1, 2, TQ, TK), lambda h: (h, 0, 0, 0)),
        out_shape=jax.ShapeDtypeStruct((DIFF_HEADS, 2, TQ, TK), F32),
        compiler_params=_params(1),
        name="bias_tiles",
    )(rel_bias)


FAR, SUB, DIAG = 0, 1, 2


def _lane_fold(x, op):
    out = x[:, :LANES]
    for c in range(1, x.shape[1] // LANES):
        out = op(out, x[:, c * LANES:(c + 1) * LANES])
    return out


def _qk(q, k):
    return lax.dot_general(q, k, (((1,), (1,)), ((), ())), preferred_element_type=F32)


def _scores(q, k, bias, s_ref, mc_ref):
    s = _qk(q, k)
    if bias is not None:
        s = s + bias
    s_ref[...] = s
    m_cur = jnp.max(_lane_fold(s, jnp.maximum), axis=1, keepdims=True)
    mc_ref[...] = jnp.broadcast_to(m_cur, mc_ref.shape)


def _softmax_pv(s_ref, mc_ref, v, m_ref, l_ref, acc_ref):
    m_prev = m_ref[...]
    m_new = jnp.maximum(m_prev, mc_ref[...])
    alpha = jnp.exp2(m_prev - m_new)
    p = jnp.exp2(s_ref[...] - jnp.tile(m_new, (1, s_ref.shape[1] // LANES)))
    if l_ref is not None:
        l_ref[...] = alpha * l_ref[...] + _lane_fold(p, jnp.add)
    acc_ref[...] = alpha * acc_ref[...] + jnp.dot(p.astype(BF16), v, preferred_element_type=F32)
    m_ref[...] = m_new


def _sweep(i, near, qk, consume):
    def run(kinds, j0, prefetched):
        for n, kind in enumerate(kinds):
            if n == 0 and not prefetched:
                qk(j0, 0, kind)
            if n + 1 < len(kinds):
                qk(j0 + n + 1, (n + 1) % 2, kinds[n + 1])
            consume(j0 + n, n % 2)

    for count in range(1, len(near) + 1):
        @pl.when(i + 1 == count)
        def _(count=count):
            run(near[-count:], 0, False)

    @pl.when(i + 1 > len(near))
    def _():
        n_far = i + 1 - len(near)
        pairs = (n_far - 1) // 2
        qk(0, 0, FAR)

        def body(t, carry):
            j = 2 * t
            qk(j + 1, 1, FAR)
            consume(j, 0)
            qk(j + 2, 0, FAR)
            consume(j + 1, 1)
            return carry

        lax.fori_loop(0, pairs, body, 0)
        j0 = 2 * pairs
        for left in (1, 2):
            @pl.when(n_far - j0 == left)
            def _(left=left):
                run((FAR,) * left + near, j0, True)


def _kv_rows(j):
    return pl.ds(pl.multiple_of(j * TK, TK), TK)


MLA_STREAMS = 2


def _mla_kernel(q_ref, k_ref, v_ref, mask_ref, o_ref, s_ref, mc_ref, m_ref, acc_ref):
    i = pl.program_id(2)
    lanes = [slice(st * HEAD_SLOT, (st + 1) * HEAD_SLOT) for st in range(MLA_STREAMS)]
    qs = [q_ref[0, :, ln] for ln in lanes]
    m_ref[...] = jnp.full(m_ref.shape, NEG_INF, F32)
    acc_ref[...] = jnp.zeros(acc_ref.shape, F32)

    def qk(j, slot, kind):
        bias = mask_ref[...] if kind == DIAG else None
        for st, ln in enumerate(lanes):
            _scores(qs[st], k_ref[0, _kv_rows(j), ln], bias, s_ref.at[st, slot], mc_ref.at[st, slot])

    def consume(j, slot):
        for st, ln in enumerate(lanes):
            _softmax_pv(s_ref.at[st, slot], mc_ref.at[st, slot], v_ref[0, _kv_rows(j), ln],
                        m_ref.at[st], None, acc_ref.at[st])

    _sweep(i, (DIAG,), qk, consume)
    for st, ln in enumerate(lanes):
        acc = acc_ref[st]
        o_ref[0, :, ln] = (acc / acc[:, MLA_V:MLA_V + 1]).astype(BF16)


def _mla_attention(q, k, v, mask):
    b, s, w = q.shape
    width = MLA_STREAMS * HEAD_SLOT
    qspec = pl.BlockSpec((1, TQ, width), lambda bi, h, i: (bi, i, h))
    kvspec = pl.BlockSpec((1, s, width), lambda bi, h, i: (bi, 0, h))
    return pl.pallas_call(
        _mla_kernel,
        grid=(b, w // width, s // TQ),
        in_specs=[qspec, kvspec, kvspec, _const_spec(mask.shape)],
        out_specs=qspec,
        out_shape=jax.ShapeDtypeStruct((b, s, w), BF16),
        scratch_shapes=[pltpu.VMEM((MLA_STREAMS, 2, TQ, TK), F32),
                        pltpu.VMEM((MLA_STREAMS, 2, TQ, LANES), F32),
                        pltpu.VMEM((MLA_STREAMS, TQ, LANES), F32),
                        pltpu.VMEM((MLA_STREAMS, TQ, LANES), F32)],
        compiler_params=_params(3),
        name="mla_attn",
    )(q, k, v, mask)


def _diff_kernel(q_ref, k_ref, v_ref, bias_ref, lq1_ref, lk1_ref, lq2_ref, lk2_ref, sub_ref,
                 o_ref, s_ref, mc_ref, m_ref, l_ref, acc_ref, *, lam_init):
    i = pl.program_id(2)
    q = q_ref[0]
    lane = lax.broadcasted_iota(jnp.int32, q.shape, 1)
    zero = jnp.zeros_like(q)
    qs = [jnp.where(lane < DIFF_DIM, q, zero), jnp.where(lane >= DIFF_DIM, q, zero)]
    m_ref[...] = jnp.full(m_ref.shape, NEG_INF, F32)
    l_ref[...] = jnp.zeros(l_ref.shape, F32)
    acc_ref[...] = jnp.zeros(acc_ref.shape, F32)

    def qk(j, slot, kind):
        k = k_ref[0, _kv_rows(j), :]
        bias = None if kind == FAR else bias_ref[0, 0 if kind == DIAG else 1]
        for st in range(2):
            _scores(qs[st], k, bias, s_ref.at[st, slot], mc_ref.at[st, slot])

    def consume(j, slot):
        v = v_ref[0, _kv_rows(j), :]
        for st in range(2):
            _softmax_pv(s_ref.at[st, slot], mc_ref.at[st, slot], v,
                        m_ref.at[st], l_ref.at[st], acc_ref.at[st])

    _sweep(i, (SUB, DIAG), qk, consume)

    lam = (jnp.exp(jnp.sum(lq1_ref[...] * lk1_ref[...], axis=1, keepdims=True))
           - jnp.exp(jnp.sum(lq2_ref[...] * lk2_ref[...], axis=1, keepdims=True)) + lam_init)
    outs = [acc_ref[st] / jnp.sum(l_ref[st], axis=1, keepdims=True) for st in range(2)]
    o = outs[0] - lam * outs[1]
    o_ref[0] = (_rms(o, sub_ref[...]) * (1.0 - lam_init)).astype(BF16)


def _diff_attention(q, k, v, bias, lq1, lk1, lq2, lk2, sub, lam_init):
    b, s, w = q.shape
    heads = w // HEAD_SLOT
    qspec = pl.BlockSpec((1, TQ, HEAD_SLOT), lambda bi, h, i: (bi, i, h))
    kvspec = pl.BlockSpec((1, s, HEAD_SLOT), lambda bi, h, i: (bi, 0, h))
    vec = _const_spec(lq1.shape)
    return pl.pallas_call(
        functools.partial(_diff_kernel, lam_init=lam_init),
        grid=(b, heads, s // TQ),
        in_specs=[qspec, kvspec, kvspec,
                  pl.BlockSpec((1, 2, TQ, TK), lambda bi, h, i: (h, 0, 0, 0)),
                  vec, vec, vec, vec, _const_spec(sub.shape)],
        out_specs=qspec,
        out_shape=jax.ShapeDtypeStruct((b, s, w), BF16),
        scratch_shapes=[pltpu.VMEM((2, 2, TQ, TK), F32), pltpu.VMEM((2, 2, TQ, LANES), F32)]
        + [pltpu.VMEM((2, TQ, LANES), F32)] * 3,
        compiler_params=_params(3),
        name="diff_attn",
    )(q, k, v, bias, lq1, lk1, lq2, lk2, sub)


def _merge_kernel(h_ref, oa_ref, ob_ref, ga_ref, gb_ref, wa_ref, wb_ref, wo_ref, o_ref):
    ya = jnp.dot(oa_ref[...], wa_ref[...], preferred_element_type=F32)
    yb = jnp.dot(ob_ref[...], wb_ref[...], preferred_element_type=F32)
    merged = ga_ref[...].astype(F32) * ya + gb_ref[...].astype(F32) * yb
    o_ref[...] = h_ref[...] + jnp.dot(merged.astype(BF16), wo_ref[...], preferred_element_type=F32)


def _merge(h, oa, ob, ga, gb, wa, wb, wo):
    t, d = h.shape
    tok = pl.BlockSpec((TM_PROJ, d), lambda i: (i, 0))
    return pl.pallas_call(
        _merge_kernel,
        grid=(t // TM_PROJ,),
        in_specs=[tok] * 5 + [_const_spec(wa.shape), _const_spec(wb.shape), _const_spec(wo.shape)],
        out_specs=tok,
        out_shape=jax.ShapeDtypeStruct((t, d), F32),
        compiler_params=_params(1),
        name="merge",
    )(h, oa, ob, ga, gb, wa, wb, wo)


def _memkv_kernel(m_ref, g_ref, w_ref, o_ref):
    mn = _rms(m_ref[...], g_ref[...]).astype(BF16)
    o_ref[...] = jnp.dot(mn, w_ref[...], preferred_element_type=F32).astype(BF16)


def _memkv(mem2d, g, w):
    rows, d = mem2d.shape
    return pl.pallas_call(
        _memkv_kernel,
        grid=(1,),
        in_specs=[_const_spec(mem2d.shape), _const_spec(g.shape), _const_spec(w.shape)],
        out_specs=_const_spec((rows, w.shape[1])),
        out_shape=jax.ShapeDtypeStruct((rows, w.shape[1]), BF16),
        compiler_params=_params(1),
        name="mem_kv",
    )(mem2d, g, w)


def _xattn_kernel(h_ref, g_ref, wq_ref, kv_ref, wo_ref, o_ref, *, scale):
    h = h_ref[...]
    u = _rms(h, g_ref[...]).astype(BF16)
    q = (jnp.dot(u, wq_ref[...], preferred_element_type=F32) * scale).astype(BF16)
    kv = kv_ref[...]
    outs = []
    for hd in range(XATTN_HEADS):
        qh = q[:, hd * XATTN_DIM:(hd + 1) * XATTN_DIM]
        kh = kv[:, 2 * hd * XATTN_DIM:(2 * hd + 1) * XATTN_DIM]
        vh = kv[:, (2 * hd + 1) * XATTN_DIM:(2 * hd + 2) * XATTN_DIM]
        s = lax.dot_general(qh, kh, (((1,), (1,)), ((), ())), preferred_element_type=F32)
        m = jnp.max(s, axis=1, keepdims=True)
        p = jnp.exp2(s - m)
        l = jnp.sum(p, axis=1, keepdims=True)
        oh = jnp.dot(p.astype(BF16), vh, preferred_element_type=F32) / l
        outs.append(oh.astype(BF16))
    o = jnp.concatenate(outs, axis=1)
    o_ref[...] = h + jnp.dot(o, wo_ref[...], preferred_element_type=F32)


def _xattn(h, g, wq, kvmem, wo, seq, mem_len):
    t, d = h.shape
    nseq = seq // TM_PROJ
    tok = pl.BlockSpec((TM_PROJ, d), lambda i: (i, 0))
    return pl.pallas_call(
        functools.partial(_xattn_kernel, scale=XATTN_DIM ** -0.5 * LOG2E),
        grid=(t // TM_PROJ,),
        in_specs=[tok, _const_spec(g.shape), _const_spec(wq.shape),
                  pl.BlockSpec((mem_len, kvmem.shape[1]), lambda i: (i // nseq, 0)),
                  _const_spec(wo.shape)],
        out_specs=tok,
        out_shape=jax.ShapeDtypeStruct((t, d), F32),
        compiler_params=_params(1),
        name="xattn",
    )(h, g, wq, kvmem, wo)


def _pad_heads(w, heads, width):
    k = w.shape[0]
    w = w.reshape(k, heads, width)
    return jnp.pad(w, ((0, 0), (0, 0), (0, HEAD_SLOT - width))).reshape(k, heads * HEAD_SLOT)


def _rope_tables(seq):
    half = MLA_ROPE // 2
    pos = jnp.arange(seq)
    freqs = ROPE_BASE ** (-jnp.arange(half, dtype=F32) / half)
    ang = pos.astype(F32)[:, None] * freqs[None, :]
    cos, sin = jnp.cos(ang), jnp.sin(ang)
    ones = jnp.ones((seq, MLA_NOPE), F32)
    z_nope = jnp.zeros((seq, MLA_NOPE), F32)
    z_pad = jnp.zeros((seq, HEAD_SLOT - MLA_NOPE - MLA_ROPE), F32)
    c_rope = jnp.concatenate([cos, cos], axis=1)
    s_rope = jnp.concatenate([-sin, sin], axis=1)
    qscale = (MLA_NOPE + MLA_ROPE) ** -0.5 * LOG2E
    cq = jnp.concatenate([ones, c_rope, z_pad], axis=1) * qscale
    sq = jnp.concatenate([z_nope, s_rope, z_pad], axis=1) * qscale
    ck = jnp.concatenate([z_nope, c_rope, z_pad], axis=1)
    sk = jnp.concatenate([z_nope, s_rope, z_pad], axis=1)
    return cq, sq, ck, sk


def _swap_halves(w):
    half = w.shape[-1] // 2
    return jnp.concatenate([w[..., half:], w[..., :half]], axis=-1)


def kernel(x, mem, ffn1_norm, ffn1_w_gate, ffn1_w_up, ffn1_w_down, mix_norm, w_in, mla_q_norm, mla_w_q_up, mla_kv_norm, mla_w_kv_up, diff_lambda_q1, diff_lambda_k1, diff_lambda_q2, diff_lambda_k2, diff_sub_norm, rel_bias, w_branch_a, w_branch_b, w_out, xattn_norm, mem_norm, xattn_w_q, xattn_w_kv, xattn_w_o, ffn2_norm, ffn2_w_gate, ffn2_w_up, ffn2_w_down, final_norm):
    b, s, d = x.shape
    depth = ffn1_norm.shape[0]
    mem_len = mem.shape[1]
    t = b * s
    bf = lambda a: a.astype(BF16)
    row = lambda a: a.reshape(1, -1)

    cq, sq, ck, sk = _rope_tables(s)
    r = jnp.arange(TQ)[:, None]
    c = jnp.arange(TK)[None, :]
    mla_mask = jnp.where(c // CHUNK <= r // CHUNK, 0.0, NEG_INF).astype(F32)
    bias = _bias_tiles(rel_bias.astype(F32))

    h = x.reshape(t, d)
    for l in range(depth):
        h = _ffn(h, row(ffn1_norm[l]), bf(ffn1_w_gate[l]), bf(ffn1_w_up[l]), bf(ffn1_w_down[l]),
                 row(final_norm), final_norm=False)

        w = w_in[l]
        o_kv = MLA_Q_LORA
        o_kr = o_kv + MLA_KV_LORA
        o_dq = o_kr + MLA_ROPE
        w_kr = w[:, o_kr:o_dq]
        pad_l = jnp.zeros((d, MLA_NOPE), F32)
        pad_r = jnp.zeros((d, HEAD_SLOT - MLA_NOPE - MLA_ROPE), F32)
        wlat = jnp.concatenate([w[:, :o_kr], pad_l, w_kr, pad_r, pad_l, _swap_halves(w_kr), pad_r], axis=1)
        wbig = w[:, o_dq:o_dq + 3 * d]
        wgate = w[:, o_dq + 3 * d:]

        qk_w = MLA_NOPE + MLA_ROPE
        wq3 = mla_w_q_up[l].reshape(MLA_Q_LORA, MLA_HEADS, qk_w)
        wq_sw = jnp.concatenate([jnp.zeros((MLA_Q_LORA, MLA_HEADS, MLA_NOPE), F32),
                                 _swap_halves(wq3[..., MLA_NOPE:])], axis=-1)
        wq = jnp.concatenate([_pad_heads(mla_w_q_up[l], MLA_HEADS, qk_w),
                              _pad_heads(wq_sw.reshape(MLA_Q_LORA, -1), MLA_HEADS, qk_w)], axis=1)
        wkv3 = mla_w_kv_up[l].reshape(MLA_KV_LORA, MLA_HEADS, MLA_NOPE + MLA_V)
        wkv = jnp.concatenate([_pad_heads(wkv3[..., :MLA_NOPE].reshape(MLA_KV_LORA, -1), MLA_HEADS, MLA_NOPE),
                               _pad_heads(wkv3[..., MLA_NOPE:].reshape(MLA_KV_LORA, -1), MLA_HEADS, MLA_V)], axis=1)

        q, k, v, dq, dk, dv, ga, gb = _proj(
            h, row(mix_norm[l]), bf(wlat), bf(wbig), bf(wgate), row(mla_q_norm[l]), bf(wq),
            row(mla_kv_norm[l]), bf(wkv), (cq, sq, ck, sk), s)

        sh = lambda a: a.reshape(b, s, d)
        oa = _mla_attention(sh(q), sh(k), sh(v), mla_mask)
        lam_init = 0.8 - 0.6 * math.exp(-0.3 * l)
        ob = _diff_attention(sh(dq), sh(dk), sh(dv), bias,
                             row(diff_lambda_q1[l]), row(diff_lambda_k1[l]),
                             row(diff_lambda_q2[l]), row(diff_lambda_k2[l]),
                             row(diff_sub_norm[l]), lam_init)

        wa = w_branch_a[l].reshape(MLA_HEADS, MLA_V, d)
        wa = jnp.pad(wa, ((0, 0), (0, HEAD_SLOT - MLA_V), (0, 0))).reshape(MLA_HEADS * HEAD_SLOT, d)
        h = _merge(h, oa.reshape(t, d), ob.reshape(t, d), ga, gb, bf(wa), bf(w_branch_b[l]), bf(w_out[l]))

        kvmem = _memkv(mem.reshape(b * mem_len, d), row(mem_norm[l]), bf(xattn_w_kv[l]))
        h = _xattn(h, row(xattn_norm[l]), bf(xattn_w_q[l]), kvmem, bf(xattn_w_o[l]), s, mem_len)

        last = l == depth - 1
        h = _ffn(h, row(ffn2_norm[l]), bf(ffn2_w_gate[l]), bf(ffn2_w_up[l]), bf(ffn2_w_down[l]),
                 row(final_norm), final_norm=last)
    return h.reshape(b, s, d)
```

```python
import functools
import math

import jax
import jax.numpy as jnp
from jax import lax
from jax.experimental import pallas as pl
from jax.experimental.pallas import tpu as pltpu

F32 = jnp.float32
BF16 = jnp.bfloat16

CHUNK = 64
EPS = 1e-6
NEG_INF = -1e30
MLA_HEADS = 8
MLA_Q_LORA = 384
MLA_KV_LORA = 256
MLA_NOPE = 64
MLA_ROPE = 32
MLA_V = 64
ROPE_BASE = 10000.0
DIFF_HEADS = 8
DIFF_DIM = 64
N_BUCKETS = 32
XATTN_HEADS = 4
XATTN_DIM = 128
LOG2E = math.log2(math.e)

LANES = 128
HEAD_SLOT = LANES
VMEM_LIMIT = 56 * 1024 * 1024

TM_FFN = 256
TM_PROJ = 256
TQ = 512
TK = 512


def _rms(x, g):
    return x * lax.rsqrt(jnp.mean(x * x, axis=-1, keepdims=True) + EPS) * g


def _const_spec(shape):
    nd = len(shape)
    return pl.BlockSpec(shape, lambda *_: (0,) * nd)


def _params(n_grid):
    return pltpu.CompilerParams(
        dimension_semantics=("arbitrary",) * n_grid, vmem_limit_bytes=VMEM_LIMIT)


def _ffn_kernel(x_ref, g_ref, wg_ref, wu_ref, wd_ref, fin_ref, o_ref, *, splits, final_norm):
    x = x_ref[...]
    nb = _rms(x, g_ref[...]).astype(BF16)
    y = None
    for lo, hi in splits:
        g = jnp.dot(nb, wg_ref[:, lo:hi], preferred_element_type=F32)
        u = jnp.dot(nb, wu_ref[:, lo:hi], preferred_element_type=F32)
        h = (g * jax.nn.sigmoid(g) * u).astype(BF16)
        part = jnp.dot(h, wd_ref[lo:hi, :], preferred_element_type=F32)
        y = part if y is None else y + part
    out = x + 0.5 * y
    if final_norm:
        out = _rms(out, fin_ref[...])
    o_ref[...] = out


def _ffn(x, norm_g, wg, wu, wd, fin_g, final_norm):
    t, d = x.shape
    dff = wg.shape[1]
    half = (dff // 2 + 255) // 256 * 256
    splits = ((0, half), (half, dff))
    return pl.pallas_call(
        functools.partial(_ffn_kernel, splits=splits, final_norm=final_norm),
        grid=(t // TM_FFN,),
        in_specs=[
            pl.BlockSpec((TM_FFN, d), lambda i: (i, 0)),
            _const_spec((1, d)),
            _const_spec((d, dff)),
            _const_spec((d, dff)),
            _const_spec((dff, d)),
            _const_spec((1, d)),
        ],
        out_specs=pl.BlockSpec((TM_FFN, d), lambda i: (i, 0)),
        out_shape=jax.ShapeDtypeStruct((t, d), F32),
        compiler_params=_params(1),
        name="ffn",
    )(x, norm_g, wg, wu, wd, fin_g)


def _proj_kernel(h_ref, g_ref, wlat_ref, wbig_ref, wgate_ref, qn_ref, wq_ref, kvn_ref, wkv_ref,
                 cq_ref, sq_ref, ck_ref, sk_ref,
                 q_ref, k_ref, v_ref, dq_ref, dk_ref, dv_ref, ga_ref, gb_ref, *, diff_scale):
    d = h_ref.shape[1]
    u = _rms(h_ref[...], g_ref[...]).astype(BF16)

    lat = jnp.dot(u, wlat_ref[...], preferred_element_type=F32)
    q_lat = lat[:, :MLA_Q_LORA]
    kv_lat = lat[:, MLA_Q_LORA:MLA_Q_LORA + MLA_KV_LORA]
    o = MLA_Q_LORA + MLA_KV_LORA
    kr_a = lat[:, o:o + LANES]
    kr_b = lat[:, o + LANES:o + 2 * LANES]

    qn = _rms(q_lat, qn_ref[...]).astype(BF16)
    qq = jnp.dot(qn, wq_ref[...], preferred_element_type=F32)
    kvn = _rms(kv_lat, kvn_ref[...]).astype(BF16)
    kk = jnp.dot(kvn, wkv_ref[...], preferred_element_type=F32)

    cq, sq = cq_ref[...], sq_ref[...]
    kr = kr_a * ck_ref[...] + kr_b * sk_ref[...]
    nh = MLA_HEADS * HEAD_SLOT
    for hd in range(MLA_HEADS):
        sl = slice(hd * HEAD_SLOT, (hd + 1) * HEAD_SLOT)
        sl2 = slice(nh + hd * HEAD_SLOT, nh + (hd + 1) * HEAD_SLOT)
        q_ref[:, sl] = (qq[:, sl] * cq + qq[:, sl2] * sq).astype(BF16)
        k_ref[:, sl] = (kk[:, sl] + kr).astype(BF16)
    slot_lane = lax.broadcasted_iota(jnp.int32, (1, nh), 1) % HEAD_SLOT
    v_ref[...] = (kk[:, nh:] + jnp.where(slot_lane == MLA_V, 1.0, 0.0)).astype(BF16)

    big = jnp.dot(u, wbig_ref[...], preferred_element_type=F32)
    dq_ref[...] = (big[:, :d] * diff_scale).astype(BF16)
    dk_ref[...] = big[:, d:2 * d].astype(BF16)
    dv_ref[...] = big[:, 2 * d:].astype(BF16)

    gates = jax.nn.sigmoid(jnp.dot(u, wgate_ref[...], preferred_element_type=F32))
    ga_ref[...] = gates[:, :d].astype(BF16)
    gb_ref[...] = gates[:, d:].astype(BF16)


def _proj(h, g, wlat, wbig, wgate, qn, wq, kvn, wkv, tabs, seq):
    t, d = h.shape
    nseq = seq // TM_PROJ
    tok = lambda i: (i, 0)
    pos = lambda i: (i % nseq, 0)
    outs = [jax.ShapeDtypeStruct((t, d), BF16)] * 8
    return pl.pallas_call(
        functools.partial(_proj_kernel, diff_scale=DIFF_DIM ** -0.5 * LOG2E),
        grid=(t // TM_PROJ,),
        in_specs=[
            pl.BlockSpec((TM_PROJ, d), tok),
            _const_spec(g.shape), _const_spec(wlat.shape), _const_spec(wbig.shape),
            _const_spec(wgate.shape), _const_spec(qn.shape), _const_spec(wq.shape),
            _const_spec(kvn.shape), _const_spec(wkv.shape),
        ] + [pl.BlockSpec((TM_PROJ, LANES), pos)] * 4,
        out_specs=[pl.BlockSpec((TM_PROJ, d), tok)] * 8,
        out_shape=outs,
        compiler_params=_params(1),
        name="in_proj",
    )(h, g, wlat, wbig, wgate, qn, wq, kvn, wkv, *tabs)


def _t5_bucket(rel):
    half = N_BUCKETS // 2
    max_exact = half // 2
    n = jnp.abs(rel)
    n2 = n * n
    large = max_exact
    for k in range(1, half - max_exact):
        large = large + (n2 >= (max_exact * max_exact) << k).astype(jnp.int32)
    return jnp.where(rel > 0, half, 0) + jnp.where(n < max_exact, n, large)


def _bias_kernel(tab_ref, o_ref):
    hd = pl.program_id(0)
    r = lax.broadcasted_iota(jnp.int32, (TQ, TK), 0)
    c = lax.broadcasted_iota(jnp.int32, (TQ, TK), 1)
    for delta in range(2):
        bucket = _t5_bucket(c - r - delta * TK)
        val = jnp.zeros((TQ, TK), F32)
        for b in range(N_BUCKETS):
            val = jnp.where(bucket == b, tab_ref[b, hd], val)
        val = (val - tab_ref[N_BUCKETS // 2 - 1, hd]) * LOG2E
        if delta == 0:
            val = jnp.where(c // CHUNK <= r // CHUNK, val, NEG_INF)
        o_ref[0, delta] = val


def _bias_tiles(rel_bias):
    return pl.pallas_call(
        _bias_kernel,
        grid=(DIFF_HEADS,),
        in_specs=[pl.BlockSpec(memory_space=pltpu.SMEM)],
        out_specs=pl.BlockSpec((1, 2, TQ, TK), lambda h: (h, 0, 0, 0)),
        out_shape=jax.ShapeDtypeStruct((DIFF_HEADS, 2, TQ, TK), F32),
        compiler_params=_params(1),
        name="bias_tiles",
    )(rel_bias)


FAR, SUB, DIAG = 0, 1, 2


def _lane_fold(x, op):
    out = x[:, :LANES]
    for c in range(1, x.shape[1] // LANES):
        out = op(out, x[:, c * LANES:(c + 1) * LANES])
    return out


def _qk(q, k):
    return lax.dot_general(q, k, (((1,), (1,)), ((), ())), preferred_element_type=F32)


def _scores(q, k, bias, s_ref, mc_ref):
    s = _qk(q, k)
    if bias is not None:
        s = s + bias
    s_ref[...] = s
    m_cur = jnp.max(_lane_fold(s, jnp.maximum), axis=1, keepdims=True)
    mc_ref[...] = jnp.broadcast_to(m_cur, mc_ref.shape)


def _softmax_pv(s_ref, mc_ref, v, m_ref, acc_ref):
    m_prev = m_ref[...]
    m_new = jnp.maximum(m_prev, mc_ref[...])
    alpha = jnp.exp2(m_prev - m_new)
    p = jnp.exp2(s_ref[...] - jnp.tile(m_new, (1, s_ref.shape[1] // LANES)))
    pv = jnp.dot(p.astype(BF16), v, preferred_element_type=F32)
    acc_ref[...] = jnp.tile(alpha, (1, pv.shape[1] // LANES)) * acc_ref[...] + pv
    m_ref[...] = m_new


def _sweep(i, near, qk, consume):
    def run(kinds, j0, prefetched):
        for n, kind in enumerate(kinds):
            if n == 0 and not prefetched:
                qk(j0, 0, kind)
            if n + 1 < len(kinds):
                qk(j0 + n + 1, (n + 1) % 2, kinds[n + 1])
            consume(j0 + n, n % 2)

    for count in range(1, len(near) + 1):
        @pl.when(i + 1 == count)
        def _(count=count):
            run(near[-count:], 0, False)

    @pl.when(i + 1 > len(near))
    def _():
        n_far = i + 1 - len(near)
        pairs = (n_far - 1) // 2
        qk(0, 0, FAR)

        def body(t, carry):
            j = 2 * t
            qk(j + 1, 1, FAR)
            consume(j, 0)
            qk(j + 2, 0, FAR)
            consume(j + 1, 1)
            return carry

        lax.fori_loop(0, pairs, body, 0)
        j0 = 2 * pairs
        for left in (1, 2):
            @pl.when(n_far - j0 == left)
            def _(left=left):
                run((FAR,) * left + near, j0, True)


def _kv_rows(j):
    return pl.ds(pl.multiple_of(j * TK, TK), TK)


MLA_STREAMS = 2


def _mla_kernel(q_ref, k_ref, v_ref, mask_ref, o_ref, s_ref, mc_ref, m_ref, acc_ref):
    i = pl.program_id(2)
    lanes = [slice(st * HEAD_SLOT, (st + 1) * HEAD_SLOT) for st in range(MLA_STREAMS)]
    qs = [q_ref[0, :, ln] for ln in lanes]
    m_ref[...] = jnp.full(m_ref.shape, NEG_INF, F32)
    acc_ref[...] = jnp.zeros(acc_ref.shape, F32)

    def qk(j, slot, kind):
        bias = mask_ref[...] if kind == DIAG else None
        for st, ln in enumerate(lanes):
            _scores(qs[st], k_ref[0, _kv_rows(j), ln], bias, s_ref.at[st, slot], mc_ref.at[st, slot])

    def consume(j, slot):
        for st, ln in enumerate(lanes):
            _softmax_pv(s_ref.at[st, slot], mc_ref.at[st, slot], v_ref[0, _kv_rows(j), ln],
                        m_ref.at[st], acc_ref.at[st])

    _sweep(i, (DIAG,), qk, consume)
    for st, ln in enumerate(lanes):
        acc = acc_ref[st]
        o_ref[0, :, ln] = (acc / acc[:, MLA_V:MLA_V + 1]).astype(BF16)


def _mla_attention(q, k, v, mask):
    b, s, w = q.shape
    width = MLA_STREAMS * HEAD_SLOT
    qspec = pl.BlockSpec((1, TQ, width), lambda bi, h, i: (bi, i, h))
    kvspec = pl.BlockSpec((1, s, width), lambda bi, h, i: (bi, 0, h))
    return pl.pallas_call(
        _mla_kernel,
        grid=(b, w // width, s // TQ),
        in_specs=[qspec, kvspec, kvspec, _const_spec(mask.shape)],
        out_specs=qspec,
        out_shape=jax.ShapeDtypeStruct((b, s, w), BF16),
        scratch_shapes=[pltpu.VMEM((MLA_STREAMS, 2, TQ, TK), F32),
                        pltpu.VMEM((MLA_STREAMS, 2, TQ, LANES), F32),
                        pltpu.VMEM((MLA_STREAMS, TQ, LANES), F32),
                        pltpu.VMEM((MLA_STREAMS, TQ, LANES), F32)],
        compiler_params=_params(3),
        name="mla_attn",
    )(q, k, v, mask)


DIFF_HEADS_PER_STEP = 1


def _diff_kernel(q_ref, k_ref, v_ref, bias_ref, lq1_ref, lk1_ref, lq2_ref, lk2_ref, sub_ref,
                 o_ref, s_ref, mc_ref, m_ref, acc_ref, *, lam_init):
    i = pl.program_id(2)
    lanes = [slice(hd * HEAD_SLOT, (hd + 1) * HEAD_SLOT) for hd in range(DIFF_HEADS_PER_STEP)]
    lane = lax.broadcasted_iota(jnp.int32, (TQ, HEAD_SLOT), 1)
    zero = jnp.zeros((TQ, HEAD_SLOT), BF16)
    qs = []
    for ln in lanes:
        q = q_ref[0, :, ln]
        qs += [jnp.where(lane < DIFF_DIM, q, zero), jnp.where(lane >= DIFF_DIM, q, zero)]
    m_ref[...] = jnp.full(m_ref.shape, NEG_INF, F32)
    acc_ref[...] = jnp.zeros(acc_ref.shape, F32)
    ones_tile = jnp.where(lax.broadcasted_iota(jnp.int32, (TK, LANES), 1) == 0, 1.0, 0.0).astype(BF16)

    def qk(j, slot, kind):
        for hd, ln in enumerate(lanes):
            k = k_ref[0, _kv_rows(j), ln]
            bias = None if kind == FAR else bias_ref[hd, 0 if kind == DIAG else 1]
            for st in (2 * hd, 2 * hd + 1):
                _scores(qs[st], k, bias, s_ref.at[st, slot], mc_ref.at[st, slot])

    def consume(j, slot):
        for hd, ln in enumerate(lanes):
            v = jnp.concatenate([v_ref[0, _kv_rows(j), ln], ones_tile], axis=1)
            for st in (2 * hd, 2 * hd + 1):
                _softmax_pv(s_ref.at[st, slot], mc_ref.at[st, slot], v, m_ref.at[st], acc_ref.at[st])

    _sweep(i, (SUB, DIAG), qk, consume)

    lam = (jnp.exp(jnp.sum(lq1_ref[...] * lk1_ref[...], axis=1, keepdims=True))
           - jnp.exp(jnp.sum(lq2_ref[...] * lk2_ref[...], axis=1, keepdims=True)) + lam_init)
    for hd, ln in enumerate(lanes):
        outs = [acc_ref[st, :, :HEAD_SLOT] / acc_ref[st, :, HEAD_SLOT:HEAD_SLOT + 1]
                for st in (2 * hd, 2 * hd + 1)]
        o = outs[0] - lam * outs[1]
        o_ref[0, :, ln] = (_rms(o, sub_ref[...]) * (1.0 - lam_init)).astype(BF16)


def _diff_attention(q, k, v, bias, lq1, lk1, lq2, lk2, sub, lam_init):
    b, s, w = q.shape
    hps = DIFF_HEADS_PER_STEP
    width = hps * HEAD_SLOT
    qspec = pl.BlockSpec((1, TQ, width), lambda bi, h, i: (bi, i, h))
    kvspec = pl.BlockSpec((1, s, width), lambda bi, h, i: (bi, 0, h))
    vec = _const_spec(lq1.shape)
    return pl.pallas_call(
        functools.partial(_diff_kernel, lam_init=lam_init),
        grid=(b, w // width, s // TQ),
        in_specs=[qspec, kvspec, kvspec,
                  pl.BlockSpec((hps, 2, TQ, TK), lambda bi, h, i: (h, 0, 0, 0)),
                  vec, vec, vec, vec, _const_spec(sub.shape)],
        out_specs=qspec,
        out_shape=jax.ShapeDtypeStruct((b, s, w), BF16),
        scratch_shapes=[pltpu.VMEM((2 * hps, 2, TQ, TK), F32), pltpu.VMEM((2 * hps, 2, TQ, LANES), F32),
                        pltpu.VMEM((2 * hps, TQ, LANES), F32), pltpu.VMEM((2 * hps, TQ, 2 * LANES), F32)],
        compiler_params=_params(3),
        name="diff_attn",
    )(q, k, v, bias, lq1, lk1, lq2, lk2, sub)


def _merge_kernel(h_ref, oa_ref, ob_ref, ga_ref, gb_ref, wa_ref, wb_ref, wo_ref, o_ref):
    ya = jnp.dot(oa_ref[...], wa_ref[...], preferred_element_type=F32)
    yb = jnp.dot(ob_ref[...], wb_ref[...], preferred_element_type=F32)
    merged = ga_ref[...].astype(F32) * ya + gb_ref[...].astype(F32) * yb
    o_ref[...] = h_ref[...] + jnp.dot(merged.astype(BF16), wo_ref[...], preferred_element_type=F32)


def _merge(h, oa, ob, ga, gb, wa, wb, wo):
    t, d = h.shape
    tok = pl.BlockSpec((TM_PROJ, d), lambda i: (i, 0))
    return pl.pallas_call(
        _merge_kernel,
        grid=(t // TM_PROJ,),
        in_specs=[tok] * 5 + [_const_spec(wa.shape), _const_spec(wb.shape), _const_spec(wo.shape)],
        out_specs=tok,
        out_shape=jax.ShapeDtypeStruct((t, d), F32),
        compiler_params=_params(1),
        name="merge",
    )(h, oa, ob, ga, gb, wa, wb, wo)


def _memkv_kernel(m_ref, g_ref, w_ref, o_ref):
    mn = _rms(m_ref[...], g_ref[...]).astype(BF16)
    o_ref[...] = jnp.dot(mn, w_ref[...], preferred_element_type=F32).astype(BF16)


def _memkv(mem2d, g, w):
    rows, d = mem2d.shape
    return pl.pallas_call(
        _memkv_kernel,
        grid=(1,),
        in_specs=[_const_spec(mem2d.shape), _const_spec(g.shape), _const_spec(w.shape)],
        out_specs=_const_spec((rows, w.shape[1])),
        out_shape=jax.ShapeDtypeStruct((rows, w.shape[1]), BF16),
        compiler_params=_params(1),
        name="mem_kv",
    )(mem2d, g, w)


def _xattn_kernel(h_ref, g_ref, wq_ref, kv_ref, wo_ref, o_ref, *, scale):
    h = h_ref[...]
    u = _rms(h, g_ref[...]).astype(BF16)
    q = (jnp.dot(u, wq_ref[...], preferred_element_type=F32) * scale).astype(BF16)
    kv = kv_ref[...]
    outs = []
    for hd in range(XATTN_HEADS):
        qh = q[:, hd * XATTN_DIM:(hd + 1) * XATTN_DIM]
        kh = kv[:, 2 * hd * XATTN_DIM:(2 * hd + 1) * XATTN_DIM]
        vh = kv[:, (2 * hd + 1) * XATTN_DIM:(2 * hd + 2) * XATTN_DIM]
        s = lax.dot_general(qh, kh, (((1,), (1,)), ((), ())), preferred_element_type=F32)
        m = jnp.max(s, axis=1, keepdims=True)
        p = jnp.exp2(s - m)
        l = jnp.sum(p, axis=1, keepdims=True)
        oh = jnp.dot(p.astype(BF16), vh, preferred_element_type=F32) / l
        outs.append(oh.astype(BF16))
    o = jnp.concatenate(outs, axis=1)
    o_ref[...] = h + jnp.dot(o, wo_ref[...], preferred_element_type=F32)


def _xattn(h, g, wq, kvmem, wo, seq, mem_len):
    t, d = h.shape
    nseq = seq // TM_PROJ
    tok = pl.BlockSpec((TM_PROJ, d), lambda i: (i, 0))
    return pl.pallas_call(
        functools.partial(_xattn_kernel, scale=XATTN_DIM ** -0.5 * LOG2E),
        grid=(t // TM_PROJ,),
        in_specs=[tok, _const_spec(g.shape), _const_spec(wq.shape),
                  pl.BlockSpec((mem_len, kvmem.shape[1]), lambda i: (i // nseq, 0)),
                  _const_spec(wo.shape)],
        out_specs=tok,
        out_shape=jax.ShapeDtypeStruct((t, d), F32),
        compiler_params=_params(1),
        name="xattn",
    )(h, g, wq, kvmem, wo)


def _pad_heads(w, heads, width):
    k = w.shape[0]
    w = w.reshape(k, heads, width)
    return jnp.pad(w, ((0, 0), (0, 0), (0, HEAD_SLOT - width))).reshape(k, heads * HEAD_SLOT)


def _rope_tables(seq):
    half = MLA_ROPE // 2
    pos = jnp.arange(seq)
    freqs = ROPE_BASE ** (-jnp.arange(half, dtype=F32) / half)
    ang = pos.astype(F32)[:, None] * freqs[None, :]
    cos, sin = jnp.cos(ang), jnp.sin(ang)
    ones = jnp.ones((seq, MLA_NOPE), F32)
    z_nope = jnp.zeros((seq, MLA_NOPE), F32)
    z_pad = jnp.zeros((seq, HEAD_SLOT - MLA_NOPE - MLA_ROPE), F32)
    c_rope = jnp.concatenate([cos, cos], axis=1)
    s_rope = jnp.concatenate([-sin, sin], axis=1)
    qscale = (MLA_NOPE + MLA_ROPE) ** -0.5 * LOG2E
    cq = jnp.concatenate([ones, c_rope, z_pad], axis=1) * qscale
    sq = jnp.concatenate([z_nope, s_rope, z_pad], axis=1) * qscale
    ck = jnp.concatenate([z_nope, c_rope, z_pad], axis=1)
    sk = jnp.concatenate([z_nope, s_rope, z_pad], axis=1)
    return cq, sq, ck, sk


def _swap_halves(w):
    half = w.shape[-1] // 2
    return jnp.concatenate([w[..., half:], w[..., :half]], axis=-1)


def kernel(x, mem, ffn1_norm, ffn1_w_gate, ffn1_w_up, ffn1_w_down, mix_norm, w_in, mla_q_norm, mla_w_q_up, mla_kv_norm, mla_w_kv_up, diff_lambda_q1, diff_lambda_k1, diff_lambda_q2, diff_lambda_k2, diff_sub_norm, rel_bias, w_branch_a, w_branch_b, w_out, xattn_norm, mem_norm, xattn_w_q, xattn_w_kv, xattn_w_o, ffn2_norm, ffn2_w_gate, ffn2_w_up, ffn2_w_down, final_norm):
    b, s, d = x.shape
    depth = ffn1_norm.shape[0]
    mem_len = mem.shape[1]
    t = b * s
    bf = lambda a: a.astype(BF16)
    row = lambda a: a.reshape(1, -1)

    cq, sq, ck, sk = _rope_tables(s)
    r = jnp.arange(TQ)[:, None]
    c = jnp.arange(TK)[None, :]
    mla_mask = jnp.where(c // CHUNK <= r // CHUNK, 0.0, NEG_INF).astype(F32)
    bias = _bias_tiles(rel_bias.astype(F32))

    h = x.reshape(t, d)
    for l in range(depth):
        h = _ffn(h, row(ffn1_norm[l]), bf(ffn1_w_gate[l]), bf(ffn1_w_up[l]), bf(ffn1_w_down[l]),
                 row(final_norm), final_norm=False)

        w = w_in[l]
        o_kv = MLA_Q_LORA
        o_kr = o_kv + MLA_KV_LORA
        o_dq = o_kr + MLA_ROPE
        w_kr = w[:, o_kr:o_dq]
        pad_l = jnp.zeros((d, MLA_NOPE), F32)
        pad_r = jnp.zeros((d, HEAD_SLOT - MLA_NOPE - MLA_ROPE), F32)
        wlat = jnp.concatenate([w[:, :o_kr], pad_l, w_kr, pad_r, pad_l, _swap_halves(w_kr), pad_r], axis=1)
        wbig = w[:, o_dq:o_dq + 3 * d]
        wgate = w[:, o_dq + 3 * d:]

        qk_w = MLA_NOPE + MLA_ROPE
        wq3 = mla_w_q_up[l].reshape(MLA_Q_LORA, MLA_HEADS, qk_w)
        wq_sw = jnp.concatenate([jnp.zeros((MLA_Q_LORA, MLA_HEADS, MLA_NOPE), F32),
                                 _swap_halves(wq3[..., MLA_NOPE:])], axis=-1)
        wq = jnp.concatenate([_pad_heads(mla_w_q_up[l], MLA_HEADS, qk_w),
                              _pad_heads(wq_sw.reshape(MLA_Q_LORA, -1), MLA_HEADS, qk_w)], axis=1)
        wkv3 = mla_w_kv_up[l].reshape(MLA_KV_LORA, MLA_HEADS, MLA_NOPE + MLA_V)
        wkv = jnp.concatenate([_pad_heads(wkv3[..., :MLA_NOPE].reshape(MLA_KV_LORA, -1), MLA_HEADS, MLA_NOPE),
                               _pad_heads(wkv3[..., MLA_NOPE:].reshape(MLA_KV_LORA, -1), MLA_HEADS, MLA_V)], axis=1)

        q, k, v, dq, dk, dv, ga, gb = _proj(
            h, row(mix_norm[l]), bf(wlat), bf(wbig), bf(wgate), row(mla_q_norm[l]), bf(wq),
            row(mla_kv_norm[l]), bf(wkv), (cq, sq, ck, sk), s)

        sh = lambda a: a.reshape(b, s, d)
        oa = _mla_attention(sh(q), sh(k), sh(v), mla_mask)
        lam_init = 0.8 - 0.6 * math.exp(-0.3 * l)
        ob = _diff_attention(sh(dq), sh(dk), sh(dv), bias,
                             row(diff_lambda_q1[l]), row(diff_lambda_k1[l]),
                             row(diff_lambda_q2[l]), row(diff_lambda_k2[l]),
                             row(diff_sub_norm[l]), lam_init)

        wa = w_branch_a[l].reshape(MLA_HEADS, MLA_V, d)
        wa = jnp.pad(wa, ((0, 0), (0, HEAD_SLOT - MLA_V), (0, 0))).reshape(MLA_HEADS * HEAD_SLOT, d)
        h = _merge(h, oa.reshape(t, d), ob.reshape(t, d), ga, gb, bf(wa), bf(w_branch_b[l]), bf(w_out[l]))

        kvmem = _memkv(mem.reshape(b * mem_len, d), row(mem_norm[l]), bf(xattn_w_kv[l]))
        h = _xattn(h, row(xattn_norm[l]), bf(xattn_w_q[l]), kvmem, bf(xattn_w_o[l]), s, mem_len)

        last = l == depth - 1
        h = _ffn(h, row(ffn2_norm[l]), bf(ffn2_w_gate[l]), bf(ffn2_w_up[l]), bf(ffn2_w_down[l]),
                 row(final_norm), final_norm=last)
    return h.reshape(b, s, d)
```

```python
import functools
import math

import jax
import jax.numpy as jnp
from jax import lax
from jax.experimental import pallas as pl
from jax.experimental.pallas import tpu as pltpu

F32 = jnp.float32
BF16 = jnp.bfloat16

CHUNK = 64
EPS = 1e-6
NEG_INF = -1e30
MLA_HEADS = 8
MLA_Q_LORA = 384
MLA_KV_LORA = 256
MLA_NOPE = 64
MLA_ROPE = 32
MLA_V = 64
ROPE_BASE = 10000.0
DIFF_HEADS = 8
DIFF_DIM = 64
DIFF_V = 2 * DIFF_DIM
N_BUCKETS = 32
XATTN_HEADS = 4
XATTN_DIM = 128
LOG2E = math.log2(math.e)

LANES = 128
HEAD_SLOT = LANES
BF16_ROWS = 16
VMEM_LIMIT = 56 * 1024 * 1024

TM_FFN = 256
TM_PROJ = 256
TQ = 512
TK = 512

MLA_VROWS = (MLA_V + 1 + BF16_ROWS - 1) // BF16_ROWS * BF16_ROWS
DIFF_VROWS = (DIFF_V + 1 + BF16_ROWS - 1) // BF16_ROWS * BF16_ROWS

NT_DIMS = (((1,), (1,)), ((), ()))
TN_DIMS = (((0,), (0,)), ((), ()))


def _rms(x, g):
    return x * lax.rsqrt(jnp.mean(x * x, axis=-1, keepdims=True) + EPS) * g


def _const_spec(shape):
    nd = len(shape)
    return pl.BlockSpec(shape, lambda *_: (0,) * nd)


def _params(n_grid):
    return pltpu.CompilerParams(
        dimension_semantics=("arbitrary",) * n_grid, vmem_limit_bytes=VMEM_LIMIT)


def _ffn_kernel(x_ref, g_ref, wg_ref, wu_ref, wd_ref, fin_ref, o_ref, *, splits, final_norm):
    x = x_ref[...]
    nb = _rms(x, g_ref[...]).astype(BF16)
    y = None
    for lo, hi in splits:
        g = jnp.dot(nb, wg_ref[:, lo:hi], preferred_element_type=F32)
        u = jnp.dot(nb, wu_ref[:, lo:hi], preferred_element_type=F32)
        h = (g * jax.nn.sigmoid(g) * u).astype(BF16)
        part = jnp.dot(h, wd_ref[lo:hi, :], preferred_element_type=F32)
        y = part if y is None else y + part
    out = x + 0.5 * y
    if final_norm:
        out = _rms(out, fin_ref[...])
    o_ref[...] = out


def _ffn(x, norm_g, wg, wu, wd, fin_g, final_norm):
    t, d = x.shape
    dff = wg.shape[1]
    half = (dff // 2 + 255) // 256 * 256
    splits = ((0, half), (half, dff))
    return pl.pallas_call(
        functools.partial(_ffn_kernel, splits=splits, final_norm=final_norm),
        grid=(t // TM_FFN,),
        in_specs=[
            pl.BlockSpec((TM_FFN, d), lambda i: (i, 0)),
            _const_spec((1, d)),
            _const_spec((d, dff)),
            _const_spec((d, dff)),
            _const_spec((dff, d)),
            _const_spec((1, d)),
        ],
        out_specs=pl.BlockSpec((TM_FFN, d), lambda i: (i, 0)),
        out_shape=jax.ShapeDtypeStruct((t, d), F32),
        compiler_params=_params(1),
        name="ffn",
    )(x, norm_g, wg, wu, wd, fin_g)


def _ones_rows(n_rows, rows_per_head, one_row):
    r = lax.broadcasted_iota(jnp.int32, (n_rows, 1), 0) % rows_per_head
    return jnp.where(r == one_row, 1.0, 0.0)


def _proj_kernel(h_ref, g_ref, wlat_ref, wdk_ref, wgate_ref, wdqt_ref, wdvt_ref,
                 qn_ref, wqt_ref, kvn_ref, wk_ref, wvt_ref,
                 cqt_ref, sqt_ref, ck_ref, sk_ref,
                 qt_ref, k_ref, vt_ref, dqt_ref, dk_ref, dvt_ref, ga_ref, gb_ref, *, diff_scale):
    d = h_ref.shape[1]
    u = _rms(h_ref[...], g_ref[...]).astype(BF16)

    lat = jnp.dot(u, wlat_ref[...], preferred_element_type=F32)
    q_lat = lat[:, :MLA_Q_LORA]
    kv_lat = lat[:, MLA_Q_LORA:MLA_Q_LORA + MLA_KV_LORA]
    o = MLA_Q_LORA + MLA_KV_LORA
    kr_a = lat[:, o:o + LANES]
    kr_b = lat[:, o + LANES:o + 2 * LANES]

    qn = _rms(q_lat, qn_ref[...]).astype(BF16)
    kvn = _rms(kv_lat, kvn_ref[...]).astype(BF16)

    nh = MLA_HEADS * HEAD_SLOT
    qqt = lax.dot_general(wqt_ref[...], qn, NT_DIMS, preferred_element_type=F32)
    cqt, sqt = cqt_ref[...], sqt_ref[...]
    for hd in range(MLA_HEADS):
        rows = slice(hd * HEAD_SLOT, (hd + 1) * HEAD_SLOT)
        rows_sw = slice(nh + hd * HEAD_SLOT, nh + (hd + 1) * HEAD_SLOT)
        qt_ref[0, rows, :] = (qqt[rows] * cqt + qqt[rows_sw] * sqt).astype(BF16)

    kk = jnp.dot(kvn, wk_ref[...], preferred_element_type=F32)
    kr = kr_a * ck_ref[...] + kr_b * sk_ref[...]
    for hd in range(MLA_HEADS):
        sl = slice(hd * HEAD_SLOT, (hd + 1) * HEAD_SLOT)
        k_ref[:, sl] = (kk[:, sl] + kr).astype(BF16)

    vt = lax.dot_general(wvt_ref[...], kvn, NT_DIMS, preferred_element_type=F32)
    vt_ref[0] = (vt + _ones_rows(vt.shape[0], MLA_VROWS, MLA_V)).astype(BF16)

    dqt = lax.dot_general(wdqt_ref[...], u, NT_DIMS, preferred_element_type=F32)
    dqt_ref[0] = (dqt * diff_scale).astype(BF16)
    dk_ref[...] = jnp.dot(u, wdk_ref[...], preferred_element_type=F32).astype(BF16)
    dvt = lax.dot_general(wdvt_ref[...], u, NT_DIMS, preferred_element_type=F32)
    dvt_ref[0] = (dvt + _ones_rows(dvt.shape[0], DIFF_VROWS, DIFF_V)).astype(BF16)

    gates = jax.nn.sigmoid(jnp.dot(u, wgate_ref[...], preferred_element_type=F32))
    ga_ref[...] = gates[:, :d].astype(BF16)
    gb_ref[...] = gates[:, d:].astype(BF16)


def _proj(h, g, wlat, wdk, wgate, wdqt, wdvt, qn, wqt, kvn, wk, wvt, tabs_t, tabs, batch, seq):
    t, d = h.shape
    nseq = seq // TM_PROJ
    tok = pl.BlockSpec((TM_PROJ, d), lambda i: (i, 0))
    tok_t = lambda rows: pl.BlockSpec((1, rows, TM_PROJ), lambda i: (i // nseq, 0, i % nseq))
    consts = [g, wlat, wdk, wgate, wdqt, wdvt, qn, wqt, kvn, wk, wvt]
    out_t = lambda rows: jax.ShapeDtypeStruct((batch, rows, seq), BF16)
    out_n = jax.ShapeDtypeStruct((t, d), BF16)
    return pl.pallas_call(
        functools.partial(_proj_kernel, diff_scale=DIFF_DIM ** -0.5 * LOG2E),
        grid=(t // TM_PROJ,),
        in_specs=[tok] + [_const_spec(c.shape) for c in consts]
        + [pl.BlockSpec((HEAD_SLOT, TM_PROJ), lambda i: (0, i % nseq))] * 2
        + [pl.BlockSpec((TM_PROJ, LANES), lambda i: (i % nseq, 0))] * 2,
        out_specs=[tok_t(wqt.shape[0] // 2), tok, tok_t(wvt.shape[0]),
                   tok_t(wdqt.shape[0]), tok, tok_t(wdvt.shape[0]), tok, tok],
        out_shape=[out_t(wqt.shape[0] // 2), out_n, out_t(wvt.shape[0]),
                   out_t(wdqt.shape[0]), out_n, out_t(wdvt.shape[0]), out_n, out_n],
        compiler_params=_params(1),
        name="in_proj",
    )(h, *consts, *tabs_t, *tabs)


def _t5_bucket(rel):
    half = N_BUCKETS // 2
    max_exact = half // 2
    n = jnp.abs(rel)
    n2 = n * n
    large = max_exact
    for k in range(1, half - max_exact):
        large = large + (n2 >= (max_exact * max_exact) << k).astype(jnp.int32)
    return jnp.where(rel > 0, half, 0) + jnp.where(n < max_exact, n, large)


def _bias_kernel(tab_ref, o_ref):
    hd = pl.program_id(0)
    kpos = lax.broadcasted_iota(jnp.int32, (TK, TQ), 0)
    qpos = lax.broadcasted_iota(jnp.int32, (TK, TQ), 1)
    for delta in range(2):
        bucket = _t5_bucket(kpos - qpos - delta * TK)
        val = jnp.zeros((TK, TQ), F32)
        for b in range(N_BUCKETS):
            val = jnp.where(bucket == b, tab_ref[b, hd], val)
        val = (val - tab_ref[N_BUCKETS // 2 - 1, hd]) * LOG2E
        if delta == 0:
            val = jnp.where(kpos // CHUNK <= qpos // CHUNK, val, NEG_INF)
        o_ref[0, delta] = val


def _bias_tiles(rel_bias):
    return pl.pallas_call(
        _bias_kernel,
        grid=(DIFF_HEADS,),
        in_specs=[pl.BlockSpec(memory_space=pltpu.SMEM)],
        out_specs=pl.BlockSpec((1, 2, TK, TQ), lambda h: (h, 0, 0, 0)),
        out_shape=jax.ShapeDtypeStruct((DIFF_HEADS, 2, TK, TQ), F32),
        compiler_params=_params(1),
        name="bias_tiles",
    )(rel_bias)


FAR, SUB, DIAG = 0, 1, 2


def _scores(k, qt, bias, s_ref, mc_ref):
    s = jnp.dot(k, qt, preferred_element_type=F32)
    if bias is not None:
        s = s + bias
    s_ref[...] = s
    mc_ref[...] = jnp.max(s, axis=0, keepdims=True)


def _softmax_pv(s_ref, mc_ref, vt, m_ref, acc_ref):
    m_prev = m_ref[...]
    m_new = jnp.maximum(m_prev, mc_ref[...])
    alpha = jnp.exp2(m_prev - m_new)
    p = jnp.exp2(s_ref[...] - m_new)
    pv = jnp.dot(vt, p.astype(BF16), preferred_element_type=F32)
    acc_ref[...] = alpha * acc_ref[...] + pv
    m_ref[...] = m_new


def _sweep(i, near, qk, consume):
    def run(kinds, j0, prefetched):
        for n, kind in enumerate(kinds):
            if n == 0 and not prefetched:
                qk(j0, 0, kind)
            if n + 1 < len(kinds):
                qk(j0 + n + 1, (n + 1) % 2, kinds[n + 1])
            consume(j0 + n, n % 2)

    for count in range(1, len(near) + 1):
        @pl.when(i + 1 == count)
        def _(count=count):
            run(near[-count:], 0, False)

    @pl.when(i + 1 > len(near))
    def _():
        n_far = i + 1 - len(near)
        pairs = (n_far - 1) // 2
        qk(0, 0, FAR)

        def body(t, carry):
            j = 2 * t
            qk(j + 1, 1, FAR)
            consume(j, 0)
            qk(j + 2, 0, FAR)
            consume(j + 1, 1)
            return carry

        lax.fori_loop(0, pairs, body, 0)
        j0 = 2 * pairs
        for left in (1, 2):
            @pl.when(n_far - j0 == left)
            def _(left=left):
                run((FAR,) * left + near, j0, True)


def _kv_rows(j):
    return pl.ds(pl.multiple_of(j * TK, TK), TK)


def _attn_scratch(streams, v_rows):
    return [pltpu.VMEM((streams, 2, TK, TQ), F32),
            pltpu.VMEM((streams, 2, 1, TQ), F32),
            pltpu.VMEM((streams, 1, TQ), F32),
            pltpu.VMEM((streams, v_rows, TQ), F32)]


MLA_STREAMS = 2


def _mla_kernel(qt_ref, k_ref, vt_ref, mask_ref, ot_ref, s_ref, mc_ref, m_ref, acc_ref):
    i = pl.program_id(2)
    qts = [qt_ref[0, st * HEAD_SLOT:(st + 1) * HEAD_SLOT, :] for st in range(MLA_STREAMS)]
    m_ref[...] = jnp.full(m_ref.shape, NEG_INF, F32)
    acc_ref[...] = jnp.zeros(acc_ref.shape, F32)

    def qk(j, slot, kind):
        bias = mask_ref[...] if kind == DIAG else None
        for st in range(MLA_STREAMS):
            k = k_ref[0, _kv_rows(j), st * HEAD_SLOT:(st + 1) * HEAD_SLOT]
            _scores(k, qts[st], bias, s_ref.at[st, slot], mc_ref.at[st, slot])

    def consume(j, slot):
        for st in range(MLA_STREAMS):
            vt = vt_ref[0, st * MLA_VROWS:(st + 1) * MLA_VROWS, _kv_rows(j)]
            _softmax_pv(s_ref.at[st, slot], mc_ref.at[st, slot], vt, m_ref.at[st], acc_ref.at[st])

    _sweep(i, (DIAG,), qk, consume)
    for st in range(MLA_STREAMS):
        acc = acc_ref[st]
        ot_ref[0, st * MLA_V:(st + 1) * MLA_V, :] = (acc[:MLA_V] / acc[MLA_V:MLA_V + 1]).astype(BF16)


def _mla_attention(qt, k, vt, mask):
    b, _, s = qt.shape
    groups = MLA_HEADS // MLA_STREAMS
    return pl.pallas_call(
        _mla_kernel,
        grid=(b, groups, s // TQ),
        in_specs=[pl.BlockSpec((1, MLA_STREAMS * HEAD_SLOT, TQ), lambda bi, h, i: (bi, h, i)),
                  pl.BlockSpec((1, s, MLA_STREAMS * HEAD_SLOT), lambda bi, h, i: (bi, 0, h)),
                  pl.BlockSpec((1, MLA_STREAMS * MLA_VROWS, s), lambda bi, h, i: (bi, h, 0)),
                  _const_spec(mask.shape)],
        out_specs=pl.BlockSpec((1, MLA_STREAMS * MLA_V, TQ), lambda bi, h, i: (bi, h, i)),
        out_shape=jax.ShapeDtypeStruct((b, MLA_HEADS * MLA_V, s), BF16),
        scratch_shapes=_attn_scratch(MLA_STREAMS, MLA_VROWS),
        compiler_params=_params(3),
        name="mla_attn",
    )(qt, k, vt, mask)


def _diff_kernel(qt_ref, k_ref, vt_ref, bias_ref, lq1_ref, lk1_ref, lq2_ref, lk2_ref, sub_ref,
                 ot_ref, s_ref, mc_ref, m_ref, acc_ref, *, lam_init):
    i = pl.program_id(2)
    qt = qt_ref[0]
    row = lax.broadcasted_iota(jnp.int32, qt.shape, 0)
    zero = jnp.zeros_like(qt)
    qts = [jnp.where(row < DIFF_DIM, qt, zero), jnp.where(row >= DIFF_DIM, qt, zero)]
    m_ref[...] = jnp.full(m_ref.shape, NEG_INF, F32)
    acc_ref[...] = jnp.zeros(acc_ref.shape, F32)

    def qk(j, slot, kind):
        k = k_ref[0, _kv_rows(j), :]
        bias = None if kind == FAR else bias_ref[0, 0 if kind == DIAG else 1]
        for st in range(2):
            _scores(k, qts[st], bias, s_ref.at[st, slot], mc_ref.at[st, slot])

    def consume(j, slot):
        vt = vt_ref[0, :, _kv_rows(j)]
        for st in range(2):
            _softmax_pv(s_ref.at[st, slot], mc_ref.at[st, slot], vt, m_ref.at[st], acc_ref.at[st])

    _sweep(i, (SUB, DIAG), qk, consume)

    lam = (jnp.exp(jnp.sum(lq1_ref[...] * lk1_ref[...], axis=1, keepdims=True))
           - jnp.exp(jnp.sum(lq2_ref[...] * lk2_ref[...], axis=1, keepdims=True)) + lam_init)
    outs = [acc_ref[st, :DIFF_V] / acc_ref[st, DIFF_V:DIFF_V + 1] for st in range(2)]
    o = outs[0] - lam * outs[1]
    o = o * lax.rsqrt(jnp.mean(o * o, axis=0, keepdims=True) + EPS) * sub_ref[...]
    ot_ref[0] = (o * (1.0 - lam_init)).astype(BF16)


def _diff_attention(qt, k, vt, bias, lq1, lk1, lq2, lk2, sub_col, lam_init):
    b, _, s = qt.shape
    vec = _const_spec(lq1.shape)
    return pl.pallas_call(
        functools.partial(_diff_kernel, lam_init=lam_init),
        grid=(b, DIFF_HEADS, s // TQ),
        in_specs=[pl.BlockSpec((1, HEAD_SLOT, TQ), lambda bi, h, i: (bi, h, i)),
                  pl.BlockSpec((1, s, HEAD_SLOT), lambda bi, h, i: (bi, 0, h)),
                  pl.BlockSpec((1, DIFF_VROWS, s), lambda bi, h, i: (bi, h, 0)),
                  pl.BlockSpec((1, 2, TK, TQ), lambda bi, h, i: (h, 0, 0, 0)),
                  vec, vec, vec, vec, _const_spec(sub_col.shape)],
        out_specs=pl.BlockSpec((1, DIFF_V, TQ), lambda bi, h, i: (bi, h, i)),
        out_shape=jax.ShapeDtypeStruct((b, DIFF_HEADS * DIFF_V, s), BF16),
        scratch_shapes=_attn_scratch(2, DIFF_VROWS),
        compiler_params=_params(3),
        name="diff_attn",
    )(qt, k, vt, bias, lq1, lk1, lq2, lk2, sub_col)


def _merge_kernel(h_ref, oat_ref, obt_ref, ga_ref, gb_ref, wa_ref, wb_ref, wo_ref, o_ref):
    ya = lax.dot_general(oat_ref[0], wa_ref[...], TN_DIMS, preferred_element_type=F32)
    yb = lax.dot_general(obt_ref[0], wb_ref[...], TN_DIMS, preferred_element_type=F32)
    merged = ga_ref[...].astype(F32) * ya + gb_ref[...].astype(F32) * yb
    o_ref[...] = h_ref[...] + jnp.dot(merged.astype(BF16), wo_ref[...], preferred_element_type=F32)


def _merge(h, oat, obt, ga, gb, wa, wb, wo, seq):
    t, d = h.shape
    nseq = seq // TM_PROJ
    tok = pl.BlockSpec((TM_PROJ, d), lambda i: (i, 0))
    tok_t = lambda rows: pl.BlockSpec((1, rows, TM_PROJ), lambda i: (i // nseq, 0, i % nseq))
    return pl.pallas_call(
        _merge_kernel,
        grid=(t // TM_PROJ,),
        in_specs=[tok, tok_t(oat.shape[1]), tok_t(obt.shape[1]), tok, tok,
                  _const_spec(wa.shape), _const_spec(wb.shape), _const_spec(wo.shape)],
        out_specs=tok,
        out_shape=jax.ShapeDtypeStruct((t, d), F32),
        compiler_params=_params(1),
        name="merge",
    )(h, oat, obt, ga, gb, wa, wb, wo)


def _memkv_kernel(m_ref, g_ref, w_ref, o_ref):
    mn = _rms(m_ref[...], g_ref[...]).astype(BF16)
    o_ref[...] = jnp.dot(mn, w_ref[...], preferred_element_type=F32).astype(BF16)


def _memkv(mem2d, g, w):
    rows, d = mem2d.shape
    return pl.pallas_call(
        _memkv_kernel,
        grid=(1,),
        in_specs=[_const_spec(mem2d.shape), _const_spec(g.shape), _const_spec(w.shape)],
        out_specs=_const_spec((rows, w.shape[1])),
        out_shape=jax.ShapeDtypeStruct((rows, w.shape[1]), BF16),
        compiler_params=_params(1),
        name="mem_kv",
    )(mem2d, g, w)


def _xattn_kernel(h_ref, g_ref, wq_ref, kv_ref, wo_ref, o_ref, *, scale):
    h = h_ref[...]
    u = _rms(h, g_ref[...]).astype(BF16)
    q = (jnp.dot(u, wq_ref[...], preferred_element_type=F32) * scale).astype(BF16)
    kv = kv_ref[...]
    outs = []
    for hd in range(XATTN_HEADS):
        qh = q[:, hd * XATTN_DIM:(hd + 1) * XATTN_DIM]
        kh = kv[:, 2 * hd * XATTN_DIM:(2 * hd + 1) * XATTN_DIM]
        vh = kv[:, (2 * hd + 1) * XATTN_DIM:(2 * hd + 2) * XATTN_DIM]
        s = lax.dot_general(qh, kh, NT_DIMS, preferred_element_type=F32)
        m = jnp.max(s, axis=1, keepdims=True)
        p = jnp.exp2(s - m)
        l = jnp.sum(p, axis=1, keepdims=True)
        oh = jnp.dot(p.astype(BF16), vh, preferred_element_type=F32) / l
        outs.append(oh.astype(BF16))
    o = jnp.concatenate(outs, axis=1)
    o_ref[...] = h + jnp.dot(o, wo_ref[...], preferred_element_type=F32)


def _xattn(h, g, wq, kvmem, wo, seq, mem_len):
    t, d = h.shape
    nseq = seq // TM_PROJ
    tok = pl.BlockSpec((TM_PROJ, d), lambda i: (i, 0))
    return pl.pallas_call(
        functools.partial(_xattn_kernel, scale=XATTN_DIM ** -0.5 * LOG2E),
        grid=(t // TM_PROJ,),
        in_specs=[tok, _const_spec(g.shape), _const_spec(wq.shape),
                  pl.BlockSpec((mem_len, kvmem.shape[1]), lambda i: (i // nseq, 0)),
                  _const_spec(wo.shape)],
        out_specs=tok,
        out_shape=jax.ShapeDtypeStruct((t, d), F32),
        compiler_params=_params(1),
        name="xattn",
    )(h, g, wq, kvmem, wo)


def _pad_heads(w, heads, width, slot):
    k = w.shape[0]
    w = w.reshape(k, heads, width)
    return jnp.pad(w, ((0, 0), (0, 0), (0, slot - width))).reshape(k, heads * slot)


def _rope_tables(seq):
    half = MLA_ROPE // 2
    pos = jnp.arange(seq)
    freqs = ROPE_BASE ** (-jnp.arange(half, dtype=F32) / half)
    ang = pos.astype(F32)[:, None] * freqs[None, :]
    cos, sin = jnp.cos(ang), jnp.sin(ang)
    ones = jnp.ones((seq, MLA_NOPE), F32)
    z_nope = jnp.zeros((seq, MLA_NOPE), F32)
    z_pad = jnp.zeros((seq, HEAD_SLOT - MLA_NOPE - MLA_ROPE), F32)
    c_rope = jnp.concatenate([cos, cos], axis=1)
    s_rope = jnp.concatenate([-sin, sin], axis=1)
    qscale = (MLA_NOPE + MLA_ROPE) ** -0.5 * LOG2E
    cq = jnp.concatenate([ones, c_rope, z_pad], axis=1) * qscale
    sq = jnp.concatenate([z_nope, s_rope, z_pad], axis=1) * qscale
    ck = jnp.concatenate([z_nope, c_rope, z_pad], axis=1)
    sk = jnp.concatenate([z_nope, s_rope, z_pad], axis=1)
    return (cq.T, sq.T), (ck, sk)


def _swap_halves(w):
    half = w.shape[-1] // 2
    return jnp.concatenate([w[..., half:], w[..., :half]], axis=-1)


def kernel(x, mem, ffn1_norm, ffn1_w_gate, ffn1_w_up, ffn1_w_down, mix_norm, w_in, mla_q_norm, mla_w_q_up, mla_kv_norm, mla_w_kv_up, diff_lambda_q1, diff_lambda_k1, diff_lambda_q2, diff_lambda_k2, diff_sub_norm, rel_bias, w_branch_a, w_branch_b, w_out, xattn_norm, mem_norm, xattn_w_q, xattn_w_kv, xattn_w_o, ffn2_norm, ffn2_w_gate, ffn2_w_up, ffn2_w_down, final_norm):
    b, s, d = x.shape
    depth = ffn1_norm.shape[0]
    mem_len = mem.shape[1]
    t = b * s
    bf = lambda a: a.astype(BF16)
    row = lambda a: a.reshape(1, -1)

    tabs_t, tabs = _rope_tables(s)
    kpos = jnp.arange(TK)[:, None]
    qpos = jnp.arange(TQ)[None, :]
    mla_mask = jnp.where(kpos // CHUNK <= qpos // CHUNK, 0.0, NEG_INF).astype(F32)
    bias = _bias_tiles(rel_bias.astype(F32))

    h = x.reshape(t, d)
    for l in range(depth):
        h = _ffn(h, row(ffn1_norm[l]), bf(ffn1_w_gate[l]), bf(ffn1_w_up[l]), bf(ffn1_w_down[l]),
                 row(final_norm), final_norm=False)

        w = w_in[l]
        o_kv = MLA_Q_LORA
        o_kr = o_kv + MLA_KV_LORA
        o_dq = o_kr + MLA_ROPE
        w_kr = w[:, o_kr:o_dq]
        pad_l = jnp.zeros((d, MLA_NOPE), F32)
        pad_r = jnp.zeros((d, HEAD_SLOT - MLA_NOPE - MLA_ROPE), F32)
        wlat = jnp.concatenate([w[:, :o_kr], pad_l, w_kr, pad_r, pad_l, _swap_halves(w_kr), pad_r], axis=1)
        wdq = w[:, o_dq:o_dq + d]
        wdk = w[:, o_dq + d:o_dq + 2 * d]
        wdv = _pad_heads(w[:, o_dq + 2 * d:o_dq + 3 * d], DIFF_HEADS, DIFF_V, DIFF_VROWS)
        wgate = w[:, o_dq + 3 * d:]

        qk_w = MLA_NOPE + MLA_ROPE
        wq3 = mla_w_q_up[l].reshape(MLA_Q_LORA, MLA_HEADS, qk_w)
        wq_sw = jnp.concatenate([jnp.zeros((MLA_Q_LORA, MLA_HEADS, MLA_NOPE), F32),
                                 _swap_halves(wq3[..., MLA_NOPE:])], axis=-1)
        wq = jnp.concatenate([_pad_heads(mla_w_q_up[l], MLA_HEADS, qk_w, HEAD_SLOT),
                              _pad_heads(wq_sw.reshape(MLA_Q_LORA, -1), MLA_HEADS, qk_w, HEAD_SLOT)], axis=1)
        wkv3 = mla_w_kv_up[l].reshape(MLA_KV_LORA, MLA_HEADS, MLA_NOPE + MLA_V)
        wk = _pad_heads(wkv3[..., :MLA_NOPE].reshape(MLA_KV_LORA, -1), MLA_HEADS, MLA_NOPE, HEAD_SLOT)
        wv = _pad_heads(wkv3[..., MLA_NOPE:].reshape(MLA_KV_LORA, -1), MLA_HEADS, MLA_V, MLA_VROWS)

        qt, k, vt, dqt, dk, dvt, ga, gb = _proj(
            h, row(mix_norm[l]), bf(wlat), bf(wdk), bf(wgate), bf(wdq.T), bf(wdv.T),
            row(mla_q_norm[l]), bf(wq.T), row(mla_kv_norm[l]), bf(wk), bf(wv.T),
            tabs_t, tabs, b, s)

        oat = _mla_attention(qt, k.reshape(b, s, d), vt, mla_mask)
        lam_init = 0.8 - 0.6 * math.exp(-0.3 * l)
        obt = _diff_attention(dqt, dk.reshape(b, s, d), dvt, bias,
                              row(diff_lambda_q1[l]), row(diff_lambda_k1[l]),
                              row(diff_lambda_q2[l]), row(diff_lambda_k2[l]),
                              diff_sub_norm[l].reshape(-1, 1), lam_init)

        h = _merge(h, oat, obt, ga, gb, bf(w_branch_a[l]), bf(w_branch_b[l]), bf(w_out[l]), s)

        kvmem = _memkv(mem.reshape(b * mem_len, d), row(mem_norm[l]), bf(xattn_w_kv[l]))
        h = _xattn(h, row(xattn_norm[l]), bf(xattn_w_q[l]), kvmem, bf(xattn_w_o[l]), s, mem_len)

        last = l == depth - 1
        h = _ffn(h, row(ffn2_norm[l]), bf(ffn2_w_gate[l]), bf(ffn2_w_up[l]), bf(ffn2_w_down[l]),
                 row(final_norm), final_norm=last)
    return h.reshape(b, s, d)
```

```python
import functools
import math

import jax
import jax.numpy as jnp
from jax import lax
from jax.experimental import pallas as pl
from jax.experimental.pallas import tpu as pltpu

F32 = jnp.float32
BF16 = jnp.bfloat16

CHUNK = 64
EPS = 1e-6
NEG_INF = -1e30
MLA_HEADS = 8
MLA_Q_LORA = 384
MLA_KV_LORA = 256
MLA_NOPE = 64
MLA_ROPE = 32
MLA_V = 64
ROPE_BASE = 10000.0
DIFF_HEADS = 8
DIFF_DIM = 64
DIFF_V = 2 * DIFF_DIM
N_BUCKETS = 32
XATTN_HEADS = 4
XATTN_DIM = 128
LOG2E = math.log2(math.e)

LANES = 128
HEAD_SLOT = LANES
BF16_ROWS = 16
VMEM_LIMIT = 56 * 1024 * 1024

TM_FFN = 256
TM_PROJ = 256
TQ = 512
TK = 512

MLA_VROWS = (MLA_V + 1 + BF16_ROWS - 1) // BF16_ROWS * BF16_ROWS
DIFF_VROWS = (DIFF_V + 1 + BF16_ROWS - 1) // BF16_ROWS * BF16_ROWS

NT_DIMS = (((1,), (1,)), ((), ()))
TN_DIMS = (((0,), (0,)), ((), ()))


def _rms(x, g):
    return x * lax.rsqrt(jnp.mean(x * x, axis=-1, keepdims=True) + EPS) * g


def _const_spec(shape):
    nd = len(shape)
    return pl.BlockSpec(shape, lambda *_: (0,) * nd)


def _params(n_grid):
    return pltpu.CompilerParams(
        dimension_semantics=("arbitrary",) * n_grid, vmem_limit_bytes=VMEM_LIMIT)


def _ffn_kernel(x_ref, g_ref, wg_ref, wu_ref, wd_ref, fin_ref, o_ref, *, splits, final_norm):
    x = x_ref[...]
    nb = _rms(x, g_ref[...]).astype(BF16)
    y = None
    for lo, hi in splits:
        g = jnp.dot(nb, wg_ref[:, lo:hi], preferred_element_type=F32)
        u = jnp.dot(nb, wu_ref[:, lo:hi], preferred_element_type=F32)
        h = (g * jax.nn.sigmoid(g) * u).astype(BF16)
        part = jnp.dot(h, wd_ref[lo:hi, :], preferred_element_type=F32)
        y = part if y is None else y + part
    out = x + 0.5 * y
    if final_norm:
        out = _rms(out, fin_ref[...])
    o_ref[...] = out


def _ffn(x, norm_g, wg, wu, wd, fin_g, final_norm):
    t, d = x.shape
    dff = wg.shape[1]
    half = (dff // 2 + 255) // 256 * 256
    splits = ((0, half), (half, dff))
    return pl.pallas_call(
        functools.partial(_ffn_kernel, splits=splits, final_norm=final_norm),
        grid=(t // TM_FFN,),
        in_specs=[
            pl.BlockSpec((TM_FFN, d), lambda i: (i, 0)),
            _const_spec((1, d)),
            _const_spec((d, dff)),
            _const_spec((d, dff)),
            _const_spec((dff, d)),
            _const_spec((1, d)),
        ],
        out_specs=pl.BlockSpec((TM_FFN, d), lambda i: (i, 0)),
        out_shape=jax.ShapeDtypeStruct((t, d), F32),
        compiler_params=_params(1),
        name="ffn",
    )(x, norm_g, wg, wu, wd, fin_g)


def _ones_rows(n_rows, rows_per_head, one_row):
    r = lax.broadcasted_iota(jnp.int32, (n_rows, 1), 0) % rows_per_head
    return jnp.where(r == one_row, 1.0, 0.0)


def _proj_kernel(h_ref, g_ref, wlat_ref, wdk_ref, wgate_ref, wdqt_ref, wdvt_ref,
                 qn_ref, wqt_ref, kvn_ref, wk_ref, wvt_ref,
                 cqt_ref, sqt_ref, ck_ref, sk_ref,
                 qt_ref, k_ref, vt_ref, dqt_ref, dk_ref, dvt_ref, ga_ref, gb_ref, *, diff_scale):
    d = h_ref.shape[1]
    u = _rms(h_ref[...], g_ref[...]).astype(BF16)

    lat = jnp.dot(u, wlat_ref[...], preferred_element_type=F32)
    q_lat = lat[:, :MLA_Q_LORA]
    kv_lat = lat[:, MLA_Q_LORA:MLA_Q_LORA + MLA_KV_LORA]
    o = MLA_Q_LORA + MLA_KV_LORA
    kr_a = lat[:, o:o + LANES]
    kr_b = lat[:, o + LANES:o + 2 * LANES]

    qn = _rms(q_lat, qn_ref[...]).astype(BF16)
    kvn = _rms(kv_lat, kvn_ref[...]).astype(BF16)

    nh = MLA_HEADS * HEAD_SLOT
    qqt = lax.dot_general(wqt_ref[...], qn, NT_DIMS, preferred_element_type=F32)
    cqt, sqt = cqt_ref[...], sqt_ref[...]
    for hd in range(MLA_HEADS):
        rows = slice(hd * HEAD_SLOT, (hd + 1) * HEAD_SLOT)
        rows_sw = slice(nh + hd * HEAD_SLOT, nh + (hd + 1) * HEAD_SLOT)
        qt_ref[0, rows, :] = (qqt[rows] * cqt + qqt[rows_sw] * sqt).astype(BF16)

    kk = jnp.dot(kvn, wk_ref[...], preferred_element_type=F32)
    kr = kr_a * ck_ref[...] + kr_b * sk_ref[...]
    for hd in range(MLA_HEADS):
        sl = slice(hd * HEAD_SLOT, (hd + 1) * HEAD_SLOT)
        k_ref[:, sl] = (kk[:, sl] + kr).astype(BF16)

    vt = lax.dot_general(wvt_ref[...], kvn, NT_DIMS, preferred_element_type=F32)
    vt_ref[0] = (vt + _ones_rows(vt.shape[0], MLA_VROWS, MLA_V)).astype(BF16)

    dqt = lax.dot_general(wdqt_ref[...], u, NT_DIMS, preferred_element_type=F32)
    dqt_ref[0] = (dqt * diff_scale).astype(BF16)
    dk_ref[...] = jnp.dot(u, wdk_ref[...], preferred_element_type=F32).astype(BF16)
    dvt = lax.dot_general(wdvt_ref[...], u, NT_DIMS, preferred_element_type=F32)
    dvt_ref[0] = (dvt + _ones_rows(dvt.shape[0], DIFF_VROWS, DIFF_V)).astype(BF16)

    gates = jax.nn.sigmoid(jnp.dot(u, wgate_ref[...], preferred_element_type=F32))
    ga_ref[...] = gates[:, :d].astype(BF16)
    gb_ref[...] = gates[:, d:].astype(BF16)


def _proj(h, g, wlat, wdk, wgate, wdqt, wdvt, qn, wqt, kvn, wk, wvt, tabs_t, tabs, batch, seq):
    t, d = h.shape
    nseq = seq // TM_PROJ
    tok = pl.BlockSpec((TM_PROJ, d), lambda i: (i, 0))
    tok_t = lambda rows: pl.BlockSpec((1, rows, TM_PROJ), lambda i: (i // nseq, 0, i % nseq))
    consts = [g, wlat, wdk, wgate, wdqt, wdvt, qn, wqt, kvn, wk, wvt]
    out_t = lambda rows: jax.ShapeDtypeStruct((batch, rows, seq), BF16)
    out_n = jax.ShapeDtypeStruct((t, d), BF16)
    return pl.pallas_call(
        functools.partial(_proj_kernel, diff_scale=DIFF_DIM ** -0.5 * LOG2E),
        grid=(t // TM_PROJ,),
        in_specs=[tok] + [_const_spec(c.shape) for c in consts]
        + [pl.BlockSpec((HEAD_SLOT, TM_PROJ), lambda i: (0, i % nseq))] * 2
        + [pl.BlockSpec((TM_PROJ, LANES), lambda i: (i % nseq, 0))] * 2,
        out_specs=[tok_t(wqt.shape[0] // 2), tok, tok_t(wvt.shape[0]),
                   tok_t(wdqt.shape[0]), tok, tok_t(wdvt.shape[0]), tok, tok],
        out_shape=[out_t(wqt.shape[0] // 2), out_n, out_t(wvt.shape[0]),
                   out_t(wdqt.shape[0]), out_n, out_t(wdvt.shape[0]), out_n, out_n],
        compiler_params=_params(1),
        name="in_proj",
    )(h, *consts, *tabs_t, *tabs)


def _t5_bucket(rel):
    half = N_BUCKETS // 2
    max_exact = half // 2
    n = jnp.abs(rel)
    n2 = n * n
    large = max_exact
    for k in range(1, half - max_exact):
        large = large + (n2 >= (max_exact * max_exact) << k).astype(jnp.int32)
    return jnp.where(rel > 0, half, 0) + jnp.where(n < max_exact, n, large)


def _bias_kernel(tab_ref, o_ref):
    hd = pl.program_id(0)
    kpos = lax.broadcasted_iota(jnp.int32, (TK, TQ), 0)
    qpos = lax.broadcasted_iota(jnp.int32, (TK, TQ), 1)
    for delta in range(2):
        bucket = _t5_bucket(kpos - qpos - delta * TK)
        val = jnp.zeros((TK, TQ), F32)
        for b in range(N_BUCKETS):
            val = jnp.where(bucket == b, tab_ref[b, hd], val)
        val = (val - tab_ref[N_BUCKETS // 2 - 1, hd]) * LOG2E
        if delta == 0:
            val = jnp.where(kpos // CHUNK <= qpos // CHUNK, val, NEG_INF)
        o_ref[0, delta] = val


def _bias_tiles(rel_bias):
    return pl.pallas_call(
        _bias_kernel,
        grid=(DIFF_HEADS,),
        in_specs=[pl.BlockSpec(memory_space=pltpu.SMEM)],
        out_specs=pl.BlockSpec((1, 2, TK, TQ), lambda h: (h, 0, 0, 0)),
        out_shape=jax.ShapeDtypeStruct((DIFF_HEADS, 2, TK, TQ), F32),
        compiler_params=_params(1),
        name="bias_tiles",
    )(rel_bias)


FAR, SUB, DIAG = 0, 1, 2


def _scores(k, qt, bias, s_ref, mc_ref):
    s = jnp.dot(k, qt, preferred_element_type=F32)
    if bias is not None:
        s = s + bias
    s_ref[...] = s
    mc_ref[...] = jnp.max(s, axis=0, keepdims=True)


def _softmax_pv(s_ref, mc_ref, vt, m_ref, acc_ref):
    m_prev = m_ref[...]
    m_new = jnp.maximum(m_prev, mc_ref[...])
    alpha = jnp.exp2(m_prev - m_new)
    p = jnp.exp2(s_ref[...] - m_new)
    pv = jnp.dot(vt, p.astype(BF16), preferred_element_type=F32)
    acc_ref[...] = alpha * acc_ref[...] + pv
    m_ref[...] = m_new


def _sweep(i, near, qk, consume):
    def run(kinds, j0, prefetched):
        for n, kind in enumerate(kinds):
            if n == 0 and not prefetched:
                qk(j0, 0, kind)
            if n + 1 < len(kinds):
                qk(j0 + n + 1, (n + 1) % 2, kinds[n + 1])
            consume(j0 + n, n % 2)

    for count in range(1, len(near) + 1):
        @pl.when(i + 1 == count)
        def _(count=count):
            run(near[-count:], 0, False)

    @pl.when(i + 1 > len(near))
    def _():
        n_far = i + 1 - len(near)
        pairs = (n_far - 1) // 2
        qk(0, 0, FAR)

        def body(t, carry):
            j = 2 * t
            qk(j + 1, 1, FAR)
            consume(j, 0)
            qk(j + 2, 0, FAR)
            consume(j + 1, 1)
            return carry

        lax.fori_loop(0, pairs, body, 0)
        j0 = 2 * pairs
        for left in (1, 2):
            @pl.when(n_far - j0 == left)
            def _(left=left):
                run((FAR,) * left + near, j0, True)


def _kv_rows(j):
    return pl.ds(pl.multiple_of(j * TK, TK), TK)


def _attn_scratch(streams, v_rows):
    return [pltpu.VMEM((streams, 2, TK, TQ), F32),
            pltpu.VMEM((streams, 2, 1, TQ), F32),
            pltpu.VMEM((streams, 1, TQ), F32),
            pltpu.VMEM((streams, v_rows, TQ), F32)]


MLA_STREAMS = 2


def _q_cols(i):
    return pl.ds(pl.multiple_of(i * TQ, TQ), TQ)


def _mla_kernel(qt_ref, k_ref, vt_ref, mask_ref, ot_ref, s_ref, mc_ref, m_ref, acc_ref):
    def qk(j, slot, kind, qts):
        bias = mask_ref[...] if kind == DIAG else None
        for st in range(MLA_STREAMS):
            k = k_ref[0, _kv_rows(j), st * HEAD_SLOT:(st + 1) * HEAD_SLOT]
            _scores(k, qts[st], bias, s_ref.at[st, slot], mc_ref.at[st, slot])

    def consume(j, slot):
        for st in range(MLA_STREAMS):
            vt = vt_ref[0, st * MLA_VROWS:(st + 1) * MLA_VROWS, _kv_rows(j)]
            _softmax_pv(s_ref.at[st, slot], mc_ref.at[st, slot], vt, m_ref.at[st], acc_ref.at[st])

    def q_tile(i, carry):
        qts = [qt_ref[0, st * HEAD_SLOT:(st + 1) * HEAD_SLOT, _q_cols(i)] for st in range(MLA_STREAMS)]
        m_ref[...] = jnp.full(m_ref.shape, NEG_INF, F32)
        acc_ref[...] = jnp.zeros(acc_ref.shape, F32)
        _sweep(i, (DIAG,), functools.partial(qk, qts=qts), consume)
        for st in range(MLA_STREAMS):
            acc = acc_ref[st]
            ot_ref[0, st * MLA_V:(st + 1) * MLA_V, _q_cols(i)] = (
                acc[:MLA_V] / acc[MLA_V:MLA_V + 1]).astype(BF16)
        return carry

    lax.fori_loop(0, qt_ref.shape[2] // TQ, q_tile, 0)


def _mla_attention(qt, k, vt, mask):
    b, _, s = qt.shape
    groups = MLA_HEADS // MLA_STREAMS
    return pl.pallas_call(
        _mla_kernel,
        grid=(b, groups),
        in_specs=[pl.BlockSpec((1, MLA_STREAMS * HEAD_SLOT, s), lambda bi, h: (bi, h, 0)),
                  pl.BlockSpec((1, s, MLA_STREAMS * HEAD_SLOT), lambda bi, h: (bi, 0, h)),
                  pl.BlockSpec((1, MLA_STREAMS * MLA_VROWS, s), lambda bi, h: (bi, h, 0)),
                  _const_spec(mask.shape)],
        out_specs=pl.BlockSpec((1, MLA_STREAMS * MLA_V, s), lambda bi, h: (bi, h, 0)),
        out_shape=jax.ShapeDtypeStruct((b, MLA_HEADS * MLA_V, s), BF16),
        scratch_shapes=_attn_scratch(MLA_STREAMS, MLA_VROWS),
        compiler_params=_params(2),
        name="mla_attn",
    )(qt, k, vt, mask)


def _diff_kernel(qt_ref, k_ref, vt_ref, bias_ref, lq1_ref, lk1_ref, lq2_ref, lk2_ref, sub_ref,
                 ot_ref, s_ref, mc_ref, m_ref, acc_ref, *, lam_init):
    row = lax.broadcasted_iota(jnp.int32, (HEAD_SLOT, TQ), 0)
    zero = jnp.zeros((HEAD_SLOT, TQ), BF16)
    lam = (jnp.exp(jnp.sum(lq1_ref[...] * lk1_ref[...], axis=1, keepdims=True))
           - jnp.exp(jnp.sum(lq2_ref[...] * lk2_ref[...], axis=1, keepdims=True)) + lam_init)

    def qk(j, slot, kind, qts):
        k = k_ref[0, _kv_rows(j), :]
        bias = None if kind == FAR else bias_ref[0, 0 if kind == DIAG else 1]
        for st in range(2):
            _scores(k, qts[st], bias, s_ref.at[st, slot], mc_ref.at[st, slot])

    def consume(j, slot):
        vt = vt_ref[0, :, _kv_rows(j)]
        for st in range(2):
            _softmax_pv(s_ref.at[st, slot], mc_ref.at[st, slot], vt, m_ref.at[st], acc_ref.at[st])

    def q_tile(i, carry):
        qt = qt_ref[0, :, _q_cols(i)]
        qts = [jnp.where(row < DIFF_DIM, qt, zero), jnp.where(row >= DIFF_DIM, qt, zero)]
        m_ref[...] = jnp.full(m_ref.shape, NEG_INF, F32)
        acc_ref[...] = jnp.zeros(acc_ref.shape, F32)
        _sweep(i, (SUB, DIAG), functools.partial(qk, qts=qts), consume)
        outs = [acc_ref[st, :DIFF_V] / acc_ref[st, DIFF_V:DIFF_V + 1] for st in range(2)]
        o = outs[0] - lam * outs[1]
        o = o * lax.rsqrt(jnp.mean(o * o, axis=0, keepdims=True) + EPS) * sub_ref[...]
        ot_ref[0, :, _q_cols(i)] = (o * (1.0 - lam_init)).astype(BF16)
        return carry

    lax.fori_loop(0, qt_ref.shape[2] // TQ, q_tile, 0)


def _diff_attention(qt, k, vt, bias, lq1, lk1, lq2, lk2, sub_col, lam_init):
    b, _, s = qt.shape
    vec = _const_spec(lq1.shape)
    return pl.pallas_call(
        functools.partial(_diff_kernel, lam_init=lam_init),
        grid=(b, DIFF_HEADS),
        in_specs=[pl.BlockSpec((1, HEAD_SLOT, s), lambda bi, h: (bi, h, 0)),
                  pl.BlockSpec((1, s, HEAD_SLOT), lambda bi, h: (bi, 0, h)),
                  pl.BlockSpec((1, DIFF_VROWS, s), lambda bi, h: (bi, h, 0)),
                  pl.BlockSpec((1, 2, TK, TQ), lambda bi, h: (h, 0, 0, 0)),
                  vec, vec, vec, vec, _const_spec(sub_col.shape)],
        out_specs=pl.BlockSpec((1, DIFF_V, s), lambda bi, h: (bi, h, 0)),
        out_shape=jax.ShapeDtypeStruct((b, DIFF_HEADS * DIFF_V, s), BF16),
        scratch_shapes=_attn_scratch(2, DIFF_VROWS),
        compiler_params=_params(2),
        name="diff_attn",
    )(qt, k, vt, bias, lq1, lk1, lq2, lk2, sub_col)


def _merge_kernel(h_ref, oat_ref, obt_ref, ga_ref, gb_ref, wa_ref, wb_ref, wo_ref, o_ref):
    ya = lax.dot_general(oat_ref[0], wa_ref[...], TN_DIMS, preferred_element_type=F32)
    yb = lax.dot_general(obt_ref[0], wb_ref[...], TN_DIMS, preferred_element_type=F32)
    merged = ga_ref[...].astype(F32) * ya + gb_ref[...].astype(F32) * yb
    o_ref[...] = h_ref[...] + jnp.dot(merged.astype(BF16), wo_ref[...], preferred_element_type=F32)


def _merge(h, oat, obt, ga, gb, wa, wb, wo, seq):
    t, d = h.shape
    nseq = seq // TM_PROJ
    tok = pl.BlockSpec((TM_PROJ, d), lambda i: (i, 0))
    tok_t = lambda rows: pl.BlockSpec((1, rows, TM_PROJ), lambda i: (i // nseq, 0, i % nseq))
    return pl.pallas_call(
        _merge_kernel,
        grid=(t // TM_PROJ,),
        in_specs=[tok, tok_t(oat.shape[1]), tok_t(obt.shape[1]), tok, tok,
                  _const_spec(wa.shape), _const_spec(wb.shape), _const_spec(wo.shape)],
        out_specs=tok,
        out_shape=jax.ShapeDtypeStruct((t, d), F32),
        compiler_params=_params(1),
        name="merge",
    )(h, oat, obt, ga, gb, wa, wb, wo)


def _memkv_kernel(m_ref, g_ref, w_ref, o_ref):
    mn = _rms(m_ref[...], g_ref[...]).astype(BF16)
    o_ref[...] = jnp.dot(mn, w_ref[...], preferred_element_type=F32).astype(BF16)


def _memkv(mem2d, g, w):
    rows, d = mem2d.shape
    return pl.pallas_call(
        _memkv_kernel,
        grid=(1,),
        in_specs=[_const_spec(mem2d.shape), _const_spec(g.shape), _const_spec(w.shape)],
        out_specs=_const_spec((rows, w.shape[1])),
        out_shape=jax.ShapeDtypeStruct((rows, w.shape[1]), BF16),
        compiler_params=_params(1),
        name="mem_kv",
    )(mem2d, g, w)


def _xattn_kernel(h_ref, g_ref, wq_ref, kv_ref, wo_ref, o_ref, *, scale):
    h = h_ref[...]
    u = _rms(h, g_ref[...]).astype(BF16)
    q = (jnp.dot(u, wq_ref[...], preferred_element_type=F32) * scale).astype(BF16)
    kv = kv_ref[...]
    outs = []
    for hd in range(XATTN_HEADS):
        qh = q[:, hd * XATTN_DIM:(hd + 1) * XATTN_DIM]
        kh = kv[:, 2 * hd * XATTN_DIM:(2 * hd + 1) * XATTN_DIM]
        vh = kv[:, (2 * hd + 1) * XATTN_DIM:(2 * hd + 2) * XATTN_DIM]
        s = lax.dot_general(qh, kh, NT_DIMS, preferred_element_type=F32)
        m = jnp.max(s, axis=1, keepdims=True)
        p = jnp.exp2(s - m)
        l = jnp.sum(p, axis=1, keepdims=True)
        oh = jnp.dot(p.astype(BF16), vh, preferred_element_type=F32) / l
        outs.append(oh.astype(BF16))
    o = jnp.concatenate(outs, axis=1)
    o_ref[...] = h + jnp.dot(o, wo_ref[...], preferred_element_type=F32)


def _xattn(h, g, wq, kvmem, wo, seq, mem_len):
    t, d = h.shape
    nseq = seq // TM_PROJ
    tok = pl.BlockSpec((TM_PROJ, d), lambda i: (i, 0))
    return pl.pallas_call(
        functools.partial(_xattn_kernel, scale=XATTN_DIM ** -0.5 * LOG2E),
        grid=(t // TM_PROJ,),
        in_specs=[tok, _const_spec(g.shape), _const_spec(wq.shape),
                  pl.BlockSpec((mem_len, kvmem.shape[1]), lambda i: (i // nseq, 0)),
                  _const_spec(wo.shape)],
        out_specs=tok,
        out_shape=jax.ShapeDtypeStruct((t, d), F32),
        compiler_params=_params(1),
        name="xattn",
    )(h, g, wq, kvmem, wo)


def _pad_heads(w, heads, width, slot):
    k = w.shape[0]
    w = w.reshape(k, heads, width)
    return jnp.pad(w, ((0, 0), (0, 0), (0, slot - width))).reshape(k, heads * slot)


def _rope_tables(seq):
    half = MLA_ROPE // 2
    pos = jnp.arange(seq)
    freqs = ROPE_BASE ** (-jnp.arange(half, dtype=F32) / half)
    ang = pos.astype(F32)[:, None] * freqs[None, :]
    cos, sin = jnp.cos(ang), jnp.sin(ang)
    ones = jnp.ones((seq, MLA_NOPE), F32)
    z_nope = jnp.zeros((seq, MLA_NOPE), F32)
    z_pad = jnp.zeros((seq, HEAD_SLOT - MLA_NOPE - MLA_ROPE), F32)
    c_rope = jnp.concatenate([cos, cos], axis=1)
    s_rope = jnp.concatenate([-sin, sin], axis=1)
    qscale = (MLA_NOPE + MLA_ROPE) ** -0.5 * LOG2E
    cq = jnp.concatenate([ones, c_rope, z_pad], axis=1) * qscale
    sq = jnp.concatenate([z_nope, s_rope, z_pad], axis=1) * qscale
    ck = jnp.concatenate([z_nope, c_rope, z_pad], axis=1)
    sk = jnp.concatenate([z_nope, s_rope, z_pad], axis=1)
    return (cq.T, sq.T), (ck, sk)


def _swap_halves(w):
    half = w.shape[-1] // 2
    return jnp.concatenate([w[..., half:], w[..., :half]], axis=-1)


def kernel(x, mem, ffn1_norm, ffn1_w_gate, ffn1_w_up, ffn1_w_down, mix_norm, w_in, mla_q_norm, mla_w_q_up, mla_kv_norm, mla_w_kv_up, diff_lambda_q1, diff_lambda_k1, diff_lambda_q2, diff_lambda_k2, diff_sub_norm, rel_bias, w_branch_a, w_branch_b, w_out, xattn_norm, mem_norm, xattn_w_q, xattn_w_kv, xattn_w_o, ffn2_norm, ffn2_w_gate, ffn2_w_up, ffn2_w_down, final_norm):
    b, s, d = x.shape
    depth = ffn1_norm.shape[0]
    mem_len = mem.shape[1]
    t = b * s
    bf = lambda a: a.astype(BF16)
    row = lambda a: a.reshape(1, -1)

    tabs_t, tabs = _rope_tables(s)
    kpos = jnp.arange(TK)[:, None]
    qpos = jnp.arange(TQ)[None, :]
    mla_mask = jnp.where(kpos // CHUNK <= qpos // CHUNK, 0.0, NEG_INF).astype(F32)
    bias = _bias_tiles(rel_bias.astype(F32))

    h = x.reshape(t, d)
    for l in range(depth):
        h = _ffn(h, row(ffn1_norm[l]), bf(ffn1_w_gate[l]), bf(ffn1_w_up[l]), bf(ffn1_w_down[l]),
                 row(final_norm), final_norm=False)

        w = w_in[l]
        o_kv = MLA_Q_LORA
        o_kr = o_kv + MLA_KV_LORA
        o_dq = o_kr + MLA_ROPE
        w_kr = w[:, o_kr:o_dq]
        pad_l = jnp.zeros((d, MLA_NOPE), F32)
        pad_r = jnp.zeros((d, HEAD_SLOT - MLA_NOPE - MLA_ROPE), F32)
        wlat = jnp.concatenate([w[:, :o_kr], pad_l, w_kr, pad_r, pad_l, _swap_halves(w_kr), pad_r], axis=1)
        wdq = w[:, o_dq:o_dq + d]
        wdk = w[:, o_dq + d:o_dq + 2 * d]
        wdv = _pad_heads(w[:, o_dq + 2 * d:o_dq + 3 * d], DIFF_HEADS, DIFF_V, DIFF_VROWS)
        wgate = w[:, o_dq + 3 * d:]

        qk_w = MLA_NOPE + MLA_ROPE
        wq3 = mla_w_q_up[l].reshape(MLA_Q_LORA, MLA_HEADS, qk_w)
        wq_sw = jnp.concatenate([jnp.zeros((MLA_Q_LORA, MLA_HEADS, MLA_NOPE), F32),
                                 _swap_halves(wq3[..., MLA_NOPE:])], axis=-1)
        wq = jnp.concatenate([_pad_heads(mla_w_q_up[l], MLA_HEADS, qk_w, HEAD_SLOT),
                              _pad_heads(wq_sw.reshape(MLA_Q_LORA, -1), MLA_HEADS, qk_w, HEAD_SLOT)], axis=1)
        wkv3 = mla_w_kv_up[l].reshape(MLA_KV_LORA, MLA_HEADS, MLA_NOPE + MLA_V)
        wk = _pad_heads(wkv3[..., :MLA_NOPE].reshape(MLA_KV_LORA, -1), MLA_HEADS, MLA_NOPE, HEAD_SLOT)
        wv = _pad_heads(wkv3[..., MLA_NOPE:].reshape(MLA_KV_LORA, -1), MLA_HEADS, MLA_V, MLA_VROWS)

        qt, k, vt, dqt, dk, dvt, ga, gb = _proj(
            h, row(mix_norm[l]), bf(wlat), bf(wdk), bf(wgate), bf(wdq.T), bf(wdv.T),
            row(mla_q_norm[l]), bf(wq.T), row(mla_kv_norm[l]), bf(wk), bf(wv.T),
            tabs_t, tabs, b, s)

        oat = _mla_attention(qt, k.reshape(b, s, d), vt, mla_mask)
        lam_init = 0.8 - 0.6 * math.exp(-0.3 * l)
        obt = _diff_attention(dqt, dk.reshape(b, s, d), dvt, bias,
                              row(diff_lambda_q1[l]), row(diff_lambda_k1[l]),
                              row(diff_lambda_q2[l]), row(diff_lambda_k2[l]),
                              diff_sub_norm[l].reshape(-1, 1), lam_init)

        h = _merge(h, oat, obt, ga, gb, bf(w_branch_a[l]), bf(w_branch_b[l]), bf(w_out[l]), s)

        kvmem = _memkv(mem.reshape(b * mem_len, d), row(mem_norm[l]), bf(xattn_w_kv[l]))
        h = _xattn(h, row(xattn_norm[l]), bf(xattn_w_q[l]), kvmem, bf(xattn_w_o[l]), s, mem_len)

        last = l == depth - 1
        h = _ffn(h, row(ffn2_norm[l]), bf(ffn2_w_gate[l]), bf(ffn2_w_up[l]), bf(ffn2_w_down[l]),
                 row(final_norm), final_norm=last)
    return h.reshape(b, s, d)
```

```python
import functools
import math

import jax
import jax.numpy as jnp
from jax import lax
from jax.experimental import pallas as pl
from jax.experimental.pallas import tpu as pltpu

F32 = jnp.float32
BF16 = jnp.bfloat16

CHUNK = 64
EPS = 1e-6
NEG_INF = -1e30
MLA_HEADS = 8
MLA_Q_LORA = 384
MLA_KV_LORA = 256
MLA_NOPE = 64
MLA_ROPE = 32
MLA_V = 64
ROPE_BASE = 10000.0
DIFF_HEADS = 8
DIFF_DIM = 64
DIFF_V = 2 * DIFF_DIM
N_BUCKETS = 32
XATTN_HEADS = 4
XATTN_DIM = 128
LOG2E = math.log2(math.e)

LANES = 128
HEAD_SLOT = LANES
BF16_ROWS = 16
VMEM_LIMIT = 56 * 1024 * 1024

TM_FFN = 256
TM_PROJ = 256
TQ = 512
TK = 512

MLA_VROWS = (MLA_V + 1 + BF16_ROWS - 1) // BF16_ROWS * BF16_ROWS
DIFF_VROWS = (DIFF_V + 1 + BF16_ROWS - 1) // BF16_ROWS * BF16_ROWS

NT_DIMS = (((1,), (1,)), ((), ()))
TN_DIMS = (((0,), (0,)), ((), ()))


def _rms(x, g):
    return x * lax.rsqrt(jnp.mean(x * x, axis=-1, keepdims=True) + EPS) * g


def _const_spec(shape):
    nd = len(shape)
    return pl.BlockSpec(shape, lambda *_: (0,) * nd)


def _params(n_grid):
    return pltpu.CompilerParams(
        dimension_semantics=("arbitrary",) * n_grid, vmem_limit_bytes=VMEM_LIMIT)


def _ffn_kernel(x_ref, g_ref, wg_ref, wu_ref, wd_ref, fin_ref, o_ref, *, splits, final_norm):
    x = x_ref[...]
    nb = _rms(x, g_ref[...]).astype(BF16)
    y = None
    for lo, hi in splits:
        g = jnp.dot(nb, wg_ref[:, lo:hi], preferred_element_type=F32)
        u = jnp.dot(nb, wu_ref[:, lo:hi], preferred_element_type=F32)
        h = (g * jax.nn.sigmoid(g) * u).astype(BF16)
        part = jnp.dot(h, wd_ref[lo:hi, :], preferred_element_type=F32)
        y = part if y is None else y + part
    out = x + 0.5 * y
    if final_norm:
        out = _rms(out, fin_ref[...])
    o_ref[...] = out


def _ffn(x, norm_g, wg, wu, wd, fin_g, final_norm):
    t, d = x.shape
    dff = wg.shape[1]
    half = (dff // 2 + 255) // 256 * 256
    splits = ((0, half), (half, dff))
    return pl.pallas_call(
        functools.partial(_ffn_kernel, splits=splits, final_norm=final_norm),
        grid=(t // TM_FFN,),
        in_specs=[
            pl.BlockSpec((TM_FFN, d), lambda i: (i, 0)),
            _const_spec((1, d)),
            _const_spec((d, dff)),
            _const_spec((d, dff)),
            _const_spec((dff, d)),
            _const_spec((1, d)),
        ],
        out_specs=pl.BlockSpec((TM_FFN, d), lambda i: (i, 0)),
        out_shape=jax.ShapeDtypeStruct((t, d), F32),
        compiler_params=_params(1),
        name="ffn",
    )(x, norm_g, wg, wu, wd, fin_g)


def _ones_rows(n_rows, rows_per_head, one_row):
    r = lax.broadcasted_iota(jnp.int32, (n_rows, 1), 0) % rows_per_head
    return jnp.where(r == one_row, 1.0, 0.0)


def _proj_kernel(h_ref, g_ref, wlat_ref, wdk_ref, wgate_ref, wdqt_ref, wdvt_ref,
                 qn_ref, wqt_ref, kvn_ref, wk_ref, wvt_ref,
                 cqt_ref, sqt_ref, ck_ref, sk_ref,
                 qt_ref, k_ref, vt_ref, dqt_ref, dk_ref, dvt_ref, ga_ref, gb_ref, *, diff_scale):
    d = h_ref.shape[1]
    u = _rms(h_ref[...], g_ref[...]).astype(BF16)

    lat = jnp.dot(u, wlat_ref[...], preferred_element_type=F32)
    q_lat = lat[:, :MLA_Q_LORA]
    kv_lat = lat[:, MLA_Q_LORA:MLA_Q_LORA + MLA_KV_LORA]
    o = MLA_Q_LORA + MLA_KV_LORA
    kr_a = lat[:, o:o + LANES]
    kr_b = lat[:, o + LANES:o + 2 * LANES]

    qn = _rms(q_lat, qn_ref[...]).astype(BF16)
    kvn = _rms(kv_lat, kvn_ref[...]).astype(BF16)

    qqt = lax.dot_general(wqt_ref[...], qn, NT_DIMS, preferred_element_type=F32)
    cqt, sqt = cqt_ref[...], sqt_ref[...]
    r0, r1, r2 = MLA_NOPE, MLA_NOPE + MLA_ROPE // 2, MLA_NOPE + MLA_ROPE
    for hd in range(MLA_HEADS):
        rows = slice(hd * HEAD_SLOT, (hd + 1) * HEAD_SLOT)
        qh = qqt[rows]
        qh_sw = jnp.concatenate([qh[:r0], qh[r1:r2], qh[r0:r1], qh[r2:]], axis=0)
        qt_ref[0, rows, :] = (qh * cqt + qh_sw * sqt).astype(BF16)

    kk = jnp.dot(kvn, wk_ref[...], preferred_element_type=F32)
    kr = kr_a * ck_ref[...] + kr_b * sk_ref[...]
    for hd in range(MLA_HEADS):
        sl = slice(hd * HEAD_SLOT, (hd + 1) * HEAD_SLOT)
        k_ref[:, sl] = (kk[:, sl] + kr).astype(BF16)

    vt = lax.dot_general(wvt_ref[...], kvn, NT_DIMS, preferred_element_type=F32)
    vt_ref[0] = (vt + _ones_rows(vt.shape[0], MLA_VROWS, MLA_V)).astype(BF16)

    dqt = lax.dot_general(wdqt_ref[...], u, NT_DIMS, preferred_element_type=F32)
    dqt_ref[0] = (dqt * diff_scale).astype(BF16)
    dk_ref[...] = jnp.dot(u, wdk_ref[...], preferred_element_type=F32).astype(BF16)
    dvt = lax.dot_general(wdvt_ref[...], u, NT_DIMS, preferred_element_type=F32)
    dvt_ref[0] = (dvt + _ones_rows(dvt.shape[0], DIFF_VROWS, DIFF_V)).astype(BF16)

    gates = jax.nn.sigmoid(jnp.dot(u, wgate_ref[...], preferred_element_type=F32))
    ga_ref[...] = gates[:, :d].astype(BF16)
    gb_ref[...] = gates[:, d:].astype(BF16)


def _proj(h, g, wlat, wdk, wgate, wdqt, wdvt, qn, wqt, kvn, wk, wvt, tabs_t, tabs, batch, seq):
    t, d = h.shape
    nseq = seq // TM_PROJ
    tok = pl.BlockSpec((TM_PROJ, d), lambda i: (i, 0))
    tok_t = lambda rows: pl.BlockSpec((1, rows, TM_PROJ), lambda i: (i // nseq, 0, i % nseq))
    consts = [g, wlat, wdk, wgate, wdqt, wdvt, qn, wqt, kvn, wk, wvt]
    out_t = lambda rows: jax.ShapeDtypeStruct((batch, rows, seq), BF16)
    out_n = jax.ShapeDtypeStruct((t, d), BF16)
    return pl.pallas_call(
        functools.partial(_proj_kernel, diff_scale=DIFF_DIM ** -0.5 * LOG2E),
        grid=(t // TM_PROJ,),
        in_specs=[tok] + [_const_spec(c.shape) for c in consts]
        + [pl.BlockSpec((HEAD_SLOT, TM_PROJ), lambda i: (0, i % nseq))] * 2
        + [pl.BlockSpec((TM_PROJ, LANES), lambda i: (i % nseq, 0))] * 2,
        out_specs=[tok_t(wqt.shape[0]), tok, tok_t(wvt.shape[0]),
                   tok_t(wdqt.shape[0]), tok, tok_t(wdvt.shape[0]), tok, tok],
        out_shape=[out_t(wqt.shape[0]), out_n, out_t(wvt.shape[0]),
                   out_t(wdqt.shape[0]), out_n, out_t(wdvt.shape[0]), out_n, out_n],
        compiler_params=_params(1),
        name="in_proj",
    )(h, *consts, *tabs_t, *tabs)


def _t5_bucket(rel):
    half = N_BUCKETS // 2
    max_exact = half // 2
    n = jnp.abs(rel)
    n2 = n * n
    large = max_exact
    for k in range(1, half - max_exact):
        large = large + (n2 >= (max_exact * max_exact) << k).astype(jnp.int32)
    return jnp.where(rel > 0, half, 0) + jnp.where(n < max_exact, n, large)


def _bias_kernel(tab_ref, o_ref):
    hd = pl.program_id(0)
    kpos = lax.broadcasted_iota(jnp.int32, (TK, TQ), 0)
    qpos = lax.broadcasted_iota(jnp.int32, (TK, TQ), 1)
    for delta in range(2):
        bucket = _t5_bucket(kpos - qpos - delta * TK)
        val = jnp.zeros((TK, TQ), F32)
        for b in range(N_BUCKETS):
            val = jnp.where(bucket == b, tab_ref[b, hd], val)
        val = (val - tab_ref[N_BUCKETS // 2 - 1, hd]) * LOG2E
        if delta == 0:
            val = jnp.where(kpos // CHUNK <= qpos // CHUNK, val, NEG_INF)
        o_ref[0, delta] = val


def _bias_tiles(rel_bias):
    return pl.pallas_call(
        _bias_kernel,
        grid=(DIFF_HEADS,),
        in_specs=[pl.BlockSpec(memory_space=pltpu.SMEM)],
        out_specs=pl.BlockSpec((1, 2, TK, TQ), lambda h: (h, 0, 0, 0)),
        out_shape=jax.ShapeDtypeStruct((DIFF_HEADS, 2, TK, TQ), F32),
        compiler_params=_params(1),
        name="bias_tiles",
    )(rel_bias)


FAR, SUB, DIAG = 0, 1, 2


def _scores(k, qt, bias, s_ref, mc_ref):
    s = jnp.dot(k, qt, preferred_element_type=F32)
    if bias is not None:
        s = s + bias
    s_ref[:, :TQ] = s
    mc_ref[...] = jnp.max(s, axis=0, keepdims=True)


def _softmax_pv(s_ref, mc_ref, vt, m_ref, acc_ref):
    m_prev = m_ref[...]
    m_new = jnp.maximum(m_prev, mc_ref[...])
    alpha = jnp.exp2(m_prev - m_new)
    p = jnp.exp2(s_ref[:, :TQ] - m_new)
    pv = jnp.dot(vt, p.astype(BF16), preferred_element_type=F32)
    acc_ref[...] = alpha * acc_ref[...] + pv
    m_ref[...] = m_new


def _sweep(i, near, qk, consume):
    def run(kinds, j0, prefetched):
        for n, kind in enumerate(kinds):
            if n == 0 and not prefetched:
                qk(j0, 0, kind)
            if n + 1 < len(kinds):
                qk(j0 + n + 1, (n + 1) % 2, kinds[n + 1])
            consume(j0 + n, n % 2)

    for count in range(1, len(near) + 1):
        @pl.when(i + 1 == count)
        def _(count=count):
            run(near[-count:], 0, False)

    @pl.when(i + 1 > len(near))
    def _():
        n_far = i + 1 - len(near)
        pairs = (n_far - 1) // 2
        qk(0, 0, FAR)

        def pair(j):
            qk(j + 1, 1, FAR)
            consume(j, 0)
            qk(j + 2, 0, FAR)
            consume(j + 1, 1)

        def body(t, carry):
            pair(4 * t)
            pair(4 * t + 2)
            return carry

        lax.fori_loop(0, pairs // 2, body, 0)

        @pl.when(pairs % 2 == 1)
        def _():
            pair(2 * (pairs - 1))

        j0 = 2 * pairs
        for left in (1, 2):
            @pl.when(n_far - j0 == left)
            def _(left=left):
                run((FAR,) * left + near, j0, True)


def _kv_rows(j):
    return pl.ds(pl.multiple_of(j * TK, TK), TK)


def _attn_scratch(streams, v_rows):
    return [pltpu.VMEM((streams, 2, TK, TQ + LANES), F32),
            pltpu.VMEM((streams, 2, 1, TQ), F32),
            pltpu.VMEM((streams, 1, TQ), F32),
            pltpu.VMEM((streams, v_rows, TQ), F32)]


MLA_STREAMS = 2


def _q_cols(i):
    return pl.ds(pl.multiple_of(i * TQ, TQ), TQ)


def _mla_kernel(qt_ref, k_ref, vt_ref, mask_ref, ot_ref, s_ref, mc_ref, m_ref, acc_ref):
    def qk(j, slot, kind, qts):
        bias = mask_ref[...] if kind == DIAG else None
        for st in range(MLA_STREAMS):
            k = k_ref[0, _kv_rows(j), st * HEAD_SLOT:(st + 1) * HEAD_SLOT]
            _scores(k, qts[st], bias, s_ref.at[st, slot], mc_ref.at[st, slot])

    def consume(j, slot):
        for st in range(MLA_STREAMS):
            vt = vt_ref[0, st * MLA_VROWS:(st + 1) * MLA_VROWS, _kv_rows(j)]
            _softmax_pv(s_ref.at[st, slot], mc_ref.at[st, slot], vt, m_ref.at[st], acc_ref.at[st])

    def q_tile(i, carry):
        qts = [qt_ref[0, st * HEAD_SLOT:(st + 1) * HEAD_SLOT, _q_cols(i)] for st in range(MLA_STREAMS)]
        m_ref[...] = jnp.full(m_ref.shape, NEG_INF, F32)
        acc_ref[...] = jnp.zeros(acc_ref.shape, F32)
        _sweep(i, (DIAG,), functools.partial(qk, qts=qts), consume)
        for st in range(MLA_STREAMS):
            acc = acc_ref[st]
            ot_ref[0, st * MLA_V:(st + 1) * MLA_V, _q_cols(i)] = (
                acc[:MLA_V] / acc[MLA_V:MLA_V + 1]).astype(BF16)
        return carry

    lax.fori_loop(0, qt_ref.shape[2] // TQ, q_tile, 0)


def _mla_attention(qt, k, vt, mask):
    b, _, s = qt.shape
    groups = MLA_HEADS // MLA_STREAMS
    return pl.pallas_call(
        _mla_kernel,
        grid=(b, groups),
        in_specs=[pl.BlockSpec((1, MLA_STREAMS * HEAD_SLOT, s), lambda bi, h: (bi, h, 0)),
                  pl.BlockSpec((1, s, MLA_STREAMS * HEAD_SLOT), lambda bi, h: (bi, 0, h)),
                  pl.BlockSpec((1, MLA_STREAMS * MLA_VROWS, s), lambda bi, h: (bi, h, 0)),
                  _const_spec(mask.shape)],
        out_specs=pl.BlockSpec((1, MLA_STREAMS * MLA_V, s), lambda bi, h: (bi, h, 0)),
        out_shape=jax.ShapeDtypeStruct((b, MLA_HEADS * MLA_V, s), BF16),
        scratch_shapes=_attn_scratch(MLA_STREAMS, MLA_VROWS),
        compiler_params=_params(2),
        name="mla_attn",
    )(qt, k, vt, mask)


def _diff_kernel(qt_ref, k_ref, vt_ref, bias_ref, lq1_ref, lk1_ref, lq2_ref, lk2_ref, sub_ref,
                 ot_ref, s_ref, mc_ref, m_ref, acc_ref, *, lam_init):
    row = lax.broadcasted_iota(jnp.int32, (HEAD_SLOT, TQ), 0)
    zero = jnp.zeros((HEAD_SLOT, TQ), BF16)
    lam = (jnp.exp(jnp.sum(lq1_ref[...] * lk1_ref[...], axis=1, keepdims=True))
           - jnp.exp(jnp.sum(lq2_ref[...] * lk2_ref[...], axis=1, keepdims=True)) + lam_init)

    def qk(j, slot, kind, qts):
        k = k_ref[0, _kv_rows(j), :]
        bias = None if kind == FAR else bias_ref[0, 0 if kind == DIAG else 1]
        for st in range(2):
            _scores(k, qts[st], bias, s_ref.at[st, slot], mc_ref.at[st, slot])

    def consume(j, slot):
        vt = vt_ref[0, :, _kv_rows(j)]
        for st in range(2):
            _softmax_pv(s_ref.at[st, slot], mc_ref.at[st, slot], vt, m_ref.at[st], acc_ref.at[st])

    def q_tile(i, carry):
        qt = qt_ref[0, :, _q_cols(i)]
        qts = [jnp.where(row < DIFF_DIM, qt, zero), jnp.where(row >= DIFF_DIM, qt, zero)]
        m_ref[...] = jnp.full(m_ref.shape, NEG_INF, F32)
        acc_ref[...] = jnp.zeros(acc_ref.shape, F32)
        _sweep(i, (SUB, DIAG), functools.partial(qk, qts=qts), consume)
        outs = [acc_ref[st, :DIFF_V] / acc_ref[st, DIFF_V:DIFF_V + 1] for st in range(2)]
        o = outs[0] - lam * outs[1]
        o = o * lax.rsqrt(jnp.mean(o * o, axis=0, keepdims=True) + EPS) * sub_ref[...]
        ot_ref[0, :, _q_cols(i)] = (o * (1.0 - lam_init)).astype(BF16)
        return carry

    lax.fori_loop(0, qt_ref.shape[2] // TQ, q_tile, 0)


def _diff_attention(qt, k, vt, bias, lq1, lk1, lq2, lk2, sub_col, lam_init):
    b, _, s = qt.shape
    vec = _const_spec(lq1.shape)
    return pl.pallas_call(
        functools.partial(_diff_kernel, lam_init=lam_init),
        grid=(b, DIFF_HEADS),
        in_specs=[pl.BlockSpec((1, HEAD_SLOT, s), lambda bi, h: (bi, h, 0)),
                  pl.BlockSpec((1, s, HEAD_SLOT), lambda bi, h: (bi, 0, h)),
                  pl.BlockSpec((1, DIFF_VROWS, s), lambda bi, h: (bi, h, 0)),
                  pl.BlockSpec((1, 2, TK, TQ), lambda bi, h: (h, 0, 0, 0)),
                  vec, vec, vec, vec, _const_spec(sub_col.shape)],
        out_specs=pl.BlockSpec((1, DIFF_V, s), lambda bi, h: (bi, h, 0)),
        out_shape=jax.ShapeDtypeStruct((b, DIFF_HEADS * DIFF_V, s), BF16),
        scratch_shapes=_attn_scratch(2, DIFF_VROWS),
        compiler_params=_params(2),
        name="diff_attn",
    )(qt, k, vt, bias, lq1, lk1, lq2, lk2, sub_col)


def _merge_kernel(h_ref, oat_ref, obt_ref, ga_ref, gb_ref, wa_ref, wb_ref, wo_ref, o_ref):
    ya = lax.dot_general(oat_ref[0], wa_ref[...], TN_DIMS, preferred_element_type=F32)
    yb = lax.dot_general(obt_ref[0], wb_ref[...], TN_DIMS, preferred_element_type=F32)
    merged = ga_ref[...].astype(F32) * ya + gb_ref[...].astype(F32) * yb
    o_ref[...] = h_ref[...] + jnp.dot(merged.astype(BF16), wo_ref[...], preferred_element_type=F32)


def _merge(h, oat, obt, ga, gb, wa, wb, wo, seq):
    t, d = h.shape
    nseq = seq // TM_PROJ
    tok = pl.BlockSpec((TM_PROJ, d), lambda i: (i, 0))
    tok_t = lambda rows: pl.BlockSpec((1, rows, TM_PROJ), lambda i: (i // nseq, 0, i % nseq))
    return pl.pallas_call(
        _merge_kernel,
        grid=(t // TM_PROJ,),
        in_specs=[tok, tok_t(oat.shape[1]), tok_t(obt.shape[1]), tok, tok,
                  _const_spec(wa.shape), _const_spec(wb.shape), _const_spec(wo.shape)],
        out_specs=tok,
        out_shape=jax.ShapeDtypeStruct((t, d), F32),
        compiler_params=_params(1),
        name="merge",
    )(h, oat, obt, ga, gb, wa, wb, wo)


def _memkv_kernel(m_ref, g_ref, w_ref, o_ref):
    mn = _rms(m_ref[...], g_ref[...]).astype(BF16)
    o_ref[...] = jnp.dot(mn, w_ref[...], preferred_element_type=F32).astype(BF16)


def _memkv(mem2d, g, w):
    rows, d = mem2d.shape
    return pl.pallas_call(
        _memkv_kernel,
        grid=(1,),
        in_specs=[_const_spec(mem2d.shape), _const_spec(g.shape), _const_spec(w.shape)],
        out_specs=_const_spec((rows, w.shape[1])),
        out_shape=jax.ShapeDtypeStruct((rows, w.shape[1]), BF16),
        compiler_params=_params(1),
        name="mem_kv",
    )(mem2d, g, w)


def _xattn_kernel(h_ref, g_ref, wq_ref, kv_ref, wo_ref, o_ref, *, scale):
    h = h_ref[...]
    u = _rms(h, g_ref[...]).astype(BF16)
    q = (jnp.dot(u, wq_ref[...], preferred_element_type=F32) * scale).astype(BF16)
    kv = kv_ref[...]
    outs = []
    for hd in range(XATTN_HEADS):
        qh = q[:, hd * XATTN_DIM:(hd + 1) * XATTN_DIM]
        kh = kv[:, 2 * hd * XATTN_DIM:(2 * hd + 1) * XATTN_DIM]
        vh = kv[:, (2 * hd + 1) * XATTN_DIM:(2 * hd + 2) * XATTN_DIM]
        s = lax.dot_general(qh, kh, NT_DIMS, preferred_element_type=F32)
        m = jnp.max(s, axis=1, keepdims=True)
        p = jnp.exp2(s - m)
        l = jnp.sum(p, axis=1, keepdims=True)
        oh = jnp.dot(p.astype(BF16), vh, preferred_element_type=F32) / l
        outs.append(oh.astype(BF16))
    o = jnp.concatenate(outs, axis=1)
    o_ref[...] = h + jnp.dot(o, wo_ref[...], preferred_element_type=F32)


def _xattn(h, g, wq, kvmem, wo, seq, mem_len):
    t, d = h.shape
    nseq = seq // TM_PROJ
    tok = pl.BlockSpec((TM_PROJ, d), lambda i: (i, 0))
    return pl.pallas_call(
        functools.partial(_xattn_kernel, scale=XATTN_DIM ** -0.5 * LOG2E),
        grid=(t // TM_PROJ,),
        in_specs=[tok, _const_spec(g.shape), _const_spec(wq.shape),
                  pl.BlockSpec((mem_len, kvmem.shape[1]), lambda i: (i // nseq, 0)),
                  _const_spec(wo.shape)],
        out_specs=tok,
        out_shape=jax.ShapeDtypeStruct((t, d), F32),
        compiler_params=_params(1),
        name="xattn",
    )(h, g, wq, kvmem, wo)


def _pad_heads(w, heads, width, slot):
    k = w.shape[0]
    w = w.reshape(k, heads, width)
    return jnp.pad(w, ((0, 0), (0, 0), (0, slot - width))).reshape(k, heads * slot)


def _rope_tables(seq):
    half = MLA_ROPE // 2
    pos = jnp.arange(seq)
    freqs = ROPE_BASE ** (-jnp.arange(half, dtype=F32) / half)
    ang = pos.astype(F32)[:, None] * freqs[None, :]
    cos, sin = jnp.cos(ang), jnp.sin(ang)
    ones = jnp.ones((seq, MLA_NOPE), F32)
    z_nope = jnp.zeros((seq, MLA_NOPE), F32)
    z_pad = jnp.zeros((seq, HEAD_SLOT - MLA_NOPE - MLA_ROPE), F32)
    c_rope = jnp.concatenate([cos, cos], axis=1)
    s_rope = jnp.concatenate([-sin, sin], axis=1)
    qscale = (MLA_NOPE + MLA_ROPE) ** -0.5 * LOG2E
    cq = jnp.concatenate([ones, c_rope, z_pad], axis=1) * qscale
    sq = jnp.concatenate([z_nope, s_rope, z_pad], axis=1) * qscale
    ck = jnp.concatenate([z_nope, c_rope, z_pad], axis=1)
    sk = jnp.concatenate([z_nope, s_rope, z_pad], axis=1)
    return (cq.T, sq.T), (ck, sk)


def _swap_halves(w):
    half = w.shape[-1] // 2
    return jnp.concatenate([w[..., half:], w[..., :half]], axis=-1)


def kernel(x, mem, ffn1_norm, ffn1_w_gate, ffn1_w_up, ffn1_w_down, mix_norm, w_in, mla_q_norm, mla_w_q_up, mla_kv_norm, mla_w_kv_up, diff_lambda_q1, diff_lambda_k1, diff_lambda_q2, diff_lambda_k2, diff_sub_norm, rel_bias, w_branch_a, w_branch_b, w_out, xattn_norm, mem_norm, xattn_w_q, xattn_w_kv, xattn_w_o, ffn2_norm, ffn2_w_gate, ffn2_w_up, ffn2_w_down, final_norm):
    b, s, d = x.shape
    depth = ffn1_norm.shape[0]
    mem_len = mem.shape[1]
    t = b * s
    bf = lambda a: a.astype(BF16)
    row = lambda a: a.reshape(1, -1)

    tabs_t, tabs = _rope_tables(s)
    kpos = jnp.arange(TK)[:, None]
    qpos = jnp.arange(TQ)[None, :]
    mla_mask = jnp.where(kpos // CHUNK <= qpos // CHUNK, 0.0, NEG_INF).astype(F32)
    bias = _bias_tiles(rel_bias.astype(F32))

    h = x.reshape(t, d)
    for l in range(depth):
        h = _ffn(h, row(ffn1_norm[l]), bf(ffn1_w_gate[l]), bf(ffn1_w_up[l]), bf(ffn1_w_down[l]),
                 row(final_norm), final_norm=False)

        w = w_in[l]
        o_kv = MLA_Q_LORA
        o_kr = o_kv + MLA_KV_LORA
        o_dq = o_kr + MLA_ROPE
        w_kr = w[:, o_kr:o_dq]
        pad_l = jnp.zeros((d, MLA_NOPE), F32)
        pad_r = jnp.zeros((d, HEAD_SLOT - MLA_NOPE - MLA_ROPE), F32)
        wlat = jnp.concatenate([w[:, :o_kr], pad_l, w_kr, pad_r, pad_l, _swap_halves(w_kr), pad_r], axis=1)
        wdq = w[:, o_dq:o_dq + d]
        wdk = w[:, o_dq + d:o_dq + 2 * d]
        wdv = _pad_heads(w[:, o_dq + 2 * d:o_dq + 3 * d], DIFF_HEADS, DIFF_V, DIFF_VROWS)
        wgate = w[:, o_dq + 3 * d:]

        wq = _pad_heads(mla_w_q_up[l], MLA_HEADS, MLA_NOPE + MLA_ROPE, HEAD_SLOT)
        wkv3 = mla_w_kv_up[l].reshape(MLA_KV_LORA, MLA_HEADS, MLA_NOPE + MLA_V)
        wk = _pad_heads(wkv3[..., :MLA_NOPE].reshape(MLA_KV_LORA, -1), MLA_HEADS, MLA_NOPE, HEAD_SLOT)
        wv = _pad_heads(wkv3[..., MLA_NOPE:].reshape(MLA_KV_LORA, -1), MLA_HEADS, MLA_V, MLA_VROWS)

        qt, k, vt, dqt, dk, dvt, ga, gb = _proj(
            h, row(mix_norm[l]), bf(wlat), bf(wdk), bf(wgate), bf(wdq.T), bf(wdv.T),
            row(mla_q_norm[l]), bf(wq.T), row(mla_kv_norm[l]), bf(wk), bf(wv.T),
            tabs_t, tabs, b, s)

        oat = _mla_attention(qt, k.reshape(b, s, d), vt, mla_mask)
        lam_init = 0.8 - 0.6 * math.exp(-0.3 * l)
        obt = _diff_attention(dqt, dk.reshape(b, s, d), dvt, bias,
                              row(diff_lambda_q1[l]), row(diff_lambda_k1[l]),
                              row(diff_lambda_q2[l]), row(diff_lambda_k2[l]),
                              diff_sub_norm[l].reshape(-1, 1), lam_init)

        h = _merge(h, oat, obt, ga, gb, bf(w_branch_a[l]), bf(w_branch_b[l]), bf(w_out[l]), s)

        kvmem = _memkv(mem.reshape(b * mem_len, d), row(mem_norm[l]), bf(xattn_w_kv[l]))
        h = _xattn(h, row(xattn_norm[l]), bf(xattn_w_q[l]), kvmem, bf(xattn_w_o[l]), s, mem_len)

        last = l == depth - 1
        h = _ffn(h, row(ffn2_norm[l]), bf(ffn2_w_gate[l]), bf(ffn2_w_up[l]), bf(ffn2_w_down[l]),
                 row(final_norm), final_norm=last)
    return h.reshape(b, s, d)
```

```python
import functools
import math

import jax
import jax.numpy as jnp
from jax import lax
from jax.experimental import pallas as pl
from jax.experimental.pallas import tpu as pltpu

F32 = jnp.float32
BF16 = jnp.bfloat16

CHUNK = 64
EPS = 1e-6
NEG_INF = -1e30
MLA_HEADS = 8
MLA_Q_LORA = 384
MLA_KV_LORA = 256
MLA_NOPE = 64
MLA_ROPE = 32
MLA_V = 64
ROPE_BASE = 10000.0
DIFF_HEADS = 8
DIFF_DIM = 64
DIFF_V = 2 * DIFF_DIM
N_BUCKETS = 32
XATTN_HEADS = 4
XATTN_DIM = 128
LOG2E = math.log2(math.e)

LANES = 128
HEAD_SLOT = LANES
BF16_ROWS = 16
VMEM_LIMIT = 56 * 1024 * 1024

TM_FFN = 512
TM_PROJ = 512
TQ = 512
TK = 512

MLA_VROWS = (MLA_V + 1 + BF16_ROWS - 1) // BF16_ROWS * BF16_ROWS
DIFF_VROWS = (DIFF_V + 1 + BF16_ROWS - 1) // BF16_ROWS * BF16_ROWS

NT_DIMS = (((1,), (1,)), ((), ()))
TN_DIMS = (((0,), (0,)), ((), ()))


def _rms(x, g):
    return x * lax.rsqrt(jnp.mean(x * x, axis=-1, keepdims=True) + EPS) * g


def _const_spec(shape):
    nd = len(shape)
    return pl.BlockSpec(shape, lambda *_: (0,) * nd, pipeline_mode=pl.Buffered(1))


def _params(n_grid):
    return pltpu.CompilerParams(
        dimension_semantics=("arbitrary",) * n_grid, vmem_limit_bytes=VMEM_LIMIT)


def _ffn_kernel(x_ref, g_ref, wg_ref, wu_ref, wd_ref, fin_ref, o_ref, *, splits, final_norm):
    x = x_ref[...]
    nb = _rms(x, g_ref[...]).astype(BF16)
    y = None
    for lo, hi in splits:
        g = jnp.dot(nb, wg_ref[:, lo:hi], preferred_element_type=F32)
        u = jnp.dot(nb, wu_ref[:, lo:hi], preferred_element_type=F32)
        h = (g * jax.nn.sigmoid(g) * u).astype(BF16)
        part = jnp.dot(h, wd_ref[lo:hi, :], preferred_element_type=F32)
        y = part if y is None else y + part
    out = x + 0.5 * y
    if final_norm:
        out = _rms(out, fin_ref[...])
    o_ref[...] = out


def _ffn(x, norm_g, wg, wu, wd, fin_g, final_norm):
    t, d = x.shape
    dff = wg.shape[1]
    half = (dff // 2 + 255) // 256 * 256
    splits = ((0, half), (half, dff))
    return pl.pallas_call(
        functools.partial(_ffn_kernel, splits=splits, final_norm=final_norm),
        grid=(t // TM_FFN,),
        in_specs=[
            pl.BlockSpec((TM_FFN, d), lambda i: (i, 0)),
            _const_spec((1, d)),
            _const_spec((d, dff)),
            _const_spec((d, dff)),
            _const_spec((dff, d)),
            _const_spec((1, d)),
        ],
        out_specs=pl.BlockSpec((TM_FFN, d), lambda i: (i, 0)),
        out_shape=jax.ShapeDtypeStruct((t, d), F32),
        compiler_params=_params(1),
        name="ffn",
    )(x, norm_g, wg, wu, wd, fin_g)


def _ones_rows(n_rows, rows_per_head, one_row):
    r = lax.broadcasted_iota(jnp.int32, (n_rows, 1), 0) % rows_per_head
    return jnp.where(r == one_row, 1.0, 0.0)


def _proj_kernel(h_ref, g_ref, wlat_ref, wdk_ref, wdqt_ref, wdvt_ref,
                 qn_ref, wqt_ref, kvn_ref, wk_ref, wvt_ref,
                 cqt_ref, sqt_ref, ck_ref, sk_ref,
                 qt_ref, k_ref, vt_ref, dqt_ref, dk_ref, dvt_ref, *, diff_scale):
    u = _rms(h_ref[...], g_ref[...]).astype(BF16)

    lat = jnp.dot(u, wlat_ref[...], preferred_element_type=F32)
    q_lat = lat[:, :MLA_Q_LORA]
    kv_lat = lat[:, MLA_Q_LORA:MLA_Q_LORA + MLA_KV_LORA]
    o = MLA_Q_LORA + MLA_KV_LORA
    kr_a = lat[:, o:o + LANES]
    kr_b = lat[:, o + LANES:o + 2 * LANES]

    qn = _rms(q_lat, qn_ref[...]).astype(BF16)
    kvn = _rms(kv_lat, kvn_ref[...]).astype(BF16)

    qqt = lax.dot_general(wqt_ref[...], qn, NT_DIMS, preferred_element_type=F32)
    cqt, sqt = cqt_ref[...], sqt_ref[...]
    r0, r1, r2 = MLA_NOPE, MLA_NOPE + MLA_ROPE // 2, MLA_NOPE + MLA_ROPE
    for hd in range(MLA_HEADS):
        rows = slice(hd * HEAD_SLOT, (hd + 1) * HEAD_SLOT)
        qh = qqt[rows]
        qh_sw = jnp.concatenate([qh[:r0], qh[r1:r2], qh[r0:r1], qh[r2:]], axis=0)
        qt_ref[0, rows, :] = (qh * cqt + qh_sw * sqt).astype(BF16)

    kk = jnp.dot(kvn, wk_ref[...], preferred_element_type=F32)
    kr = kr_a * ck_ref[...] + kr_b * sk_ref[...]
    for hd in range(MLA_HEADS):
        sl = slice(hd * HEAD_SLOT, (hd + 1) * HEAD_SLOT)
        k_ref[:, sl] = (kk[:, sl] + kr).astype(BF16)

    vt = lax.dot_general(wvt_ref[...], kvn, NT_DIMS, preferred_element_type=F32)
    vt_ref[0] = (vt + _ones_rows(vt.shape[0], MLA_VROWS, MLA_V)).astype(BF16)

    dqt = lax.dot_general(wdqt_ref[...], u, NT_DIMS, preferred_element_type=F32)
    dqt_ref[0] = (dqt * diff_scale).astype(BF16)
    dk_ref[...] = jnp.dot(u, wdk_ref[...], preferred_element_type=F32).astype(BF16)
    dvt = lax.dot_general(wdvt_ref[...], u, NT_DIMS, preferred_element_type=F32)
    dvt_ref[0] = (dvt + _ones_rows(dvt.shape[0], DIFF_VROWS, DIFF_V)).astype(BF16)


def _proj(h, g, wlat, wdk, wdqt, wdvt, qn, wqt, kvn, wk, wvt, tabs_t, tabs, batch, seq):
    t, d = h.shape
    nseq = seq // TM_PROJ
    tok = pl.BlockSpec((TM_PROJ, d), lambda i: (i, 0))
    tok_t = lambda rows: pl.BlockSpec((1, rows, TM_PROJ), lambda i: (i // nseq, 0, i % nseq))
    consts = [g, wlat, wdk, wdqt, wdvt, qn, wqt, kvn, wk, wvt]
    out_t = lambda rows: jax.ShapeDtypeStruct((batch, rows, seq), BF16)
    out_n = jax.ShapeDtypeStruct((t, d), BF16)
    return pl.pallas_call(
        functools.partial(_proj_kernel, diff_scale=DIFF_DIM ** -0.5 * LOG2E),
        grid=(t // TM_PROJ,),
        in_specs=[tok] + [_const_spec(c.shape) for c in consts]
        + [pl.BlockSpec((HEAD_SLOT, TM_PROJ), lambda i: (0, i % nseq))] * 2
        + [pl.BlockSpec((TM_PROJ, LANES), lambda i: (i % nseq, 0))] * 2,
        out_specs=[tok_t(wqt.shape[0]), tok, tok_t(wvt.shape[0]),
                   tok_t(wdqt.shape[0]), tok, tok_t(wdvt.shape[0])],
        out_shape=[out_t(wqt.shape[0]), out_n, out_t(wvt.shape[0]),
                   out_t(wdqt.shape[0]), out_n, out_t(wdvt.shape[0])],
        compiler_params=_params(1),
        name="in_proj",
    )(h, *consts, *tabs_t, *tabs)


def _t5_bucket(rel):
    half = N_BUCKETS // 2
    max_exact = half // 2
    n = jnp.abs(rel)
    n2 = n * n
    large = max_exact
    for k in range(1, half - max_exact):
        large = large + (n2 >= (max_exact * max_exact) << k).astype(jnp.int32)
    return jnp.where(rel > 0, half, 0) + jnp.where(n < max_exact, n, large)


def _bias_kernel(tab_ref, o_ref):
    hd = pl.program_id(0)
    kpos = lax.broadcasted_iota(jnp.int32, (TK, TQ), 0)
    qpos = lax.broadcasted_iota(jnp.int32, (TK, TQ), 1)
    for delta in range(2):
        bucket = _t5_bucket(kpos - qpos - delta * TK)
        val = jnp.zeros((TK, TQ), F32)
        for b in range(N_BUCKETS):
            val = jnp.where(bucket == b, tab_ref[b, hd], val)
        val = (val - tab_ref[N_BUCKETS // 2 - 1, hd]) * LOG2E
        if delta == 0:
            val = jnp.where(kpos // CHUNK <= qpos // CHUNK, val, NEG_INF)
        o_ref[0, delta] = val


def _bias_tiles(rel_bias):
    return pl.pallas_call(
        _bias_kernel,
        grid=(DIFF_HEADS,),
        in_specs=[pl.BlockSpec(memory_space=pltpu.SMEM)],
        out_specs=pl.BlockSpec((1, 2, TK, TQ), lambda h: (h, 0, 0, 0)),
        out_shape=jax.ShapeDtypeStruct((DIFF_HEADS, 2, TK, TQ), F32),
        compiler_params=_params(1),
        name="bias_tiles",
    )(rel_bias)


FAR, SUB, DIAG = 0, 1, 2


def _scores(k, qt, bias, s_ref, mc_ref):
    s = jnp.dot(k, qt, preferred_element_type=F32)
    if bias is not None:
        s = s + bias
    s_ref[:, :TQ] = s
    mc_ref[...] = jnp.max(s, axis=0, keepdims=True)


def _softmax_pv(s_ref, mc_ref, vt, m_ref, acc_ref):
    m_prev = m_ref[...]
    m_new = jnp.maximum(m_prev, mc_ref[...])
    alpha = jnp.exp2(m_prev - m_new)
    p = jnp.exp2(s_ref[:, :TQ] - m_new)
    pv = jnp.dot(vt, p.astype(BF16), preferred_element_type=F32)
    acc_ref[...] = alpha * acc_ref[...] + pv
    m_ref[...] = m_new


def _sweep(i, near, qk, consume):
    def run(kinds, j0, prefetched):
        for n, kind in enumerate(kinds):
            if n == 0 and not prefetched:
                qk(j0, 0, kind)
            if n + 1 < len(kinds):
                qk(j0 + n + 1, (n + 1) % 2, kinds[n + 1])
            consume(j0 + n, n % 2)

    for count in range(1, len(near) + 1):
        @pl.when(i + 1 == count)
        def _(count=count):
            run(near[-count:], 0, False)

    @pl.when(i + 1 > len(near))
    def _():
        n_far = i + 1 - len(near)
        pairs = (n_far - 1) // 2
        qk(0, 0, FAR)

        def pair(j):
            qk(j + 1, 1, FAR)
            consume(j, 0)
            qk(j + 2, 0, FAR)
            consume(j + 1, 1)

        def body(t, carry):
            pair(4 * t)
            pair(4 * t + 2)
            return carry

        lax.fori_loop(0, pairs // 2, body, 0)

        @pl.when(pairs % 2 == 1)
        def _():
            pair(2 * (pairs - 1))

        j0 = 2 * pairs
        for left in (1, 2):
            @pl.when(n_far - j0 == left)
            def _(left=left):
                run((FAR,) * left + near, j0, True)


def _kv_rows(j):
    return pl.ds(pl.multiple_of(j * TK, TK), TK)


def _attn_scratch(streams, v_rows):
    return [pltpu.VMEM((streams, 2, TK, TQ + LANES), F32),
            pltpu.VMEM((streams, 2, 1, TQ), F32),
            pltpu.VMEM((streams, 1, TQ), F32),
            pltpu.VMEM((streams, v_rows, TQ), F32)]


MLA_STREAMS = 2


def _q_cols(i):
    return pl.ds(pl.multiple_of(i * TQ, TQ), TQ)


def _mla_kernel(qt_ref, k_ref, vt_ref, mask_ref, ot_ref, s_ref, mc_ref, m_ref, acc_ref):
    def qk(j, slot, kind, qts):
        bias = mask_ref[...] if kind == DIAG else None
        for st in range(MLA_STREAMS):
            k = k_ref[0, _kv_rows(j), st * HEAD_SLOT:(st + 1) * HEAD_SLOT]
            _scores(k, qts[st], bias, s_ref.at[st, slot], mc_ref.at[st, slot])

    def consume(j, slot):
        for st in range(MLA_STREAMS):
            vt = vt_ref[0, st * MLA_VROWS:(st + 1) * MLA_VROWS, _kv_rows(j)]
            _softmax_pv(s_ref.at[st, slot], mc_ref.at[st, slot], vt, m_ref.at[st], acc_ref.at[st])

    def q_tile(i, carry):
        qts = [qt_ref[0, st * HEAD_SLOT:(st + 1) * HEAD_SLOT, _q_cols(i)] for st in range(MLA_STREAMS)]
        m_ref[...] = jnp.full(m_ref.shape, NEG_INF, F32)
        acc_ref[...] = jnp.zeros(acc_ref.shape, F32)
        _sweep(i, (DIAG,), functools.partial(qk, qts=qts), consume)
        for st in range(MLA_STREAMS):
            acc = acc_ref[st]
            ot_ref[0, st * MLA_V:(st + 1) * MLA_V, _q_cols(i)] = (
                acc[:MLA_V] / acc[MLA_V:MLA_V + 1]).astype(BF16)
        return carry

    lax.fori_loop(0, qt_ref.shape[2] // TQ, q_tile, 0)


def _mla_attention(qt, k, vt, mask):
    b, _, s = qt.shape
    groups = MLA_HEADS // MLA_STREAMS
    return pl.pallas_call(
        _mla_kernel,
        grid=(b, groups),
        in_specs=[pl.BlockSpec((1, MLA_STREAMS * HEAD_SLOT, s), lambda bi, h: (bi, h, 0)),
                  pl.BlockSpec((1, s, MLA_STREAMS * HEAD_SLOT), lambda bi, h: (bi, 0, h)),
                  pl.BlockSpec((1, MLA_STREAMS * MLA_VROWS, s), lambda bi, h: (bi, h, 0)),
                  _const_spec(mask.shape)],
        out_specs=pl.BlockSpec((1, MLA_STREAMS * MLA_V, s), lambda bi, h: (bi, h, 0)),
        out_shape=jax.ShapeDtypeStruct((b, MLA_HEADS * MLA_V, s), BF16),
        scratch_shapes=_attn_scratch(MLA_STREAMS, MLA_VROWS),
        compiler_params=_params(2),
        name="mla_attn",
    )(qt, k, vt, mask)


def _diff_kernel(qt_ref, k_ref, vt_ref, bias_ref, lq1_ref, lk1_ref, lq2_ref, lk2_ref, sub_ref,
                 ot_ref, s_ref, mc_ref, m_ref, acc_ref, *, lam_init):
    row = lax.broadcasted_iota(jnp.int32, (HEAD_SLOT, TQ), 0)
    zero = jnp.zeros((HEAD_SLOT, TQ), BF16)
    lam = (jnp.exp(jnp.sum(lq1_ref[...] * lk1_ref[...], axis=1, keepdims=True))
           - jnp.exp(jnp.sum(lq2_ref[...] * lk2_ref[...], axis=1, keepdims=True)) + lam_init)

    def qk(j, slot, kind, qts):
        k = k_ref[0, _kv_rows(j), :]
        bias = None if kind == FAR else bias_ref[0, 0 if kind == DIAG else 1]
        for st in range(2):
            _scores(k, qts[st], bias, s_ref.at[st, slot], mc_ref.at[st, slot])

    def consume(j, slot):
        vt = vt_ref[0, :, _kv_rows(j)]
        for st in range(2):
            _softmax_pv(s_ref.at[st, slot], mc_ref.at[st, slot], vt, m_ref.at[st], acc_ref.at[st])

    def q_tile(i, carry):
        qt = qt_ref[0, :, _q_cols(i)]
        qts = [jnp.where(row < DIFF_DIM, qt, zero), jnp.where(row >= DIFF_DIM, qt, zero)]
        m_ref[...] = jnp.full(m_ref.shape, NEG_INF, F32)
        acc_ref[...] = jnp.zeros(acc_ref.shape, F32)
        _sweep(i, (SUB, DIAG), functools.partial(qk, qts=qts), consume)
        outs = [acc_ref[st, :DIFF_V] / acc_ref[st, DIFF_V:DIFF_V + 1] for st in range(2)]
        o = outs[0] - lam * outs[1]
        o = o * lax.rsqrt(jnp.mean(o * o, axis=0, keepdims=True) + EPS) * sub_ref[...]
        ot_ref[0, :, _q_cols(i)] = (o * (1.0 - lam_init)).astype(BF16)
        return carry

    lax.fori_loop(0, qt_ref.shape[2] // TQ, q_tile, 0)


def _diff_attention(qt, k, vt, bias, lq1, lk1, lq2, lk2, sub_col, lam_init):
    b, _, s = qt.shape
    vec = _const_spec(lq1.shape)
    return pl.pallas_call(
        functools.partial(_diff_kernel, lam_init=lam_init),
        grid=(b, DIFF_HEADS),
        in_specs=[pl.BlockSpec((1, HEAD_SLOT, s), lambda bi, h: (bi, h, 0)),
                  pl.BlockSpec((1, s, HEAD_SLOT), lambda bi, h: (bi, 0, h)),
                  pl.BlockSpec((1, DIFF_VROWS, s), lambda bi, h: (bi, h, 0)),
                  pl.BlockSpec((1, 2, TK, TQ), lambda bi, h: (h, 0, 0, 0)),
                  vec, vec, vec, vec, _const_spec(sub_col.shape)],
        out_specs=pl.BlockSpec((1, DIFF_V, s), lambda bi, h: (bi, h, 0)),
        out_shape=jax.ShapeDtypeStruct((b, DIFF_HEADS * DIFF_V, s), BF16),
        scratch_shapes=_attn_scratch(2, DIFF_VROWS),
        compiler_params=_params(2),
        name="diff_attn",
    )(qt, k, vt, bias, lq1, lk1, lq2, lk2, sub_col)


def _memkv_kernel(m_ref, g_ref, w_ref, o_ref):
    mn = _rms(m_ref[...], g_ref[...]).astype(BF16)
    o_ref[...] = jnp.dot(mn, w_ref[...], preferred_element_type=F32).astype(BF16)


def _memkv(mem2d, g, w):
    rows, d = mem2d.shape
    return pl.pallas_call(
        _memkv_kernel,
        grid=(1,),
        in_specs=[_const_spec(mem2d.shape), _const_spec(g.shape), _const_spec(w.shape)],
        out_specs=_const_spec((rows, w.shape[1])),
        out_shape=jax.ShapeDtypeStruct((rows, w.shape[1]), BF16),
        compiler_params=_params(1),
        name="mem_kv",
    )(mem2d, g, w)


def _post_kernel(h_ref, oat_ref, obt_ref, g_ref, wgate_ref, wa_ref, wb_ref, wo_ref,
                 xg_ref, xwq_ref, kv_ref, xwo_ref, o_ref, *, scale):
    h = h_ref[...]
    d = h.shape[1]
    u = _rms(h, g_ref[...]).astype(BF16)
    gates = jax.nn.sigmoid(jnp.dot(u, wgate_ref[...], preferred_element_type=F32))
    ya = lax.dot_general(oat_ref[0], wa_ref[...], TN_DIMS, preferred_element_type=F32)
    yb = lax.dot_general(obt_ref[0], wb_ref[...], TN_DIMS, preferred_element_type=F32)
    merged = gates[:, :d] * ya + gates[:, d:] * yb
    h = h + jnp.dot(merged.astype(BF16), wo_ref[...], preferred_element_type=F32)

    x = _rms(h, xg_ref[...]).astype(BF16)
    q = (jnp.dot(x, xwq_ref[...], preferred_element_type=F32) * scale).astype(BF16)
    kv = kv_ref[...]
    outs = []
    for hd in range(XATTN_HEADS):
        qh = q[:, hd * XATTN_DIM:(hd + 1) * XATTN_DIM]
        kh = kv[:, 2 * hd * XATTN_DIM:(2 * hd + 1) * XATTN_DIM]
        vh = kv[:, (2 * hd + 1) * XATTN_DIM:(2 * hd + 2) * XATTN_DIM]
        s = lax.dot_general(qh, kh, NT_DIMS, preferred_element_type=F32)
        m = jnp.max(s, axis=1, keepdims=True)
        p = jnp.exp2(s - m)
        l = jnp.sum(p, axis=1, keepdims=True)
        oh = jnp.dot(p.astype(BF16), vh, preferred_element_type=F32) / l
        outs.append(oh.astype(BF16))
    o = jnp.concatenate(outs, axis=1)
    o_ref[...] = h + jnp.dot(o, xwo_ref[...], preferred_element_type=F32)


def _post(h, oat, obt, g, wgate, wa, wb, wo, xg, xwq, kvmem, xwo, seq, mem_len):
    t, d = h.shape
    nseq = seq // TM_PROJ
    tok = pl.BlockSpec((TM_PROJ, d), lambda i: (i, 0))
    tok_t = lambda rows: pl.BlockSpec((1, rows, TM_PROJ), lambda i: (i // nseq, 0, i % nseq))
    cs = lambda a: _const_spec(a.shape)
    return pl.pallas_call(
        functools.partial(_post_kernel, scale=XATTN_DIM ** -0.5 * LOG2E),
        grid=(t // TM_PROJ,),
        in_specs=[tok, tok_t(oat.shape[1]), tok_t(obt.shape[1]), cs(g), cs(wgate), cs(wa), cs(wb), cs(wo),
                  cs(xg), cs(xwq),
                  pl.BlockSpec((mem_len, kvmem.shape[1]), lambda i: (i // nseq, 0)), cs(xwo)],
        out_specs=tok,
        out_shape=jax.ShapeDtypeStruct((t, d), F32),
        compiler_params=_params(1),
        name="post_attn",
    )(h, oat, obt, g, wgate, wa, wb, wo, xg, xwq, kvmem, xwo)


def _pad_heads(w, heads, width, slot):
    k = w.shape[0]
    w = w.reshape(k, heads, width)
    return jnp.pad(w, ((0, 0), (0, 0), (0, slot - width))).reshape(k, heads * slot)


def _rope_tables(seq):
    half = MLA_ROPE // 2
    pos = jnp.arange(seq)
    freqs = ROPE_BASE ** (-jnp.arange(half, dtype=F32) / half)
    ang = pos.astype(F32)[:, None] * freqs[None, :]
    cos, sin = jnp.cos(ang), jnp.sin(ang)
    ones = jnp.ones((seq, MLA_NOPE), F32)
    z_nope = jnp.zeros((seq, MLA_NOPE), F32)
    z_pad = jnp.zeros((seq, HEAD_SLOT - MLA_NOPE - MLA_ROPE), F32)
    c_rope = jnp.concatenate([cos, cos], axis=1)
    s_rope = jnp.concatenate([-sin, sin], axis=1)
    qscale = (MLA_NOPE + MLA_ROPE) ** -0.5 * LOG2E
    cq = jnp.concatenate([ones, c_rope, z_pad], axis=1) * qscale
    sq = jnp.concatenate([z_nope, s_rope, z_pad], axis=1) * qscale
    ck = jnp.concatenate([z_nope, c_rope, z_pad], axis=1)
    sk = jnp.concatenate([z_nope, s_rope, z_pad], axis=1)
    return (cq.T, sq.T), (ck, sk)


def _swap_halves(w):
    half = w.shape[-1] // 2
    return jnp.concatenate([w[..., half:], w[..., :half]], axis=-1)


def kernel(x, mem, ffn1_norm, ffn1_w_gate, ffn1_w_up, ffn1_w_down, mix_norm, w_in, mla_q_norm, mla_w_q_up, mla_kv_norm, mla_w_kv_up, diff_lambda_q1, diff_lambda_k1, diff_lambda_q2, diff_lambda_k2, diff_sub_norm, rel_bias, w_branch_a, w_branch_b, w_out, xattn_norm, mem_norm, xattn_w_q, xattn_w_kv, xattn_w_o, ffn2_norm, ffn2_w_gate, ffn2_w_up, ffn2_w_down, final_norm):
    b, s, d = x.shape
    depth = ffn1_norm.shape[0]
    mem_len = mem.shape[1]
    t = b * s
    bf = lambda a: a.astype(BF16)
    row = lambda a: a.reshape(1, -1)

    tabs_t, tabs = _rope_tables(s)
    kpos = jnp.arange(TK)[:, None]
    qpos = jnp.arange(TQ)[None, :]
    mla_mask = jnp.where(kpos // CHUNK <= qpos // CHUNK, 0.0, NEG_INF).astype(F32)
    bias = _bias_tiles(rel_bias.astype(F32))

    h = x.reshape(t, d)
    for l in range(depth):
        h = _ffn(h, row(ffn1_norm[l]), bf(ffn1_w_gate[l]), bf(ffn1_w_up[l]), bf(ffn1_w_down[l]),
                 row(final_norm), final_norm=False)

        w = w_in[l]
        o_kv = MLA_Q_LORA
        o_kr = o_kv + MLA_KV_LORA
        o_dq = o_kr + MLA_ROPE
        w_kr = w[:, o_kr:o_dq]
        pad_l = jnp.zeros((d, MLA_NOPE), F32)
        pad_r = jnp.zeros((d, HEAD_SLOT - MLA_NOPE - MLA_ROPE), F32)
        wlat = jnp.concatenate([w[:, :o_kr], pad_l, w_kr, pad_r, pad_l, _swap_halves(w_kr), pad_r], axis=1)
        wdq = w[:, o_dq:o_dq + d]
        wdk = w[:, o_dq + d:o_dq + 2 * d]
        wdv = _pad_heads(w[:, o_dq + 2 * d:o_dq + 3 * d], DIFF_HEADS, DIFF_V, DIFF_VROWS)
        wgate = w[:, o_dq + 3 * d:]

        wq = _pad_heads(mla_w_q_up[l], MLA_HEADS, MLA_NOPE + MLA_ROPE, HEAD_SLOT)
        wkv3 = mla_w_kv_up[l].reshape(MLA_KV_LORA, MLA_HEADS, MLA_NOPE + MLA_V)
        wk = _pad_heads(wkv3[..., :MLA_NOPE].reshape(MLA_KV_LORA, -1), MLA_HEADS, MLA_NOPE, HEAD_SLOT)
        wv = _pad_heads(wkv3[..., MLA_NOPE:].reshape(MLA_KV_LORA, -1), MLA_HEADS, MLA_V, MLA_VROWS)

        qt, k, vt, dqt, dk, dvt = _proj(
            h, row(mix_norm[l]), bf(wlat), bf(wdk), bf(wdq.T), bf(wdv.T),
            row(mla_q_norm[l]), bf(wq.T), row(mla_kv_norm[l]), bf(wk), bf(wv.T),
            tabs_t, tabs, b, s)

        oat = _mla_attention(qt, k.reshape(b, s, d), vt, mla_mask)
        lam_init = 0.8 - 0.6 * math.exp(-0.3 * l)
        obt = _diff_attention(dqt, dk.reshape(b, s, d), dvt, bias,
                              row(diff_lambda_q1[l]), row(diff_lambda_k1[l]),
                              row(diff_lambda_q2[l]), row(diff_lambda_k2[l]),
                              diff_sub_norm[l].reshape(-1, 1), lam_init)

        kvmem = _memkv(mem.reshape(b * mem_len, d), row(mem_norm[l]), bf(xattn_w_kv[l]))
        h = _post(h, oat, obt, row(mix_norm[l]), bf(wgate), bf(w_branch_a[l]), bf(w_branch_b[l]),
                  bf(w_out[l]), row(xattn_norm[l]), bf(xattn_w_q[l]), kvmem, bf(xattn_w_o[l]), s, mem_len)

        last = l == depth - 1
        h = _ffn(h, row(ffn2_norm[l]), bf(ffn2_w_gate[l]), bf(ffn2_w_up[l]), bf(ffn2_w_down[l]),
                 row(final_norm), final_norm=last)
    return h.reshape(b, s, d)
```

```python
import functools
import math

import jax
import jax.numpy as jnp
from jax import lax
from jax.experimental import pallas as pl
from jax.experimental.pallas import tpu as pltpu

F32 = jnp.float32
BF16 = jnp.bfloat16

CHUNK = 64
EPS = 1e-6
NEG_INF = -1e30
MLA_HEADS = 8
MLA_Q_LORA = 384
MLA_KV_LORA = 256
MLA_NOPE = 64
MLA_ROPE = 32
MLA_V = 64
ROPE_BASE = 10000.0
DIFF_HEADS = 8
DIFF_DIM = 64
DIFF_V = 2 * DIFF_DIM
N_BUCKETS = 32
XATTN_HEADS = 4
XATTN_DIM = 128
LOG2E = math.log2(math.e)

LANES = 128
HEAD_SLOT = LANES
BF16_ROWS = 16
VMEM_LIMIT = 56 * 1024 * 1024

TM_FFN = 512
TM_PROJ = 512
TQ = 512
TK = 512

MLA_VROWS = (MLA_V + 1 + BF16_ROWS - 1) // BF16_ROWS * BF16_ROWS
DIFF_VROWS = (DIFF_V + 1 + BF16_ROWS - 1) // BF16_ROWS * BF16_ROWS

NT_DIMS = (((1,), (1,)), ((), ()))
TN_DIMS = (((0,), (0,)), ((), ()))


def _rms(x, g):
    return x * lax.rsqrt(jnp.mean(x * x, axis=-1, keepdims=True) + EPS) * g


def _const_spec(shape):
    nd = len(shape)
    return pl.BlockSpec(shape, lambda *_: (0,) * nd, pipeline_mode=pl.Buffered(1))


def _params(n_grid):
    return pltpu.CompilerParams(
        dimension_semantics=("arbitrary",) * n_grid, vmem_limit_bytes=VMEM_LIMIT)


def _ffn_kernel(x_ref, g_ref, wg_ref, wu_ref, wd_ref, fin_ref, o_ref, *, splits, final_norm):
    x = x_ref[...]
    nb = _rms(x, g_ref[...]).astype(BF16)
    y = None
    for lo, hi in splits:
        g = jnp.dot(nb, wg_ref[:, lo:hi], preferred_element_type=F32)
        u = jnp.dot(nb, wu_ref[:, lo:hi], preferred_element_type=F32)
        h = (g * jax.nn.sigmoid(g) * u).astype(BF16)
        part = jnp.dot(h, wd_ref[lo:hi, :], preferred_element_type=F32)
        y = part if y is None else y + part
    out = x + 0.5 * y
    if final_norm:
        out = _rms(out, fin_ref[...])
    o_ref[...] = out


def _ffn(x, norm_g, wg, wu, wd, fin_g, final_norm):
    t, d = x.shape
    dff = wg.shape[1]
    half = (dff // 2 + 255) // 256 * 256
    splits = ((0, half), (half, dff))
    return pl.pallas_call(
        functools.partial(_ffn_kernel, splits=splits, final_norm=final_norm),
        grid=(t // TM_FFN,),
        in_specs=[
            pl.BlockSpec((TM_FFN, d), lambda i: (i, 0)),
            _const_spec((1, d)),
            _const_spec((d, dff)),
            _const_spec((d, dff)),
            _const_spec((dff, d)),
            _const_spec((1, d)),
        ],
        out_specs=pl.BlockSpec((TM_FFN, d), lambda i: (i, 0)),
        out_shape=jax.ShapeDtypeStruct((t, d), F32),
        compiler_params=_params(1),
        name="ffn",
    )(x, norm_g, wg, wu, wd, fin_g)


def _ones_rows(n_rows, rows_per_head, one_row):
    r = lax.broadcasted_iota(jnp.int32, (n_rows, 1), 0) % rows_per_head
    return jnp.where(r == one_row, 1.0, 0.0)


def _proj_kernel(h_ref, g_ref, wlat_ref, wdk_ref, wdqt_ref, wdvt_ref,
                 qn_ref, wqt_ref, kvn_ref, wk_ref, wvt_ref,
                 cqt_ref, sqt_ref, ck_ref, sk_ref,
                 qt_ref, k_ref, vt_ref, dqt_ref, dk_ref, dvt_ref, *, diff_scale):
    u = _rms(h_ref[...], g_ref[...]).astype(BF16)

    lat = jnp.dot(u, wlat_ref[...], preferred_element_type=F32)
    q_lat = lat[:, :MLA_Q_LORA]
    kv_lat = lat[:, MLA_Q_LORA:MLA_Q_LORA + MLA_KV_LORA]
    o = MLA_Q_LORA + MLA_KV_LORA
    kr_a = lat[:, o:o + LANES]
    kr_b = lat[:, o + LANES:o + 2 * LANES]

    qn = _rms(q_lat, qn_ref[...]).astype(BF16)
    kvn = _rms(kv_lat, kvn_ref[...]).astype(BF16)

    qqt = lax.dot_general(wqt_ref[...], qn, NT_DIMS, preferred_element_type=F32)
    cqt, sqt = cqt_ref[...], sqt_ref[...]
    r0, r1, r2 = MLA_NOPE, MLA_NOPE + MLA_ROPE // 2, MLA_NOPE + MLA_ROPE
    for hd in range(MLA_HEADS):
        rows = slice(hd * HEAD_SLOT, (hd + 1) * HEAD_SLOT)
        qh = qqt[rows]
        qh_sw = jnp.concatenate([qh[:r0], qh[r1:r2], qh[r0:r1], qh[r2:]], axis=0)
        qt_ref[0, rows, :] = (qh * cqt + qh_sw * sqt).astype(BF16)

    kk = jnp.dot(kvn, wk_ref[...], preferred_element_type=F32)
    kr = kr_a * ck_ref[...] + kr_b * sk_ref[...]
    for hd in range(MLA_HEADS):
        sl = slice(hd * HEAD_SLOT, (hd + 1) * HEAD_SLOT)
        k_ref[:, sl] = (kk[:, sl] + kr).astype(BF16)

    vt = lax.dot_general(wvt_ref[...], kvn, NT_DIMS, preferred_element_type=F32)
    vt_ref[0] = (vt + _ones_rows(vt.shape[0], MLA_VROWS, MLA_V)).astype(BF16)

    dqt = lax.dot_general(wdqt_ref[...], u, NT_DIMS, preferred_element_type=F32)
    dqt_ref[0] = (dqt * diff_scale).astype(BF16)
    dk_ref[...] = jnp.dot(u, wdk_ref[...], preferred_element_type=F32).astype(BF16)
    dvt = lax.dot_general(wdvt_ref[...], u, NT_DIMS, preferred_element_type=F32)
    dvt_ref[0] = (dvt + _ones_rows(dvt.shape[0], DIFF_VROWS, DIFF_V)).astype(BF16)


def _proj(h, g, wlat, wdk, wdqt, wdvt, qn, wqt, kvn, wk, wvt, tabs_t, tabs, batch, seq):
    t, d = h.shape
    nseq = seq // TM_PROJ
    tok = pl.BlockSpec((TM_PROJ, d), lambda i: (i, 0))
    tok_t = lambda rows: pl.BlockSpec((1, rows, TM_PROJ), lambda i: (i // nseq, 0, i % nseq))
    consts = [g, wlat, wdk, wdqt, wdvt, qn, wqt, kvn, wk, wvt]
    out_t = lambda rows: jax.ShapeDtypeStruct((batch, rows, seq), BF16)
    out_n = jax.ShapeDtypeStruct((t, d), BF16)
    return pl.pallas_call(
        functools.partial(_proj_kernel, diff_scale=DIFF_DIM ** -0.5 * LOG2E),
        grid=(t // TM_PROJ,),
        in_specs=[tok] + [_const_spec(c.shape) for c in consts]
        + [pl.BlockSpec((HEAD_SLOT, TM_PROJ), lambda i: (0, i % nseq))] * 2
        + [pl.BlockSpec((TM_PROJ, LANES), lambda i: (i % nseq, 0))] * 2,
        out_specs=[tok_t(wqt.shape[0]), tok, tok_t(wvt.shape[0]),
                   tok_t(wdqt.shape[0]), tok, tok_t(wdvt.shape[0])],
        out_shape=[out_t(wqt.shape[0]), out_n, out_t(wvt.shape[0]),
                   out_t(wdqt.shape[0]), out_n, out_t(wdvt.shape[0])],
        compiler_params=_params(1),
        name="in_proj",
    )(h, *consts, *tabs_t, *tabs)


def _t5_bucket(rel):
    half = N_BUCKETS // 2
    max_exact = half // 2
    n = jnp.abs(rel)
    n2 = n * n
    large = max_exact
    for k in range(1, half - max_exact):
        large = large + (n2 >= (max_exact * max_exact) << k).astype(jnp.int32)
    return jnp.where(rel > 0, half, 0) + jnp.where(n < max_exact, n, large)


def _far_distance():
    half = N_BUCKETS // 2
    max_exact = half // 2
    n = max_exact
    while max_exact + sum(n * n >= (max_exact * max_exact) << k
                          for k in range(1, half - max_exact)) < half - 1:
        n += 1
    return n


def _bias_kernel(tab_ref, o_ref):
    hd = pl.program_id(0)
    far = _far_distance()
    blk = 2 * CHUNK
    kloc = lax.broadcasted_iota(jnp.int32, (blk, blk), 0)
    qloc = lax.broadcasted_iota(jnp.int32, (blk, blk), 1)
    far_val = tab_ref[N_BUCKETS // 2 - 1, hd]
    for delta in range(2):
        for kb in range(TK // blk):
            for qb in range(TQ // blk):
                off = (kb - qb) * blk - delta * TK
                if delta == 0 and kb > qb:
                    val = jnp.full((blk, blk), NEG_INF, F32)
                elif off + blk - 1 <= -far:
                    val = jnp.zeros((blk, blk), F32)
                else:
                    bucket = _t5_bucket(kloc - qloc + off)
                    val = jnp.zeros((blk, blk), F32)
                    for b in range(N_BUCKETS):
                        val = jnp.where(bucket == b, tab_ref[b, hd], val)
                    val = (val - far_val) * LOG2E
                    if delta == 0 and kb == qb:
                        val = jnp.where(kloc // CHUNK <= qloc // CHUNK, val, NEG_INF)
                o_ref[0, delta, kb * blk:(kb + 1) * blk, qb * blk:(qb + 1) * blk] = val


def _bias_tiles(rel_bias):
    return pl.pallas_call(
        _bias_kernel,
        grid=(DIFF_HEADS,),
        in_specs=[pl.BlockSpec(memory_space=pltpu.SMEM)],
        out_specs=pl.BlockSpec((1, 2, TK, TQ), lambda h: (h, 0, 0, 0)),
        out_shape=jax.ShapeDtypeStruct((DIFF_HEADS, 2, TK, TQ), F32),
        compiler_params=_params(1),
        name="bias_tiles",
    )(rel_bias)


FAR, SUB, DIAG = 0, 1, 2


def _scores(k, qt, bias, s_ref, mc_ref):
    s = jnp.dot(k, qt, preferred_element_type=F32)
    if bias is not None:
        s = s + bias
    s_ref[:, :TQ] = s
    mc_ref[...] = jnp.max(s, axis=0, keepdims=True)


def _softmax_pv(s_ref, mc_ref, vt, m_ref, acc_ref):
    m_prev = m_ref[...]
    m_new = jnp.maximum(m_prev, mc_ref[...])
    alpha = jnp.exp2(m_prev - m_new)
    p = jnp.exp2(s_ref[:, :TQ] - m_new)
    pv = jnp.dot(vt, p.astype(BF16), preferred_element_type=F32)
    acc_ref[...] = alpha * acc_ref[...] + pv
    m_ref[...] = m_new


FIRST_SLOT = 2
N_SLOTS = 3


def _sweep(i, near, qk, consume, qk_next):
    n_near = len(near)

    def slot_of(n):
        return FIRST_SLOT if n == 0 else (n + 1) % 2

    def run(tiles, own_first, next_kind):
        if own_first:
            qk(*tiles[0])
        for n, (j, slot, _) in enumerate(tiles):
            if n + 1 < len(tiles):
                qk(*tiles[n + 1])
            elif next_kind is not None:
                qk_next(next_kind)
            consume(j, slot)

    def next_kind_after(count):
        return FAR if count + 1 > n_near else near[n_near - count - 1]

    for count in range(1, n_near + 1):
        @pl.when(i + 1 == count)
        def _(count=count):
            tiles = [(n, slot_of(n), kind) for n, kind in enumerate(near[-count:])]
            run(tiles, count <= 2, next_kind_after(count) if count > 1 else None)

    @pl.when(i + 1 > n_near)
    def _():
        n_far = i + 1 - n_near

        @pl.when(n_far == 1)
        def _():
            tiles = [(n, slot_of(n), kind) for n, kind in enumerate((FAR,) + near)]
            run(tiles, n_near <= 1, FAR)

        @pl.when(n_far >= 2)
        def _():
            rest = n_far - 1
            pairs = (rest - 1) // 2
            qk(1, 0, FAR)
            consume(0, FIRST_SLOT)

            def pair(j):
                qk(j + 1, 1, FAR)
                consume(j, 0)
                qk(j + 2, 0, FAR)
                consume(j + 1, 1)

            def body(t, carry):
                pair(4 * t + 1)
                pair(4 * t + 3)
                return carry

            lax.fori_loop(0, pairs // 2, body, 0)

            @pl.when(pairs % 2 == 1)
            def _():
                pair(2 * pairs - 1)

            j0 = 2 * pairs + 1
            for left in (1, 2):
                @pl.when(rest - 2 * pairs == left)
                def _(left=left):
                    tiles = [(j0 + n, n % 2, kind) for n, kind in enumerate((FAR,) * left + near)]
                    run(tiles, False, FAR)


def _kv_rows(j):
    return pl.ds(pl.multiple_of(j * TK, TK), TK)


def _attn_scratch(streams, v_rows):
    return [pltpu.VMEM((streams, N_SLOTS, TK, TQ + LANES), F32),
            pltpu.VMEM((streams, N_SLOTS, 1, TQ), F32),
            pltpu.VMEM((streams, 1, TQ), F32),
            pltpu.VMEM((streams, v_rows, TQ), F32)]


MLA_STREAMS = 2


def _q_cols(i):
    return pl.ds(pl.multiple_of(i * TQ, TQ), TQ)


def _mla_kernel(qt_ref, k_ref, vt_ref, mask_ref, ot_ref, s_ref, mc_ref, m_ref, acc_ref):
    def qk(j, slot, kind, qts):
        bias = mask_ref[...] if kind == DIAG else None
        for st in range(MLA_STREAMS):
            k = k_ref[0, _kv_rows(j), st * HEAD_SLOT:(st + 1) * HEAD_SLOT]
            _scores(k, qts[st], bias, s_ref.at[st, slot], mc_ref.at[st, slot])

    def consume(j, slot):
        for st in range(MLA_STREAMS):
            vt = vt_ref[0, st * MLA_VROWS:(st + 1) * MLA_VROWS, _kv_rows(j)]
            _softmax_pv(s_ref.at[st, slot], mc_ref.at[st, slot], vt, m_ref.at[st], acc_ref.at[st])

    def q_tile(i, carry):
        def q_tiles(t):
            return [qt_ref[0, st * HEAD_SLOT:(st + 1) * HEAD_SLOT, _q_cols(t)] for st in range(MLA_STREAMS)]

        qts_next = q_tiles(jnp.minimum(i + 1, n_q - 1))
        m_ref[...] = jnp.full(m_ref.shape, NEG_INF, F32)
        acc_ref[...] = jnp.zeros(acc_ref.shape, F32)
        _sweep(i, (DIAG,), functools.partial(qk, qts=q_tiles(i)), consume,
               lambda kind: qk(0, FIRST_SLOT, kind, qts_next))
        for st in range(MLA_STREAMS):
            acc = acc_ref[st]
            ot_ref[0, st * MLA_V:(st + 1) * MLA_V, _q_cols(i)] = (
                acc[:MLA_V] / acc[MLA_V:MLA_V + 1]).astype(BF16)
        return carry

    n_q = qt_ref.shape[2] // TQ
    lax.fori_loop(0, n_q, q_tile, 0)


def _mla_attention(qt, k, vt, mask):
    b, _, s = qt.shape
    groups = MLA_HEADS // MLA_STREAMS
    return pl.pallas_call(
        _mla_kernel,
        grid=(b, groups),
        in_specs=[pl.BlockSpec((1, MLA_STREAMS * HEAD_SLOT, s), lambda bi, h: (bi, h, 0)),
                  pl.BlockSpec((1, s, MLA_STREAMS * HEAD_SLOT), lambda bi, h: (bi, 0, h)),
                  pl.BlockSpec((1, MLA_STREAMS * MLA_VROWS, s), lambda bi, h: (bi, h, 0)),
                  _const_spec(mask.shape)],
        out_specs=pl.BlockSpec((1, MLA_STREAMS * MLA_V, s), lambda bi, h: (bi, h, 0)),
        out_shape=jax.ShapeDtypeStruct((b, MLA_HEADS * MLA_V, s), BF16),
        scratch_shapes=_attn_scratch(MLA_STREAMS, MLA_VROWS),
        compiler_params=_params(2),
        name="mla_attn",
    )(qt, k, vt, mask)


def _diff_kernel(qt_ref, k_ref, vt_ref, bias_ref, lq1_ref, lk1_ref, lq2_ref, lk2_ref, sub_ref,
                 ot_ref, s_ref, mc_ref, m_ref, acc_ref, *, lam_init):
    row = lax.broadcasted_iota(jnp.int32, (HEAD_SLOT, TQ), 0)
    zero = jnp.zeros((HEAD_SLOT, TQ), BF16)
    lam = (jnp.exp(jnp.sum(lq1_ref[...] * lk1_ref[...], axis=1, keepdims=True))
           - jnp.exp(jnp.sum(lq2_ref[...] * lk2_ref[...], axis=1, keepdims=True)) + lam_init)

    def qk(j, slot, kind, qts):
        k = k_ref[0, _kv_rows(j), :]
        bias = None if kind == FAR else bias_ref[0, 0 if kind == DIAG else 1]
        for st in range(2):
            _scores(k, qts[st], bias, s_ref.at[st, slot], mc_ref.at[st, slot])

    def consume(j, slot):
        vt = vt_ref[0, :, _kv_rows(j)]
        for st in range(2):
            _softmax_pv(s_ref.at[st, slot], mc_ref.at[st, slot], vt, m_ref.at[st], acc_ref.at[st])

    def q_tile(i, carry):
        def q_tiles(t):
            qt = qt_ref[0, :, _q_cols(t)]
            return [jnp.where(row < DIFF_DIM, qt, zero), jnp.where(row >= DIFF_DIM, qt, zero)]

        qts_next = q_tiles(jnp.minimum(i + 1, n_q - 1))
        m_ref[...] = jnp.full(m_ref.shape, NEG_INF, F32)
        acc_ref[...] = jnp.zeros(acc_ref.shape, F32)
        _sweep(i, (SUB, DIAG), functools.partial(qk, qts=q_tiles(i)), consume,
               lambda kind: qk(0, FIRST_SLOT, kind, qts_next))
        outs = [acc_ref[st, :DIFF_V] / acc_ref[st, DIFF_V:DIFF_V + 1] for st in range(2)]
        o = outs[0] - lam * outs[1]
        o = o * lax.rsqrt(jnp.mean(o * o, axis=0, keepdims=True) + EPS) * sub_ref[...]
        ot_ref[0, :, _q_cols(i)] = (o * (1.0 - lam_init)).astype(BF16)
        return carry

    n_q = qt_ref.shape[2] // TQ
    lax.fori_loop(0, n_q, q_tile, 0)


def _diff_attention(qt, k, vt, bias, lq1, lk1, lq2, lk2, sub_col, lam_init):
    b, _, s = qt.shape
    vec = _const_spec(lq1.shape)
    return pl.pallas_call(
        functools.partial(_diff_kernel, lam_init=lam_init),
        grid=(b, DIFF_HEADS),
        in_specs=[pl.BlockSpec((1, HEAD_SLOT, s), lambda bi, h: (bi, h, 0)),
                  pl.BlockSpec((1, s, HEAD_SLOT), lambda bi, h: (bi, 0, h)),
                  pl.BlockSpec((1, DIFF_VROWS, s), lambda bi, h: (bi, h, 0)),
                  pl.BlockSpec((1, 2, TK, TQ), lambda bi, h: (h, 0, 0, 0)),
                  vec, vec, vec, vec, _const_spec(sub_col.shape)],
        out_specs=pl.BlockSpec((1, DIFF_V, s), lambda bi, h: (bi, h, 0)),
        out_shape=jax.ShapeDtypeStruct((b, DIFF_HEADS * DIFF_V, s), BF16),
        scratch_shapes=_attn_scratch(2, DIFF_VROWS),
        compiler_params=_params(2),
        name="diff_attn",
    )(qt, k, vt, bias, lq1, lk1, lq2, lk2, sub_col)


def _memkv_kernel(m_ref, g_ref, w_ref, o_ref):
    mn = _rms(m_ref[...], g_ref[...]).astype(BF16)
    o_ref[...] = jnp.dot(mn, w_ref[...], preferred_element_type=F32).astype(BF16)


def _memkv(mem2d, g, w):
    rows, d = mem2d.shape
    return pl.pallas_call(
        _memkv_kernel,
        grid=(1,),
        in_specs=[_const_spec(mem2d.shape), _const_spec(g.shape), _const_spec(w.shape)],
        out_specs=_const_spec((rows, w.shape[1])),
        out_shape=jax.ShapeDtypeStruct((rows, w.shape[1]), BF16),
        compiler_params=_params(1),
        name="mem_kv",
    )(mem2d, g, w)


def _post_kernel(h_ref, oat_ref, obt_ref, g_ref, wgate_ref, wa_ref, wb_ref, wo_ref,
                 xg_ref, xwq_ref, kv_ref, xwo_ref, o_ref, *, scale):
    h = h_ref[...]
    d = h.shape[1]
    u = _rms(h, g_ref[...]).astype(BF16)
    gates = jax.nn.sigmoid(jnp.dot(u, wgate_ref[...], preferred_element_type=F32))
    ya = lax.dot_general(oat_ref[0], wa_ref[...], TN_DIMS, preferred_element_type=F32)
    yb = lax.dot_general(obt_ref[0], wb_ref[...], TN_DIMS, preferred_element_type=F32)
    merged = gates[:, :d] * ya + gates[:, d:] * yb
    h = h + jnp.dot(merged.astype(BF16), wo_ref[...], preferred_element_type=F32)

    x = _rms(h, xg_ref[...]).astype(BF16)
    q = (jnp.dot(x, xwq_ref[...], preferred_element_type=F32) * scale).astype(BF16)
    kv = kv_ref[...]
    outs = []
    for hd in range(XATTN_HEADS):
        qh = q[:, hd * XATTN_DIM:(hd + 1) * XATTN_DIM]
        kh = kv[:, 2 * hd * XATTN_DIM:(2 * hd + 1) * XATTN_DIM]
        vh = kv[:, (2 * hd + 1) * XATTN_DIM:(2 * hd + 2) * XATTN_DIM]
        s = lax.dot_general(qh, kh, NT_DIMS, preferred_element_type=F32)
        m = jnp.max(s, axis=1, keepdims=True)
        p = jnp.exp2(s - m)
        l = jnp.sum(p, axis=1, keepdims=True)
        oh = jnp.dot(p.astype(BF16), vh, preferred_element_type=F32) / l
        outs.append(oh.astype(BF16))
    o = jnp.concatenate(outs, axis=1)
    o_ref[...] = h + jnp.dot(o, xwo_ref[...], preferred_element_type=F32)


def _post(h, oat, obt, g, wgate, wa, wb, wo, xg, xwq, kvmem, xwo, seq, mem_len):
    t, d = h.shape
    nseq = seq // TM_PROJ
    tok = pl.BlockSpec((TM_PROJ, d), lambda i: (i, 0))
    tok_t = lambda rows: pl.BlockSpec((1, rows, TM_PROJ), lambda i: (i // nseq, 0, i % nseq))
    cs = lambda a: _const_spec(a.shape)
    return pl.pallas_call(
        functools.partial(_post_kernel, scale=XATTN_DIM ** -0.5 * LOG2E),
        grid=(t // TM_PROJ,),
        in_specs=[tok, tok_t(oat.shape[1]), tok_t(obt.shape[1]), cs(g), cs(wgate), cs(wa), cs(wb), cs(wo),
                  cs(xg), cs(xwq),
                  pl.BlockSpec((mem_len, kvmem.shape[1]), lambda i: (i // nseq, 0)), cs(xwo)],
        out_specs=tok,
        out_shape=jax.ShapeDtypeStruct((t, d), F32),
        compiler_params=_params(1),
        name="post_attn",
    )(h, oat, obt, g, wgate, wa, wb, wo, xg, xwq, kvmem, xwo)


def _pad_heads(w, heads, width, slot):
    k = w.shape[0]
    w = w.reshape(k, heads, width)
    return jnp.pad(w, ((0, 0), (0, 0), (0, slot - width))).reshape(k, heads * slot)


def _rope_tables(seq):
    half = MLA_ROPE // 2
    pos = jnp.arange(seq)
    freqs = ROPE_BASE ** (-jnp.arange(half, dtype=F32) / half)
    ang = pos.astype(F32)[:, None] * freqs[None, :]
    cos, sin = jnp.cos(ang), jnp.sin(ang)
    ones = jnp.ones((seq, MLA_NOPE), F32)
    z_nope = jnp.zeros((seq, MLA_NOPE), F32)
    z_pad = jnp.zeros((seq, HEAD_SLOT - MLA_NOPE - MLA_ROPE), F32)
    c_rope = jnp.concatenate([cos, cos], axis=1)
    s_rope = jnp.concatenate([-sin, sin], axis=1)
    qscale = (MLA_NOPE + MLA_ROPE) ** -0.5 * LOG2E
    cq = jnp.concatenate([ones, c_rope, z_pad], axis=1) * qscale
    sq = jnp.concatenate([z_nope, s_rope, z_pad], axis=1) * qscale
    ck = jnp.concatenate([z_nope, c_rope, z_pad], axis=1)
    sk = jnp.concatenate([z_nope, s_rope, z_pad], axis=1)
    return (cq.T, sq.T), (ck, sk)


def _swap_halves(w):
    half = w.shape[-1] // 2
    return jnp.concatenate([w[..., half:], w[..., :half]], axis=-1)


def kernel(x, mem, ffn1_norm, ffn1_w_gate, ffn1_w_up, ffn1_w_down, mix_norm, w_in, mla_q_norm, mla_w_q_up, mla_kv_norm, mla_w_kv_up, diff_lambda_q1, diff_lambda_k1, diff_lambda_q2, diff_lambda_k2, diff_sub_norm, rel_bias, w_branch_a, w_branch_b, w_out, xattn_norm, mem_norm, xattn_w_q, xattn_w_kv, xattn_w_o, ffn2_norm, ffn2_w_gate, ffn2_w_up, ffn2_w_down, final_norm):
    b, s, d = x.shape
    depth = ffn1_norm.shape[0]
    mem_len = mem.shape[1]
    t = b * s
    bf = lambda a: a.astype(BF16)
    row = lambda a: a.reshape(1, -1)

    tabs_t, tabs = _rope_tables(s)
    kpos = jnp.arange(TK)[:, None]
    qpos = jnp.arange(TQ)[None, :]
    mla_mask = jnp.where(kpos // CHUNK <= qpos // CHUNK, 0.0, NEG_INF).astype(F32)
    bias = _bias_tiles(rel_bias.astype(F32))

    h = x.reshape(t, d)
    for l in range(depth):
        h = _ffn(h, row(ffn1_norm[l]), bf(ffn1_w_gate[l]), bf(ffn1_w_up[l]), bf(ffn1_w_down[l]),
                 row(final_norm), final_norm=False)

        w = w_in[l]
        o_kv = MLA_Q_LORA
        o_kr = o_kv + MLA_KV_LORA
        o_dq = o_kr + MLA_ROPE
        w_kr = w[:, o_kr:o_dq]
        pad_l = jnp.zeros((d, MLA_NOPE), F32)
        pad_r = jnp.zeros((d, HEAD_SLOT - MLA_NOPE - MLA_ROPE), F32)
        wlat = jnp.concatenate([w[:, :o_kr], pad_l, w_kr, pad_r, pad_l, _swap_halves(w_kr), pad_r], axis=1)
        wdq = w[:, o_dq:o_dq + d]
        wdk = w[:, o_dq + d:o_dq + 2 * d]
        wdv = _pad_heads(w[:, o_dq + 2 * d:o_dq + 3 * d], DIFF_HEADS, DIFF_V, DIFF_VROWS)
        wgate = w[:, o_dq + 3 * d:]

        wq = _pad_heads(mla_w_q_up[l], MLA_HEADS, MLA_NOPE + MLA_ROPE, HEAD_SLOT)
        wkv3 = mla_w_kv_up[l].reshape(MLA_KV_LORA, MLA_HEADS, MLA_NOPE + MLA_V)
        wk = _pad_heads(wkv3[..., :MLA_NOPE].reshape(MLA_KV_LORA, -1), MLA_HEADS, MLA_NOPE, HEAD_SLOT)
        wv = _pad_heads(wkv3[..., MLA_NOPE:].reshape(MLA_KV_LORA, -1), MLA_HEADS, MLA_V, MLA_VROWS)

        qt, k, vt, dqt, dk, dvt = _proj(
            h, row(mix_norm[l]), bf(wlat), bf(wdk), bf(wdq.T), bf(wdv.T),
            row(mla_q_norm[l]), bf(wq.T), row(mla_kv_norm[l]), bf(wk), bf(wv.T),
            tabs_t, tabs, b, s)

        oat = _mla_attention(qt, k.reshape(b, s, d), vt, mla_mask)
        lam_init = 0.8 - 0.6 * math.exp(-0.3 * l)
        obt = _diff_attention(dqt, dk.reshape(b, s, d), dvt, bias,
                              row(diff_lambda_q1[l]), row(diff_lambda_k1[l]),
                              row(diff_lambda_q2[l]), row(diff_lambda_k2[l]),
                              diff_sub_norm[l].reshape(-1, 1), lam_init)

        kvmem = _memkv(mem.reshape(b * mem_len, d), row(mem_norm[l]), bf(xattn_w_kv[l]))
        h = _post(h, oat, obt, row(mix_norm[l]), bf(wgate), bf(w_branch_a[l]), bf(w_branch_b[l]),
                  bf(w_out[l]), row(xattn_norm[l]), bf(xattn_w_q[l]), kvmem, bf(xattn_w_o[l]), s, mem_len)

        last = l == depth - 1
        h = _ffn(h, row(ffn2_norm[l]), bf(ffn2_w_gate[l]), bf(ffn2_w_up[l]), bf(ffn2_w_down[l]),
                 row(final_norm), final_norm=last)
    return h.reshape(b, s, d)
```

```python
import functools
import math

import jax
import jax.numpy as jnp
from jax import lax
from jax.experimental import pallas as pl
from jax.experimental.pallas import tpu as pltpu

F32 = jnp.float32
BF16 = jnp.bfloat16

CHUNK = 64
EPS = 1e-6
NEG_INF = -1e30
MLA_HEADS = 8
MLA_Q_LORA = 384
MLA_KV_LORA = 256
MLA_NOPE = 64
MLA_ROPE = 32
MLA_V = 64
ROPE_BASE = 10000.0
DIFF_HEADS = 8
DIFF_DIM = 64
DIFF_V = 2 * DIFF_DIM
N_BUCKETS = 32
XATTN_HEADS = 4
XATTN_DIM = 128
LOG2E = math.log2(math.e)

LANES = 128
HEAD_SLOT = LANES
BF16_ROWS = 16
VMEM_LIMIT = 56 * 1024 * 1024

TM_FFN = 512
TM_PROJ = 512
TQ = 512
TK = 512

MLA_VROWS = (MLA_V + 1 + BF16_ROWS - 1) // BF16_ROWS * BF16_ROWS
DIFF_VROWS = (DIFF_V + 1 + BF16_ROWS - 1) // BF16_ROWS * BF16_ROWS

NT_DIMS = (((1,), (1,)), ((), ()))
TN_DIMS = (((0,), (0,)), ((), ()))


def _rms(x, g):
    return x * lax.rsqrt(jnp.mean(x * x, axis=-1, keepdims=True) + EPS) * g


def _const_spec(shape):
    nd = len(shape)
    return pl.BlockSpec(shape, lambda *_: (0,) * nd, pipeline_mode=pl.Buffered(1))


def _params(n_grid):
    return pltpu.CompilerParams(
        dimension_semantics=("arbitrary",) * n_grid, vmem_limit_bytes=VMEM_LIMIT)


def _ffn_splits(dff):
    half = (dff // 2 + 255) // 256 * 256
    return ((0, half), (half, dff))


def _ffn_half_step(x, g_ref, wg_ref, wu_ref, wd_ref):
    nb = _rms(x, g_ref[...]).astype(BF16)
    y = None
    for lo, hi in _ffn_splits(wg_ref.shape[1]):
        g = jnp.dot(nb, wg_ref[:, lo:hi], preferred_element_type=F32)
        u = jnp.dot(nb, wu_ref[:, lo:hi], preferred_element_type=F32)
        h = (g * jax.nn.sigmoid(g) * u).astype(BF16)
        part = jnp.dot(h, wd_ref[lo:hi, :], preferred_element_type=F32)
        y = part if y is None else y + part
    return x + 0.5 * y


def _ffn_kernel(x_ref, g_ref, wg_ref, wu_ref, wd_ref, *refs):
    n_cast = (len(refs) - 1) // 2
    o_ref = refs[n_cast]
    o_ref[...] = _ffn_half_step(x_ref[...], g_ref, wg_ref, wu_ref, wd_ref)
    for src, dst in zip(refs[:n_cast], refs[n_cast + 1:]):
        dst[...] = src[...].astype(BF16)


def _cast_band(rows, steps):
    band = -(-rows // steps)
    band = -(-band // BF16_ROWS) * BF16_ROWS
    while rows % band:
        band += BF16_ROWS
    return band


def _ffn(x, norm_g, wg, wu, wd, to_cast):
    t, d = x.shape
    steps = t // TM_FFN
    tok = pl.BlockSpec((TM_FFN, d), lambda i: (i, 0))
    cast_specs = []
    for a in to_cast:
        band = _cast_band(a.shape[0], steps)
        last = a.shape[0] // band - 1
        cast_specs.append(pl.BlockSpec((band, a.shape[1]), lambda i, last=last: (jnp.minimum(i, last), 0)))
    outs = pl.pallas_call(
        _ffn_kernel,
        grid=(steps,),
        in_specs=[tok] + [_const_spec(a.shape) for a in (norm_g, wg, wu, wd)] + cast_specs,
        out_specs=[tok] + cast_specs,
        out_shape=[jax.ShapeDtypeStruct((t, d), F32)]
        + [jax.ShapeDtypeStruct(a.shape, BF16) for a in to_cast],
        compiler_params=_params(1),
        name="ffn",
    )(x, norm_g, wg, wu, wd, *to_cast)
    return outs[0], outs[1:]


def _ones_rows(n_rows, rows_per_head, one_row):
    r = lax.broadcasted_iota(jnp.int32, (n_rows, 1), 0) % rows_per_head
    return jnp.where(r == one_row, 1.0, 0.0)


def _proj_kernel(h_ref, g_ref, wlat_ref, wdk_ref, wdqt_ref, wdvt_ref,
                 qn_ref, wqt_ref, kvn_ref, wk_ref, wvt_ref,
                 ct_ref, st_ref,
                 qt_ref, k_ref, vt_ref, dqt_ref, dk_ref, dvt_ref, *, q_scale, diff_scale):
    u = _rms(h_ref[...], g_ref[...]).astype(BF16)

    lat = jnp.dot(u, wlat_ref[...], preferred_element_type=F32)
    q_lat = lat[:, :MLA_Q_LORA]
    kv_lat = lat[:, MLA_Q_LORA:MLA_Q_LORA + MLA_KV_LORA]
    o = MLA_Q_LORA + MLA_KV_LORA
    kr_a = lat[:, o:o + LANES]
    kr_b = lat[:, o + LANES:o + 2 * LANES]

    qn = _rms(q_lat, qn_ref[...]).astype(BF16)
    kvn = _rms(kv_lat, kvn_ref[...]).astype(BF16)

    qqt = lax.dot_general(wqt_ref[...], qn, NT_DIMS, preferred_element_type=F32)
    ct, st = ct_ref[...], st_ref[...]
    nope_rows = lax.broadcasted_iota(jnp.int32, (HEAD_SLOT, 1), 0) < MLA_NOPE
    cqt = (ct + jnp.where(nope_rows, 1.0, 0.0)) * q_scale
    sqt = st * q_scale
    r0, r1, r2 = MLA_NOPE, MLA_NOPE + MLA_ROPE // 2, MLA_NOPE + MLA_ROPE
    for hd in range(MLA_HEADS):
        rows = slice(hd * HEAD_SLOT, (hd + 1) * HEAD_SLOT)
        qh = qqt[rows]
        qh_sw = jnp.concatenate([qh[:r0], qh[r1:r2], qh[r0:r1], qh[r2:]], axis=0)
        qt_ref[0, rows, :] = (qh * cqt + qh_sw * sqt).astype(BF16)

    kk = jnp.dot(kvn, wk_ref[...], preferred_element_type=F32)
    kr = kr_a * ct.T + kr_b * st.T
    for hd in range(MLA_HEADS):
        sl = slice(hd * HEAD_SLOT, (hd + 1) * HEAD_SLOT)
        k_ref[:, sl] = (kk[:, sl] + kr).astype(BF16)

    vt = lax.dot_general(wvt_ref[...], kvn, NT_DIMS, preferred_element_type=F32)
    vt_ref[0] = (vt + _ones_rows(vt.shape[0], MLA_VROWS, MLA_V)).astype(BF16)

    dqt = lax.dot_general(wdqt_ref[...], u, NT_DIMS, preferred_element_type=F32)
    dqt_ref[0] = (dqt * diff_scale).astype(BF16)
    dk_ref[...] = jnp.dot(u, wdk_ref[...], preferred_element_type=F32).astype(BF16)
    dvt = lax.dot_general(wdvt_ref[...], u, NT_DIMS, preferred_element_type=F32)
    dvt_ref[0] = (dvt + _ones_rows(dvt.shape[0], DIFF_VROWS, DIFF_V)).astype(BF16)


def _proj(h, g, wlat, wdk, wdqt, wdvt, qn, wqt, kvn, wk, wvt, tabs_t, batch, seq):
    t, d = h.shape
    nseq = seq // TM_PROJ
    tok = pl.BlockSpec((TM_PROJ, d), lambda i: (i, 0))
    tok_t = lambda rows: pl.BlockSpec((1, rows, TM_PROJ), lambda i: (i // nseq, 0, i % nseq))
    consts = [g, wlat, wdk, wdqt, wdvt, qn, wqt, kvn, wk, wvt]
    out_t = lambda rows: jax.ShapeDtypeStruct((batch, rows, seq), BF16)
    out_n = jax.ShapeDtypeStruct((t, d), BF16)
    return pl.pallas_call(
        functools.partial(_proj_kernel, q_scale=(MLA_NOPE + MLA_ROPE) ** -0.5 * LOG2E,
                          diff_scale=DIFF_DIM ** -0.5 * LOG2E),
        grid=(t // TM_PROJ,),
        in_specs=[tok] + [_const_spec(c.shape) for c in consts]
        + [pl.BlockSpec((HEAD_SLOT, TM_PROJ), lambda i: (0, i % nseq))] * 2,
        out_specs=[tok_t(wqt.shape[0]), tok, tok_t(wvt.shape[0]),
                   tok_t(wdqt.shape[0]), tok, tok_t(wdvt.shape[0])],
        out_shape=[out_t(wqt.shape[0]), out_n, out_t(wvt.shape[0]),
                   out_t(wdqt.shape[0]), out_n, out_t(wdvt.shape[0])],
        compiler_params=_params(1),
        name="in_proj",
    )(h, *consts, *tabs_t)


def _t5_bucket(rel):
    half = N_BUCKETS // 2
    max_exact = half // 2
    n = jnp.abs(rel)
    n2 = n * n
    large = max_exact
    for k in range(1, half - max_exact):
        large = large + (n2 >= (max_exact * max_exact) << k).astype(jnp.int32)
    return jnp.where(rel > 0, half, 0) + jnp.where(n < max_exact, n, large)


def _far_distance():
    half = N_BUCKETS // 2
    max_exact = half // 2
    n = max_exact
    while max_exact + sum(n * n >= (max_exact * max_exact) << k
                          for k in range(1, half - max_exact)) < half - 1:
        n += 1
    return n


def _bias_kernel(tab_ref, o_ref):
    hd = pl.program_id(0)
    far = _far_distance()
    blk = 2 * CHUNK
    kloc = lax.broadcasted_iota(jnp.int32, (blk, blk), 0)
    qloc = lax.broadcasted_iota(jnp.int32, (blk, blk), 1)
    far_val = tab_ref[N_BUCKETS // 2 - 1, hd]
    for delta in range(2):
        for kb in range(TK // blk):
            for qb in range(TQ // blk):
                off = (kb - qb) * blk - delta * TK
                if delta == 0 and kb > qb:
                    val = jnp.full((blk, blk), NEG_INF, F32)
                elif off + blk - 1 <= -far:
                    val = jnp.zeros((blk, blk), F32)
                else:
                    bucket = _t5_bucket(kloc - qloc + off)
                    val = jnp.zeros((blk, blk), F32)
                    for b in range(N_BUCKETS):
                        val = jnp.where(bucket == b, tab_ref[b, hd], val)
                    val = (val - far_val) * LOG2E
                    if delta == 0 and kb == qb:
                        val = jnp.where(kloc // CHUNK <= qloc // CHUNK, val, NEG_INF)
                o_ref[0, delta, kb * blk:(kb + 1) * blk, qb * blk:(qb + 1) * blk] = val


def _bias_tiles(rel_bias):
    return pl.pallas_call(
        _bias_kernel,
        grid=(DIFF_HEADS,),
        in_specs=[pl.BlockSpec(memory_space=pltpu.SMEM)],
        out_specs=pl.BlockSpec((1, 2, TK, TQ), lambda h: (h, 0, 0, 0)),
        out_shape=jax.ShapeDtypeStruct((DIFF_HEADS, 2, TK, TQ), F32),
        compiler_params=_params(1),
        name="bias_tiles",
    )(rel_bias)


FAR, SUB, DIAG = 0, 1, 2


def _scores(k, qt, bias, s_ref, mc_ref):
    s = jnp.dot(k, qt, preferred_element_type=F32)
    if bias is not None:
        s = s + bias
    s_ref[:, :TQ] = s
    mc_ref[...] = jnp.max(s, axis=0, keepdims=True)


def _softmax_pv(s_ref, mc_ref, vt, m_ref, acc_ref):
    m_prev = m_ref[...]
    m_new = jnp.maximum(m_prev, mc_ref[...])
    alpha = jnp.exp2(m_prev - m_new)
    p = jnp.exp2(s_ref[:, :TQ] - m_new)
    pv = jnp.dot(vt, p.astype(BF16), preferred_element_type=F32)
    acc_ref[...] = alpha * acc_ref[...] + pv
    m_ref[...] = m_new


FIRST_SLOT = 2
N_SLOTS = 3


def _sweep(i, near, qk, consume, qk_next):
    n_near = len(near)

    def slot_of(n):
        return FIRST_SLOT if n == 0 else (n + 1) % 2

    def run(tiles, own_first, next_kind):
        if own_first:
            qk(*tiles[0])
        for n, (j, slot, _) in enumerate(tiles):
            if n + 1 < len(tiles):
                qk(*tiles[n + 1])
            elif next_kind is not None:
                qk_next(next_kind)
            consume(j, slot)

    def next_kind_after(count):
        return FAR if count + 1 > n_near else near[n_near - count - 1]

    for count in range(1, n_near + 1):
        @pl.when(i + 1 == count)
        def _(count=count):
            tiles = [(n, slot_of(n), kind) for n, kind in enumerate(near[-count:])]
            run(tiles, count <= 2, next_kind_after(count) if count > 1 else None)

    @pl.when(i + 1 > n_near)
    def _():
        n_far = i + 1 - n_near

        @pl.when(n_far == 1)
        def _():
            tiles = [(n, slot_of(n), kind) for n, kind in enumerate((FAR,) + near)]
            run(tiles, n_near <= 1, FAR)

        @pl.when(n_far >= 2)
        def _():
            rest = n_far - 1
            pairs = (rest - 1) // 2
            qk(1, 0, FAR)
            consume(0, FIRST_SLOT)

            def pair(j):
                qk(j + 1, 1, FAR)
                consume(j, 0)
                qk(j + 2, 0, FAR)
                consume(j + 1, 1)

            def body(t, carry):
                pair(4 * t + 1)
                pair(4 * t + 3)
                return carry

            lax.fori_loop(0, pairs // 2, body, 0)

            @pl.when(pairs % 2 == 1)
            def _():
                pair(2 * pairs - 1)

            j0 = 2 * pairs + 1
            for left in (1, 2):
                @pl.when(rest - 2 * pairs == left)
                def _(left=left):
                    tiles = [(j0 + n, n % 2, kind) for n, kind in enumerate((FAR,) * left + near)]
                    run(tiles, False, FAR)


def _kv_rows(j):
    return pl.ds(pl.multiple_of(j * TK, TK), TK)


def _attn_scratch(streams, v_rows):
    return [pltpu.VMEM((streams, N_SLOTS, TK, TQ + LANES), F32),
            pltpu.VMEM((streams, N_SLOTS, 1, TQ), F32),
            pltpu.VMEM((streams, 1, TQ), F32),
            pltpu.VMEM((streams, v_rows, TQ), F32)]


MLA_STREAMS = 2


def _q_cols(i):
    return pl.ds(pl.multiple_of(i * TQ, TQ), TQ)


def _mla_kernel(qt_ref, k_ref, vt_ref, mask_ref, ot_ref, s_ref, mc_ref, m_ref, acc_ref):
    def qk(j, slot, kind, qts):
        bias = mask_ref[...] if kind == DIAG else None
        for st in range(MLA_STREAMS):
            k = k_ref[0, _kv_rows(j), st * HEAD_SLOT:(st + 1) * HEAD_SLOT]
            _scores(k, qts[st], bias, s_ref.at[st, slot], mc_ref.at[st, slot])

    def consume(j, slot):
        for st in range(MLA_STREAMS):
            vt = vt_ref[0, st * MLA_VROWS:(st + 1) * MLA_VROWS, _kv_rows(j)]
            _softmax_pv(s_ref.at[st, slot], mc_ref.at[st, slot], vt, m_ref.at[st], acc_ref.at[st])

    def q_tile(i, carry):
        def q_tiles(t):
            return [qt_ref[0, st * HEAD_SLOT:(st + 1) * HEAD_SLOT, _q_cols(t)] for st in range(MLA_STREAMS)]

        qts_next = q_tiles(jnp.minimum(i + 1, n_q - 1))
        m_ref[...] = jnp.full(m_ref.shape, NEG_INF, F32)
        acc_ref[...] = jnp.zeros(acc_ref.shape, F32)
        _sweep(i, (DIAG,), functools.partial(qk, qts=q_tiles(i)), consume,
               lambda kind: qk(0, FIRST_SLOT, kind, qts_next))
        for st in range(MLA_STREAMS):
            acc = acc_ref[st]
            ot_ref[0, st * MLA_V:(st + 1) * MLA_V, _q_cols(i)] = (
                acc[:MLA_V] / acc[MLA_V:MLA_V + 1]).astype(BF16)
        return carry

    n_q = qt_ref.shape[2] // TQ
    lax.fori_loop(0, n_q, q_tile, 0)


def _mla_attention(qt, k, vt, mask):
    b, _, s = qt.shape
    groups = MLA_HEADS // MLA_STREAMS
    return pl.pallas_call(
        _mla_kernel,
        grid=(b, groups),
        in_specs=[pl.BlockSpec((1, MLA_STREAMS * HEAD_SLOT, s), lambda bi, h: (bi, h, 0)),
                  pl.BlockSpec((1, s, MLA_STREAMS * HEAD_SLOT), lambda bi, h: (bi, 0, h)),
                  pl.BlockSpec((1, MLA_STREAMS * MLA_VROWS, s), lambda bi, h: (bi, h, 0)),
                  _const_spec(mask.shape)],
        out_specs=pl.BlockSpec((1, MLA_STREAMS * MLA_V, s), lambda bi, h: (bi, h, 0)),
        out_shape=jax.ShapeDtypeStruct((b, MLA_HEADS * MLA_V, s), BF16),
        scratch_shapes=_attn_scratch(MLA_STREAMS, MLA_VROWS),
        compiler_params=_params(2),
        name="mla_attn",
    )(qt, k, vt, mask)


def _diff_kernel(qt_ref, k_ref, vt_ref, bias_ref, lq1_ref, lk1_ref, lq2_ref, lk2_ref, sub_ref,
                 ot_ref, s_ref, mc_ref, m_ref, acc_ref, *, lam_init):
    row = lax.broadcasted_iota(jnp.int32, (HEAD_SLOT, TQ), 0)
    zero = jnp.zeros((HEAD_SLOT, TQ), BF16)
    lam = (jnp.exp(jnp.sum(lq1_ref[...] * lk1_ref[...], axis=1, keepdims=True))
           - jnp.exp(jnp.sum(lq2_ref[...] * lk2_ref[...], axis=1, keepdims=True)) + lam_init)

    def qk(j, slot, kind, qts):
        k = k_ref[0, _kv_rows(j), :]
        bias = None if kind == FAR else bias_ref[0, 0 if kind == DIAG else 1]
        for st in range(2):
            _scores(k, qts[st], bias, s_ref.at[st, slot], mc_ref.at[st, slot])

    def consume(j, slot):
        vt = vt_ref[0, :, _kv_rows(j)]
        for st in range(2):
            _softmax_pv(s_ref.at[st, slot], mc_ref.at[st, slot], vt, m_ref.at[st], acc_ref.at[st])

    def q_tile(i, carry):
        def q_tiles(t):
            qt = qt_ref[0, :, _q_cols(t)]
            return [jnp.where(row < DIFF_DIM, qt, zero), jnp.where(row >= DIFF_DIM, qt, zero)]

        qts_next = q_tiles(jnp.minimum(i + 1, n_q - 1))
        m_ref[...] = jnp.full(m_ref.shape, NEG_INF, F32)
        acc_ref[...] = jnp.zeros(acc_ref.shape, F32)
        _sweep(i, (SUB, DIAG), functools.partial(qk, qts=q_tiles(i)), consume,
               lambda kind: qk(0, FIRST_SLOT, kind, qts_next))
        outs = [acc_ref[st, :DIFF_V] / acc_ref[st, DIFF_V:DIFF_V + 1] for st in range(2)]
        o = outs[0] - lam * outs[1]
        o = o * lax.rsqrt(jnp.mean(o * o, axis=0, keepdims=True) + EPS) * sub_ref[...]
        ot_ref[0, :, _q_cols(i)] = (o * (1.0 - lam_init)).astype(BF16)
        return carry

    n_q = qt_ref.shape[2] // TQ
    lax.fori_loop(0, n_q, q_tile, 0)


def _diff_attention(qt, k, vt, bias, lq1, lk1, lq2, lk2, sub_col, lam_init):
    b, _, s = qt.shape
    vec = _const_spec(lq1.shape)
    return pl.pallas_call(
        functools.partial(_diff_kernel, lam_init=lam_init),
        grid=(b, DIFF_HEADS),
        in_specs=[pl.BlockSpec((1, HEAD_SLOT, s), lambda bi, h: (bi, h, 0)),
                  pl.BlockSpec((1, s, HEAD_SLOT), lambda bi, h: (bi, 0, h)),
                  pl.BlockSpec((1, DIFF_VROWS, s), lambda bi, h: (bi, h, 0)),
                  pl.BlockSpec((1, 2, TK, TQ), lambda bi, h: (h, 0, 0, 0)),
                  vec, vec, vec, vec, _const_spec(sub_col.shape)],
        out_specs=pl.BlockSpec((1, DIFF_V, s), lambda bi, h: (bi, h, 0)),
        out_shape=jax.ShapeDtypeStruct((b, DIFF_HEADS * DIFF_V, s), BF16),
        scratch_shapes=_attn_scratch(2, DIFF_VROWS),
        compiler_params=_params(2),
        name="diff_attn",
    )(qt, k, vt, bias, lq1, lk1, lq2, lk2, sub_col)


def _memkv_kernel(m_ref, g_ref, w_ref, o_ref):
    mn = _rms(m_ref[...], g_ref[...]).astype(BF16)
    o_ref[...] = jnp.dot(mn, w_ref[...], preferred_element_type=F32).astype(BF16)


def _memkv(mem2d, g, w):
    rows, d = mem2d.shape
    return pl.pallas_call(
        _memkv_kernel,
        grid=(1,),
        in_specs=[_const_spec(mem2d.shape), _const_spec(g.shape), _const_spec(w.shape)],
        out_specs=_const_spec((rows, w.shape[1])),
        out_shape=jax.ShapeDtypeStruct((rows, w.shape[1]), BF16),
        compiler_params=_params(1),
        name="mem_kv",
    )(mem2d, g, w)


def _post_kernel(h_ref, oat_ref, obt_ref, g_ref, wgate_ref, wa_ref, wb_ref, wo_ref,
                 xg_ref, xwq_ref, kv_ref, xwo_ref, fg_ref, fwg_ref, fwu_ref, fwd_ref, fin_ref,
                 o_ref, *, scale, final_norm):
    h = h_ref[...]
    d = h.shape[1]
    u = _rms(h, g_ref[...]).astype(BF16)
    gates = jax.nn.sigmoid(jnp.dot(u, wgate_ref[...], preferred_element_type=F32))
    ya = lax.dot_general(oat_ref[0], wa_ref[...], TN_DIMS, preferred_element_type=F32)
    yb = lax.dot_general(obt_ref[0], wb_ref[...], TN_DIMS, preferred_element_type=F32)
    merged = gates[:, :d] * ya + gates[:, d:] * yb
    h = h + jnp.dot(merged.astype(BF16), wo_ref[...], preferred_element_type=F32)

    x = _rms(h, xg_ref[...]).astype(BF16)
    q = (jnp.dot(x, xwq_ref[...], preferred_element_type=F32) * scale).astype(BF16)
    kv = kv_ref[...]
    outs = []
    for hd in range(XATTN_HEADS):
        qh = q[:, hd * XATTN_DIM:(hd + 1) * XATTN_DIM]
        kh = kv[:, 2 * hd * XATTN_DIM:(2 * hd + 1) * XATTN_DIM]
        vh = kv[:, (2 * hd + 1) * XATTN_DIM:(2 * hd + 2) * XATTN_DIM]
        s = lax.dot_general(qh, kh, NT_DIMS, preferred_element_type=F32)
        m = jnp.max(s, axis=1, keepdims=True)
        p = jnp.exp2(s - m)
        l = jnp.sum(p, axis=1, keepdims=True)
        oh = jnp.dot(p.astype(BF16), vh, preferred_element_type=F32) / l
        outs.append(oh.astype(BF16))
    o = jnp.concatenate(outs, axis=1)
    h = h + jnp.dot(o, xwo_ref[...], preferred_element_type=F32)

    h = _ffn_half_step(h, fg_ref, fwg_ref, fwu_ref, fwd_ref)
    if final_norm:
        h = _rms(h, fin_ref[...])
    o_ref[...] = h


def _post(h, oat, obt, g, wgate, wa, wb, wo, xg, xwq, kvmem, xwo, ffn, fin_g, final_norm, seq, mem_len):
    t, d = h.shape
    nseq = seq // TM_PROJ
    tok = pl.BlockSpec((TM_PROJ, d), lambda i: (i, 0))
    tok_t = lambda rows: pl.BlockSpec((1, rows, TM_PROJ), lambda i: (i // nseq, 0, i % nseq))
    cs = lambda a: _const_spec(a.shape)
    return pl.pallas_call(
        functools.partial(_post_kernel, scale=XATTN_DIM ** -0.5 * LOG2E, final_norm=final_norm),
        grid=(t // TM_PROJ,),
        in_specs=[tok, tok_t(oat.shape[1]), tok_t(obt.shape[1]), cs(g), cs(wgate), cs(wa), cs(wb), cs(wo),
                  cs(xg), cs(xwq),
                  pl.BlockSpec((mem_len, kvmem.shape[1]), lambda i: (i // nseq, 0)), cs(xwo)]
        + [cs(a) for a in ffn] + [cs(fin_g)],
        out_specs=tok,
        out_shape=jax.ShapeDtypeStruct((t, d), F32),
        compiler_params=_params(1),
        name="post_attn",
    )(h, oat, obt, g, wgate, wa, wb, wo, xg, xwq, kvmem, xwo, *ffn, fin_g)


def _pad_heads(w, heads, width, slot):
    k = w.shape[0]
    w = w.reshape(k, heads, width)
    return jnp.pad(w, ((0, 0), (0, 0), (0, slot - width))).reshape(k, heads * slot)


def _rope_tables(seq):
    half = MLA_ROPE // 2
    pos = jnp.arange(seq)
    freqs = ROPE_BASE ** (-jnp.arange(half, dtype=F32) / half)
    ang = freqs[:, None] * pos.astype(F32)[None, :]
    cos, sin = jnp.cos(ang), jnp.sin(ang)
    z_nope = jnp.zeros((MLA_NOPE, seq), F32)
    z_pad = jnp.zeros((HEAD_SLOT - MLA_NOPE - MLA_ROPE, seq), F32)
    return (jnp.concatenate([z_nope, cos, cos, z_pad], axis=0),
            jnp.concatenate([z_nope, -sin, sin, z_pad], axis=0))


def _swap_halves(w):
    half = w.shape[-1] // 2
    return jnp.concatenate([w[..., half:], w[..., :half]], axis=-1)


def kernel(x, mem, ffn1_norm, ffn1_w_gate, ffn1_w_up, ffn1_w_down, mix_norm, w_in, mla_q_norm, mla_w_q_up, mla_kv_norm, mla_w_kv_up, diff_lambda_q1, diff_lambda_k1, diff_lambda_q2, diff_lambda_k2, diff_sub_norm, rel_bias, w_branch_a, w_branch_b, w_out, xattn_norm, mem_norm, xattn_w_q, xattn_w_kv, xattn_w_o, ffn2_norm, ffn2_w_gate, ffn2_w_up, ffn2_w_down, final_norm):
    b, s, d = x.shape
    depth = ffn1_norm.shape[0]
    mem_len = mem.shape[1]
    t = b * s
    bf = lambda a: a.astype(BF16)
    row = lambda a: a.reshape(1, -1)

    tabs_t = _rope_tables(s)
    kpos = jnp.arange(TK)[:, None]
    qpos = jnp.arange(TQ)[None, :]
    mla_mask = jnp.where(kpos // CHUNK <= qpos // CHUNK, 0.0, NEG_INF).astype(F32)
    bias = _bias_tiles(rel_bias.astype(F32))

    h = x.reshape(t, d)
    for l in range(depth):
        h, (f2g, f2u, f2d, w) = _ffn(
            h, row(ffn1_norm[l]), bf(ffn1_w_gate[l]), bf(ffn1_w_up[l]), bf(ffn1_w_down[l]),
            (ffn2_w_gate[l], ffn2_w_up[l], ffn2_w_down[l], w_in[l]))

        o_kv = MLA_Q_LORA
        o_kr = o_kv + MLA_KV_LORA
        o_dq = o_kr + MLA_ROPE
        w_kr = w[:, o_kr:o_dq]
        pad_l = jnp.zeros((d, MLA_NOPE), BF16)
        pad_r = jnp.zeros((d, HEAD_SLOT - MLA_NOPE - MLA_ROPE), BF16)
        wlat = jnp.concatenate([w[:, :o_kr], pad_l, w_kr, pad_r, pad_l, _swap_halves(w_kr), pad_r], axis=1)
        wdq = w[:, o_dq:o_dq + d]
        wdk = w[:, o_dq + d:o_dq + 2 * d]
        wdv = _pad_heads(w[:, o_dq + 2 * d:o_dq + 3 * d], DIFF_HEADS, DIFF_V, DIFF_VROWS)
        wgate = w[:, o_dq + 3 * d:]

        wq = _pad_heads(mla_w_q_up[l], MLA_HEADS, MLA_NOPE + MLA_ROPE, HEAD_SLOT)
        wkv3 = mla_w_kv_up[l].reshape(MLA_KV_LORA, MLA_HEADS, MLA_NOPE + MLA_V)
        wk = _pad_heads(wkv3[..., :MLA_NOPE].reshape(MLA_KV_LORA, -1), MLA_HEADS, MLA_NOPE, HEAD_SLOT)
        wv = _pad_heads(wkv3[..., MLA_NOPE:].reshape(MLA_KV_LORA, -1), MLA_HEADS, MLA_V, MLA_VROWS)

        qt, k, vt, dqt, dk, dvt = _proj(
            h, row(mix_norm[l]), bf(wlat), bf(wdk), bf(wdq.T), bf(wdv.T),
            row(mla_q_norm[l]), bf(wq.T), row(mla_kv_norm[l]), bf(wk), bf(wv.T),
            tabs_t, b, s)

        oat = _mla_attention(qt, k.reshape(b, s, d), vt, mla_mask)
        lam_init = 0.8 - 0.6 * math.exp(-0.3 * l)
        obt = _diff_attention(dqt, dk.reshape(b, s, d), dvt, bias,
                              row(diff_lambda_q1[l]), row(diff_lambda_k1[l]),
                              row(diff_lambda_q2[l]), row(diff_lambda_k2[l]),
                              diff_sub_norm[l].reshape(-1, 1), lam_init)

        kvmem = _memkv(mem.reshape(b * mem_len, d), row(mem_norm[l]), bf(xattn_w_kv[l]))
        h = _post(h, oat, obt, row(mix_norm[l]), bf(wgate), bf(w_branch_a[l]), bf(w_branch_b[l]),
                  bf(w_out[l]), row(xattn_norm[l]), bf(xattn_w_q[l]), kvmem, bf(xattn_w_o[l]),
                  (row(ffn2_norm[l]), f2g, f2u, f2d),
                  row(final_norm), l == depth - 1, s, mem_len)
    return h.reshape(b, s, d)
```

```python
import functools
import math

import jax
import jax.numpy as jnp
from jax import lax
from jax.experimental import pallas as pl
from jax.experimental.pallas import tpu as pltpu

F32 = jnp.float32
BF16 = jnp.bfloat16

CHUNK = 64
EPS = 1e-6
NEG_INF = -1e30
MLA_HEADS = 8
MLA_Q_LORA = 384
MLA_KV_LORA = 256
MLA_NOPE = 64
MLA_ROPE = 32
MLA_V = 64
ROPE_BASE = 10000.0
DIFF_HEADS = 8
DIFF_DIM = 64
DIFF_V = 2 * DIFF_DIM
N_BUCKETS = 32
XATTN_HEADS = 4
XATTN_DIM = 128
LOG2E = math.log2(math.e)

LANES = 128
HEAD_SLOT = LANES
BF16_ROWS = 16
VMEM_LIMIT = 56 * 1024 * 1024

TM_FFN = 512
TM_PROJ = 512
TQ = 512
TK = 512

MLA_VROWS = (MLA_V + 1 + BF16_ROWS - 1) // BF16_ROWS * BF16_ROWS
DIFF_VROWS = (DIFF_V + 1 + BF16_ROWS - 1) // BF16_ROWS * BF16_ROWS

NT_DIMS = (((1,), (1,)), ((), ()))
TN_DIMS = (((0,), (0,)), ((), ()))


def _rms(x, g):
    return x * lax.rsqrt(jnp.mean(x * x, axis=-1, keepdims=True) + EPS) * g


def _const_spec(shape):
    nd = len(shape)
    return pl.BlockSpec(shape, lambda *_: (0,) * nd, pipeline_mode=pl.Buffered(1))


def _params(n_grid):
    return pltpu.CompilerParams(
        dimension_semantics=("arbitrary",) * n_grid, vmem_limit_bytes=VMEM_LIMIT)


def _ffn_splits(dff):
    half = (dff // 2 + 255) // 256 * 256
    return ((0, half), (half, dff))


def _ffn_half_step(x, g_ref, wg_ref, wu_ref, wd_ref):
    nb = _rms(x, g_ref[...]).astype(BF16)
    y = None
    for lo, hi in _ffn_splits(wg_ref.shape[1]):
        g = jnp.dot(nb, wg_ref[:, lo:hi], preferred_element_type=F32)
        u = jnp.dot(nb, wu_ref[:, lo:hi], preferred_element_type=F32)
        h = (g * jax.nn.sigmoid(g) * u).astype(BF16)
        part = jnp.dot(h, wd_ref[lo:hi, :], preferred_element_type=F32)
        y = part if y is None else y + part
    return x + 0.5 * y


def _ffn_kernel(x_ref, g_ref, wg_ref, wu_ref, wd_ref, *refs):
    n_cast = (len(refs) - 1) // 2
    o_ref = refs[n_cast]
    o_ref[...] = _ffn_half_step(x_ref[...], g_ref, wg_ref, wu_ref, wd_ref)
    for src, dst in zip(refs[:n_cast], refs[n_cast + 1:]):
        dst[...] = src[...].astype(BF16)


def _cast_band(rows, steps):
    band = -(-rows // steps)
    band = -(-band // BF16_ROWS) * BF16_ROWS
    while rows % band:
        band += BF16_ROWS
    return band


def _ffn(x, norm_g, wg, wu, wd, to_cast):
    t, d = x.shape
    steps = t // TM_FFN
    tok = pl.BlockSpec((TM_FFN, d), lambda i: (i, 0))
    cast_specs = []
    for a in to_cast:
        band = _cast_band(a.shape[0], steps)
        last = a.shape[0] // band - 1
        cast_specs.append(pl.BlockSpec((band, a.shape[1]), lambda i, last=last: (jnp.minimum(i, last), 0)))
    outs = pl.pallas_call(
        _ffn_kernel,
        grid=(steps,),
        in_specs=[tok] + [_const_spec(a.shape) for a in (norm_g, wg, wu, wd)] + cast_specs,
        out_specs=[tok] + cast_specs,
        out_shape=[jax.ShapeDtypeStruct((t, d), F32)]
        + [jax.ShapeDtypeStruct(a.shape, BF16) for a in to_cast],
        compiler_params=_params(1),
        name="ffn",
    )(x, norm_g, wg, wu, wd, *to_cast)
    return outs[0], outs[1:]


def _ones_rows(n_rows, rows_per_head, one_row):
    r = lax.broadcasted_iota(jnp.int32, (n_rows, 1), 0) % rows_per_head
    return jnp.where(r == one_row, 1.0, 0.0)


def _proj_kernel(h_ref, g_ref, wlat_ref, wdk_ref, wdqt_ref, wdvt_ref,
                 qn_ref, wqt_ref, kvn_ref, wk_ref, wvt_ref,
                 ct_ref, st_ref,
                 qt_ref, k_ref, vt_ref, dqt_ref, dk_ref, dvt_ref, *, q_scale, diff_scale):
    u = _rms(h_ref[...], g_ref[...]).astype(BF16)

    lat = jnp.dot(u, wlat_ref[...], preferred_element_type=F32)
    q_lat = lat[:, :MLA_Q_LORA]
    kv_lat = lat[:, MLA_Q_LORA:MLA_Q_LORA + MLA_KV_LORA]
    o = MLA_Q_LORA + MLA_KV_LORA
    kr_a = lat[:, o:o + LANES]
    kr_b = lat[:, o + LANES:o + 2 * LANES]

    qn = _rms(q_lat, qn_ref[...]).astype(BF16)
    kvn = _rms(kv_lat, kvn_ref[...]).astype(BF16)

    qqt = lax.dot_general(wqt_ref[...], qn, NT_DIMS, preferred_element_type=F32)
    ct, st = ct_ref[...], st_ref[...]
    nope_rows = lax.broadcasted_iota(jnp.int32, (HEAD_SLOT, 1), 0) < MLA_NOPE
    cqt = (ct + jnp.where(nope_rows, 1.0, 0.0)) * q_scale
    sqt = st * q_scale
    r0, r1, r2 = MLA_NOPE, MLA_NOPE + MLA_ROPE // 2, MLA_NOPE + MLA_ROPE
    for hd in range(MLA_HEADS):
        rows = slice(hd * HEAD_SLOT, (hd + 1) * HEAD_SLOT)
        qh = qqt[rows]
        qh_sw = jnp.concatenate([qh[:r0], qh[r1:r2], qh[r0:r1], qh[r2:]], axis=0)
        qt_ref[0, rows, :] = (qh * cqt + qh_sw * sqt).astype(BF16)

    kk = jnp.dot(kvn, wk_ref[...], preferred_element_type=F32)
    kr = kr_a * ct.T + kr_b * st.T
    for hd in range(MLA_HEADS):
        sl = slice(hd * HEAD_SLOT, (hd + 1) * HEAD_SLOT)
        k_ref[:, sl] = (kk[:, sl] + kr).astype(BF16)

    vt = lax.dot_general(wvt_ref[...], kvn, NT_DIMS, preferred_element_type=F32)
    vt_ref[0] = (vt + _ones_rows(vt.shape[0], MLA_VROWS, MLA_V)).astype(BF16)

    dqt = lax.dot_general(wdqt_ref[...], u, NT_DIMS, preferred_element_type=F32)
    dqt_ref[0] = (dqt * diff_scale).astype(BF16)
    dk_ref[...] = jnp.dot(u, wdk_ref[...], preferred_element_type=F32).astype(BF16)
    dvt = lax.dot_general(wdvt_ref[...], u, NT_DIMS, preferred_element_type=F32)
    dvt_ref[0] = (dvt + _ones_rows(dvt.shape[0], DIFF_VROWS, DIFF_V)).astype(BF16)


def _proj(h, g, wlat, wdk, wdqt, wdvt, qn, wqt, kvn, wk, wvt, tabs_t, batch, seq):
    t, d = h.shape
    nseq = seq // TM_PROJ
    tok = pl.BlockSpec((TM_PROJ, d), lambda i: (i, 0))
    tok_t = lambda rows: pl.BlockSpec((1, rows, TM_PROJ), lambda i: (i // nseq, 0, i % nseq))
    consts = [g, wlat, wdk, wdqt, wdvt, qn, wqt, kvn, wk, wvt]
    out_t = lambda rows: jax.ShapeDtypeStruct((batch, rows, seq), BF16)
    out_n = jax.ShapeDtypeStruct((t, d), BF16)
    return pl.pallas_call(
        functools.partial(_proj_kernel, q_scale=(MLA_NOPE + MLA_ROPE) ** -0.5 * LOG2E,
                          diff_scale=DIFF_DIM ** -0.5 * LOG2E),
        grid=(t // TM_PROJ,),
        in_specs=[tok] + [_const_spec(c.shape) for c in consts]
        + [pl.BlockSpec((HEAD_SLOT, TM_PROJ), lambda i: (0, i % nseq))] * 2,
        out_specs=[tok_t(wqt.shape[0]), tok, tok_t(wvt.shape[0]),
                   tok_t(wdqt.shape[0]), tok, tok_t(wdvt.shape[0])],
        out_shape=[out_t(wqt.shape[0]), out_n, out_t(wvt.shape[0]),
                   out_t(wdqt.shape[0]), out_n, out_t(wdvt.shape[0])],
        compiler_params=_params(1),
        name="in_proj",
    )(h, *consts, *tabs_t)


def _t5_bucket(rel):
    half = N_BUCKETS // 2
    max_exact = half // 2
    n = jnp.abs(rel)
    n2 = n * n
    large = max_exact
    for k in range(1, half - max_exact):
        large = large + (n2 >= (max_exact * max_exact) << k).astype(jnp.int32)
    return jnp.where(rel > 0, half, 0) + jnp.where(n < max_exact, n, large)


def _far_distance():
    half = N_BUCKETS // 2
    max_exact = half // 2
    n = max_exact
    while max_exact + sum(n * n >= (max_exact * max_exact) << k
                          for k in range(1, half - max_exact)) < half - 1:
        n += 1
    return n


def _bias_kernel(tab_ref, o_ref):
    hd = pl.program_id(0)
    far = _far_distance()
    blk = 2 * CHUNK
    kloc = lax.broadcasted_iota(jnp.int32, (blk, blk), 0)
    qloc = lax.broadcasted_iota(jnp.int32, (blk, blk), 1)
    far_val = tab_ref[N_BUCKETS // 2 - 1, hd]
    for delta in range(2):
        for kb in range(TK // blk):
            for qb in range(TQ // blk):
                off = (kb - qb) * blk - delta * TK
                if delta == 0 and kb > qb:
                    val = jnp.full((blk, blk), NEG_INF, F32)
                elif off + blk - 1 <= -far:
                    val = jnp.zeros((blk, blk), F32)
                else:
                    bucket = _t5_bucket(kloc - qloc + off)
                    val = jnp.zeros((blk, blk), F32)
                    for b in range(N_BUCKETS):
                        val = jnp.where(bucket == b, tab_ref[b, hd], val)
                    val = (val - far_val) * LOG2E
                    if delta == 0 and kb == qb:
                        val = jnp.where(kloc // CHUNK <= qloc // CHUNK, val, NEG_INF)
                o_ref[0, delta, kb * blk:(kb + 1) * blk, qb * blk:(qb + 1) * blk] = val


def _bias_tiles(rel_bias):
    return pl.pallas_call(
        _bias_kernel,
        grid=(DIFF_HEADS,),
        in_specs=[pl.BlockSpec(memory_space=pltpu.SMEM)],
        out_specs=pl.BlockSpec((1, 2, TK, TQ), lambda h: (h, 0, 0, 0)),
        out_shape=jax.ShapeDtypeStruct((DIFF_HEADS, 2, TK, TQ), F32),
        compiler_params=_params(1),
        name="bias_tiles",
    )(rel_bias)


FAR, SUB, DIAG = 0, 1, 2


def _scores(k, qt, bias, s_ref, mc_ref, diag):
    if diag:
        h = TK // 2
        parts = [(slice(0, h), slice(0, h)), (slice(0, TK), slice(h, TQ))]
    else:
        parts = [(slice(0, TK), slice(0, TQ))]
    for rows, cols in parts:
        s = jnp.dot(k[rows], qt[:, cols], preferred_element_type=F32)
        if bias is not None:
            s = s + bias[rows, cols]
        s_ref[rows, cols] = s
        mc_ref[:, cols] = jnp.max(s, axis=0, keepdims=True)


def _softmax_pv(s_ref, mc_ref, vt, m_ref, acc_ref, diag):
    m_prev = m_ref[...]
    m_new = jnp.maximum(m_prev, mc_ref[...])
    alpha = jnp.exp2(m_prev - m_new)
    if diag:
        h = TK // 2
        parts = [(slice(0, h), slice(0, h)), (slice(0, TK), slice(h, TQ))]
    else:
        parts = [(slice(0, TK), slice(0, TQ))]
    pv = []
    for rows, cols in parts:
        p = jnp.exp2(s_ref[rows, cols] - m_new[:, cols])
        pv.append(jnp.dot(vt[:, rows], p.astype(BF16), preferred_element_type=F32))
    pv = pv[0] if len(pv) == 1 else jnp.concatenate(pv, axis=1)
    acc_ref[...] = alpha * acc_ref[...] + pv
    m_ref[...] = m_new


FIRST_SLOT = 2
N_SLOTS = 3


def _sweep(i, near, qk, consume, qk_next):
    n_near = len(near)

    def slot_of(n):
        return FIRST_SLOT if n == 0 else (n + 1) % 2

    def run(tiles, own_first, next_kind):
        if own_first:
            qk(*tiles[0])
        for n, tile in enumerate(tiles):
            if n + 1 < len(tiles):
                qk(*tiles[n + 1])
            elif next_kind is not None:
                qk_next(next_kind)
            consume(*tile)

    def next_kind_after(count):
        return FAR if count + 1 > n_near else near[n_near - count - 1]

    for count in range(1, n_near + 1):
        @pl.when(i + 1 == count)
        def _(count=count):
            tiles = [(n, slot_of(n), kind) for n, kind in enumerate(near[-count:])]
            run(tiles, count <= 2, next_kind_after(count) if count > 1 else None)

    @pl.when(i + 1 > n_near)
    def _():
        n_far = i + 1 - n_near

        @pl.when(n_far == 1)
        def _():
            tiles = [(n, slot_of(n), kind) for n, kind in enumerate((FAR,) + near)]
            run(tiles, n_near <= 1, FAR)

        @pl.when(n_far >= 2)
        def _():
            rest = n_far - 1
            pairs = (rest - 1) // 2
            qk(1, 0, FAR)
            consume(0, FIRST_SLOT, FAR)

            def pair(j):
                qk(j + 1, 1, FAR)
                consume(j, 0, FAR)
                qk(j + 2, 0, FAR)
                consume(j + 1, 1, FAR)

            def body(t, carry):
                pair(4 * t + 1)
                pair(4 * t + 3)
                return carry

            lax.fori_loop(0, pairs // 2, body, 0)

            @pl.when(pairs % 2 == 1)
            def _():
                pair(2 * pairs - 1)

            j0 = 2 * pairs + 1
            for left in (1, 2):
                @pl.when(rest - 2 * pairs == left)
                def _(left=left):
                    tiles = [(j0 + n, n % 2, kind) for n, kind in enumerate((FAR,) * left + near)]
                    run(tiles, False, FAR)


def _kv_rows(j):
    return pl.ds(pl.multiple_of(j * TK, TK), TK)


def _attn_scratch(streams, v_rows):
    return [pltpu.VMEM((streams, N_SLOTS, TK, TQ + LANES), F32),
            pltpu.VMEM((streams, N_SLOTS, 1, TQ), F32),
            pltpu.VMEM((streams, 1, TQ), F32),
            pltpu.VMEM((streams, v_rows, TQ), F32)]


MLA_STREAMS = 2


def _q_cols(i):
    return pl.ds(pl.multiple_of(i * TQ, TQ), TQ)


def _mla_kernel(qt_ref, k_ref, vt_ref, mask_ref, ot_ref, s_ref, mc_ref, m_ref, acc_ref):
    def qk(j, slot, kind, qts):
        bias = mask_ref[...] if kind == DIAG else None
        for st in range(MLA_STREAMS):
            k = k_ref[0, _kv_rows(j), st * HEAD_SLOT:(st + 1) * HEAD_SLOT]
            _scores(k, qts[st], bias, s_ref.at[st, slot], mc_ref.at[st, slot], kind == DIAG)

    def consume(j, slot, kind):
        for st in range(MLA_STREAMS):
            vt = vt_ref[0, st * MLA_VROWS:(st + 1) * MLA_VROWS, _kv_rows(j)]
            _softmax_pv(s_ref.at[st, slot], mc_ref.at[st, slot], vt, m_ref.at[st], acc_ref.at[st],
                        kind == DIAG)

    def q_tile(i, carry):
        def q_tiles(t):
            return [qt_ref[0, st * HEAD_SLOT:(st + 1) * HEAD_SLOT, _q_cols(t)] for st in range(MLA_STREAMS)]

        qts_next = q_tiles(jnp.minimum(i + 1, n_q - 1))
        m_ref[...] = jnp.full(m_ref.shape, NEG_INF, F32)
        acc_ref[...] = jnp.zeros(acc_ref.shape, F32)
        _sweep(i, (DIAG,), functools.partial(qk, qts=q_tiles(i)), consume,
               lambda kind: qk(0, FIRST_SLOT, kind, qts_next))
        for st in range(MLA_STREAMS):
            acc = acc_ref[st]
            ot_ref[0, st * MLA_V:(st + 1) * MLA_V, _q_cols(i)] = (
                acc[:MLA_V] / acc[MLA_V:MLA_V + 1]).astype(BF16)
        return carry

    n_q = qt_ref.shape[2] // TQ
    lax.fori_loop(0, n_q, q_tile, 0)


def _mla_attention(qt, k, vt, mask):
    b, _, s = qt.shape
    groups = MLA_HEADS // MLA_STREAMS
    return pl.pallas_call(
        _mla_kernel,
        grid=(b, groups),
        in_specs=[pl.BlockSpec((1, MLA_STREAMS * HEAD_SLOT, s), lambda bi, h: (bi, h, 0)),
                  pl.BlockSpec((1, s, MLA_STREAMS * HEAD_SLOT), lambda bi, h: (bi, 0, h)),
                  pl.BlockSpec((1, MLA_STREAMS * MLA_VROWS, s), lambda bi, h: (bi, h, 0)),
                  _const_spec(mask.shape)],
        out_specs=pl.BlockSpec((1, MLA_STREAMS * MLA_V, s), lambda bi, h: (bi, h, 0)),
        out_shape=jax.ShapeDtypeStruct((b, MLA_HEADS * MLA_V, s), BF16),
        scratch_shapes=_attn_scratch(MLA_STREAMS, MLA_VROWS),
        compiler_params=_params(2),
        name="mla_attn",
    )(qt, k, vt, mask)


def _diff_kernel(qt_ref, k_ref, vt_ref, bias_ref, lq1_ref, lk1_ref, lq2_ref, lk2_ref, sub_ref,
                 ot_ref, s_ref, mc_ref, m_ref, acc_ref, *, lam_init):
    row = lax.broadcasted_iota(jnp.int32, (HEAD_SLOT, TQ), 0)
    zero = jnp.zeros((HEAD_SLOT, TQ), BF16)
    lam = (jnp.exp(jnp.sum(lq1_ref[...] * lk1_ref[...], axis=1, keepdims=True))
           - jnp.exp(jnp.sum(lq2_ref[...] * lk2_ref[...], axis=1, keepdims=True)) + lam_init)

    def qk(j, slot, kind, qts):
        k = k_ref[0, _kv_rows(j), :]
        bias = None if kind == FAR else bias_ref[0, 0 if kind == DIAG else 1]
        for st in range(2):
            _scores(k, qts[st], bias, s_ref.at[st, slot], mc_ref.at[st, slot], kind == DIAG)

    def consume(j, slot, kind):
        vt = vt_ref[0, :, _kv_rows(j)]
        for st in range(2):
            _softmax_pv(s_ref.at[st, slot], mc_ref.at[st, slot], vt, m_ref.at[st], acc_ref.at[st],
                        kind == DIAG)

    def q_tile(i, carry):
        def q_tiles(t):
            qt = qt_ref[0, :, _q_cols(t)]
            return [jnp.where(row < DIFF_DIM, qt, zero), jnp.where(row >= DIFF_DIM, qt, zero)]

        qts_next = q_tiles(jnp.minimum(i + 1, n_q - 1))
        m_ref[...] = jnp.full(m_ref.shape, NEG_INF, F32)
        acc_ref[...] = jnp.zeros(acc_ref.shape, F32)
        _sweep(i, (SUB, DIAG), functools.partial(qk, qts=q_tiles(i)), consume,
               lambda kind: qk(0, FIRST_SLOT, kind, qts_next))
        outs = [acc_ref[st, :DIFF_V] / acc_ref[st, DIFF_V:DIFF_V + 1] for st in range(2)]
        o = outs[0] - lam * outs[1]
        o = o * lax.rsqrt(jnp.mean(o * o, axis=0, keepdims=True) + EPS) * sub_ref[...]
        ot_ref[0, :, _q_cols(i)] = (o * (1.0 - lam_init)).astype(BF16)
        return carry

    n_q = qt_ref.shape[2] // TQ
    lax.fori_loop(0, n_q, q_tile, 0)


def _diff_attention(qt, k, vt, bias, lq1, lk1, lq2, lk2, sub_col, lam_init):
    b, _, s = qt.shape
    vec = _const_spec(lq1.shape)
    return pl.pallas_call(
        functools.partial(_diff_kernel, lam_init=lam_init),
        grid=(b, DIFF_HEADS),
        in_specs=[pl.BlockSpec((1, HEAD_SLOT, s), lambda bi, h: (bi, h, 0)),
                  pl.BlockSpec((1, s, HEAD_SLOT), lambda bi, h: (bi, 0, h)),
                  pl.BlockSpec((1, DIFF_VROWS, s), lambda bi, h: (bi, h, 0)),
                  pl.BlockSpec((1, 2, TK, TQ), lambda bi, h: (h, 0, 0, 0)),
                  vec, vec, vec, vec, _const_spec(sub_col.shape)],
        out_specs=pl.BlockSpec((1, DIFF_V, s), lambda bi, h: (bi, h, 0)),
        out_shape=jax.ShapeDtypeStruct((b, DIFF_HEADS * DIFF_V, s), BF16),
        scratch_shapes=_attn_scratch(2, DIFF_VROWS),
        compiler_params=_params(2),
        name="diff_attn",
    )(qt, k, vt, bias, lq1, lk1, lq2, lk2, sub_col)


def _memkv_kernel(m_ref, g_ref, w_ref, o_ref):
    mn = _rms(m_ref[...], g_ref[...]).astype(BF16)
    o_ref[...] = jnp.dot(mn, w_ref[...], preferred_element_type=F32).astype(BF16)


def _memkv(mem2d, g, w):
    rows, d = mem2d.shape
    return pl.pallas_call(
        _memkv_kernel,
        grid=(1,),
        in_specs=[_const_spec(mem2d.shape), _const_spec(g.shape), _const_spec(w.shape)],
        out_specs=_const_spec((rows, w.shape[1])),
        out_shape=jax.ShapeDtypeStruct((rows, w.shape[1]), BF16),
        compiler_params=_params(1),
        name="mem_kv",
    )(mem2d, g, w)


def _post_kernel(h_ref, oat_ref, obt_ref, g_ref, wgate_ref, wa_ref, wb_ref, wo_ref,
                 xg_ref, xwq_ref, kv_ref, xwo_ref, fg_ref, fwg_ref, fwu_ref, fwd_ref, fin_ref,
                 o_ref, *, scale, final_norm):
    h = h_ref[...]
    d = h.shape[1]
    u = _rms(h, g_ref[...]).astype(BF16)
    gates = jax.nn.sigmoid(jnp.dot(u, wgate_ref[...], preferred_element_type=F32))
    ya = lax.dot_general(oat_ref[0], wa_ref[...], TN_DIMS, preferred_element_type=F32)
    yb = lax.dot_general(obt_ref[0], wb_ref[...], TN_DIMS, preferred_element_type=F32)
    merged = gates[:, :d] * ya + gates[:, d:] * yb
    h = h + jnp.dot(merged.astype(BF16), wo_ref[...], preferred_element_type=F32)

    x = _rms(h, xg_ref[...]).astype(BF16)
    q = (jnp.dot(x, xwq_ref[...], preferred_element_type=F32) * scale).astype(BF16)
    kv = kv_ref[...]
    outs = []
    for hd in range(XATTN_HEADS):
        qh = q[:, hd * XATTN_DIM:(hd + 1) * XATTN_DIM]
        kh = kv[:, 2 * hd * XATTN_DIM:(2 * hd + 1) * XATTN_DIM]
        vh = kv[:, (2 * hd + 1) * XATTN_DIM:(2 * hd + 2) * XATTN_DIM]
        s = lax.dot_general(qh, kh, NT_DIMS, preferred_element_type=F32)
        m = jnp.max(s, axis=1, keepdims=True)
        p = jnp.exp2(s - m)
        l = jnp.sum(p, axis=1, keepdims=True)
        oh = jnp.dot(p.astype(BF16), vh, preferred_element_type=F32) / l
        outs.append(oh.astype(BF16))
    o = jnp.concatenate(outs, axis=1)
    h = h + jnp.dot(o, xwo_ref[...], preferred_element_type=F32)

    h = _ffn_half_step(h, fg_ref, fwg_ref, fwu_ref, fwd_ref)
    if final_norm:
        h = _rms(h, fin_ref[...])
    o_ref[...] = h


def _post(h, oat, obt, g, wgate, wa, wb, wo, xg, xwq, kvmem, xwo, ffn, fin_g, final_norm, seq, mem_len):
    t, d = h.shape
    nseq = seq // TM_PROJ
    tok = pl.BlockSpec((TM_PROJ, d), lambda i: (i, 0))
    tok_t = lambda rows: pl.BlockSpec((1, rows, TM_PROJ), lambda i: (i // nseq, 0, i % nseq))
    cs = lambda a: _const_spec(a.shape)
    return pl.pallas_call(
        functools.partial(_post_kernel, scale=XATTN_DIM ** -0.5 * LOG2E, final_norm=final_norm),
        grid=(t // TM_PROJ,),
        in_specs=[tok, tok_t(oat.shape[1]), tok_t(obt.shape[1]), cs(g), cs(wgate), cs(wa), cs(wb), cs(wo),
                  cs(xg), cs(xwq),
                  pl.BlockSpec((mem_len, kvmem.shape[1]), lambda i: (i // nseq, 0)), cs(xwo)]
        + [cs(a) for a in ffn] + [cs(fin_g)],
        out_specs=tok,
        out_shape=jax.ShapeDtypeStruct((t, d), F32),
        compiler_params=_params(1),
        name="post_attn",
    )(h, oat, obt, g, wgate, wa, wb, wo, xg, xwq, kvmem, xwo, *ffn, fin_g)


def _pad_heads(w, heads, width, slot):
    k = w.shape[0]
    w = w.reshape(k, heads, width)
    return jnp.pad(w, ((0, 0), (0, 0), (0, slot - width))).reshape(k, heads * slot)


def _rope_tables(seq):
    half = MLA_ROPE // 2
    pos = jnp.arange(seq)
    freqs = ROPE_BASE ** (-jnp.arange(half, dtype=F32) / half)
    ang = freqs[:, None] * pos.astype(F32)[None, :]
    cos, sin = jnp.cos(ang), jnp.sin(ang)
    z_nope = jnp.zeros((MLA_NOPE, seq), F32)
    z_pad = jnp.zeros((HEAD_SLOT - MLA_NOPE - MLA_ROPE, seq), F32)
    return (jnp.concatenate([z_nope, cos, cos, z_pad], axis=0),
            jnp.concatenate([z_nope, -sin, sin, z_pad], axis=0))


def _swap_halves(w):
    half = w.shape[-1] // 2
    return jnp.concatenate([w[..., half:], w[..., :half]], axis=-1)


def kernel(x, mem, ffn1_norm, ffn1_w_gate, ffn1_w_up, ffn1_w_down, mix_norm, w_in, mla_q_norm, mla_w_q_up, mla_kv_norm, mla_w_kv_up, diff_lambda_q1, diff_lambda_k1, diff_lambda_q2, diff_lambda_k2, diff_sub_norm, rel_bias, w_branch_a, w_branch_b, w_out, xattn_norm, mem_norm, xattn_w_q, xattn_w_kv, xattn_w_o, ffn2_norm, ffn2_w_gate, ffn2_w_up, ffn2_w_down, final_norm):
    b, s, d = x.shape
    depth = ffn1_norm.shape[0]
    mem_len = mem.shape[1]
    t = b * s
    bf = lambda a: a.astype(BF16)
    row = lambda a: a.reshape(1, -1)

    tabs_t = _rope_tables(s)
    kpos = jnp.arange(TK)[:, None]
    qpos = jnp.arange(TQ)[None, :]
    mla_mask = jnp.where(kpos // CHUNK <= qpos // CHUNK, 0.0, NEG_INF).astype(F32)
    bias = _bias_tiles(rel_bias.astype(F32))

    h = x.reshape(t, d)
    for l in range(depth):
        h, (f2g, f2u, f2d, w) = _ffn(
            h, row(ffn1_norm[l]), bf(ffn1_w_gate[l]), bf(ffn1_w_up[l]), bf(ffn1_w_down[l]),
            (ffn2_w_gate[l], ffn2_w_up[l], ffn2_w_down[l], w_in[l]))

        o_kv = MLA_Q_LORA
        o_kr = o_kv + MLA_KV_LORA
        o_dq = o_kr + MLA_ROPE
        w_kr = w[:, o_kr:o_dq]
        pad_l = jnp.zeros((d, MLA_NOPE), BF16)
        pad_r = jnp.zeros((d, HEAD_SLOT - MLA_NOPE - MLA_ROPE), BF16)
        wlat = jnp.concatenate([w[:, :o_kr], pad_l, w_kr, pad_r, pad_l, _swap_halves(w_kr), pad_r], axis=1)
        wdq = w[:, o_dq:o_dq + d]
        wdk = w[:, o_dq + d:o_dq + 2 * d]
        wdv = _pad_heads(w[:, o_dq + 2 * d:o_dq + 3 * d], DIFF_HEADS, DIFF_V, DIFF_VROWS)
        wgate = w[:, o_dq + 3 * d:]

        wq = _pad_heads(mla_w_q_up[l], MLA_HEADS, MLA_NOPE + MLA_ROPE, HEAD_SLOT)
        wkv3 = mla_w_kv_up[l].reshape(MLA_KV_LORA, MLA_HEADS, MLA_NOPE + MLA_V)
        wk = _pad_heads(wkv3[..., :MLA_NOPE].reshape(MLA_KV_LORA, -1), MLA_HEADS, MLA_NOPE, HEAD_SLOT)
        wv = _pad_heads(wkv3[..., MLA_NOPE:].reshape(MLA_KV_LORA, -1), MLA_HEADS, MLA_V, MLA_VROWS)

        qt, k, vt, dqt, dk, dvt = _proj(
            h, row(mix_norm[l]), bf(wlat), bf(wdk), bf(wdq.T), bf(wdv.T),
            row(mla_q_norm[l]), bf(wq.T), row(mla_kv_norm[l]), bf(wk), bf(wv.T),
            tabs_t, b, s)

        oat = _mla_attention(qt, k.reshape(b, s, d), vt, mla_mask)
        lam_init = 0.8 - 0.6 * math.exp(-0.3 * l)
        obt = _diff_attention(dqt, dk.reshape(b, s, d), dvt, bias,
                              row(diff_lambda_q1[l]), row(diff_lambda_k1[l]),
                              row(diff_lambda_q2[l]), row(diff_lambda_k2[l]),
                              diff_sub_norm[l].reshape(-1, 1), lam_init)

        kvmem = _memkv(mem.reshape(b * mem_len, d), row(mem_norm[l]), bf(xattn_w_kv[l]))
        h = _post(h, oat, obt, row(mix_norm[l]), bf(wgate), bf(w_branch_a[l]), bf(w_branch_b[l]),
                  bf(w_out[l]), row(xattn_norm[l]), bf(xattn_w_q[l]), kvmem, bf(xattn_w_o[l]),
                  (row(ffn2_norm[l]), f2g, f2u, f2d),
                  row(final_norm), l == depth - 1, s, mem_len)
    return h.reshape(b, s, d)
```

```python
import functools
import math

import jax
import jax.numpy as jnp
from jax import lax
from jax.experimental import pallas as pl
from jax.experimental.pallas import tpu as pltpu

F32 = jnp.float32
BF16 = jnp.bfloat16

CHUNK = 64
EPS = 1e-6
NEG_INF = -1e30
MLA_HEADS = 8
MLA_Q_LORA = 384
MLA_KV_LORA = 256
MLA_NOPE = 64
MLA_ROPE = 32
MLA_V = 64
ROPE_BASE = 10000.0
DIFF_HEADS = 8
DIFF_DIM = 64
DIFF_V = 2 * DIFF_DIM
N_BUCKETS = 32
XATTN_HEADS = 4
XATTN_DIM = 128
LOG2E = math.log2(math.e)

LANES = 128
HEAD_SLOT = LANES
BF16_ROWS = 16
VMEM_LIMIT = 56 * 1024 * 1024

TM_FFN = 512
TM_PROJ = 512
TQ = 512
TK = 512

MLA_VROWS = (MLA_V + 1 + BF16_ROWS - 1) // BF16_ROWS * BF16_ROWS
DIFF_VROWS = (DIFF_V + 1 + BF16_ROWS - 1) // BF16_ROWS * BF16_ROWS

NT_DIMS = (((1,), (1,)), ((), ()))
TN_DIMS = (((0,), (0,)), ((), ()))


def _rms(x, g):
    return x * lax.rsqrt(jnp.mean(x * x, axis=-1, keepdims=True) + EPS) * g


def _const_spec(shape):
    nd = len(shape)
    return pl.BlockSpec(shape, lambda *_: (0,) * nd, pipeline_mode=pl.Buffered(1))


def _params(n_grid):
    return pltpu.CompilerParams(
        dimension_semantics=("arbitrary",) * n_grid, vmem_limit_bytes=VMEM_LIMIT)


def _ffn_splits(dff):
    half = (dff // 2 + 255) // 256 * 256
    return ((0, half), (half, dff))


def _ffn_half_step(x, g_ref, wg_ref, wu_ref, wd_ref):
    nb = _rms(x, g_ref[...]).astype(BF16)
    y = None
    for lo, hi in _ffn_splits(wg_ref.shape[1]):
        g = jnp.dot(nb, wg_ref[:, lo:hi], preferred_element_type=F32)
        u = jnp.dot(nb, wu_ref[:, lo:hi], preferred_element_type=F32)
        h = (g * jax.nn.sigmoid(g) * u).astype(BF16)
        part = jnp.dot(h, wd_ref[lo:hi, :], preferred_element_type=F32)
        y = part if y is None else y + part
    return x + 0.5 * y


def _ffn_kernel(x_ref, g_ref, wg_ref, wu_ref, wd_ref, *refs):
    n_cast = (len(refs) - 1) // 2
    o_ref = refs[n_cast]
    o_ref[...] = _ffn_half_step(x_ref[...], g_ref, wg_ref, wu_ref, wd_ref)
    for src, dst in zip(refs[:n_cast], refs[n_cast + 1:]):
        dst[...] = src[...].astype(BF16)


def _cast_band(start, rows, steps):
    band = -(-rows // steps)
    band = -(-band // BF16_ROWS) * BF16_ROWS
    while rows % band or start % band:
        band += BF16_ROWS
    return band


def _ffn(x, norm_g, wg, wu, wd, to_cast):
    t, d = x.shape
    steps = t // TM_FFN
    tok = pl.BlockSpec((TM_FFN, d), lambda i: (i, 0))
    in_specs, out_specs, out_shapes = [], [], []
    for a, start, rows in to_cast:
        band = _cast_band(start, rows, steps)
        first, last = start // band, rows // band - 1
        in_specs.append(pl.BlockSpec(
            (band, a.shape[1]), lambda i, first=first, last=last: (first + jnp.minimum(i, last), 0)))
        out_specs.append(pl.BlockSpec((band, a.shape[1]), lambda i, last=last: (jnp.minimum(i, last), 0)))
        out_shapes.append(jax.ShapeDtypeStruct((rows, a.shape[1]), BF16))
    outs = pl.pallas_call(
        _ffn_kernel,
        grid=(steps,),
        in_specs=[tok] + [_const_spec(a.shape) for a in (norm_g, wg, wu, wd)] + in_specs,
        out_specs=[tok] + out_specs,
        out_shape=[jax.ShapeDtypeStruct((t, d), F32)] + out_shapes,
        compiler_params=_params(1),
        name="ffn",
    )(x, norm_g, wg, wu, wd, *[a for a, _, _ in to_cast])
    return outs[0], outs[1:]


def _ones_rows(n_rows, rows_per_head, one_row):
    r = lax.broadcasted_iota(jnp.int32, (n_rows, 1), 0) % rows_per_head
    return jnp.where(r == one_row, 1.0, 0.0)


def _proj_kernel(h_ref, g_ref, wlat_ref, wdk_ref, wdqt_ref, wdvt_ref,
                 qn_ref, wqt_ref, kvn_ref, wk_ref, wvt_ref,
                 ct_ref, st_ref,
                 qt_ref, k_ref, vt_ref, dqt_ref, dk_ref, dvt_ref, *, q_scale, diff_scale):
    u = _rms(h_ref[...], g_ref[...]).astype(BF16)

    lat = lax.dot_general(u, wlat_ref[...], NT_DIMS, preferred_element_type=F32)
    q_lat = lat[:, :MLA_Q_LORA]
    kv_lat = lat[:, MLA_Q_LORA:MLA_Q_LORA + MLA_KV_LORA]
    o = MLA_Q_LORA + MLA_KV_LORA
    kr_a = lat[:, o:o + LANES]
    kr_b = lat[:, o + LANES:o + 2 * LANES]

    qn = _rms(q_lat, qn_ref[...]).astype(BF16)
    kvn = _rms(kv_lat, kvn_ref[...]).astype(BF16)

    qqt = lax.dot_general(wqt_ref[...], qn, NT_DIMS, preferred_element_type=F32)
    ct, st = ct_ref[...], st_ref[...]
    nope_rows = lax.broadcasted_iota(jnp.int32, (HEAD_SLOT, 1), 0) < MLA_NOPE
    cqt = (ct + jnp.where(nope_rows, 1.0, 0.0)) * q_scale
    sqt = st * q_scale
    r0, r1, r2 = MLA_NOPE, MLA_NOPE + MLA_ROPE // 2, MLA_NOPE + MLA_ROPE
    for hd in range(MLA_HEADS):
        rows = slice(hd * HEAD_SLOT, (hd + 1) * HEAD_SLOT)
        qh = qqt[rows]
        qh_sw = jnp.concatenate([qh[:r0], qh[r1:r2], qh[r0:r1], qh[r2:]], axis=0)
        qt_ref[0, rows, :] = (qh * cqt + qh_sw * sqt).astype(BF16)

    kk = jnp.dot(kvn, wk_ref[...], preferred_element_type=F32)
    kr = kr_a * ct.T + kr_b * st.T
    for hd in range(MLA_HEADS):
        sl = slice(hd * HEAD_SLOT, (hd + 1) * HEAD_SLOT)
        k_ref[:, sl] = (kk[:, sl] + kr).astype(BF16)

    vt = lax.dot_general(wvt_ref[...], kvn, NT_DIMS, preferred_element_type=F32)
    vt_ref[0] = (vt + _ones_rows(vt.shape[0], MLA_VROWS, MLA_V)).astype(BF16)

    dqt = lax.dot_general(wdqt_ref[...], u, NT_DIMS, preferred_element_type=F32)
    dqt_ref[0] = (dqt * diff_scale).astype(BF16)
    dk_ref[...] = lax.dot_general(u, wdk_ref[...], NT_DIMS, preferred_element_type=F32).astype(BF16)
    dvt = lax.dot_general(wdvt_ref[...], u, NT_DIMS, preferred_element_type=F32)
    dvt_ref[0] = (dvt + _ones_rows(dvt.shape[0], DIFF_VROWS, DIFF_V)).astype(BF16)


def _proj(h, g, wlat, wdk, wdqt, wdvt, qn, wqt, kvn, wk, wvt, tabs_t, batch, seq):
    t, d = h.shape
    nseq = seq // TM_PROJ
    tok = pl.BlockSpec((TM_PROJ, d), lambda i: (i, 0))
    tok_t = lambda rows: pl.BlockSpec((1, rows, TM_PROJ), lambda i: (i // nseq, 0, i % nseq))
    consts = [g, wlat, wdk, wdqt, wdvt, qn, wqt, kvn, wk, wvt]
    out_t = lambda rows: jax.ShapeDtypeStruct((batch, rows, seq), BF16)
    out_n = jax.ShapeDtypeStruct((t, d), BF16)
    return pl.pallas_call(
        functools.partial(_proj_kernel, q_scale=(MLA_NOPE + MLA_ROPE) ** -0.5 * LOG2E,
                          diff_scale=DIFF_DIM ** -0.5 * LOG2E),
        grid=(t // TM_PROJ,),
        in_specs=[tok] + [_const_spec(c.shape) for c in consts]
        + [pl.BlockSpec((HEAD_SLOT, TM_PROJ), lambda i: (0, i % nseq))] * 2,
        out_specs=[tok_t(wqt.shape[0]), tok, tok_t(wvt.shape[0]),
                   tok_t(wdqt.shape[0]), tok, tok_t(wdvt.shape[0])],
        out_shape=[out_t(wqt.shape[0]), out_n, out_t(wvt.shape[0]),
                   out_t(wdqt.shape[0]), out_n, out_t(wdvt.shape[0])],
        compiler_params=_params(1),
        name="in_proj",
    )(h, *consts, *tabs_t)


def _t5_bucket(rel):
    half = N_BUCKETS // 2
    max_exact = half // 2
    n = jnp.abs(rel)
    n2 = n * n
    large = max_exact
    for k in range(1, half - max_exact):
        large = large + (n2 >= (max_exact * max_exact) << k).astype(jnp.int32)
    return jnp.where(rel > 0, half, 0) + jnp.where(n < max_exact, n, large)


def _far_distance():
    half = N_BUCKETS // 2
    max_exact = half // 2
    n = max_exact
    while max_exact + sum(n * n >= (max_exact * max_exact) << k
                          for k in range(1, half - max_exact)) < half - 1:
        n += 1
    return n


def _bias_kernel(tab_ref, o_ref):
    hd = pl.program_id(0)
    far = _far_distance()
    blk = 2 * CHUNK
    kloc = lax.broadcasted_iota(jnp.int32, (blk, blk), 0)
    qloc = lax.broadcasted_iota(jnp.int32, (blk, blk), 1)
    far_val = tab_ref[N_BUCKETS // 2 - 1, hd]
    for delta in range(2):
        for kb in range(TK // blk):
            for qb in range(TQ // blk):
                off = (kb - qb) * blk - delta * TK
                if delta == 0 and kb > qb:
                    val = jnp.full((blk, blk), NEG_INF, F32)
                elif off + blk - 1 <= -far:
                    val = jnp.zeros((blk, blk), F32)
                else:
                    bucket = _t5_bucket(kloc - qloc + off)
                    val = jnp.zeros((blk, blk), F32)
                    for b in range(N_BUCKETS):
                        val = jnp.where(bucket == b, tab_ref[b, hd], val)
                    val = (val - far_val) * LOG2E
                    if delta == 0 and kb == qb:
                        val = jnp.where(kloc // CHUNK <= qloc // CHUNK, val, NEG_INF)
                o_ref[0, delta, kb * blk:(kb + 1) * blk, qb * blk:(qb + 1) * blk] = val


def _bias_tiles(rel_bias):
    return pl.pallas_call(
        _bias_kernel,
        grid=(DIFF_HEADS,),
        in_specs=[pl.BlockSpec(memory_space=pltpu.SMEM)],
        out_specs=pl.BlockSpec((1, 2, TK, TQ), lambda h: (h, 0, 0, 0)),
        out_shape=jax.ShapeDtypeStruct((DIFF_HEADS, 2, TK, TQ), F32),
        compiler_params=_params(1),
        name="bias_tiles",
    )(rel_bias)


FAR, SUB, DIAG = 0, 1, 2


def _scores(k, qt, bias, s_ref, mc_ref, diag):
    if diag:
        h = TK // 2
        parts = [(slice(0, h), slice(0, h)), (slice(0, TK), slice(h, TQ))]
    else:
        parts = [(slice(0, TK), slice(0, TQ))]
    for rows, cols in parts:
        s = jnp.dot(k[rows], qt[:, cols], preferred_element_type=F32)
        if bias is not None:
            s = s + bias[rows, cols]
        s_ref[rows, cols] = s
        mc_ref[:, cols] = jnp.max(s, axis=0, keepdims=True)


def _softmax_pv(s_ref, mc_ref, vt, m_ref, acc_ref, diag):
    m_prev = m_ref[...]
    m_new = jnp.maximum(m_prev, mc_ref[...])
    alpha = jnp.exp2(m_prev - m_new)
    if diag:
        h = TK // 2
        parts = [(slice(0, h), slice(0, h)), (slice(0, TK), slice(h, TQ))]
    else:
        parts = [(slice(0, TK), slice(0, TQ))]
    pv = []
    for rows, cols in parts:
        p = jnp.exp2(s_ref[rows, cols] - m_new[:, cols])
        pv.append(jnp.dot(vt[:, rows], p.astype(BF16), preferred_element_type=F32))
    pv = pv[0] if len(pv) == 1 else jnp.concatenate(pv, axis=1)
    acc_ref[...] = alpha * acc_ref[...] + pv
    m_ref[...] = m_new


FIRST_SLOT = 2
N_SLOTS = 3


def _sweep(i, near, qk, consume, qk_next):
    n_near = len(near)

    def slot_of(n):
        return FIRST_SLOT if n == 0 else (n + 1) % 2

    def run(tiles, own_first, next_kind):
        if own_first:
            qk(*tiles[0])
        for n, tile in enumerate(tiles):
            if n + 1 < len(tiles):
                qk(*tiles[n + 1])
            elif next_kind is not None:
                qk_next(next_kind)
            consume(*tile)

    def next_kind_after(count):
        return FAR if count + 1 > n_near else near[n_near - count - 1]

    for count in range(1, n_near + 1):
        @pl.when(i + 1 == count)
        def _(count=count):
            tiles = [(n, slot_of(n), kind) for n, kind in enumerate(near[-count:])]
            run(tiles, count <= 2, next_kind_after(count) if count > 1 else None)

    @pl.when(i + 1 > n_near)
    def _():
        n_far = i + 1 - n_near

        @pl.when(n_far == 1)
        def _():
            tiles = [(n, slot_of(n), kind) for n, kind in enumerate((FAR,) + near)]
            run(tiles, n_near <= 1, FAR)

        @pl.when(n_far >= 2)
        def _():
            rest = n_far - 1
            pairs = (rest - 1) // 2
            qk(1, 0, FAR)
            consume(0, FIRST_SLOT, FAR)

            def pair(j):
                qk(j + 1, 1, FAR)
                consume(j, 0, FAR)
                qk(j + 2, 0, FAR)
                consume(j + 1, 1, FAR)

            def body(t, carry):
                pair(4 * t + 1)
                pair(4 * t + 3)
                return carry

            lax.fori_loop(0, pairs // 2, body, 0)

            @pl.when(pairs % 2 == 1)
            def _():
                pair(2 * pairs - 1)

            j0 = 2 * pairs + 1
            for left in (1, 2):
                @pl.when(rest - 2 * pairs == left)
                def _(left=left):
                    tiles = [(j0 + n, n % 2, kind) for n, kind in enumerate((FAR,) * left + near)]
                    run(tiles, False, FAR)


def _kv_rows(j):
    return pl.ds(pl.multiple_of(j * TK, TK), TK)


def _attn_scratch(streams, v_rows):
    return [pltpu.VMEM((streams, N_SLOTS, TK, TQ + LANES), F32),
            pltpu.VMEM((streams, N_SLOTS, 1, TQ), F32),
            pltpu.VMEM((streams, 1, TQ), F32),
            pltpu.VMEM((streams, v_rows, TQ), F32)]


MLA_STREAMS = 2


def _q_cols(i):
    return pl.ds(pl.multiple_of(i * TQ, TQ), TQ)


def _mla_kernel(qt_ref, k_ref, vt_ref, mask_ref, ot_ref, s_ref, mc_ref, m_ref, acc_ref):
    def qk(j, slot, kind, qts):
        bias = mask_ref[...] if kind == DIAG else None
        for st in range(MLA_STREAMS):
            k = k_ref[0, _kv_rows(j), st * HEAD_SLOT:(st + 1) * HEAD_SLOT]
            _scores(k, qts[st], bias, s_ref.at[st, slot], mc_ref.at[st, slot], kind == DIAG)

    def consume(j, slot, kind):
        for st in range(MLA_STREAMS):
            vt = vt_ref[0, st * MLA_VROWS:(st + 1) * MLA_VROWS, _kv_rows(j)]
            _softmax_pv(s_ref.at[st, slot], mc_ref.at[st, slot], vt, m_ref.at[st], acc_ref.at[st],
                        kind == DIAG)

    def q_tile(i, carry):
        def q_tiles(t):
            return [qt_ref[0, st * HEAD_SLOT:(st + 1) * HEAD_SLOT, _q_cols(t)] for st in range(MLA_STREAMS)]

        qts_next = q_tiles(jnp.minimum(i + 1, n_q - 1))
        m_ref[...] = jnp.full(m_ref.shape, NEG_INF, F32)
        acc_ref[...] = jnp.zeros(acc_ref.shape, F32)
        _sweep(i, (DIAG,), functools.partial(qk, qts=q_tiles(i)), consume,
               lambda kind: qk(0, FIRST_SLOT, kind, qts_next))
        for st in range(MLA_STREAMS):
            acc = acc_ref[st]
            ot_ref[0, st * MLA_V:(st + 1) * MLA_V, _q_cols(i)] = (
                acc[:MLA_V] / acc[MLA_V:MLA_V + 1]).astype(BF16)
        return carry

    n_q = qt_ref.shape[2] // TQ
    lax.fori_loop(0, n_q, q_tile, 0)


def _mla_attention(qt, k, vt, mask):
    b, _, s = qt.shape
    groups = MLA_HEADS // MLA_STREAMS
    return pl.pallas_call(
        _mla_kernel,
        grid=(b, groups),
        in_specs=[pl.BlockSpec((1, MLA_STREAMS * HEAD_SLOT, s), lambda bi, h: (bi, h, 0)),
                  pl.BlockSpec((1, s, MLA_STREAMS * HEAD_SLOT), lambda bi, h: (bi, 0, h)),
                  pl.BlockSpec((1, MLA_STREAMS * MLA_VROWS, s), lambda bi, h: (bi, h, 0)),
                  _const_spec(mask.shape)],
        out_specs=pl.BlockSpec((1, MLA_STREAMS * MLA_V, s), lambda bi, h: (bi, h, 0)),
        out_shape=jax.ShapeDtypeStruct((b, MLA_HEADS * MLA_V, s), BF16),
        scratch_shapes=_attn_scratch(MLA_STREAMS, MLA_VROWS),
        compiler_params=_params(2),
        name="mla_attn",
    )(qt, k, vt, mask)


def _diff_kernel(qt_ref, k_ref, vt_ref, bias_ref, lq1_ref, lk1_ref, lq2_ref, lk2_ref, sub_ref,
                 ot_ref, s_ref, mc_ref, m_ref, acc_ref, *, lam_init):
    row = lax.broadcasted_iota(jnp.int32, (HEAD_SLOT, TQ), 0)
    zero = jnp.zeros((HEAD_SLOT, TQ), BF16)
    lam = (jnp.exp(jnp.sum(lq1_ref[...] * lk1_ref[...], axis=1, keepdims=True))
           - jnp.exp(jnp.sum(lq2_ref[...] * lk2_ref[...], axis=1, keepdims=True)) + lam_init)

    def qk(j, slot, kind, qts):
        k = k_ref[0, _kv_rows(j), :]
        bias = None if kind == FAR else bias_ref[0, 0 if kind == DIAG else 1]
        for st in range(2):
            _scores(k, qts[st], bias, s_ref.at[st, slot], mc_ref.at[st, slot], kind == DIAG)

    def consume(j, slot, kind):
        vt = vt_ref[0, :, _kv_rows(j)]
        for st in range(2):
            _softmax_pv(s_ref.at[st, slot], mc_ref.at[st, slot], vt, m_ref.at[st], acc_ref.at[st],
                        kind == DIAG)

    def q_tile(i, carry):
        def q_tiles(t):
            qt = qt_ref[0, :, _q_cols(t)]
            return [jnp.where(row < DIFF_DIM, qt, zero), jnp.where(row >= DIFF_DIM, qt, zero)]

        qts_next = q_tiles(jnp.minimum(i + 1, n_q - 1))
        m_ref[...] = jnp.full(m_ref.shape, NEG_INF, F32)
        acc_ref[...] = jnp.zeros(acc_ref.shape, F32)
        _sweep(i, (SUB, DIAG), functools.partial(qk, qts=q_tiles(i)), consume,
               lambda kind: qk(0, FIRST_SLOT, kind, qts_next))
        outs = [acc_ref[st, :DIFF_V] / acc_ref[st, DIFF_V:DIFF_V + 1] for st in range(2)]
        o = outs[0] - lam * outs[1]
        o = o * lax.rsqrt(jnp.mean(o * o, axis=0, keepdims=True) + EPS) * sub_ref[...]
        ot_ref[0, :, _q_cols(i)] = (o * (1.0 - lam_init)).astype(BF16)
        return carry

    n_q = qt_ref.shape[2] // TQ
    lax.fori_loop(0, n_q, q_tile, 0)


def _diff_attention(qt, k, vt, bias, lq1, lk1, lq2, lk2, sub_col, lam_init):
    b, _, s = qt.shape
    vec = _const_spec(lq1.shape)
    return pl.pallas_call(
        functools.partial(_diff_kernel, lam_init=lam_init),
        grid=(b, DIFF_HEADS),
        in_specs=[pl.BlockSpec((1, HEAD_SLOT, s), lambda bi, h: (bi, h, 0)),
                  pl.BlockSpec((1, s, HEAD_SLOT), lambda bi, h: (bi, 0, h)),
                  pl.BlockSpec((1, DIFF_VROWS, s), lambda bi, h: (bi, h, 0)),
                  pl.BlockSpec((1, 2, TK, TQ), lambda bi, h: (h, 0, 0, 0)),
                  vec, vec, vec, vec, _const_spec(sub_col.shape)],
        out_specs=pl.BlockSpec((1, DIFF_V, s), lambda bi, h: (bi, h, 0)),
        out_shape=jax.ShapeDtypeStruct((b, DIFF_HEADS * DIFF_V, s), BF16),
        scratch_shapes=_attn_scratch(2, DIFF_VROWS),
        compiler_params=_params(2),
        name="diff_attn",
    )(qt, k, vt, bias, lq1, lk1, lq2, lk2, sub_col)


def _memkv_kernel(m_ref, g_ref, w_ref, o_ref):
    mn = _rms(m_ref[...], g_ref[...]).astype(BF16)
    o_ref[...] = jnp.dot(mn, w_ref[...], preferred_element_type=F32).astype(BF16)


def _memkv(mem2d, g, w):
    rows, d = mem2d.shape
    return pl.pallas_call(
        _memkv_kernel,
        grid=(1,),
        in_specs=[_const_spec(mem2d.shape), _const_spec(g.shape), _const_spec(w.shape)],
        out_specs=_const_spec((rows, w.shape[1])),
        out_shape=jax.ShapeDtypeStruct((rows, w.shape[1]), BF16),
        compiler_params=_params(1),
        name="mem_kv",
    )(mem2d, g, w)


def _post_kernel(h_ref, oat_ref, obt_ref, g_ref, wga_ref, wgb_ref, wa_ref, wb_ref, wo_ref,
                 xg_ref, xwq_ref, kv_ref, xwo_ref, fg_ref, fwg_ref, fwu_ref, fwd_ref, fin_ref,
                 o_ref, *, scale, final_norm):
    h = h_ref[...]
    u = _rms(h, g_ref[...]).astype(BF16)
    gate_a = jax.nn.sigmoid(lax.dot_general(u, wga_ref[...], NT_DIMS, preferred_element_type=F32))
    gate_b = jax.nn.sigmoid(lax.dot_general(u, wgb_ref[...], NT_DIMS, preferred_element_type=F32))
    ya = lax.dot_general(oat_ref[0], wa_ref[...], TN_DIMS, preferred_element_type=F32)
    yb = lax.dot_general(obt_ref[0], wb_ref[...], TN_DIMS, preferred_element_type=F32)
    merged = gate_a * ya + gate_b * yb
    h = h + jnp.dot(merged.astype(BF16), wo_ref[...], preferred_element_type=F32)

    x = _rms(h, xg_ref[...]).astype(BF16)
    q = (jnp.dot(x, xwq_ref[...], preferred_element_type=F32) * scale).astype(BF16)
    kv = kv_ref[...]
    outs = []
    for hd in range(XATTN_HEADS):
        qh = q[:, hd * XATTN_DIM:(hd + 1) * XATTN_DIM]
        kh = kv[:, 2 * hd * XATTN_DIM:(2 * hd + 1) * XATTN_DIM]
        vh = kv[:, (2 * hd + 1) * XATTN_DIM:(2 * hd + 2) * XATTN_DIM]
        s = lax.dot_general(qh, kh, NT_DIMS, preferred_element_type=F32)
        m = jnp.max(s, axis=1, keepdims=True)
        p = jnp.exp2(s - m)
        l = jnp.sum(p, axis=1, keepdims=True)
        oh = jnp.dot(p.astype(BF16), vh, preferred_element_type=F32) / l
        outs.append(oh.astype(BF16))
    o = jnp.concatenate(outs, axis=1)
    h = h + jnp.dot(o, xwo_ref[...], preferred_element_type=F32)

    h = _ffn_half_step(h, fg_ref, fwg_ref, fwu_ref, fwd_ref)
    if final_norm:
        h = _rms(h, fin_ref[...])
    o_ref[...] = h


def _post(h, oat, obt, g, wga, wgb, wa, wb, wo, xg, xwq, kvmem, xwo, ffn, fin_g, final_norm, seq, mem_len):
    t, d = h.shape
    nseq = seq // TM_PROJ
    tok = pl.BlockSpec((TM_PROJ, d), lambda i: (i, 0))
    tok_t = lambda rows: pl.BlockSpec((1, rows, TM_PROJ), lambda i: (i // nseq, 0, i % nseq))
    cs = lambda a: _const_spec(a.shape)
    return pl.pallas_call(
        functools.partial(_post_kernel, scale=XATTN_DIM ** -0.5 * LOG2E, final_norm=final_norm),
        grid=(t // TM_PROJ,),
        in_specs=[tok, tok_t(oat.shape[1]), tok_t(obt.shape[1]), cs(g), cs(wga), cs(wgb), cs(wa), cs(wb), cs(wo),
                  cs(xg), cs(xwq),
                  pl.BlockSpec((mem_len, kvmem.shape[1]), lambda i: (i // nseq, 0)), cs(xwo)]
        + [cs(a) for a in ffn] + [cs(fin_g)],
        out_specs=tok,
        out_shape=jax.ShapeDtypeStruct((t, d), F32),
        compiler_params=_params(1),
        name="post_attn",
    )(h, oat, obt, g, wga, wgb, wa, wb, wo, xg, xwq, kvmem, xwo, *ffn, fin_g)


def _pad_heads(w, heads, width, slot):
    k = w.shape[0]
    w = w.reshape(k, heads, width)
    return jnp.pad(w, ((0, 0), (0, 0), (0, slot - width))).reshape(k, heads * slot)


def _rope_tables(seq):
    half = MLA_ROPE // 2
    pos = jnp.arange(seq)
    freqs = ROPE_BASE ** (-jnp.arange(half, dtype=F32) / half)
    ang = freqs[:, None] * pos.astype(F32)[None, :]
    cos, sin = jnp.cos(ang), jnp.sin(ang)
    z_nope = jnp.zeros((MLA_NOPE, seq), F32)
    z_pad = jnp.zeros((HEAD_SLOT - MLA_NOPE - MLA_ROPE, seq), F32)
    return (jnp.concatenate([z_nope, cos, cos, z_pad], axis=0),
            jnp.concatenate([z_nope, -sin, sin, z_pad], axis=0))


def kernel(x, mem, ffn1_norm, ffn1_w_gate, ffn1_w_up, ffn1_w_down, mix_norm, w_in, mla_q_norm, mla_w_q_up, mla_kv_norm, mla_w_kv_up, diff_lambda_q1, diff_lambda_k1, diff_lambda_q2, diff_lambda_k2, diff_sub_norm, rel_bias, w_branch_a, w_branch_b, w_out, xattn_norm, mem_norm, xattn_w_q, xattn_w_kv, xattn_w_o, ffn2_norm, ffn2_w_gate, ffn2_w_up, ffn2_w_down, final_norm):
    b, s, d = x.shape
    depth = ffn1_norm.shape[0]
    mem_len = mem.shape[1]
    t = b * s
    bf = lambda a: a.astype(BF16)
    row = lambda a: a.reshape(1, -1)

    tabs_t = _rope_tables(s)
    kpos = jnp.arange(TK)[:, None]
    qpos = jnp.arange(TQ)[None, :]
    mla_mask = jnp.where(kpos // CHUNK <= qpos // CHUNK, 0.0, NEG_INF).astype(F32)
    bias = _bias_tiles(rel_bias.astype(F32))

    h = x.reshape(t, d)
    for l in range(depth):
        w_t = jnp.transpose(w_in[l])
        o_kr = MLA_Q_LORA + MLA_KV_LORA
        o_dq = o_kr + MLA_ROPE
        pieces = [(w_t, 0, o_dq)] + [(w_t, o_dq + n * d, d) for n in range(5)]
        h, (f2g, f2u, f2d, w_lat, wdq_t, wdk_t, wdv_t, wga_t, wgb_t) = _ffn(
            h, row(ffn1_norm[l]), bf(ffn1_w_gate[l]), bf(ffn1_w_up[l]), bf(ffn1_w_down[l]),
            [(ffn2_w_gate[l], 0, d), (ffn2_w_up[l], 0, d), (ffn2_w_down[l], 0, ffn2_w_down.shape[1])] + pieces)

        w_kr = w_lat[o_kr:]
        zeros = lambda n: jnp.zeros((n, d), BF16)
        pad_l, pad_r = zeros(MLA_NOPE), zeros(HEAD_SLOT - MLA_NOPE - MLA_ROPE)
        w_kr_sw = jnp.concatenate([w_kr[MLA_ROPE // 2:], w_kr[:MLA_ROPE // 2]], axis=0)
        wlat_t = jnp.concatenate([w_lat[:o_kr], pad_l, w_kr, pad_r, pad_l, w_kr_sw, pad_r], axis=0)
        wdv_t = jnp.pad(wdv_t.reshape(DIFF_HEADS, DIFF_V, d),
                        ((0, 0), (0, DIFF_VROWS - DIFF_V), (0, 0))).reshape(-1, d)

        wq = _pad_heads(mla_w_q_up[l], MLA_HEADS, MLA_NOPE + MLA_ROPE, HEAD_SLOT)
        wkv3 = mla_w_kv_up[l].reshape(MLA_KV_LORA, MLA_HEADS, MLA_NOPE + MLA_V)
        wk = _pad_heads(wkv3[..., :MLA_NOPE].reshape(MLA_KV_LORA, -1), MLA_HEADS, MLA_NOPE, HEAD_SLOT)
        wv = _pad_heads(wkv3[..., MLA_NOPE:].reshape(MLA_KV_LORA, -1), MLA_HEADS, MLA_V, MLA_VROWS)

        qt, k, vt, dqt, dk, dvt = _proj(
            h, row(mix_norm[l]), wlat_t, wdk_t, wdq_t, wdv_t,
            row(mla_q_norm[l]), bf(wq.T), row(mla_kv_norm[l]), bf(wk), bf(wv.T),
            tabs_t, b, s)

        oat = _mla_attention(qt, k.reshape(b, s, d), vt, mla_mask)
        lam_init = 0.8 - 0.6 * math.exp(-0.3 * l)
        obt = _diff_attention(dqt, dk.reshape(b, s, d), dvt, bias,
                              row(diff_lambda_q1[l]), row(diff_lambda_k1[l]),
                              row(diff_lambda_q2[l]), row(diff_lambda_k2[l]),
                              diff_sub_norm[l].reshape(-1, 1), lam_init)

        kvmem = _memkv(mem.reshape(b * mem_len, d), row(mem_norm[l]), bf(xattn_w_kv[l]))
        h = _post(h, oat, obt, row(mix_norm[l]), wga_t, wgb_t, bf(w_branch_a[l]), bf(w_branch_b[l]),
                  bf(w_out[l]), row(xattn_norm[l]), bf(xattn_w_q[l]), kvmem, bf(xattn_w_o[l]),
                  (row(ffn2_norm[l]), f2g, f2u, f2d),
                  row(final_norm), l == depth - 1, s, mem_len)
    return h.reshape(b, s, d)
```

```python
import functools
import math

import jax
import jax.numpy as jnp
from jax import lax
from jax.experimental import pallas as pl
from jax.experimental.pallas import tpu as pltpu

F32 = jnp.float32
BF16 = jnp.bfloat16

CHUNK = 64
EPS = 1e-6
NEG_INF = -1e30
MLA_HEADS = 8
MLA_Q_LORA = 384
MLA_KV_LORA = 256
MLA_NOPE = 64
MLA_ROPE = 32
MLA_V = 64
ROPE_BASE = 10000.0
DIFF_HEADS = 8
DIFF_DIM = 64
DIFF_V = 2 * DIFF_DIM
N_BUCKETS = 32
MAX_DISTANCE = 128
XATTN_HEADS = 4
XATTN_DIM = 128
LOG2E = math.log2(math.e)

LANES = 128
HEAD_SLOT = LANES
BF16_ROWS = 16
MXU_TILE = 256
VMEM_LIMIT = 56 * 1024 * 1024

TM_FFN = 512
TM_PROJ = 512
TQ = 512
TK = 512

MLA_VROWS = (MLA_V + 1 + BF16_ROWS - 1) // BF16_ROWS * BF16_ROWS
DIFF_VROWS = (DIFF_V + 1 + BF16_ROWS - 1) // BF16_ROWS * BF16_ROWS

NT_DIMS = (((1,), (1,)), ((), ()))
TN_DIMS = (((0,), (0,)), ((), ()))


def _rms(x, g):
    return x * lax.rsqrt(jnp.mean(x * x, axis=-1, keepdims=True) + EPS) * g


def _const_spec(shape):
    nd = len(shape)
    return pl.BlockSpec(shape, lambda *_: (0,) * nd, pipeline_mode=pl.Buffered(1))


def _params(n_grid):
    return pltpu.CompilerParams(
        dimension_semantics=("arbitrary",) * n_grid, vmem_limit_bytes=VMEM_LIMIT)


def _ffn_splits(dff):
    half = -(-(dff // 2) // MXU_TILE) * MXU_TILE
    return ((0, half), (half, dff))


def _ffn_half_step(x, g_ref, wg_ref, wu_ref, wd_ref):
    nb = _rms(x, g_ref[...]).astype(BF16)
    y = None
    for lo, hi in _ffn_splits(wg_ref.shape[1]):
        g = jnp.dot(nb, wg_ref[:, lo:hi], preferred_element_type=F32)
        u = jnp.dot(nb, wu_ref[:, lo:hi], preferred_element_type=F32)
        h = (g * jax.nn.sigmoid(g) * u).astype(BF16)
        part = jnp.dot(h, wd_ref[lo:hi, :], preferred_element_type=F32)
        y = part if y is None else y + part
    return x + 0.5 * y


def _split_hosted(refs):
    n_cast = (len(refs) - 1) // 2
    return refs[:n_cast], refs[n_cast], refs[n_cast + 1:]


def _cast_bands(srcs, dsts):
    for src, dst in zip(srcs, dsts):
        dst[...] = src[...].astype(BF16)


def _ffn_kernel(x_ref, g_ref, wg_ref, wu_ref, wd_ref, *refs):
    srcs, o_ref, dsts = _split_hosted(refs)
    o_ref[...] = _ffn_half_step(x_ref[...], g_ref, wg_ref, wu_ref, wd_ref)
    _cast_bands(srcs, dsts)


def _cast_band(start, rows, steps):
    band = -(-rows // steps)
    band = -(-band // BF16_ROWS) * BF16_ROWS
    while rows % band or start % band:
        band += BF16_ROWS
    return band


def _cast_specs(to_cast, steps):
    in_specs, out_specs, out_shapes = [], [], []
    for a, start, rows in to_cast:
        band = _cast_band(start, rows, steps)
        first, last = start // band, rows // band - 1
        in_specs.append(pl.BlockSpec(
            (band, a.shape[1]), lambda i, first=first, last=last: (first + jnp.minimum(i, last), 0)))
        out_specs.append(pl.BlockSpec((band, a.shape[1]), lambda i, last=last: (jnp.minimum(i, last), 0)))
        out_shapes.append(jax.ShapeDtypeStruct((rows, a.shape[1]), BF16))
    return in_specs, out_specs, out_shapes


def _ffn(x, norm_g, wg, wu, wd, to_cast):
    t, d = x.shape
    steps = t // TM_FFN
    tok = pl.BlockSpec((TM_FFN, d), lambda i: (i, 0))
    in_specs, out_specs, out_shapes = _cast_specs(to_cast, steps)
    outs = pl.pallas_call(
        _ffn_kernel,
        grid=(steps,),
        in_specs=[tok] + [_const_spec(a.shape) for a in (norm_g, wg, wu, wd)] + in_specs,
        out_specs=[tok] + out_specs,
        out_shape=[jax.ShapeDtypeStruct((t, d), F32)] + out_shapes,
        compiler_params=_params(1),
        name="ffn",
    )(x, norm_g, wg, wu, wd, *[a for a, _, _ in to_cast])
    return outs[0], outs[1:]


def _ones_rows(n_rows, rows_per_head, one_row):
    r = lax.broadcasted_iota(jnp.int32, (n_rows, 1), 0) % rows_per_head
    return jnp.where(r == one_row, 1.0, 0.0)


def _proj_kernel(h_ref, g_ref, wlat_ref, wdk_ref, wdqt_ref, wdvt_ref,
                 qn_ref, wqt_ref, kvn_ref, wk_ref, wvt_ref,
                 ct_ref, st_ref,
                 qt_ref, k_ref, vt_ref, dqt_ref, dk_ref, dvt_ref, *, q_scale, diff_scale):
    u = _rms(h_ref[...], g_ref[...]).astype(BF16)

    lat = lax.dot_general(u, wlat_ref[...], NT_DIMS, preferred_element_type=F32)
    q_lat = lat[:, :MLA_Q_LORA]
    kv_lat = lat[:, MLA_Q_LORA:MLA_Q_LORA + MLA_KV_LORA]
    o = MLA_Q_LORA + MLA_KV_LORA
    kr_a = lat[:, o:o + LANES]
    kr_b = lat[:, o + LANES:o + 2 * LANES]

    qn = _rms(q_lat, qn_ref[...]).astype(BF16)
    kvn = _rms(kv_lat, kvn_ref[...]).astype(BF16)

    qqt = lax.dot_general(wqt_ref[...], qn, NT_DIMS, preferred_element_type=F32)
    ct, st = ct_ref[...], st_ref[...]
    nope_rows = lax.broadcasted_iota(jnp.int32, (HEAD_SLOT, 1), 0) < MLA_NOPE
    cqt = (ct + jnp.where(nope_rows, 1.0, 0.0)) * q_scale
    sqt = st * q_scale
    r0, r1, r2 = MLA_NOPE, MLA_NOPE + MLA_ROPE // 2, MLA_NOPE + MLA_ROPE
    for hd in range(MLA_HEADS):
        rows = slice(hd * HEAD_SLOT, (hd + 1) * HEAD_SLOT)
        qh = qqt[rows]
        qh_sw = jnp.concatenate([qh[:r0], qh[r1:r2], qh[r0:r1], qh[r2:]], axis=0)
        qt_ref[0, rows, :] = (qh * cqt + qh_sw * sqt).astype(BF16)

    kk = jnp.dot(kvn, wk_ref[...], preferred_element_type=F32)
    kr = kr_a * ct.T + kr_b * st.T
    for hd in range(MLA_HEADS):
        sl = slice(hd * HEAD_SLOT, (hd + 1) * HEAD_SLOT)
        k_ref[:, sl] = (kk[:, sl] + kr).astype(BF16)

    vt = lax.dot_general(wvt_ref[...], kvn, NT_DIMS, preferred_element_type=F32)
    vt_ref[0] = (vt + _ones_rows(vt.shape[0], MLA_VROWS, MLA_V)).astype(BF16)

    dqt = lax.dot_general(wdqt_ref[...], u, NT_DIMS, preferred_element_type=F32)
    dqt_ref[0] = (dqt * diff_scale).astype(BF16)
    dk_ref[...] = lax.dot_general(u, wdk_ref[...], NT_DIMS, preferred_element_type=F32).astype(BF16)
    dvt = lax.dot_general(wdvt_ref[...], u, NT_DIMS, preferred_element_type=F32)
    dvt_ref[0] = (dvt + _ones_rows(dvt.shape[0], DIFF_VROWS, DIFF_V)).astype(BF16)


def _proj(h, g, wlat, wdk, wdqt, wdvt, qn, wqt, kvn, wk, wvt, tabs_t, batch, seq):
    t, d = h.shape
    nseq = seq // TM_PROJ
    tok = pl.BlockSpec((TM_PROJ, d), lambda i: (i, 0))
    tok_t = lambda rows: pl.BlockSpec((1, rows, TM_PROJ), lambda i: (i // nseq, 0, i % nseq))
    consts = [g, wlat, wdk, wdqt, wdvt, qn, wqt, kvn, wk, wvt]
    out_t = lambda rows: jax.ShapeDtypeStruct((batch, rows, seq), BF16)
    out_n = jax.ShapeDtypeStruct((t, d), BF16)
    return pl.pallas_call(
        functools.partial(_proj_kernel, q_scale=(MLA_NOPE + MLA_ROPE) ** -0.5 * LOG2E,
                          diff_scale=DIFF_DIM ** -0.5 * LOG2E),
        grid=(t // TM_PROJ,),
        in_specs=[tok] + [_const_spec(c.shape) for c in consts]
        + [pl.BlockSpec((HEAD_SLOT, TM_PROJ), lambda i: (0, i % nseq))] * 2,
        out_specs=[tok_t(wqt.shape[0]), tok, tok_t(wvt.shape[0]),
                   tok_t(wdqt.shape[0]), tok, tok_t(wdvt.shape[0])],
        out_shape=[out_t(wqt.shape[0]), out_n, out_t(wvt.shape[0]),
                   out_t(wdqt.shape[0]), out_n, out_t(wdvt.shape[0])],
        compiler_params=_params(1),
        name="in_proj",
    )(h, *consts, *tabs_t)


def _t5_bucket(rel):
    half = N_BUCKETS // 2
    max_exact = half // 2
    n = jnp.abs(rel)
    n2 = n * n
    assert (MAX_DISTANCE // max_exact) ** 2 == 2 ** (half - max_exact)
    large = max_exact
    for k in range(1, half - max_exact):
        large = large + (n2 >= (max_exact * max_exact) << k).astype(jnp.int32)
    return jnp.where(rel > 0, half, 0) + jnp.where(n < max_exact, n, large)


def _far_distance():
    half = N_BUCKETS // 2
    max_exact = half // 2
    n = max_exact
    while max_exact + sum(n * n >= (max_exact * max_exact) << k
                          for k in range(1, half - max_exact)) < half - 1:
        n += 1
    return n


def _bias_kernel(tab_ref, *refs):
    srcs, o_ref, dsts = _split_hosted(refs)
    _cast_bands(srcs, dsts)
    hd = pl.program_id(0)
    far = _far_distance()
    blk = 2 * CHUNK
    kloc = lax.broadcasted_iota(jnp.int32, (blk, blk), 0)
    qloc = lax.broadcasted_iota(jnp.int32, (blk, blk), 1)
    far_val = tab_ref[N_BUCKETS // 2 - 1, hd]
    for delta in range(2):
        for kb in range(TK // blk):
            for qb in range(TQ // blk):
                off = (kb - qb) * blk - delta * TK
                if delta == 0 and kb > qb:
                    val = jnp.full((blk, blk), NEG_INF, F32)
                elif off + blk - 1 <= -far:
                    val = jnp.zeros((blk, blk), F32)
                else:
                    bucket = _t5_bucket(kloc - qloc + off)
                    val = jnp.zeros((blk, blk), F32)
                    for b in range(N_BUCKETS):
                        val = jnp.where(bucket == b, tab_ref[b, hd], val)
                    val = (val - far_val) * LOG2E
                    if delta == 0 and kb == qb:
                        val = jnp.where(kloc // CHUNK <= qloc // CHUNK, val, NEG_INF)
                o_ref[0, delta, kb * blk:(kb + 1) * blk, qb * blk:(qb + 1) * blk] = val


def _bias_tiles(rel_bias, to_cast):
    in_specs, out_specs, out_shapes = _cast_specs(to_cast, DIFF_HEADS)
    outs = pl.pallas_call(
        _bias_kernel,
        grid=(DIFF_HEADS,),
        in_specs=[pl.BlockSpec(memory_space=pltpu.SMEM)] + in_specs,
        out_specs=[pl.BlockSpec((1, 2, TK, TQ), lambda h: (h, 0, 0, 0))] + out_specs,
        out_shape=[jax.ShapeDtypeStruct((DIFF_HEADS, 2, TK, TQ), F32)] + out_shapes,
        compiler_params=_params(1),
        name="bias_tiles",
    )(rel_bias, *[a for a, _, _ in to_cast])
    return outs[0], outs[1:]


FAR, SUB, DIAG = 0, 1, 2
assert TQ == TK and (TK // 2) % CHUNK == 0


def _scores(k, qt, bias, s_ref, mc_ref, diag):
    if diag:
        h = TK // 2
        parts = [(slice(0, h), slice(0, h)), (slice(0, TK), slice(h, TQ))]
    else:
        parts = [(slice(0, TK), slice(0, TQ))]
    for rows, cols in parts:
        s = jnp.dot(k[rows], qt[:, cols], preferred_element_type=F32)
        if bias is not None:
            s = s + bias[rows, cols]
        s_ref[rows, cols] = s
        mc_ref[:, cols] = jnp.max(s, axis=0, keepdims=True)


def _softmax_pv(s_ref, mc_ref, vt, m_ref, acc_ref, diag):
    m_prev = m_ref[...]
    m_new = jnp.maximum(m_prev, mc_ref[...])
    alpha = jnp.exp2(m_prev - m_new)
    if diag:
        h = TK // 2
        parts = [(slice(0, h), slice(0, h)), (slice(0, TK), slice(h, TQ))]
    else:
        parts = [(slice(0, TK), slice(0, TQ))]
    pv = []
    for rows, cols in parts:
        p = jnp.exp2(s_ref[rows, cols] - m_new[:, cols])
        pv.append(jnp.dot(vt[:, rows], p.astype(BF16), preferred_element_type=F32))
    pv = pv[0] if len(pv) == 1 else jnp.concatenate(pv, axis=1)
    acc_ref[...] = alpha * acc_ref[...] + pv
    m_ref[...] = m_new


FIRST_SLOT = 2
N_SLOTS = 3


def _sweep(i, near, qk, consume, qk_next):
    n_near = len(near)

    def slot_of(n):
        return FIRST_SLOT if n == 0 else (n + 1) % 2

    def run(tiles, own_first, next_kind):
        if own_first:
            qk(*tiles[0])
        for n, tile in enumerate(tiles):
            if n + 1 < len(tiles):
                qk(*tiles[n + 1])
            elif next_kind is not None:
                qk_next(next_kind)
            consume(*tile)

    def next_kind_after(count):
        return FAR if count + 1 > n_near else near[n_near - count - 1]

    for count in range(1, n_near + 1):
        @pl.when(i + 1 == count)
        def _(count=count):
            tiles = [(n, slot_of(n), kind) for n, kind in enumerate(near[-count:])]
            run(tiles, count <= 2, next_kind_after(count) if count > 1 else None)

    @pl.when(i + 1 > n_near)
    def _():
        n_far = i + 1 - n_near

        @pl.when(n_far == 1)
        def _():
            tiles = [(n, slot_of(n), kind) for n, kind in enumerate((FAR,) + near)]
            run(tiles, n_near <= 1, FAR)

        @pl.when(n_far >= 2)
        def _():
            rest = n_far - 1
            pairs = (rest - 1) // 2
            qk(1, 0, FAR)
            consume(0, FIRST_SLOT, FAR)

            def pair(j):
                qk(j + 1, 1, FAR)
                consume(j, 0, FAR)
                qk(j + 2, 0, FAR)
                consume(j + 1, 1, FAR)

            def body(t, carry):
                pair(4 * t + 1)
                pair(4 * t + 3)
                return carry

            lax.fori_loop(0, pairs // 2, body, 0)

            @pl.when(pairs % 2 == 1)
            def _():
                pair(2 * pairs - 1)

            j0 = 2 * pairs + 1
            for left in (1, 2):
                @pl.when(rest - 2 * pairs == left)
                def _(left=left):
                    tiles = [(j0 + n, n % 2, kind) for n, kind in enumerate((FAR,) * left + near)]
                    run(tiles, False, FAR)


def _kv_rows(j):
    return pl.ds(pl.multiple_of(j * TK, TK), TK)


def _attn_scratch(streams, v_rows):
    return [pltpu.VMEM((streams, N_SLOTS, TK, TQ), F32),
            pltpu.VMEM((streams, N_SLOTS, 1, TQ), F32),
            pltpu.VMEM((streams, 1, TQ), F32),
            pltpu.VMEM((streams, v_rows, TQ), F32)]


MLA_STREAMS = 2


def _q_cols(i):
    return pl.ds(pl.multiple_of(i * TQ, TQ), TQ)


def _mla_kernel(qt_ref, k_ref, vt_ref, mask_ref, ot_ref, s_ref, mc_ref, m_ref, acc_ref):
    def qk(j, slot, kind, qts):
        bias = mask_ref[...] if kind == DIAG else None
        for st in range(MLA_STREAMS):
            k = k_ref[0, _kv_rows(j), st * HEAD_SLOT:(st + 1) * HEAD_SLOT]
            _scores(k, qts[st], bias, s_ref.at[st, slot], mc_ref.at[st, slot], kind == DIAG)

    def consume(j, slot, kind):
        for st in range(MLA_STREAMS):
            vt = vt_ref[0, st * MLA_VROWS:(st + 1) * MLA_VROWS, _kv_rows(j)]
            _softmax_pv(s_ref.at[st, slot], mc_ref.at[st, slot], vt, m_ref.at[st], acc_ref.at[st],
                        kind == DIAG)

    def q_tile(i, carry):
        def q_tiles(t):
            return [qt_ref[0, st * HEAD_SLOT:(st + 1) * HEAD_SLOT, _q_cols(t)] for st in range(MLA_STREAMS)]

        qts_next = q_tiles(jnp.minimum(i + 1, n_q - 1))
        m_ref[...] = jnp.full(m_ref.shape, NEG_INF, F32)
        acc_ref[...] = jnp.zeros(acc_ref.shape, F32)
        _sweep(i, (DIAG,), functools.partial(qk, qts=q_tiles(i)), consume,
               lambda kind: qk(0, FIRST_SLOT, kind, qts_next))
        for st in range(MLA_STREAMS):
            acc = acc_ref[st]
            ot_ref[0, st * MLA_V:(st + 1) * MLA_V, _q_cols(i)] = (
                acc[:MLA_V] / acc[MLA_V:MLA_V + 1]).astype(BF16)
        return carry

    n_q = qt_ref.shape[2] // TQ
    lax.fori_loop(0, n_q, q_tile, 0)


def _mla_attention(qt, k, vt, mask):
    b, _, s = qt.shape
    groups = MLA_HEADS // MLA_STREAMS
    return pl.pallas_call(
        _mla_kernel,
        grid=(b, groups),
        in_specs=[pl.BlockSpec((1, MLA_STREAMS * HEAD_SLOT, s), lambda bi, h: (bi, h, 0)),
                  pl.BlockSpec((1, s, MLA_STREAMS * HEAD_SLOT), lambda bi, h: (bi, 0, h)),
                  pl.BlockSpec((1, MLA_STREAMS * MLA_VROWS, s), lambda bi, h: (bi, h, 0)),
                  _const_spec(mask.shape)],
        out_specs=pl.BlockSpec((1, MLA_STREAMS * MLA_V, s), lambda bi, h: (bi, h, 0)),
        out_shape=jax.ShapeDtypeStruct((b, MLA_HEADS * MLA_V, s), BF16),
        scratch_shapes=_attn_scratch(MLA_STREAMS, MLA_VROWS),
        compiler_params=_params(2),
        name="mla_attn",
    )(qt, k, vt, mask)


def _diff_kernel(qt_ref, k_ref, vt_ref, bias_ref, lq1_ref, lk1_ref, lq2_ref, lk2_ref, sub_ref,
                 ot_ref, s_ref, mc_ref, m_ref, acc_ref, *, lam_init):
    row = lax.broadcasted_iota(jnp.int32, (HEAD_SLOT, TQ), 0)
    zero = jnp.zeros((HEAD_SLOT, TQ), BF16)
    lam = (jnp.exp(jnp.sum(lq1_ref[...] * lk1_ref[...], axis=1, keepdims=True))
           - jnp.exp(jnp.sum(lq2_ref[...] * lk2_ref[...], axis=1, keepdims=True)) + lam_init)

    def qk(j, slot, kind, qts):
        k = k_ref[0, _kv_rows(j), :]
        bias = None if kind == FAR else bias_ref[0, 0 if kind == DIAG else 1]
        for st in range(2):
            _scores(k, qts[st], bias, s_ref.at[st, slot], mc_ref.at[st, slot], kind == DIAG)

    def consume(j, slot, kind):
        vt = vt_ref[0, :, _kv_rows(j)]
        for st in range(2):
            _softmax_pv(s_ref.at[st, slot], mc_ref.at[st, slot], vt, m_ref.at[st], acc_ref.at[st],
                        kind == DIAG)

    def q_tile(i, carry):
        def q_tiles(t):
            qt = qt_ref[0, :, _q_cols(t)]
            return [jnp.where(row < DIFF_DIM, qt, zero), jnp.where(row >= DIFF_DIM, qt, zero)]

        qts_next = q_tiles(jnp.minimum(i + 1, n_q - 1))
        m_ref[...] = jnp.full(m_ref.shape, NEG_INF, F32)
        acc_ref[...] = jnp.zeros(acc_ref.shape, F32)
        _sweep(i, (SUB, DIAG), functools.partial(qk, qts=q_tiles(i)), consume,
               lambda kind: qk(0, FIRST_SLOT, kind, qts_next))
        outs = [acc_ref[st, :DIFF_V] / acc_ref[st, DIFF_V:DIFF_V + 1] for st in range(2)]
        o = outs[0] - lam * outs[1]
        o = o * lax.rsqrt(jnp.mean(o * o, axis=0, keepdims=True) + EPS) * sub_ref[...]
        ot_ref[0, :, _q_cols(i)] = (o * (1.0 - lam_init)).astype(BF16)
        return carry

    n_q = qt_ref.shape[2] // TQ
    lax.fori_loop(0, n_q, q_tile, 0)


def _diff_attention(qt, k, vt, bias, lq1, lk1, lq2, lk2, sub_col, lam_init):
    b, _, s = qt.shape
    vec = _const_spec(lq1.shape)
    return pl.pallas_call(
        functools.partial(_diff_kernel, lam_init=lam_init),
        grid=(b, DIFF_HEADS),
        in_specs=[pl.BlockSpec((1, HEAD_SLOT, s), lambda bi, h: (bi, h, 0)),
                  pl.BlockSpec((1, s, HEAD_SLOT), lambda bi, h: (bi, 0, h)),
                  pl.BlockSpec((1, DIFF_VROWS, s), lambda bi, h: (bi, h, 0)),
                  pl.BlockSpec((1, 2, TK, TQ), lambda bi, h: (h, 0, 0, 0)),
                  vec, vec, vec, vec, _const_spec(sub_col.shape)],
        out_specs=pl.BlockSpec((1, DIFF_V, s), lambda bi, h: (bi, h, 0)),
        out_shape=jax.ShapeDtypeStruct((b, DIFF_HEADS * DIFF_V, s), BF16),
        scratch_shapes=_attn_scratch(2, DIFF_VROWS),
        compiler_params=_params(2),
        name="diff_attn",
    )(qt, k, vt, bias, lq1, lk1, lq2, lk2, sub_col)


def _memkv_kernel(m_ref, g_ref, w_ref, o_ref):
    mn = _rms(m_ref[...], g_ref[...]).astype(BF16)
    o_ref[...] = jnp.dot(mn, w_ref[...], preferred_element_type=F32).astype(BF16)


def _memkv(mem2d, g, w):
    rows, d = mem2d.shape
    return pl.pallas_call(
        _memkv_kernel,
        grid=(1,),
        in_specs=[_const_spec(mem2d.shape), _const_spec(g.shape), _const_spec(w.shape)],
        out_specs=_const_spec((rows, w.shape[1])),
        out_shape=jax.ShapeDtypeStruct((rows, w.shape[1]), BF16),
        compiler_params=_params(1),
        name="mem_kv",
    )(mem2d, g, w)


def _post_kernel(h_ref, oat_ref, obt_ref, g_ref, wga_ref, wgb_ref, wa_ref, wb_ref, wo_ref,
                 xg_ref, xwq_ref, kv_ref, xwo_ref, fg_ref, fwg_ref, fwu_ref, fwd_ref, fin_ref,
                 o_ref, *, scale, final_norm):
    h = h_ref[...]
    u = _rms(h, g_ref[...]).astype(BF16)
    gate_a = jax.nn.sigmoid(lax.dot_general(u, wga_ref[...], NT_DIMS, preferred_element_type=F32))
    gate_b = jax.nn.sigmoid(lax.dot_general(u, wgb_ref[...], NT_DIMS, preferred_element_type=F32))
    ya = lax.dot_general(oat_ref[0], wa_ref[...], TN_DIMS, preferred_element_type=F32)
    yb = lax.dot_general(obt_ref[0], wb_ref[...], TN_DIMS, preferred_element_type=F32)
    merged = gate_a * ya + gate_b * yb
    h = h + jnp.dot(merged.astype(BF16), wo_ref[...], preferred_element_type=F32)

    x = _rms(h, xg_ref[...]).astype(BF16)
    q = (jnp.dot(x, xwq_ref[...], preferred_element_type=F32) * scale).astype(BF16)
    kv = kv_ref[...]
    outs = []
    for hd in range(XATTN_HEADS):
        qh = q[:, hd * XATTN_DIM:(hd + 1) * XATTN_DIM]
        kh = kv[:, 2 * hd * XATTN_DIM:(2 * hd + 1) * XATTN_DIM]
        vh = kv[:, (2 * hd + 1) * XATTN_DIM:(2 * hd + 2) * XATTN_DIM]
        s = lax.dot_general(qh, kh, NT_DIMS, preferred_element_type=F32)
        m = jnp.max(s, axis=1, keepdims=True)
        p = jnp.exp2(s - m)
        l = jnp.sum(p, axis=1, keepdims=True)
        oh = jnp.dot(p.astype(BF16), vh, preferred_element_type=F32) / l
        outs.append(oh.astype(BF16))
    o = jnp.concatenate(outs, axis=1)
    h = h + jnp.dot(o, xwo_ref[...], preferred_element_type=F32)

    h = _ffn_half_step(h, fg_ref, fwg_ref, fwu_ref, fwd_ref)
    if final_norm:
        h = _rms(h, fin_ref[...])
    o_ref[...] = h


def _post(h, oat, obt, g, wga, wgb, wa, wb, wo, xg, xwq, kvmem, xwo, ffn, fin_g, final_norm, seq, mem_len):
    t, d = h.shape
    nseq = seq // TM_PROJ
    tok = pl.BlockSpec((TM_PROJ, d), lambda i: (i, 0))
    tok_t = lambda rows: pl.BlockSpec((1, rows, TM_PROJ), lambda i: (i // nseq, 0, i % nseq))
    cs = lambda a: _const_spec(a.shape)
    return pl.pallas_call(
        functools.partial(_post_kernel, scale=XATTN_DIM ** -0.5 * LOG2E, final_norm=final_norm),
        grid=(t // TM_PROJ,),
        in_specs=[tok, tok_t(oat.shape[1]), tok_t(obt.shape[1]), cs(g), cs(wga), cs(wgb), cs(wa), cs(wb), cs(wo),
                  cs(xg), cs(xwq),
                  pl.BlockSpec((mem_len, kvmem.shape[1]), lambda i: (i // nseq, 0)), cs(xwo)]
        + [cs(a) for a in ffn] + [cs(fin_g)],
        out_specs=tok,
        out_shape=jax.ShapeDtypeStruct((t, d), F32),
        compiler_params=_params(1),
        name="post_attn",
    )(h, oat, obt, g, wga, wgb, wa, wb, wo, xg, xwq, kvmem, xwo, *ffn, fin_g)


def _pad_heads(w, heads, width, slot):
    k = w.shape[0]
    w = w.reshape(k, heads, width)
    return jnp.pad(w, ((0, 0), (0, 0), (0, slot - width))).reshape(k, heads * slot)


def _rope_tables(seq):
    half = MLA_ROPE // 2
    pos = jnp.arange(seq)
    freqs = ROPE_BASE ** (-jnp.arange(half, dtype=F32) / half)
    ang = freqs[:, None] * pos.astype(F32)[None, :]
    cos, sin = jnp.cos(ang), jnp.sin(ang)
    z_nope = jnp.zeros((MLA_NOPE, seq), F32)
    z_pad = jnp.zeros((HEAD_SLOT - MLA_NOPE - MLA_ROPE, seq), F32)
    return (jnp.concatenate([z_nope, cos, cos, z_pad], axis=0),
            jnp.concatenate([z_nope, -sin, sin, z_pad], axis=0))


def kernel(x, mem, ffn1_norm, ffn1_w_gate, ffn1_w_up, ffn1_w_down, mix_norm, w_in, mla_q_norm, mla_w_q_up, mla_kv_norm, mla_w_kv_up, diff_lambda_q1, diff_lambda_k1, diff_lambda_q2, diff_lambda_k2, diff_sub_norm, rel_bias, w_branch_a, w_branch_b, w_out, xattn_norm, mem_norm, xattn_w_q, xattn_w_kv, xattn_w_o, ffn2_norm, ffn2_w_gate, ffn2_w_up, ffn2_w_down, final_norm):
    b, s, d = x.shape
    depth = ffn1_norm.shape[0]
    mem_len = mem.shape[1]
    t = b * s
    bf = lambda a: a.astype(BF16)
    row = lambda a: a.reshape(1, -1)

    tabs_t = _rope_tables(s)
    kpos = jnp.arange(TK)[:, None]
    qpos = jnp.arange(TQ)[None, :]
    mla_mask = jnp.where(kpos // CHUNK <= qpos // CHUNK, 0.0, NEG_INF).astype(F32)
    whole = lambda a: (a, 0, a.shape[0])
    bias, ffn1_w = _bias_tiles(rel_bias, [whole(ffn1_w_gate[0]), whole(ffn1_w_up[0]), whole(ffn1_w_down[0])])

    h = x.reshape(t, d)
    for l in range(depth):
        w_t = jnp.transpose(w_in[l])
        o_kr = MLA_Q_LORA + MLA_KV_LORA
        o_dq = o_kr + MLA_ROPE
        pieces = [(w_t, 0, o_dq)] + [(w_t, o_dq + n * d, d) for n in range(5)]
        if l > 0:
            ffn1_w = [bf(ffn1_w_gate[l]), bf(ffn1_w_up[l]), bf(ffn1_w_down[l])]
        h, (f2g, f2u, f2d, w_lat, wdq_t, wdk_t, wdv_t, wga_t, wgb_t) = _ffn(
            h, row(ffn1_norm[l]), *ffn1_w,
            [whole(ffn2_w_gate[l]), whole(ffn2_w_up[l]), whole(ffn2_w_down[l])] + pieces)

        w_kr = w_lat[o_kr:]
        zeros = lambda n: jnp.zeros((n, d), BF16)
        pad_l, pad_r = zeros(MLA_NOPE), zeros(HEAD_SLOT - MLA_NOPE - MLA_ROPE)
        w_kr_sw = jnp.concatenate([w_kr[MLA_ROPE // 2:], w_kr[:MLA_ROPE // 2]], axis=0)
        wlat_t = jnp.concatenate([w_lat[:o_kr], pad_l, w_kr, pad_r, pad_l, w_kr_sw, pad_r], axis=0)
        wdv_t = jnp.pad(wdv_t.reshape(DIFF_HEADS, DIFF_V, d),
                        ((0, 0), (0, DIFF_VROWS - DIFF_V), (0, 0))).reshape(-1, d)

        wq = _pad_heads(mla_w_q_up[l], MLA_HEADS, MLA_NOPE + MLA_ROPE, HEAD_SLOT)
        wkv3 = mla_w_kv_up[l].reshape(MLA_KV_LORA, MLA_HEADS, MLA_NOPE + MLA_V)
        wk = _pad_heads(wkv3[..., :MLA_NOPE].reshape(MLA_KV_LORA, -1), MLA_HEADS, MLA_NOPE, HEAD_SLOT)
        wv = _pad_heads(wkv3[..., MLA_NOPE:].reshape(MLA_KV_LORA, -1), MLA_HEADS, MLA_V, MLA_VROWS)

        qt, k, vt, dqt, dk, dvt = _proj(
            h, row(mix_norm[l]), wlat_t, wdk_t, wdq_t, wdv_t,
            row(mla_q_norm[l]), bf(wq.T), row(mla_kv_norm[l]), bf(wk), bf(wv.T),
            tabs_t, b, s)

        oat = _mla_attention(qt, k.reshape(b, s, d), vt, mla_mask)
        lam_init = 0.8 - 0.6 * math.exp(-0.3 * l)
        obt = _diff_attention(dqt, dk.reshape(b, s, d), dvt, bias,
                              row(diff_lambda_q1[l]), row(diff_lambda_k1[l]),
                              row(diff_lambda_q2[l]), row(diff_lambda_k2[l]),
                              diff_sub_norm[l].reshape(-1, 1), lam_init)

        kvmem = _memkv(mem.reshape(b * mem_len, d), row(mem_norm[l]), bf(xattn_w_kv[l]))
        h = _post(h, oat, obt, row(mix_norm[l]), wga_t, wgb_t, bf(w_branch_a[l]), bf(w_branch_b[l]),
                  bf(w_out[l]), row(xattn_norm[l]), bf(xattn_w_q[l]), kvmem, bf(xattn_w_o[l]),
                  (row(ffn2_norm[l]), f2g, f2u, f2d),
                  row(final_norm), l == depth - 1, s, mem_len)
    return h.reshape(b, s, d)
```

```python
import functools
import math

import jax
import jax.numpy as jnp
from jax import lax
from jax.experimental import pallas as pl
from jax.experimental.pallas import tpu as pltpu

F32 = jnp.float32
BF16 = jnp.bfloat16

CHUNK = 64
EPS = 1e-6
NEG_INF = -1e30
MLA_HEADS = 8
MLA_Q_LORA = 384
MLA_KV_LORA = 256
MLA_NOPE = 64
MLA_ROPE = 32
MLA_V = 64
ROPE_BASE = 10000.0
DIFF_HEADS = 8
DIFF_DIM = 64
DIFF_V = 2 * DIFF_DIM
N_BUCKETS = 32
MAX_DISTANCE = 128
XATTN_HEADS = 4
XATTN_DIM = 128
LOG2E = math.log2(math.e)

LANES = 128
HEAD_SLOT = LANES
BF16_ROWS = 16
MXU_TILE = 256
VMEM_LIMIT = 56 * 1024 * 1024

TM_FFN = 512
TM_PROJ = 512
TQ = 512
TK = 512

MLA_VROWS = (MLA_V + 1 + BF16_ROWS - 1) // BF16_ROWS * BF16_ROWS
DIFF_VROWS = (DIFF_V + 1 + BF16_ROWS - 1) // BF16_ROWS * BF16_ROWS

NT_DIMS = (((1,), (1,)), ((), ()))
TN_DIMS = (((0,), (0,)), ((), ()))


def _rms(x, g):
    return x * lax.rsqrt(jnp.mean(x * x, axis=-1, keepdims=True) + EPS) * g


def _const_spec(shape):
    nd = len(shape)
    return pl.BlockSpec(shape, lambda *_: (0,) * nd, pipeline_mode=pl.Buffered(1))


def _params(n_grid):
    return pltpu.CompilerParams(
        dimension_semantics=("arbitrary",) * n_grid, vmem_limit_bytes=VMEM_LIMIT)


def _ffn_splits(dff):
    half = -(-(dff // 2) // MXU_TILE) * MXU_TILE
    return ((0, half), (half, dff))


def _ffn_half_step(x, g_ref, wg_ref, wu_ref, wd_ref):
    nb = _rms(x, g_ref[...]).astype(BF16)
    y = None
    for lo, hi in _ffn_splits(wg_ref.shape[1]):
        g = jnp.dot(nb, wg_ref[:, lo:hi], preferred_element_type=F32)
        u = jnp.dot(nb, wu_ref[:, lo:hi], preferred_element_type=F32)
        h = (g * jax.nn.sigmoid(g) * u).astype(BF16)
        part = jnp.dot(h, wd_ref[lo:hi, :], preferred_element_type=F32)
        y = part if y is None else y + part
    return x + 0.5 * y


def _split_hosted(refs):
    n_cast = (len(refs) - 1) // 2
    return refs[:n_cast], refs[n_cast], refs[n_cast + 1:]


def _cast_bands(srcs, dsts):
    for src, dst in zip(srcs, dsts):
        dst[...] = src[...].astype(BF16)


def _ffn_kernel(x_ref, g_ref, wg_ref, wu_ref, wd_ref, *refs):
    srcs, o_ref, dsts = _split_hosted(refs)
    o_ref[...] = _ffn_half_step(x_ref[...], g_ref, wg_ref, wu_ref, wd_ref)
    _cast_bands(srcs, dsts)


def _cast_band(start, rows, steps):
    band = -(-rows // steps)
    band = -(-band // BF16_ROWS) * BF16_ROWS
    while rows % band or start % band:
        band += BF16_ROWS
    return band


def _cast_specs(to_cast, steps):
    in_specs, out_specs, out_shapes = [], [], []
    for a, start, rows in to_cast:
        band = _cast_band(start, rows, steps)
        first, last = start // band, rows // band - 1
        in_specs.append(pl.BlockSpec(
            (band, a.shape[1]), lambda i, first=first, last=last: (first + jnp.minimum(i, last), 0)))
        out_specs.append(pl.BlockSpec((band, a.shape[1]), lambda i, last=last: (jnp.minimum(i, last), 0)))
        out_shapes.append(jax.ShapeDtypeStruct((rows, a.shape[1]), BF16))
    return in_specs, out_specs, out_shapes


def _ffn(x, norm_g, wg, wu, wd, to_cast):
    t, d = x.shape
    steps = t // TM_FFN
    tok = pl.BlockSpec((TM_FFN, d), lambda i: (i, 0))
    in_specs, out_specs, out_shapes = _cast_specs(to_cast, steps)
    outs = pl.pallas_call(
        _ffn_kernel,
        grid=(steps,),
        in_specs=[tok] + [_const_spec(a.shape) for a in (norm_g, wg, wu, wd)] + in_specs,
        out_specs=[tok] + out_specs,
        out_shape=[jax.ShapeDtypeStruct((t, d), F32)] + out_shapes,
        compiler_params=_params(1),
        name="ffn",
    )(x, norm_g, wg, wu, wd, *[a for a, _, _ in to_cast])
    return outs[0], outs[1:]


def _ones_rows(n_rows, rows_per_head, one_row):
    r = lax.broadcasted_iota(jnp.int32, (n_rows, 1), 0) % rows_per_head
    return jnp.where(r == one_row, 1.0, 0.0)


def _proj_kernel(h_ref, g_ref, wlat_ref, wdk_ref, wdqt_ref, wdvt_ref,
                 qn_ref, wqt_ref, kvn_ref, wk_ref, wvt_ref,
                 ct_ref, st_ref,
                 qt_ref, k_ref, vt_ref, dqt_ref, dk_ref, dvt_ref, *, q_scale, diff_scale):
    u = _rms(h_ref[...], g_ref[...]).astype(BF16)

    lat = lax.dot_general(u, wlat_ref[...], NT_DIMS, preferred_element_type=F32)
    q_lat = lat[:, :MLA_Q_LORA]
    kv_lat = lat[:, MLA_Q_LORA:MLA_Q_LORA + MLA_KV_LORA]
    o = MLA_Q_LORA + MLA_KV_LORA
    kr_a = lat[:, o:o + LANES]
    kr_b = lat[:, o + LANES:o + 2 * LANES]

    qn = _rms(q_lat, qn_ref[...]).astype(BF16)
    kvn = _rms(kv_lat, kvn_ref[...]).astype(BF16)

    qqt = lax.dot_general(wqt_ref[...], qn, NT_DIMS, preferred_element_type=F32)
    ct, st = ct_ref[...], st_ref[...]
    nope_rows = lax.broadcasted_iota(jnp.int32, (HEAD_SLOT, 1), 0) < MLA_NOPE
    cqt = (ct + jnp.where(nope_rows, 1.0, 0.0)) * q_scale
    sqt = st * q_scale
    r0, r1, r2 = MLA_NOPE, MLA_NOPE + MLA_ROPE // 2, MLA_NOPE + MLA_ROPE
    for hd in range(MLA_HEADS):
        rows = slice(hd * HEAD_SLOT, (hd + 1) * HEAD_SLOT)
        qh = qqt[rows]
        qh_sw = jnp.concatenate([qh[:r0], qh[r1:r2], qh[r0:r1], qh[r2:]], axis=0)
        qt_ref[0, rows, :] = (qh * cqt + qh_sw * sqt).astype(BF16)

    kk = jnp.dot(kvn, wk_ref[...], preferred_element_type=F32)
    kr = kr_a * ct.T + kr_b * st.T
    for hd in range(MLA_HEADS):
        sl = slice(hd * HEAD_SLOT, (hd + 1) * HEAD_SLOT)
        k_ref[:, sl] = (kk[:, sl] + kr).astype(BF16)

    vt = lax.dot_general(wvt_ref[...], kvn, NT_DIMS, preferred_element_type=F32)
    vt_ref[0] = (vt + _ones_rows(vt.shape[0], MLA_VROWS, MLA_V)).astype(BF16)

    dqt = lax.dot_general(wdqt_ref[...], u, NT_DIMS, preferred_element_type=F32)
    dqt_ref[0] = (dqt * diff_scale).astype(BF16)
    dk_ref[...] = lax.dot_general(u, wdk_ref[...], NT_DIMS, preferred_element_type=F32).astype(BF16)
    dvt = lax.dot_general(wdvt_ref[...], u, NT_DIMS, preferred_element_type=F32)
    dvt_ref[0] = (dvt + _ones_rows(dvt.shape[0], DIFF_VROWS, DIFF_V)).astype(BF16)


def _proj(h, g, wlat, wdk, wdqt, wdvt, qn, wqt, kvn, wk, wvt, tabs_t, batch, seq):
    t, d = h.shape
    nseq = seq // TM_PROJ
    tok = pl.BlockSpec((TM_PROJ, d), lambda i: (i, 0))
    tok_t = lambda rows: pl.BlockSpec((1, rows, TM_PROJ), lambda i: (i // nseq, 0, i % nseq))
    consts = [g, wlat, wdk, wdqt, wdvt, qn, wqt, kvn, wk, wvt]
    out_t = lambda rows: jax.ShapeDtypeStruct((batch, rows, seq), BF16)
    out_n = jax.ShapeDtypeStruct((t, d), BF16)
    return pl.pallas_call(
        functools.partial(_proj_kernel, q_scale=(MLA_NOPE + MLA_ROPE) ** -0.5 * LOG2E,
                          diff_scale=DIFF_DIM ** -0.5 * LOG2E),
        grid=(t // TM_PROJ,),
        in_specs=[tok] + [_const_spec(c.shape) for c in consts]
        + [pl.BlockSpec((HEAD_SLOT, TM_PROJ), lambda i: (0, i % nseq))] * 2,
        out_specs=[tok_t(wqt.shape[0]), tok, tok_t(wvt.shape[0]),
                   tok_t(wdqt.shape[0]), tok, tok_t(wdvt.shape[0])],
        out_shape=[out_t(wqt.shape[0]), out_n, out_t(wvt.shape[0]),
                   out_t(wdqt.shape[0]), out_n, out_t(wdvt.shape[0])],
        compiler_params=_params(1),
        name="in_proj",
    )(h, *consts, *tabs_t)


def _t5_bucket(rel):
    half = N_BUCKETS // 2
    max_exact = half // 2
    n = jnp.abs(rel)
    n2 = n * n
    assert (MAX_DISTANCE // max_exact) ** 2 == 2 ** (half - max_exact)
    large = max_exact
    for k in range(1, half - max_exact):
        large = large + (n2 >= (max_exact * max_exact) << k).astype(jnp.int32)
    return jnp.where(rel > 0, half, 0) + jnp.where(n < max_exact, n, large)


def _far_distance():
    half = N_BUCKETS // 2
    max_exact = half // 2
    n = max_exact
    while max_exact + sum(n * n >= (max_exact * max_exact) << k
                          for k in range(1, half - max_exact)) < half - 1:
        n += 1
    return n


def _bias_kernel(tab_ref, *refs):
    srcs, o_ref, dsts = _split_hosted(refs)
    _cast_bands(srcs, dsts)
    hd = pl.program_id(0)
    far = _far_distance()
    blk = 2 * CHUNK
    kloc = lax.broadcasted_iota(jnp.int32, (blk, blk), 0)
    qloc = lax.broadcasted_iota(jnp.int32, (blk, blk), 1)
    far_val = tab_ref[N_BUCKETS // 2 - 1, hd]
    for delta in range(2):
        for kb in range(TK // blk):
            for qb in range(TQ // blk):
                off = (kb - qb) * blk - delta * TK
                if delta == 0 and kb > qb:
                    val = jnp.full((blk, blk), NEG_INF, F32)
                elif off + blk - 1 <= -far:
                    val = jnp.zeros((blk, blk), F32)
                else:
                    bucket = _t5_bucket(kloc - qloc + off)
                    val = jnp.zeros((blk, blk), F32)
                    for b in range(N_BUCKETS):
                        val = jnp.where(bucket == b, tab_ref[b, hd], val)
                    val = (val - far_val) * LOG2E
                    if delta == 0 and kb == qb:
                        val = jnp.where(kloc // CHUNK <= qloc // CHUNK, val, NEG_INF)
                o_ref[0, delta, kb * blk:(kb + 1) * blk, qb * blk:(qb + 1) * blk] = val


def _bias_tiles(rel_bias, to_cast):
    in_specs, out_specs, out_shapes = _cast_specs(to_cast, DIFF_HEADS)
    outs = pl.pallas_call(
        _bias_kernel,
        grid=(DIFF_HEADS,),
        in_specs=[pl.BlockSpec(memory_space=pltpu.SMEM)] + in_specs,
        out_specs=[pl.BlockSpec((1, 2, TK, TQ), lambda h: (h, 0, 0, 0))] + out_specs,
        out_shape=[jax.ShapeDtypeStruct((DIFF_HEADS, 2, TK, TQ), F32)] + out_shapes,
        compiler_params=_params(1),
        name="bias_tiles",
    )(rel_bias, *[a for a, _, _ in to_cast])
    return outs[0], outs[1:]


FAR, SUB, DIAG = 0, 1, 2
assert TQ == TK and (TK // 2) % CHUNK == 0


def _scores(k, qt, bias, s_ref, mc_ref, diag):
    if diag:
        h = TK // 2
        parts = [(slice(0, h), slice(0, h)), (slice(0, TK), slice(h, TQ))]
    else:
        parts = [(slice(0, TK), slice(0, TQ))]
    for rows, cols in parts:
        s = jnp.dot(k[rows], qt[:, cols], preferred_element_type=F32)
        if bias is not None:
            s = s + bias[rows, cols]
        s_ref[rows, cols] = s
        mc_ref[:, cols] = jnp.max(s, axis=0, keepdims=True)


def _softmax_pv(s_ref, mc_ref, vt, m_ref, acc_ref, diag):
    m_prev = m_ref[...]
    m_new = jnp.maximum(m_prev, mc_ref[...])
    alpha = jnp.exp2(m_prev - m_new)
    if diag:
        h = TK // 2
        parts = [(slice(0, h), slice(0, h)), (slice(0, TK), slice(h, TQ))]
    else:
        parts = [(slice(0, TK), slice(0, TQ))]
    pv = []
    for rows, cols in parts:
        p = jnp.exp2(s_ref[rows, cols] - m_new[:, cols])
        pv.append(jnp.dot(vt[:, rows], p.astype(BF16), preferred_element_type=F32))
    pv = pv[0] if len(pv) == 1 else jnp.concatenate(pv, axis=1)
    acc_ref[...] = alpha * acc_ref[...] + pv
    m_ref[...] = m_new


FIRST_SLOT = 2
N_SLOTS = 3


def _sweep(i, near, qk, consume, qk_next):
    n_near = len(near)

    def slot_of(n):
        return FIRST_SLOT if n == 0 else (n + 1) % 2

    def run(tiles, own_first, next_kind):
        if own_first:
            qk(*tiles[0])
        for n, tile in enumerate(tiles):
            if n + 1 < len(tiles):
                qk(*tiles[n + 1])
            elif next_kind is not None:
                qk_next(next_kind)
            consume(*tile)

    def next_kind_after(count):
        return FAR if count + 1 > n_near else near[n_near - count - 1]

    for count in range(1, n_near + 1):
        @pl.when(i + 1 == count)
        def _(count=count):
            tiles = [(n, slot_of(n), kind) for n, kind in enumerate(near[-count:])]
            run(tiles, count <= 2, next_kind_after(count) if count > 1 else None)

    @pl.when(i + 1 > n_near)
    def _():
        n_far = i + 1 - n_near

        @pl.when(n_far == 1)
        def _():
            tiles = [(n, slot_of(n), kind) for n, kind in enumerate((FAR,) + near)]
            run(tiles, n_near <= 1, FAR)

        rest = n_far - 1
        pairs = (rest - 1) // 2

        for left in (1, 2):
            @pl.when(rest == left)
            def _(left=left):
                tiles = [(n, slot_of(n), kind) for n, kind in enumerate((FAR,) * (1 + left) + near)]
                run(tiles, False, FAR)

        @pl.when(pairs >= 1)
        def _():
            def pair(j):
                qk(j + 1, 1, FAR)
                consume(j, 0, FAR)
                qk(j + 2, 0, FAR)
                consume(j + 1, 1, FAR)

            for head in (1, 2):
                @pl.when(pairs % 2 == head % 2)
                def _(head=head):
                    qk(1, 0, FAR)
                    consume(0, FIRST_SLOT, FAR)
                    for n in range(head):
                        pair(2 * n + 1)

            first = 2 - pairs % 2

            def body(t, carry):
                j = 4 * t + 2 * first + 1
                pair(j)
                pair(j + 2)
                return carry

            lax.fori_loop(0, (pairs - first) // 2, body, 0)
            j0 = 2 * pairs + 1
            for left in (1, 2):
                @pl.when(rest - 2 * pairs == left)
                def _(left=left):
                    tiles = [(j0 + n, n % 2, kind) for n, kind in enumerate((FAR,) * left + near)]
                    run(tiles, False, FAR)


def _kv_rows(j):
    return pl.ds(pl.multiple_of(j * TK, TK), TK)


def _attn_scratch(streams, v_rows):
    return [pltpu.VMEM((streams, N_SLOTS, TK, TQ), F32),
            pltpu.VMEM((streams, N_SLOTS, 1, TQ), F32),
            pltpu.VMEM((streams, 1, TQ), F32),
            pltpu.VMEM((streams, v_rows, TQ), F32)]


MLA_STREAMS = 2


def _q_cols(i):
    return pl.ds(pl.multiple_of(i * TQ, TQ), TQ)


def _mla_kernel(qt_ref, k_ref, vt_ref, mask_ref, ot_ref, s_ref, mc_ref, m_ref, acc_ref):
    def qk(j, slot, kind, qts):
        bias = mask_ref[...] if kind == DIAG else None
        for st in range(MLA_STREAMS):
            k = k_ref[0, _kv_rows(j), st * HEAD_SLOT:(st + 1) * HEAD_SLOT]
            _scores(k, qts[st], bias, s_ref.at[st, slot], mc_ref.at[st, slot], kind == DIAG)

    def consume(j, slot, kind):
        for st in range(MLA_STREAMS):
            vt = vt_ref[0, st * MLA_VROWS:(st + 1) * MLA_VROWS, _kv_rows(j)]
            _softmax_pv(s_ref.at[st, slot], mc_ref.at[st, slot], vt, m_ref.at[st], acc_ref.at[st],
                        kind == DIAG)

    def q_tile(i, carry):
        def q_tiles(t):
            return [qt_ref[0, st * HEAD_SLOT:(st + 1) * HEAD_SLOT, _q_cols(t)] for st in range(MLA_STREAMS)]

        qts_next = q_tiles(jnp.minimum(i + 1, n_q - 1))
        m_ref[...] = jnp.full(m_ref.shape, NEG_INF, F32)
        acc_ref[...] = jnp.zeros(acc_ref.shape, F32)
        _sweep(i, (DIAG,), functools.partial(qk, qts=q_tiles(i)), consume,
               lambda kind: qk(0, FIRST_SLOT, kind, qts_next))
        for st in range(MLA_STREAMS):
            acc = acc_ref[st]
            ot_ref[0, st * MLA_V:(st + 1) * MLA_V, _q_cols(i)] = (
                acc[:MLA_V] / acc[MLA_V:MLA_V + 1]).astype(BF16)
        return carry

    n_q = qt_ref.shape[2] // TQ
    lax.fori_loop(0, n_q, q_tile, 0)


def _mla_attention(qt, k, vt, mask):
    b, _, s = qt.shape
    groups = MLA_HEADS // MLA_STREAMS
    return pl.pallas_call(
        _mla_kernel,
        grid=(b, groups),
        in_specs=[pl.BlockSpec((1, MLA_STREAMS * HEAD_SLOT, s), lambda bi, h: (bi, h, 0)),
                  pl.BlockSpec((1, s, MLA_STREAMS * HEAD_SLOT), lambda bi, h: (bi, 0, h)),
                  pl.BlockSpec((1, MLA_STREAMS * MLA_VROWS, s), lambda bi, h: (bi, h, 0)),
                  _const_spec(mask.shape)],
        out_specs=pl.BlockSpec((1, MLA_STREAMS * MLA_V, s), lambda bi, h: (bi, h, 0)),
        out_shape=jax.ShapeDtypeStruct((b, MLA_HEADS * MLA_V, s), BF16),
        scratch_shapes=_attn_scratch(MLA_STREAMS, MLA_VROWS),
        compiler_params=_params(2),
        name="mla_attn",
    )(qt, k, vt, mask)


def _diff_kernel(qt_ref, k_ref, vt_ref, bias_ref, lq1_ref, lk1_ref, lq2_ref, lk2_ref, sub_ref,
                 ot_ref, s_ref, mc_ref, m_ref, acc_ref, *, lam_init):
    row = lax.broadcasted_iota(jnp.int32, (HEAD_SLOT, TQ), 0)
    zero = jnp.zeros((HEAD_SLOT, TQ), BF16)
    lam = (jnp.exp(jnp.sum(lq1_ref[...] * lk1_ref[...], axis=1, keepdims=True))
           - jnp.exp(jnp.sum(lq2_ref[...] * lk2_ref[...], axis=1, keepdims=True)) + lam_init)

    def qk(j, slot, kind, qts):
        k = k_ref[0, _kv_rows(j), :]
        bias = None if kind == FAR else bias_ref[0, 0 if kind == DIAG else 1]
        for st in range(2):
            _scores(k, qts[st], bias, s_ref.at[st, slot], mc_ref.at[st, slot], kind == DIAG)

    def consume(j, slot, kind):
        vt = vt_ref[0, :, _kv_rows(j)]
        for st in range(2):
            _softmax_pv(s_ref.at[st, slot], mc_ref.at[st, slot], vt, m_ref.at[st], acc_ref.at[st],
                        kind == DIAG)

    def q_tile(i, carry):
        def q_tiles(t):
            qt = qt_ref[0, :, _q_cols(t)]
            return [jnp.where(row < DIFF_DIM, qt, zero), jnp.where(row >= DIFF_DIM, qt, zero)]

        qts_next = q_tiles(jnp.minimum(i + 1, n_q - 1))
        m_ref[...] = jnp.full(m_ref.shape, NEG_INF, F32)
        acc_ref[...] = jnp.zeros(acc_ref.shape, F32)
        _sweep(i, (SUB, DIAG), functools.partial(qk, qts=q_tiles(i)), consume,
               lambda kind: qk(0, FIRST_SLOT, kind, qts_next))
        outs = [acc_ref[st, :DIFF_V] / acc_ref[st, DIFF_V:DIFF_V + 1] for st in range(2)]
        o = outs[0] - lam * outs[1]
        o = o * lax.rsqrt(jnp.mean(o * o, axis=0, keepdims=True) + EPS) * sub_ref[...]
        ot_ref[0, :, _q_cols(i)] = (o * (1.0 - lam_init)).astype(BF16)
        return carry

    n_q = qt_ref.shape[2] // TQ
    lax.fori_loop(0, n_q, q_tile, 0)


def _diff_attention(qt, k, vt, bias, lq1, lk1, lq2, lk2, sub_col, lam_init):
    b, _, s = qt.shape
    vec = _const_spec(lq1.shape)
    return pl.pallas_call(
        functools.partial(_diff_kernel, lam_init=lam_init),
        grid=(b, DIFF_HEADS),
        in_specs=[pl.BlockSpec((1, HEAD_SLOT, s), lambda bi, h: (bi, h, 0)),
                  pl.BlockSpec((1, s, HEAD_SLOT), lambda bi, h: (bi, 0, h)),
                  pl.BlockSpec((1, DIFF_VROWS, s), lambda bi, h: (bi, h, 0)),
                  pl.BlockSpec((1, 2, TK, TQ), lambda bi, h: (h, 0, 0, 0)),
                  vec, vec, vec, vec, _const_spec(sub_col.shape)],
        out_specs=pl.BlockSpec((1, DIFF_V, s), lambda bi, h: (bi, h, 0)),
        out_shape=jax.ShapeDtypeStruct((b, DIFF_HEADS * DIFF_V, s), BF16),
        scratch_shapes=_attn_scratch(2, DIFF_VROWS),
        compiler_params=_params(2),
        name="diff_attn",
    )(qt, k, vt, bias, lq1, lk1, lq2, lk2, sub_col)


def _memkv_kernel(m_ref, g_ref, w_ref, o_ref):
    mn = _rms(m_ref[...], g_ref[...]).astype(BF16)
    o_ref[...] = jnp.dot(mn, w_ref[...], preferred_element_type=F32).astype(BF16)


def _memkv(mem2d, g, w):
    rows, d = mem2d.shape
    return pl.pallas_call(
        _memkv_kernel,
        grid=(1,),
        in_specs=[_const_spec(mem2d.shape), _const_spec(g.shape), _const_spec(w.shape)],
        out_specs=_const_spec((rows, w.shape[1])),
        out_shape=jax.ShapeDtypeStruct((rows, w.shape[1]), BF16),
        compiler_params=_params(1),
        name="mem_kv",
    )(mem2d, g, w)


def _post_kernel(h_ref, oat_ref, obt_ref, g_ref, wga_ref, wgb_ref, wa_ref, wb_ref, wo_ref,
                 xg_ref, xwq_ref, kv_ref, xwo_ref, fg_ref, fwg_ref, fwu_ref, fwd_ref, fin_ref,
                 o_ref, *, scale, final_norm):
    h = h_ref[...]
    u = _rms(h, g_ref[...]).astype(BF16)
    gate_a = jax.nn.sigmoid(lax.dot_general(u, wga_ref[...], NT_DIMS, preferred_element_type=F32))
    gate_b = jax.nn.sigmoid(lax.dot_general(u, wgb_ref[...], NT_DIMS, preferred_element_type=F32))
    ya = lax.dot_general(oat_ref[0], wa_ref[...], TN_DIMS, preferred_element_type=F32)
    yb = lax.dot_general(obt_ref[0], wb_ref[...], TN_DIMS, preferred_element_type=F32)
    merged = gate_a * ya + gate_b * yb
    h = h + jnp.dot(merged.astype(BF16), wo_ref[...], preferred_element_type=F32)

    x = _rms(h, xg_ref[...]).astype(BF16)
    q = (jnp.dot(x, xwq_ref[...], preferred_element_type=F32) * scale).astype(BF16)
    kv = kv_ref[...]
    outs = []
    for hd in range(XATTN_HEADS):
        qh = q[:, hd * XATTN_DIM:(hd + 1) * XATTN_DIM]
        kh = kv[:, 2 * hd * XATTN_DIM:(2 * hd + 1) * XATTN_DIM]
        vh = kv[:, (2 * hd + 1) * XATTN_DIM:(2 * hd + 2) * XATTN_DIM]
        s = lax.dot_general(qh, kh, NT_DIMS, preferred_element_type=F32)
        m = jnp.max(s, axis=1, keepdims=True)
        p = jnp.exp2(s - m)
        l = jnp.sum(p, axis=1, keepdims=True)
        oh = jnp.dot(p.astype(BF16), vh, preferred_element_type=F32) / l
        outs.append(oh.astype(BF16))
    o = jnp.concatenate(outs, axis=1)
    h = h + jnp.dot(o, xwo_ref[...], preferred_element_type=F32)

    h = _ffn_half_step(h, fg_ref, fwg_ref, fwu_ref, fwd_ref)
    if final_norm:
        h = _rms(h, fin_ref[...])
    o_ref[...] = h


def _post(h, oat, obt, g, wga, wgb, wa, wb, wo, xg, xwq, kvmem, xwo, ffn, fin_g, final_norm, seq, mem_len):
    t, d = h.shape
    nseq = seq // TM_PROJ
    tok = pl.BlockSpec((TM_PROJ, d), lambda i: (i, 0))
    tok_t = lambda rows: pl.BlockSpec((1, rows, TM_PROJ), lambda i: (i // nseq, 0, i % nseq))
    cs = lambda a: _const_spec(a.shape)
    return pl.pallas_call(
        functools.partial(_post_kernel, scale=XATTN_DIM ** -0.5 * LOG2E, final_norm=final_norm),
        grid=(t // TM_PROJ,),
        in_specs=[tok, tok_t(oat.shape[1]), tok_t(obt.shape[1]), cs(g), cs(wga), cs(wgb), cs(wa), cs(wb), cs(wo),
                  cs(xg), cs(xwq),
                  pl.BlockSpec((mem_len, kvmem.shape[1]), lambda i: (i // nseq, 0)), cs(xwo)]
        + [cs(a) for a in ffn] + [cs(fin_g)],
        out_specs=tok,
        out_shape=jax.ShapeDtypeStruct((t, d), F32),
        compiler_params=_params(1),
        name="post_attn",
    )(h, oat, obt, g, wga, wgb, wa, wb, wo, xg, xwq, kvmem, xwo, *ffn, fin_g)


def _pad_heads(w, heads, width, slot):
    k = w.shape[0]
    w = w.reshape(k, heads, width)
    return jnp.pad(w, ((0, 0), (0, 0), (0, slot - width))).reshape(k, heads * slot)


def _rope_tables(seq):
    half = MLA_ROPE // 2
    pos = jnp.arange(seq)
    freqs = ROPE_BASE ** (-jnp.arange(half, dtype=F32) / half)
    ang = freqs[:, None] * pos.astype(F32)[None, :]
    cos, sin = jnp.cos(ang), jnp.sin(ang)
    z_nope = jnp.zeros((MLA_NOPE, seq), F32)
    z_pad = jnp.zeros((HEAD_SLOT - MLA_NOPE - MLA_ROPE, seq), F32)
    return (jnp.concatenate([z_nope, cos, cos, z_pad], axis=0),
            jnp.concatenate([z_nope, -sin, sin, z_pad], axis=0))


def kernel(x, mem, ffn1_norm, ffn1_w_gate, ffn1_w_up, ffn1_w_down, mix_norm, w_in, mla_q_norm, mla_w_q_up, mla_kv_norm, mla_w_kv_up, diff_lambda_q1, diff_lambda_k1, diff_lambda_q2, diff_lambda_k2, diff_sub_norm, rel_bias, w_branch_a, w_branch_b, w_out, xattn_norm, mem_norm, xattn_w_q, xattn_w_kv, xattn_w_o, ffn2_norm, ffn2_w_gate, ffn2_w_up, ffn2_w_down, final_norm):
    b, s, d = x.shape
    depth = ffn1_norm.shape[0]
    mem_len = mem.shape[1]
    t = b * s
    bf = lambda a: a.astype(BF16)
    row = lambda a: a.reshape(1, -1)

    tabs_t = _rope_tables(s)
    kpos = jnp.arange(TK)[:, None]
    qpos = jnp.arange(TQ)[None, :]
    mla_mask = jnp.where(kpos // CHUNK <= qpos // CHUNK, 0.0, NEG_INF).astype(F32)
    whole = lambda a: (a, 0, a.shape[0])
    bias, ffn1_w = _bias_tiles(rel_bias, [whole(ffn1_w_gate[0]), whole(ffn1_w_up[0]), whole(ffn1_w_down[0])])

    h = x.reshape(t, d)
    for l in range(depth):
        w_t = jnp.transpose(w_in[l])
        o_kr = MLA_Q_LORA + MLA_KV_LORA
        o_dq = o_kr + MLA_ROPE
        pieces = [(w_t, 0, o_dq)] + [(w_t, o_dq + n * d, d) for n in range(5)]
        if l > 0:
            ffn1_w = [bf(ffn1_w_gate[l]), bf(ffn1_w_up[l]), bf(ffn1_w_down[l])]
        h, (f2g, f2u, f2d, w_lat, wdq_t, wdk_t, wdv_t, wga_t, wgb_t) = _ffn(
            h, row(ffn1_norm[l]), *ffn1_w,
            [whole(ffn2_w_gate[l]), whole(ffn2_w_up[l]), whole(ffn2_w_down[l])] + pieces)

        w_kr = w_lat[o_kr:]
        zeros = lambda n: jnp.zeros((n, d), BF16)
        pad_l, pad_r = zeros(MLA_NOPE), zeros(HEAD_SLOT - MLA_NOPE - MLA_ROPE)
        w_kr_sw = jnp.concatenate([w_kr[MLA_ROPE // 2:], w_kr[:MLA_ROPE // 2]], axis=0)
        wlat_t = jnp.concatenate([w_lat[:o_kr], pad_l, w_kr, pad_r, pad_l, w_kr_sw, pad_r], axis=0)
        wdv_t = jnp.pad(wdv_t.reshape(DIFF_HEADS, DIFF_V, d),
                        ((0, 0), (0, DIFF_VROWS - DIFF_V), (0, 0))).reshape(-1, d)

        wq = _pad_heads(mla_w_q_up[l], MLA_HEADS, MLA_NOPE + MLA_ROPE, HEAD_SLOT)
        wkv3 = mla_w_kv_up[l].reshape(MLA_KV_LORA, MLA_HEADS, MLA_NOPE + MLA_V)
        wk = _pad_heads(wkv3[..., :MLA_NOPE].reshape(MLA_KV_LORA, -1), MLA_HEADS, MLA_NOPE, HEAD_SLOT)
        wv = _pad_heads(wkv3[..., MLA_NOPE:].reshape(MLA_KV_LORA, -1), MLA_HEADS, MLA_V, MLA_VROWS)

        qt, k, vt, dqt, dk, dvt = _proj(
            h, row(mix_norm[l]), wlat_t, wdk_t, wdq_t, wdv_t,
            row(mla_q_norm[l]), bf(wq.T), row(mla_kv_norm[l]), bf(wk), bf(wv.T),
            tabs_t, b, s)

        oat = _mla_attention(qt, k.reshape(b, s, d), vt, mla_mask)
        lam_init = 0.8 - 0.6 * math.exp(-0.3 * l)
        obt = _diff_attention(dqt, dk.reshape(b, s, d), dvt, bias,
                              row(diff_lambda_q1[l]), row(diff_lambda_k1[l]),
                              row(diff_lambda_q2[l]), row(diff_lambda_k2[l]),
                              diff_sub_norm[l].reshape(-1, 1), lam_init)

        kvmem = _memkv(mem.reshape(b * mem_len, d), row(mem_norm[l]), bf(xattn_w_kv[l]))
        h = _post(h, oat, obt, row(mix_norm[l]), wga_t, wgb_t, bf(w_branch_a[l]), bf(w_branch_b[l]),
                  bf(w_out[l]), row(xattn_norm[l]), bf(xattn_w_q[l]), kvmem, bf(xattn_w_o[l]),
                  (row(ffn2_norm[l]), f2g, f2u, f2d),
                  row(final_norm), l == depth - 1, s, mem_len)
    return h.reshape(b, s, d)
```

```python
import functools
import math

import jax
import jax.numpy as jnp
from jax import lax
from jax.experimental import pallas as pl
from jax.experimental.pallas import tpu as pltpu

F32 = jnp.float32
BF16 = jnp.bfloat16

CHUNK = 64
EPS = 1e-6
NEG_INF = -1e30
MLA_HEADS = 8
MLA_Q_LORA = 384
MLA_KV_LORA = 256
MLA_NOPE = 64
MLA_ROPE = 32
MLA_V = 64
ROPE_BASE = 10000.0
DIFF_HEADS = 8
DIFF_DIM = 64
DIFF_V = 2 * DIFF_DIM
N_BUCKETS = 32
MAX_DISTANCE = 128
XATTN_HEADS = 4
XATTN_DIM = 128
LOG2E = math.log2(math.e)

LANES = 128
HEAD_SLOT = LANES
BF16_ROWS = 16
MXU_TILE = 256
VMEM_LIMIT = 56 * 1024 * 1024

TM_FFN = 512
TM_PROJ = 512
TQ = 512
TK = 512

MLA_VROWS = (MLA_V + 1 + BF16_ROWS - 1) // BF16_ROWS * BF16_ROWS
DIFF_VROWS = (DIFF_V + 1 + BF16_ROWS - 1) // BF16_ROWS * BF16_ROWS

NT_DIMS = (((1,), (1,)), ((), ()))
TN_DIMS = (((0,), (0,)), ((), ()))


def _rms(x, g):
    return x * lax.rsqrt(jnp.mean(x * x, axis=-1, keepdims=True) + EPS) * g


def _const_spec(shape):
    nd = len(shape)
    return pl.BlockSpec(shape, lambda *_: (0,) * nd, pipeline_mode=pl.Buffered(1))


def _params(n_grid):
    return pltpu.CompilerParams(
        dimension_semantics=("arbitrary",) * n_grid, vmem_limit_bytes=VMEM_LIMIT)


def _ffn_splits(dff):
    half = -(-(dff // 2) // MXU_TILE) * MXU_TILE
    return ((0, half), (half, dff))


def _ffn_half_step(x, g_ref, wg_ref, wu_ref, wd_ref):
    nb = _rms(x, g_ref[...]).astype(BF16)
    y = None
    for lo, hi in _ffn_splits(wg_ref.shape[1]):
        g = jnp.dot(nb, wg_ref[:, lo:hi], preferred_element_type=F32)
        u = jnp.dot(nb, wu_ref[:, lo:hi], preferred_element_type=F32)
        h = (g * jax.nn.sigmoid(g) * u).astype(BF16)
        part = jnp.dot(h, wd_ref[lo:hi, :], preferred_element_type=F32)
        y = part if y is None else y + part
    return x + 0.5 * y


def _split_hosted(refs):
    n_cast = (len(refs) - 1) // 2
    return refs[:n_cast], refs[n_cast], refs[n_cast + 1:]


def _cast_bands(srcs, dsts):
    for src, dst in zip(srcs, dsts):
        dst[...] = src[...].astype(BF16)


def _ffn_kernel(x_ref, g_ref, wg_ref, wu_ref, wd_ref, *refs):
    srcs, o_ref, dsts = _split_hosted(refs)
    o_ref[...] = _ffn_half_step(x_ref[...], g_ref, wg_ref, wu_ref, wd_ref)
    _cast_bands(srcs, dsts)


def _cast_band(start, rows, steps):
    band = -(-rows // steps)
    band = -(-band // BF16_ROWS) * BF16_ROWS
    while rows % band or start % band:
        band += BF16_ROWS
    return band


def _cast_specs(to_cast, steps):
    in_specs, out_specs, out_shapes = [], [], []
    for a, start, rows in to_cast:
        band = _cast_band(start, rows, steps)
        first, last = start // band, rows // band - 1
        in_specs.append(pl.BlockSpec(
            (band, a.shape[1]), lambda i, first=first, last=last: (first + jnp.minimum(i, last), 0)))
        out_specs.append(pl.BlockSpec((band, a.shape[1]), lambda i, last=last: (jnp.minimum(i, last), 0)))
        out_shapes.append(jax.ShapeDtypeStruct((rows, a.shape[1]), BF16))
    return in_specs, out_specs, out_shapes


def _ffn(x, norm_g, wg, wu, wd, to_cast):
    t, d = x.shape
    steps = t // TM_FFN
    tok = pl.BlockSpec((TM_FFN, d), lambda i: (i, 0))
    in_specs, out_specs, out_shapes = _cast_specs(to_cast, steps)
    outs = pl.pallas_call(
        _ffn_kernel,
        grid=(steps,),
        in_specs=[tok] + [_const_spec(a.shape) for a in (norm_g, wg, wu, wd)] + in_specs,
        out_specs=[tok] + out_specs,
        out_shape=[jax.ShapeDtypeStruct((t, d), F32)] + out_shapes,
        compiler_params=_params(1),
        name="ffn",
    )(x, norm_g, wg, wu, wd, *[a for a, _, _ in to_cast])
    return outs[0], outs[1:]


def _ones_rows(n_rows, rows_per_head, one_row):
    r = lax.broadcasted_iota(jnp.int32, (n_rows, 1), 0) % rows_per_head
    return jnp.where(r == one_row, 1.0, 0.0)


def _proj_kernel(h_ref, g_ref, wlat_ref, wdk_ref, wdqt_ref, wdvt_ref,
                 qn_ref, wqt_ref, kvn_ref, wk_ref, wvt_ref,
                 ct_ref, st_ref,
                 qt_ref, k_ref, vt_ref, dqt_ref, dk_ref, dvt_ref, *, q_scale, diff_scale):
    u = _rms(h_ref[...], g_ref[...]).astype(BF16)

    lat = lax.dot_general(u, wlat_ref[...], NT_DIMS, preferred_element_type=F32)
    q_lat = lat[:, :MLA_Q_LORA]
    kv_lat = lat[:, MLA_Q_LORA:MLA_Q_LORA + MLA_KV_LORA]
    o = MLA_Q_LORA + MLA_KV_LORA
    kr_a = lat[:, o:o + LANES]
    kr_b = lat[:, o + LANES:o + 2 * LANES]

    qn = _rms(q_lat, qn_ref[...]).astype(BF16)
    kvn = _rms(kv_lat, kvn_ref[...]).astype(BF16)

    qqt = lax.dot_general(wqt_ref[...], qn, NT_DIMS, preferred_element_type=F32)
    ct, st = ct_ref[...], st_ref[...]
    nope_rows = lax.broadcasted_iota(jnp.int32, (HEAD_SLOT, 1), 0) < MLA_NOPE
    cqt = (ct + jnp.where(nope_rows, 1.0, 0.0)) * q_scale
    sqt = st * q_scale
    r0, r1, r2 = MLA_NOPE, MLA_NOPE + MLA_ROPE // 2, MLA_NOPE + MLA_ROPE
    for hd in range(MLA_HEADS):
        rows = slice(hd * HEAD_SLOT, (hd + 1) * HEAD_SLOT)
        qh = qqt[rows]
        qh_sw = jnp.concatenate([qh[:r0], qh[r1:r2], qh[r0:r1], qh[r2:]], axis=0)
        qt_ref[0, rows, :] = (qh * cqt + qh_sw * sqt).astype(BF16)

    kk = jnp.dot(kvn, wk_ref[...], preferred_element_type=F32)
    kr = kr_a * ct.T + kr_b * st.T
    for hd in range(MLA_HEADS):
        sl = slice(hd * HEAD_SLOT, (hd + 1) * HEAD_SLOT)
        k_ref[:, sl] = (kk[:, sl] + kr).astype(BF16)

    vt = lax.dot_general(wvt_ref[...], kvn, NT_DIMS, preferred_element_type=F32)
    vt_ref[0] = (vt + _ones_rows(vt.shape[0], MLA_VROWS, MLA_V)).astype(BF16)

    dqt = lax.dot_general(wdqt_ref[...], u, NT_DIMS, preferred_element_type=F32)
    dqt_ref[0] = (dqt * diff_scale).astype(BF16)
    dk_ref[...] = lax.dot_general(u, wdk_ref[...], NT_DIMS, preferred_element_type=F32).astype(BF16)
    dvt = lax.dot_general(wdvt_ref[...], u, NT_DIMS, preferred_element_type=F32)
    dvt_ref[0] = (dvt + _ones_rows(dvt.shape[0], DIFF_VROWS, DIFF_V)).astype(BF16)


def _proj(h, g, wlat, wdk, wdqt, wdvt, qn, wqt, kvn, wk, wvt, tabs_t, batch, seq):
    t, d = h.shape
    nseq = seq // TM_PROJ
    tok = pl.BlockSpec((TM_PROJ, d), lambda i: (i, 0))
    tok_t = lambda rows: pl.BlockSpec((1, rows, TM_PROJ), lambda i: (i // nseq, 0, i % nseq))
    consts = [g, wlat, wdk, wdqt, wdvt, qn, wqt, kvn, wk, wvt]
    out_t = lambda rows: jax.ShapeDtypeStruct((batch, rows, seq), BF16)
    out_n = jax.ShapeDtypeStruct((t, d), BF16)
    return pl.pallas_call(
        functools.partial(_proj_kernel, q_scale=(MLA_NOPE + MLA_ROPE) ** -0.5 * LOG2E,
                          diff_scale=DIFF_DIM ** -0.5 * LOG2E),
        grid=(t // TM_PROJ,),
        in_specs=[tok] + [_const_spec(c.shape) for c in consts]
        + [pl.BlockSpec((HEAD_SLOT, TM_PROJ), lambda i: (0, i % nseq))] * 2,
        out_specs=[tok_t(wqt.shape[0]), tok, tok_t(wvt.shape[0]),
                   tok_t(wdqt.shape[0]), tok, tok_t(wdvt.shape[0])],
        out_shape=[out_t(wqt.shape[0]), out_n, out_t(wvt.shape[0]),
                   out_t(wdqt.shape[0]), out_n, out_t(wdvt.shape[0])],
        compiler_params=_params(1),
        name="in_proj",
    )(h, *consts, *tabs_t)


def _t5_bucket(rel):
    half = N_BUCKETS // 2
    max_exact = half // 2
    n = jnp.abs(rel)
    n2 = n * n
    assert (MAX_DISTANCE // max_exact) ** 2 == 2 ** (half - max_exact)
    large = max_exact
    for k in range(1, half - max_exact):
        large = large + (n2 >= (max_exact * max_exact) << k).astype(jnp.int32)
    return jnp.where(rel > 0, half, 0) + jnp.where(n < max_exact, n, large)


def _far_distance():
    half = N_BUCKETS // 2
    max_exact = half // 2
    n = max_exact
    while max_exact + sum(n * n >= (max_exact * max_exact) << k
                          for k in range(1, half - max_exact)) < half - 1:
        n += 1
    return n


def _bias_kernel(tab_ref, *refs):
    srcs, o_ref, dsts = _split_hosted(refs)
    _cast_bands(srcs, dsts)
    hd = pl.program_id(0)
    far = _far_distance()
    blk = 2 * CHUNK
    kloc = lax.broadcasted_iota(jnp.int32, (blk, blk), 0)
    qloc = lax.broadcasted_iota(jnp.int32, (blk, blk), 1)
    far_val = tab_ref[N_BUCKETS // 2 - 1, hd]
    for delta in range(2):
        for kb in range(TK // blk):
            for qb in range(TQ // blk):
                off = (kb - qb) * blk - delta * TK
                if delta == 0 and kb > qb:
                    val = jnp.full((blk, blk), NEG_INF, F32)
                elif off + blk - 1 <= -far:
                    val = jnp.zeros((blk, blk), F32)
                else:
                    bucket = _t5_bucket(kloc - qloc + off)
                    val = jnp.zeros((blk, blk), F32)
                    for b in range(N_BUCKETS):
                        val = jnp.where(bucket == b, tab_ref[b, hd], val)
                    val = (val - far_val) * LOG2E
                    if delta == 0 and kb == qb:
                        val = jnp.where(kloc // CHUNK <= qloc // CHUNK, val, NEG_INF)
                o_ref[0, delta, kb * blk:(kb + 1) * blk, qb * blk:(qb + 1) * blk] = val


def _bias_tiles(rel_bias, to_cast):
    in_specs, out_specs, out_shapes = _cast_specs(to_cast, DIFF_HEADS)
    outs = pl.pallas_call(
        _bias_kernel,
        grid=(DIFF_HEADS,),
        in_specs=[pl.BlockSpec(memory_space=pltpu.SMEM)] + in_specs,
        out_specs=[pl.BlockSpec((1, 2, TK, TQ), lambda h: (h, 0, 0, 0))] + out_specs,
        out_shape=[jax.ShapeDtypeStruct((DIFF_HEADS, 2, TK, TQ), F32)] + out_shapes,
        compiler_params=_params(1),
        name="bias_tiles",
    )(rel_bias, *[a for a, _, _ in to_cast])
    return outs[0], outs[1:]


FAR, SUB, DIAG = 0, 1, 2
assert TQ == TK and (TK // 2) % CHUNK == 0


def _scores(k, qt, bias, s_ref, mc_ref, diag):
    if diag:
        h = TK // 2
        parts = [(slice(0, h), slice(0, h)), (slice(0, TK), slice(h, TQ))]
    else:
        parts = [(slice(0, TK), slice(0, TQ))]
    for rows, cols in parts:
        s = jnp.dot(k[rows], qt[:, cols], preferred_element_type=F32)
        if bias is not None:
            s = s + bias[rows, cols]
        s_ref[rows, cols] = s
        mc_ref[:, cols] = jnp.max(s, axis=0, keepdims=True)


def _softmax_pv(s_ref, mc_ref, vt, m_ref, acc_ref, diag):
    m_prev = m_ref[...]
    m_new = jnp.maximum(m_prev, mc_ref[...])
    alpha = jnp.exp2(m_prev - m_new)
    if diag:
        h = TK // 2
        parts = [(slice(0, h), slice(0, h)), (slice(0, TK), slice(h, TQ))]
    else:
        parts = [(slice(0, TK), slice(0, TQ))]
    pv = []
    for rows, cols in parts:
        p = jnp.exp2(s_ref[rows, cols] - m_new[:, cols])
        pv.append(jnp.dot(vt[:, rows], p.astype(BF16), preferred_element_type=F32))
    pv = pv[0] if len(pv) == 1 else jnp.concatenate(pv, axis=1)
    acc_ref[...] = alpha * acc_ref[...] + pv
    m_ref[...] = m_new


FIRST_SLOT = 2
N_SLOTS = 3


def _sweep(i, near, qk, consume, qk_next):
    n_near = len(near)

    def slot_of(n):
        return FIRST_SLOT if n == 0 else (n + 1) % 2

    def run(tiles, own_first, next_kind):
        if own_first:
            qk(*tiles[0])
        for n, tile in enumerate(tiles):
            if n + 1 < len(tiles):
                qk(*tiles[n + 1])
            elif next_kind is not None:
                qk_next(next_kind)
            consume(*tile)

    def next_kind_after(count):
        return FAR if count + 1 > n_near else near[n_near - count - 1]

    for count in range(1, n_near + 1):
        @pl.when(i + 1 == count)
        def _(count=count):
            tiles = [(n, slot_of(n), kind) for n, kind in enumerate(near[-count:])]
            run(tiles, count <= 2, next_kind_after(count) if count > 1 else None)

    @pl.when(i + 1 > n_near)
    def _():
        n_far = i + 1 - n_near

        @pl.when(n_far == 1)
        def _():
            tiles = [(n, slot_of(n), kind) for n, kind in enumerate((FAR,) + near)]
            run(tiles, n_near <= 1, FAR)

        rest = n_far - 1
        pairs = (rest - 1) // 2

        for left in (1, 2):
            @pl.when(rest == left)
            def _(left=left):
                tiles = [(n, slot_of(n), kind) for n, kind in enumerate((FAR,) * (1 + left) + near)]
                run(tiles, False, FAR)

        @pl.when(pairs >= 1)
        def _():
            def pair(j):
                qk(j + 1, 1, FAR)
                consume(j, 0, FAR)
                qk(j + 2, 0, FAR)
                consume(j + 1, 1, FAR)

            for head in (1, 2):
                @pl.when(pairs % 2 == head % 2)
                def _(head=head):
                    qk(1, 0, FAR)
                    consume(0, FIRST_SLOT, FAR)
                    for n in range(head):
                        pair(2 * n + 1)

            first = 2 - pairs % 2

            def body(t, carry):
                j = 4 * t + 2 * first + 1
                pair(j)
                pair(j + 2)
                return carry

            lax.fori_loop(0, (pairs - first) // 2, body, 0)
            j0 = 2 * pairs + 1
            for left in (1, 2):
                @pl.when(rest - 2 * pairs == left)
                def _(left=left):
                    tiles = [(j0 + n, n % 2, kind) for n, kind in enumerate((FAR,) * left + near)]
                    run(tiles, False, FAR)


def _kv_rows(j):
    return pl.ds(pl.multiple_of(j * TK, TK), TK)


def _attn_scratch(streams, v_rows):
    return [pltpu.VMEM((streams, N_SLOTS, TK, TQ + LANES), F32),
            pltpu.VMEM((streams, N_SLOTS, 1, TQ), F32),
            pltpu.VMEM((streams, 1, TQ), F32),
            pltpu.VMEM((streams, v_rows, TQ), F32)]


MLA_STREAMS = 2


def _q_cols(i):
    return pl.ds(pl.multiple_of(i * TQ, TQ), TQ)


def _mla_kernel(qt_ref, k_ref, vt_ref, mask_ref, ot_ref, s_ref, mc_ref, m_ref, acc_ref):
    def qk(j, slot, kind, qts):
        bias = mask_ref[...] if kind == DIAG else None
        for st in range(MLA_STREAMS):
            k = k_ref[0, _kv_rows(j), st * HEAD_SLOT:(st + 1) * HEAD_SLOT]
            _scores(k, qts[st], bias, s_ref.at[st, slot], mc_ref.at[st, slot], kind == DIAG)

    def consume(j, slot, kind):
        for st in range(MLA_STREAMS):
            vt = vt_ref[0, st * MLA_VROWS:(st + 1) * MLA_VROWS, _kv_rows(j)]
            _softmax_pv(s_ref.at[st, slot], mc_ref.at[st, slot], vt, m_ref.at[st], acc_ref.at[st],
                        kind == DIAG)

    def q_tile(i, carry):
        def q_tiles(t):
            return [qt_ref[0, st * HEAD_SLOT:(st + 1) * HEAD_SLOT, _q_cols(t)] for st in range(MLA_STREAMS)]

        qts_next = q_tiles(jnp.minimum(i + 1, n_q - 1))
        m_ref[...] = jnp.full(m_ref.shape, NEG_INF, F32)
        acc_ref[...] = jnp.zeros(acc_ref.shape, F32)
        _sweep(i, (DIAG,), functools.partial(qk, qts=q_tiles(i)), consume,
               lambda kind: qk(0, FIRST_SLOT, kind, qts_next))
        for st in range(MLA_STREAMS):
            acc = acc_ref[st]
            ot_ref[0, st * MLA_V:(st + 1) * MLA_V, _q_cols(i)] = (
                acc[:MLA_V] / acc[MLA_V:MLA_V + 1]).astype(BF16)
        return carry

    n_q = qt_ref.shape[2] // TQ
    lax.fori_loop(0, n_q, q_tile, 0)


def _mla_attention(qt, k, vt, mask):
    b, _, s = qt.shape
    groups = MLA_HEADS // MLA_STREAMS
    return pl.pallas_call(
        _mla_kernel,
        grid=(b, groups),
        in_specs=[pl.BlockSpec((1, MLA_STREAMS * HEAD_SLOT, s), lambda bi, h: (bi, h, 0)),
                  pl.BlockSpec((1, s, MLA_STREAMS * HEAD_SLOT), lambda bi, h: (bi, 0, h)),
                  pl.BlockSpec((1, MLA_STREAMS * MLA_VROWS, s), lambda bi, h: (bi, h, 0)),
                  _const_spec(mask.shape)],
        out_specs=pl.BlockSpec((1, MLA_STREAMS * MLA_V, s), lambda bi, h: (bi, h, 0)),
        out_shape=jax.ShapeDtypeStruct((b, MLA_HEADS * MLA_V, s), BF16),
        scratch_shapes=_attn_scratch(MLA_STREAMS, MLA_VROWS),
        compiler_params=_params(2),
        name="mla_attn",
    )(qt, k, vt, mask)


def _diff_kernel(qt_ref, k_ref, vt_ref, bias_ref, lq1_ref, lk1_ref, lq2_ref, lk2_ref, sub_ref,
                 ot_ref, s_ref, mc_ref, m_ref, acc_ref, *, lam_init):
    row = lax.broadcasted_iota(jnp.int32, (HEAD_SLOT, TQ), 0)
    zero = jnp.zeros((HEAD_SLOT, TQ), BF16)
    lam = (jnp.exp(jnp.sum(lq1_ref[...] * lk1_ref[...], axis=1, keepdims=True))
           - jnp.exp(jnp.sum(lq2_ref[...] * lk2_ref[...], axis=1, keepdims=True)) + lam_init)

    def qk(j, slot, kind, qts):
        k = k_ref[0, _kv_rows(j), :]
        bias = None if kind == FAR else bias_ref[0, 0 if kind == DIAG else 1]
        for st in range(2):
            _scores(k, qts[st], bias, s_ref.at[st, slot], mc_ref.at[st, slot], kind == DIAG)

    def consume(j, slot, kind):
        vt = vt_ref[0, :, _kv_rows(j)]
        for st in range(2):
            _softmax_pv(s_ref.at[st, slot], mc_ref.at[st, slot], vt, m_ref.at[st], acc_ref.at[st],
                        kind == DIAG)

    def q_tile(i, carry):
        def q_tiles(t):
            qt = qt_ref[0, :, _q_cols(t)]
            return [jnp.where(row < DIFF_DIM, qt, zero), jnp.where(row >= DIFF_DIM, qt, zero)]

        qts_next = q_tiles(jnp.minimum(i + 1, n_q - 1))
        m_ref[...] = jnp.full(m_ref.shape, NEG_INF, F32)
        acc_ref[...] = jnp.zeros(acc_ref.shape, F32)
        _sweep(i, (SUB, DIAG), functools.partial(qk, qts=q_tiles(i)), consume,
               lambda kind: qk(0, FIRST_SLOT, kind, qts_next))
        outs = [acc_ref[st, :DIFF_V] / acc_ref[st, DIFF_V:DIFF_V + 1] for st in range(2)]
        o = outs[0] - lam * outs[1]
        o = o * lax.rsqrt(jnp.mean(o * o, axis=0, keepdims=True) + EPS) * sub_ref[...]
        ot_ref[0, :, _q_cols(i)] = (o * (1.0 - lam_init)).astype(BF16)
        return carry

    n_q = qt_ref.shape[2] // TQ
    lax.fori_loop(0, n_q, q_tile, 0)


def _diff_attention(qt, k, vt, bias, lq1, lk1, lq2, lk2, sub_col, lam_init):
    b, _, s = qt.shape
    vec = _const_spec(lq1.shape)
    return pl.pallas_call(
        functools.partial(_diff_kernel, lam_init=lam_init),
        grid=(b, DIFF_HEADS),
        in_specs=[pl.BlockSpec((1, HEAD_SLOT, s), lambda bi, h: (bi, h, 0)),
                  pl.BlockSpec((1, s, HEAD_SLOT), lambda bi, h: (bi, 0, h)),
                  pl.BlockSpec((1, DIFF_VROWS, s), lambda bi, h: (bi, h, 0)),
                  pl.BlockSpec((1, 2, TK, TQ), lambda bi, h: (h, 0, 0, 0)),
                  vec, vec, vec, vec, _const_spec(sub_col.shape)],
        out_specs=pl.BlockSpec((1, DIFF_V, s), lambda bi, h: (bi, h, 0)),
        out_shape=jax.ShapeDtypeStruct((b, DIFF_HEADS * DIFF_V, s), BF16),
        scratch_shapes=_attn_scratch(2, DIFF_VROWS),
        compiler_params=_params(2),
        name="diff_attn",
    )(qt, k, vt, bias, lq1, lk1, lq2, lk2, sub_col)


def _memkv_kernel(m_ref, g_ref, w_ref, o_ref):
    mn = _rms(m_ref[...], g_ref[...]).astype(BF16)
    o_ref[...] = jnp.dot(mn, w_ref[...], preferred_element_type=F32).astype(BF16)


def _memkv(mem2d, g, w):
    rows, d = mem2d.shape
    return pl.pallas_call(
        _memkv_kernel,
        grid=(1,),
        in_specs=[_const_spec(mem2d.shape), _const_spec(g.shape), _const_spec(w.shape)],
        out_specs=_const_spec((rows, w.shape[1])),
        out_shape=jax.ShapeDtypeStruct((rows, w.shape[1]), BF16),
        compiler_params=_params(1),
        name="mem_kv",
    )(mem2d, g, w)


def _post_kernel(h_ref, oat_ref, obt_ref, g_ref, wga_ref, wgb_ref, wa_ref, wb_ref, wo_ref,
                 xg_ref, xwq_ref, kv_ref, xwo_ref, fg_ref, fwg_ref, fwu_ref, fwd_ref, fin_ref,
                 o_ref, *, scale, final_norm):
    h = h_ref[...]
    u = _rms(h, g_ref[...]).astype(BF16)
    gate_a = jax.nn.sigmoid(lax.dot_general(u, wga_ref[...], NT_DIMS, preferred_element_type=F32))
    gate_b = jax.nn.sigmoid(lax.dot_general(u, wgb_ref[...], NT_DIMS, preferred_element_type=F32))
    ya = lax.dot_general(oat_ref[0], wa_ref[...], TN_DIMS, preferred_element_type=F32)
    yb = lax.dot_general(obt_ref[0], wb_ref[...], TN_DIMS, preferred_element_type=F32)
    merged = gate_a * ya + gate_b * yb
    h = h + jnp.dot(merged.astype(BF16), wo_ref[...], preferred_element_type=F32)

    x = _rms(h, xg_ref[...]).astype(BF16)
    q = (jnp.dot(x, xwq_ref[...], preferred_element_type=F32) * scale).astype(BF16)
    kv = kv_ref[...]
    outs = []
    for hd in range(XATTN_HEADS):
        qh = q[:, hd * XATTN_DIM:(hd + 1) * XATTN_DIM]
        kh = kv[:, 2 * hd * XATTN_DIM:(2 * hd + 1) * XATTN_DIM]
        vh = kv[:, (2 * hd + 1) * XATTN_DIM:(2 * hd + 2) * XATTN_DIM]
        s = lax.dot_general(qh, kh, NT_DIMS, preferred_element_type=F32)
        m = jnp.max(s, axis=1, keepdims=True)
        p = jnp.exp2(s - m)
        l = jnp.sum(p, axis=1, keepdims=True)
        oh = jnp.dot(p.astype(BF16), vh, preferred_element_type=F32) / l
        outs.append(oh.astype(BF16))
    o = jnp.concatenate(outs, axis=1)
    h = h + jnp.dot(o, xwo_ref[...], preferred_element_type=F32)

    h = _ffn_half_step(h, fg_ref, fwg_ref, fwu_ref, fwd_ref)
    if final_norm:
        h = _rms(h, fin_ref[...])
    o_ref[...] = h


def _post(h, oat, obt, g, wga, wgb, wa, wb, wo, xg, xwq, kvmem, xwo, ffn, fin_g, final_norm, seq, mem_len):
    t, d = h.shape
    nseq = seq // TM_PROJ
    tok = pl.BlockSpec((TM_PROJ, d), lambda i: (i, 0))
    tok_t = lambda rows: pl.BlockSpec((1, rows, TM_PROJ), lambda i: (i // nseq, 0, i % nseq))
    cs = lambda a: _const_spec(a.shape)
    return pl.pallas_call(
        functools.partial(_post_kernel, scale=XATTN_DIM ** -0.5 * LOG2E, final_norm=final_norm),
        grid=(t // TM_PROJ,),
        in_specs=[tok, tok_t(oat.shape[1]), tok_t(obt.shape[1]), cs(g), cs(wga), cs(wgb), cs(wa), cs(wb), cs(wo),
                  cs(xg), cs(xwq),
                  pl.BlockSpec((mem_len, kvmem.shape[1]), lambda i: (i // nseq, 0)), cs(xwo)]
        + [cs(a) for a in ffn] + [cs(fin_g)],
        out_specs=tok,
        out_shape=jax.ShapeDtypeStruct((t, d), F32),
        compiler_params=_params(1),
        name="post_attn",
    )(h, oat, obt, g, wga, wgb, wa, wb, wo, xg, xwq, kvmem, xwo, *ffn, fin_g)


def _pad_heads(w, heads, width, slot):
    k = w.shape[0]
    w = w.reshape(k, heads, width)
    return jnp.pad(w, ((0, 0), (0, 0), (0, slot - width))).reshape(k, heads * slot)


def _rope_tables(seq):
    half = MLA_ROPE // 2
    pos = jnp.arange(seq)
    freqs = ROPE_BASE ** (-jnp.arange(half, dtype=F32) / half)
    ang = freqs[:, None] * pos.astype(F32)[None, :]
    cos, sin = jnp.cos(ang), jnp.sin(ang)
    z_nope = jnp.zeros((MLA_NOPE, seq), F32)
    z_pad = jnp.zeros((HEAD_SLOT - MLA_NOPE - MLA_ROPE, seq), F32)
    return (jnp.concatenate([z_nope, cos, cos, z_pad], axis=0),
            jnp.concatenate([z_nope, -sin, sin, z_pad], axis=0))


def kernel(x, mem, ffn1_norm, ffn1_w_gate, ffn1_w_up, ffn1_w_down, mix_norm, w_in, mla_q_norm, mla_w_q_up, mla_kv_norm, mla_w_kv_up, diff_lambda_q1, diff_lambda_k1, diff_lambda_q2, diff_lambda_k2, diff_sub_norm, rel_bias, w_branch_a, w_branch_b, w_out, xattn_norm, mem_norm, xattn_w_q, xattn_w_kv, xattn_w_o, ffn2_norm, ffn2_w_gate, ffn2_w_up, ffn2_w_down, final_norm):
    b, s, d = x.shape
    depth = ffn1_norm.shape[0]
    mem_len = mem.shape[1]
    t = b * s
    bf = lambda a: a.astype(BF16)
    row = lambda a: a.reshape(1, -1)

    tabs_t = _rope_tables(s)
    kpos = jnp.arange(TK)[:, None]
    qpos = jnp.arange(TQ)[None, :]
    mla_mask = jnp.where(kpos // CHUNK <= qpos // CHUNK, 0.0, NEG_INF).astype(F32)
    whole = lambda a: (a, 0, a.shape[0])
    bias, ffn1_w = _bias_tiles(rel_bias, [whole(ffn1_w_gate[0]), whole(ffn1_w_up[0]), whole(ffn1_w_down[0])])

    h = x.reshape(t, d)
    for l in range(depth):
        w_t = jnp.transpose(w_in[l])
        o_kr = MLA_Q_LORA + MLA_KV_LORA
        o_dq = o_kr + MLA_ROPE
        pieces = [(w_t, 0, o_dq)] + [(w_t, o_dq + n * d, d) for n in range(5)]
        if l > 0:
            ffn1_w = [bf(ffn1_w_gate[l]), bf(ffn1_w_up[l]), bf(ffn1_w_down[l])]
        h, (f2g, f2u, f2d, w_lat, wdq_t, wdk_t, wdv_t, wga_t, wgb_t) = _ffn(
            h, row(ffn1_norm[l]), *ffn1_w,
            [whole(ffn2_w_gate[l]), whole(ffn2_w_up[l]), whole(ffn2_w_down[l])] + pieces)

        w_kr = w_lat[o_kr:]
        zeros = lambda n: jnp.zeros((n, d), BF16)
        pad_l, pad_r = zeros(MLA_NOPE), zeros(HEAD_SLOT - MLA_NOPE - MLA_ROPE)
        w_kr_sw = jnp.concatenate([w_kr[MLA_ROPE // 2:], w_kr[:MLA_ROPE // 2]], axis=0)
        wlat_t = jnp.concatenate([w_lat[:o_kr], pad_l, w_kr, pad_r, pad_l, w_kr_sw, pad_r], axis=0)
        wdv_t = jnp.pad(wdv_t.reshape(DIFF_HEADS, DIFF_V, d),
                        ((0, 0), (0, DIFF_VROWS - DIFF_V), (0, 0))).reshape(-1, d)

        wq = _pad_heads(mla_w_q_up[l], MLA_HEADS, MLA_NOPE + MLA_ROPE, HEAD_SLOT)
        wkv3 = mla_w_kv_up[l].reshape(MLA_KV_LORA, MLA_HEADS, MLA_NOPE + MLA_V)
        wk = _pad_heads(wkv3[..., :MLA_NOPE].reshape(MLA_KV_LORA, -1), MLA_HEADS, MLA_NOPE, HEAD_SLOT)
        wv = _pad_heads(wkv3[..., MLA_NOPE:].reshape(MLA_KV_LORA, -1), MLA_HEADS, MLA_V, MLA_VROWS)

        qt, k, vt, dqt, dk, dvt = _proj(
            h, row(mix_norm[l]), wlat_t, wdk_t, wdq_t, wdv_t,
            row(mla_q_norm[l]), bf(wq.T), row(mla_kv_norm[l]), bf(wk), bf(wv.T),
            tabs_t, b, s)

        oat = _mla_attention(qt, k.reshape(b, s, d), vt, mla_mask)
        lam_init = 0.8 - 0.6 * math.exp(-0.3 * l)
        obt = _diff_attention(dqt, dk.reshape(b, s, d), dvt, bias,
                              row(diff_lambda_q1[l]), row(diff_lambda_k1[l]),
                              row(diff_lambda_q2[l]), row(diff_lambda_k2[l]),
                              diff_sub_norm[l].reshape(-1, 1), lam_init)

        kvmem = _memkv(mem.reshape(b * mem_len, d), row(mem_norm[l]), bf(xattn_w_kv[l]))
        h = _post(h, oat, obt, row(mix_norm[l]), wga_t, wgb_t, bf(w_branch_a[l]), bf(w_branch_b[l]),
                  bf(w_out[l]), row(xattn_norm[l]), bf(xattn_w_q[l]), kvmem, bf(xattn_w_o[l]),
                  (row(ffn2_norm[l]), f2g, f2u, f2d),
                  row(final_norm), l == depth - 1, s, mem_len)
    return h.reshape(b, s, d)
```

```python
import functools
import math

import jax
import jax.numpy as jnp
from jax import lax
from jax.experimental import pallas as pl
from jax.experimental.pallas import tpu as pltpu

F32 = jnp.float32
BF16 = jnp.bfloat16

CHUNK = 64
EPS = 1e-6
NEG_INF = -1e30
MLA_HEADS = 8
MLA_Q_LORA = 384
MLA_KV_LORA = 256
MLA_NOPE = 64
MLA_ROPE = 32
MLA_V = 64
ROPE_BASE = 10000.0
DIFF_HEADS = 8
DIFF_DIM = 64
DIFF_V = 2 * DIFF_DIM
N_BUCKETS = 32
MAX_DISTANCE = 128
XATTN_HEADS = 4
XATTN_DIM = 128
LOG2E = math.log2(math.e)

LANES = 128
HEAD_SLOT = LANES
BF16_ROWS = 16
MXU_TILE = 256
VMEM_LIMIT = 56 * 1024 * 1024

TM_FFN = 512
TM_PROJ = 512
TQ = 512
TK = 512

MLA_VROWS = (MLA_V + 1 + BF16_ROWS - 1) // BF16_ROWS * BF16_ROWS
DIFF_VROWS = (DIFF_V + 1 + BF16_ROWS - 1) // BF16_ROWS * BF16_ROWS

NT_DIMS = (((1,), (1,)), ((), ()))
TN_DIMS = (((0,), (0,)), ((), ()))


def _rms(x, g):
    return x * lax.rsqrt(jnp.mean(x * x, axis=-1, keepdims=True) + EPS) * g


def _const_spec(shape):
    nd = len(shape)
    return pl.BlockSpec(shape, lambda *_: (0,) * nd, pipeline_mode=pl.Buffered(1))


def _params(n_grid):
    return pltpu.CompilerParams(
        dimension_semantics=("arbitrary",) * n_grid, vmem_limit_bytes=VMEM_LIMIT)


def _ffn_splits(dff):
    half = -(-(dff // 2) // MXU_TILE) * MXU_TILE
    return ((0, half), (half, dff))


def _ffn_half_step(x, g_ref, wg_ref, wu_ref, wd_ref):
    nb = _rms(x, g_ref[...]).astype(BF16)
    y = None
    for lo, hi in _ffn_splits(wg_ref.shape[1]):
        g = jnp.dot(nb, wg_ref[:, lo:hi], preferred_element_type=F32)
        u = jnp.dot(nb, wu_ref[:, lo:hi], preferred_element_type=F32)
        h = (g * jax.nn.sigmoid(g) * u).astype(BF16)
        part = jnp.dot(h, wd_ref[lo:hi, :], preferred_element_type=F32)
        y = part if y is None else y + part
    return x + 0.5 * y


def _split_hosted(refs):
    n_cast = (len(refs) - 1) // 2
    return refs[:n_cast], refs[n_cast], refs[n_cast + 1:]


def _cast_bands(srcs, dsts):
    for src, dst in zip(srcs, dsts):
        dst[...] = src[...].astype(BF16)


def _ffn_kernel(x_ref, g_ref, wg_ref, wu_ref, wd_ref, *refs):
    srcs, o_ref, dsts = _split_hosted(refs)
    o_ref[...] = _ffn_half_step(x_ref[...], g_ref, wg_ref, wu_ref, wd_ref)
    _cast_bands(srcs, dsts)


def _cast_band(start, rows, steps):
    band = -(-rows // steps)
    band = -(-band // BF16_ROWS) * BF16_ROWS
    while rows % band or start % band:
        band += BF16_ROWS
    return band


def _cast_specs(to_cast, steps):
    in_specs, out_specs, out_shapes = [], [], []
    for a, start, rows in to_cast:
        band = _cast_band(start, rows, steps)
        first, last = start // band, rows // band - 1
        in_specs.append(pl.BlockSpec(
            (band, a.shape[1]), lambda i, first=first, last=last: (first + jnp.minimum(i, last), 0)))
        out_specs.append(pl.BlockSpec((band, a.shape[1]), lambda i, last=last: (jnp.minimum(i, last), 0)))
        out_shapes.append(jax.ShapeDtypeStruct((rows, a.shape[1]), BF16))
    return in_specs, out_specs, out_shapes


def _ffn(x, norm_g, wg, wu, wd, to_cast):
    t, d = x.shape
    steps = t // TM_FFN
    tok = pl.BlockSpec((TM_FFN, d), lambda i: (i, 0))
    in_specs, out_specs, out_shapes = _cast_specs(to_cast, steps)
    outs = pl.pallas_call(
        _ffn_kernel,
        grid=(steps,),
        in_specs=[tok] + [_const_spec(a.shape) for a in (norm_g, wg, wu, wd)] + in_specs,
        out_specs=[tok] + out_specs,
        out_shape=[jax.ShapeDtypeStruct((t, d), F32)] + out_shapes,
        compiler_params=_params(1),
        name="ffn",
    )(x, norm_g, wg, wu, wd, *[a for a, _, _ in to_cast])
    return outs[0], outs[1:]


def _ones_rows(n_rows, rows_per_head, one_row):
    r = lax.broadcasted_iota(jnp.int32, (n_rows, 1), 0) % rows_per_head
    return jnp.where(r == one_row, 1.0, 0.0)


def _proj_kernel(h_ref, g_ref, wlat_ref, wdk_ref, wdqt_ref, wdvt_ref,
                 qn_ref, wqt_ref, kvn_ref, wk_ref, wvt_ref,
                 ct_ref, st_ref,
                 qt_ref, k_ref, vt_ref, dqt_ref, dk_ref, dvt_ref, *, q_scale, diff_scale):
    u = _rms(h_ref[...], g_ref[...]).astype(BF16)

    lat = lax.dot_general(u, wlat_ref[...], NT_DIMS, preferred_element_type=F32)
    q_lat = lat[:, :MLA_Q_LORA]
    kv_lat = lat[:, MLA_Q_LORA:MLA_Q_LORA + MLA_KV_LORA]
    o = MLA_Q_LORA + MLA_KV_LORA
    kr_a = lat[:, o:o + LANES]
    kr_b = lat[:, o + LANES:o + 2 * LANES]

    qn = _rms(q_lat, qn_ref[...]).astype(BF16)
    kvn = _rms(kv_lat, kvn_ref[...]).astype(BF16)

    qqt = lax.dot_general(wqt_ref[...], qn, NT_DIMS, preferred_element_type=F32)
    ct, st = ct_ref[...], st_ref[...]
    nope_rows = lax.broadcasted_iota(jnp.int32, (HEAD_SLOT, 1), 0) < MLA_NOPE
    cqt = (ct + jnp.where(nope_rows, 1.0, 0.0)) * q_scale
    sqt = st * q_scale
    r0, r1, r2 = MLA_NOPE, MLA_NOPE + MLA_ROPE // 2, MLA_NOPE + MLA_ROPE
    for hd in range(MLA_HEADS):
        rows = slice(hd * HEAD_SLOT, (hd + 1) * HEAD_SLOT)
        qh = qqt[rows]
        qh_sw = jnp.concatenate([qh[:r0], qh[r1:r2], qh[r0:r1], qh[r2:]], axis=0)
        qt_ref[0, rows, :] = (qh * cqt + qh_sw * sqt).astype(BF16)

    kk = jnp.dot(kvn, wk_ref[...], preferred_element_type=F32)
    kr = kr_a * ct.T + kr_b * st.T
    for hd in range(MLA_HEADS):
        sl = slice(hd * HEAD_SLOT, (hd + 1) * HEAD_SLOT)
        k_ref[:, sl] = (kk[:, sl] + kr).astype(BF16)

    vt = lax.dot_general(wvt_ref[...], kvn, NT_DIMS, preferred_element_type=F32)
    vt_ref[0] = (vt + _ones_rows(vt.shape[0], MLA_VROWS, MLA_V)).astype(BF16)

    dqt = lax.dot_general(wdqt_ref[...], u, NT_DIMS, preferred_element_type=F32)
    dqt_ref[0] = (dqt * diff_scale).astype(BF16)
    dk_ref[...] = lax.dot_general(u, wdk_ref[...], NT_DIMS, preferred_element_type=F32).astype(BF16)
    dvt = lax.dot_general(wdvt_ref[...], u, NT_DIMS, preferred_element_type=F32)
    dvt_ref[0] = (dvt + _ones_rows(dvt.shape[0], DIFF_VROWS, DIFF_V)).astype(BF16)


def _proj(h, g, wlat, wdk, wdqt, wdvt, qn, wqt, kvn, wk, wvt, tabs_t, batch, seq):
    t, d = h.shape
    nseq = seq // TM_PROJ
    tok = pl.BlockSpec((TM_PROJ, d), lambda i: (i, 0))
    tok_t = lambda rows: pl.BlockSpec((1, rows, TM_PROJ), lambda i: (i // nseq, 0, i % nseq))
    consts = [g, wlat, wdk, wdqt, wdvt, qn, wqt, kvn, wk, wvt]
    out_t = lambda rows: jax.ShapeDtypeStruct((batch, rows, seq), BF16)
    out_n = jax.ShapeDtypeStruct((t, d), BF16)
    return pl.pallas_call(
        functools.partial(_proj_kernel, q_scale=(MLA_NOPE + MLA_ROPE) ** -0.5 * LOG2E,
                          diff_scale=DIFF_DIM ** -0.5 * LOG2E),
        grid=(t // TM_PROJ,),
        in_specs=[tok] + [_const_spec(c.shape) for c in consts]
        + [pl.BlockSpec((HEAD_SLOT, TM_PROJ), lambda i: (0, i % nseq))] * 2,
        out_specs=[tok_t(wqt.shape[0]), tok, tok_t(wvt.shape[0]),
                   tok_t(wdqt.shape[0]), tok, tok_t(wdvt.shape[0])],
        out_shape=[out_t(wqt.shape[0]), out_n, out_t(wvt.shape[0]),
                   out_t(wdqt.shape[0]), out_n, out_t(wdvt.shape[0])],
        compiler_params=_params(1),
        name="in_proj",
    )(h, *consts, *tabs_t)


def _t5_bucket(rel):
    half = N_BUCKETS // 2
    max_exact = half // 2
    n = jnp.abs(rel)
    n2 = n * n
    assert (MAX_DISTANCE // max_exact) ** 2 == 2 ** (half - max_exact)
    large = max_exact
    for k in range(1, half - max_exact):
        large = large + (n2 >= (max_exact * max_exact) << k).astype(jnp.int32)
    return jnp.where(rel > 0, half, 0) + jnp.where(n < max_exact, n, large)


def _far_distance():
    half = N_BUCKETS // 2
    max_exact = half // 2
    n = max_exact
    while max_exact + sum(n * n >= (max_exact * max_exact) << k
                          for k in range(1, half - max_exact)) < half - 1:
        n += 1
    return n


def _bias_kernel(tab_ref, *refs):
    srcs, o_ref, dsts = _split_hosted(refs)
    _cast_bands(srcs, dsts)
    hd = pl.program_id(0)
    far = _far_distance()
    blk = 2 * CHUNK
    kloc = lax.broadcasted_iota(jnp.int32, (blk, blk), 0)
    qloc = lax.broadcasted_iota(jnp.int32, (blk, blk), 1)
    far_val = tab_ref[N_BUCKETS // 2 - 1, hd]
    for delta in range(2):
        for kb in range(TK // blk):
            for qb in range(TQ // blk):
                off = (kb - qb) * blk - delta * TK
                if delta == 0 and kb > qb:
                    val = jnp.full((blk, blk), NEG_INF, F32)
                elif off + blk - 1 <= -far:
                    val = jnp.zeros((blk, blk), F32)
                else:
                    bucket = _t5_bucket(kloc - qloc + off)
                    val = jnp.zeros((blk, blk), F32)
                    for b in range(N_BUCKETS):
                        val = jnp.where(bucket == b, tab_ref[b, hd], val)
                    val = (val - far_val) * LOG2E
                    if delta == 0 and kb == qb:
                        val = jnp.where(kloc // CHUNK <= qloc // CHUNK, val, NEG_INF)
                o_ref[0, delta, kb * blk:(kb + 1) * blk, qb * blk:(qb + 1) * blk] = val


def _bias_tiles(rel_bias, to_cast):
    in_specs, out_specs, out_shapes = _cast_specs(to_cast, DIFF_HEADS)
    outs = pl.pallas_call(
        _bias_kernel,
        grid=(DIFF_HEADS,),
        in_specs=[pl.BlockSpec(memory_space=pltpu.SMEM)] + in_specs,
        out_specs=[pl.BlockSpec((1, 2, TK, TQ), lambda h: (h, 0, 0, 0))] + out_specs,
        out_shape=[jax.ShapeDtypeStruct((DIFF_HEADS, 2, TK, TQ), F32)] + out_shapes,
        compiler_params=_params(1),
        name="bias_tiles",
    )(rel_bias, *[a for a, _, _ in to_cast])
    return outs[0], outs[1:]


FAR, SUB, DIAG = 0, 1, 2
assert TQ == TK and (TK // 2) % CHUNK == 0


def _scores(k, qt, bias, s_ref, mc_ref, diag):
    if diag:
        h = TK // 2
        parts = [(slice(0, h), slice(0, h)), (slice(0, TK), slice(h, TQ))]
    else:
        parts = [(slice(0, TK), slice(0, TQ))]
    for rows, cols in parts:
        s = jnp.dot(k[rows], qt[:, cols], preferred_element_type=F32)
        if bias is not None:
            s = s + bias[rows, cols]
        s_ref[rows, cols] = s
        mc_ref[:, cols] = jnp.max(s, axis=0, keepdims=True)


def _softmax_pv(s_ref, mc_ref, vt, m_ref, acc_ref, diag, first):
    m_new = mc_ref[...] if first else jnp.maximum(m_ref[...], mc_ref[...])
    if diag:
        h = TK // 2
        parts = [(slice(0, h), slice(0, h)), (slice(0, TK), slice(h, TQ))]
    else:
        parts = [(slice(0, TK), slice(0, TQ))]
    pv = []
    for rows, cols in parts:
        p = jnp.exp2(s_ref[rows, cols] - m_new[:, cols])
        pv.append(jnp.dot(vt[:, rows], p.astype(BF16), preferred_element_type=F32))
    pv = pv[0] if len(pv) == 1 else jnp.concatenate(pv, axis=1)
    acc_ref[...] = pv if first else jnp.exp2(m_ref[...] - m_new) * acc_ref[...] + pv
    m_ref[...] = m_new


FIRST_SLOT = 2
N_SLOTS = 3


def _sweep(i, near, qk, consume, qk_next):
    n_near = len(near)

    def slot_of(n):
        return FIRST_SLOT if n == 0 else (n + 1) % 2

    def run(tiles, own_first, next_kind, from_tile0=True):
        if own_first:
            qk(*tiles[0])
        for n, tile in enumerate(tiles):
            if n + 1 < len(tiles):
                qk(*tiles[n + 1])
            elif next_kind is not None:
                qk_next(next_kind)
            consume(*tile, first=from_tile0 and n == 0)

    def next_kind_after(count):
        return FAR if count + 1 > n_near else near[n_near - count - 1]

    for count in range(1, n_near + 1):
        @pl.when(i + 1 == count)
        def _(count=count):
            tiles = [(n, slot_of(n), kind) for n, kind in enumerate(near[-count:])]
            run(tiles, count <= 2, next_kind_after(count) if count > 1 else None)

    @pl.when(i + 1 > n_near)
    def _():
        n_far = i + 1 - n_near

        @pl.when(n_far == 1)
        def _():
            tiles = [(n, slot_of(n), kind) for n, kind in enumerate((FAR,) + near)]
            run(tiles, n_near <= 1, FAR)

        rest = n_far - 1
        pairs = (rest - 1) // 2

        for left in (1, 2):
            @pl.when(rest == left)
            def _(left=left):
                tiles = [(n, slot_of(n), kind) for n, kind in enumerate((FAR,) * (1 + left) + near)]
                run(tiles, False, FAR)

        @pl.when(pairs >= 1)
        def _():
            def pair(j):
                qk(j + 1, 1, FAR)
                consume(j, 0, FAR, first=False)
                qk(j + 2, 0, FAR)
                consume(j + 1, 1, FAR, first=False)

            for head in (1, 2):
                @pl.when(pairs % 2 == head % 2)
                def _(head=head):
                    qk(1, 0, FAR)
                    consume(0, FIRST_SLOT, FAR, first=True)
                    for n in range(head):
                        pair(2 * n + 1)

            first = 2 - pairs % 2

            def body(t, carry):
                j = 4 * t + 2 * first + 1
                pair(j)
                pair(j + 2)
                return carry

            lax.fori_loop(0, (pairs - first) // 2, body, 0)
            j0 = 2 * pairs + 1
            for left in (1, 2):
                @pl.when(rest - 2 * pairs == left)
                def _(left=left):
                    tiles = [(j0 + n, n % 2, kind) for n, kind in enumerate((FAR,) * left + near)]
                    run(tiles, False, FAR, from_tile0=False)


def _kv_rows(j):
    return pl.ds(pl.multiple_of(j * TK, TK), TK)


def _attn_scratch(streams, v_rows):
    return [pltpu.VMEM((streams, N_SLOTS, TK, TQ), F32),
            pltpu.VMEM((streams, N_SLOTS, 1, TQ), F32),
            pltpu.VMEM((streams, 1, TQ), F32),
            pltpu.VMEM((streams, v_rows, TQ), F32)]


MLA_STREAMS = 2


def _q_cols(i):
    return pl.ds(pl.multiple_of(i * TQ, TQ), TQ)


def _mla_kernel(qt_ref, k_ref, vt_ref, mask_ref, ot_ref, s_ref, mc_ref, m_ref, acc_ref):
    def qk(j, slot, kind, qts):
        bias = mask_ref[...] if kind == DIAG else None
        for st in range(MLA_STREAMS):
            k = k_ref[0, _kv_rows(j), st * HEAD_SLOT:(st + 1) * HEAD_SLOT]
            _scores(k, qts[st], bias, s_ref.at[st, slot], mc_ref.at[st, slot], kind == DIAG)

    def consume(j, slot, kind, first):
        for st in range(MLA_STREAMS):
            vt = vt_ref[0, st * MLA_VROWS:(st + 1) * MLA_VROWS, _kv_rows(j)]
            _softmax_pv(s_ref.at[st, slot], mc_ref.at[st, slot], vt, m_ref.at[st], acc_ref.at[st],
                        kind == DIAG, first)

    def q_tile(i, carry):
        def q_tiles(t):
            return [qt_ref[0, st * HEAD_SLOT:(st + 1) * HEAD_SLOT, _q_cols(t)] for st in range(MLA_STREAMS)]

        qts_next = q_tiles(jnp.minimum(i + 1, n_q - 1))
        _sweep(i, (DIAG,), functools.partial(qk, qts=q_tiles(i)), consume,
               lambda kind: qk(0, FIRST_SLOT, kind, qts_next))
        for st in range(MLA_STREAMS):
            acc = acc_ref[st]
            ot_ref[0, st * MLA_V:(st + 1) * MLA_V, _q_cols(i)] = (
                acc[:MLA_V] / acc[MLA_V:MLA_V + 1]).astype(BF16)
        return carry

    n_q = qt_ref.shape[2] // TQ
    lax.fori_loop(0, n_q, q_tile, 0)


def _mla_attention(qt, k, vt, mask):
    b, _, s = qt.shape
    groups = MLA_HEADS // MLA_STREAMS
    return pl.pallas_call(
        _mla_kernel,
        grid=(b, groups),
        in_specs=[pl.BlockSpec((1, MLA_STREAMS * HEAD_SLOT, s), lambda bi, h: (bi, h, 0)),
                  pl.BlockSpec((1, s, MLA_STREAMS * HEAD_SLOT), lambda bi, h: (bi, 0, h)),
                  pl.BlockSpec((1, MLA_STREAMS * MLA_VROWS, s), lambda bi, h: (bi, h, 0)),
                  _const_spec(mask.shape)],
        out_specs=pl.BlockSpec((1, MLA_STREAMS * MLA_V, s), lambda bi, h: (bi, h, 0)),
        out_shape=jax.ShapeDtypeStruct((b, MLA_HEADS * MLA_V, s), BF16),
        scratch_shapes=_attn_scratch(MLA_STREAMS, MLA_VROWS),
        compiler_params=_params(2),
        name="mla_attn",
    )(qt, k, vt, mask)


def _diff_kernel(qt_ref, k_ref, vt_ref, bias_ref, lq1_ref, lk1_ref, lq2_ref, lk2_ref, sub_ref,
                 ot_ref, s_ref, mc_ref, m_ref, acc_ref, *, lam_init):
    row = lax.broadcasted_iota(jnp.int32, (HEAD_SLOT, TQ), 0)
    zero = jnp.zeros((HEAD_SLOT, TQ), BF16)
    lam = (jnp.exp(jnp.sum(lq1_ref[...] * lk1_ref[...], axis=1, keepdims=True))
           - jnp.exp(jnp.sum(lq2_ref[...] * lk2_ref[...], axis=1, keepdims=True)) + lam_init)

    def qk(j, slot, kind, qts):
        k = k_ref[0, _kv_rows(j), :]
        bias = None if kind == FAR else bias_ref[0, 0 if kind == DIAG else 1]
        for st in range(2):
            _scores(k, qts[st], bias, s_ref.at[st, slot], mc_ref.at[st, slot], kind == DIAG)

    def consume(j, slot, kind, first):
        vt = vt_ref[0, :, _kv_rows(j)]
        for st in range(2):
            _softmax_pv(s_ref.at[st, slot], mc_ref.at[st, slot], vt, m_ref.at[st], acc_ref.at[st],
                        kind == DIAG, first)

    def q_tile(i, carry):
        def q_tiles(t):
            qt = qt_ref[0, :, _q_cols(t)]
            return [jnp.where(row < DIFF_DIM, qt, zero), jnp.where(row >= DIFF_DIM, qt, zero)]

        qts_next = q_tiles(jnp.minimum(i + 1, n_q - 1))
        _sweep(i, (SUB, DIAG), functools.partial(qk, qts=q_tiles(i)), consume,
               lambda kind: qk(0, FIRST_SLOT, kind, qts_next))
        outs = [acc_ref[st, :DIFF_V] / acc_ref[st, DIFF_V:DIFF_V + 1] for st in range(2)]
        o = outs[0] - lam * outs[1]
        o = o * lax.rsqrt(jnp.mean(o * o, axis=0, keepdims=True) + EPS) * sub_ref[...]
        ot_ref[0, :, _q_cols(i)] = (o * (1.0 - lam_init)).astype(BF16)
        return carry

    n_q = qt_ref.shape[2] // TQ
    lax.fori_loop(0, n_q, q_tile, 0)


def _diff_attention(qt, k, vt, bias, lq1, lk1, lq2, lk2, sub_col, lam_init):
    b, _, s = qt.shape
    vec = _const_spec(lq1.shape)
    return pl.pallas_call(
        functools.partial(_diff_kernel, lam_init=lam_init),
        grid=(b, DIFF_HEADS),
        in_specs=[pl.BlockSpec((1, HEAD_SLOT, s), lambda bi, h: (bi, h, 0)),
                  pl.BlockSpec((1, s, HEAD_SLOT), lambda bi, h: (bi, 0, h)),
                  pl.BlockSpec((1, DIFF_VROWS, s), lambda bi, h: (bi, h, 0)),
                  pl.BlockSpec((1, 2, TK, TQ), lambda bi, h: (h, 0, 0, 0)),
                  vec, vec, vec, vec, _const_spec(sub_col.shape)],
        out_specs=pl.BlockSpec((1, DIFF_V, s), lambda bi, h: (bi, h, 0)),
        out_shape=jax.ShapeDtypeStruct((b, DIFF_HEADS * DIFF_V, s), BF16),
        scratch_shapes=_attn_scratch(2, DIFF_VROWS),
        compiler_params=_params(2),
        name="diff_attn",
    )(qt, k, vt, bias, lq1, lk1, lq2, lk2, sub_col)


def _memkv_kernel(m_ref, g_ref, w_ref, o_ref):
    mn = _rms(m_ref[...], g_ref[...]).astype(BF16)
    o_ref[...] = jnp.dot(mn, w_ref[...], preferred_element_type=F32).astype(BF16)


def _memkv(mem2d, g, w):
    rows, d = mem2d.shape
    return pl.pallas_call(
        _memkv_kernel,
        grid=(1,),
        in_specs=[_const_spec(mem2d.shape), _const_spec(g.shape), _const_spec(w.shape)],
        out_specs=_const_spec((rows, w.shape[1])),
        out_shape=jax.ShapeDtypeStruct((rows, w.shape[1]), BF16),
        compiler_params=_params(1),
        name="mem_kv",
    )(mem2d, g, w)


def _post_kernel(h_ref, oat_ref, obt_ref, g_ref, wga_ref, wgb_ref, wa_ref, wb_ref, wo_ref,
                 xg_ref, xwq_ref, kv_ref, xwo_ref, fg_ref, fwg_ref, fwu_ref, fwd_ref, fin_ref,
                 o_ref, *, scale, final_norm):
    h = h_ref[...]
    u = _rms(h, g_ref[...]).astype(BF16)
    gate_a = jax.nn.sigmoid(lax.dot_general(u, wga_ref[...], NT_DIMS, preferred_element_type=F32))
    gate_b = jax.nn.sigmoid(lax.dot_general(u, wgb_ref[...], NT_DIMS, preferred_element_type=F32))
    ya = lax.dot_general(oat_ref[0], wa_ref[...], TN_DIMS, preferred_element_type=F32)
    yb = lax.dot_general(obt_ref[0], wb_ref[...], TN_DIMS, preferred_element_type=F32)
    merged = gate_a * ya + gate_b * yb
    h = h + jnp.dot(merged.astype(BF16), wo_ref[...], preferred_element_type=F32)

    x = _rms(h, xg_ref[...]).astype(BF16)
    q = (jnp.dot(x, xwq_ref[...], preferred_element_type=F32) * scale).astype(BF16)
    kv = kv_ref[...]
    outs = []
    for hd in range(XATTN_HEADS):
        qh = q[:, hd * XATTN_DIM:(hd + 1) * XATTN_DIM]
        kh = kv[:, 2 * hd * XATTN_DIM:(2 * hd + 1) * XATTN_DIM]
        vh = kv[:, (2 * hd + 1) * XATTN_DIM:(2 * hd + 2) * XATTN_DIM]
        s = lax.dot_general(qh, kh, NT_DIMS, preferred_element_type=F32)
        m = jnp.max(s, axis=1, keepdims=True)
        p = jnp.exp2(s - m)
        l = jnp.sum(p, axis=1, keepdims=True)
        oh = jnp.dot(p.astype(BF16), vh, preferred_element_type=F32) / l
        outs.append(oh.astype(BF16))
    o = jnp.concatenate(outs, axis=1)
    h = h + jnp.dot(o, xwo_ref[...], preferred_element_type=F32)

    h = _ffn_half_step(h, fg_ref, fwg_ref, fwu_ref, fwd_ref)
    if final_norm:
        h = _rms(h, fin_ref[...])
    o_ref[...] = h


def _post(h, oat, obt, g, wga, wgb, wa, wb, wo, xg, xwq, kvmem, xwo, ffn, fin_g, final_norm, seq, mem_len):
    t, d = h.shape
    nseq = seq // TM_PROJ
    tok = pl.BlockSpec((TM_PROJ, d), lambda i: (i, 0))
    tok_t = lambda rows: pl.BlockSpec((1, rows, TM_PROJ), lambda i: (i // nseq, 0, i % nseq))
    cs = lambda a: _const_spec(a.shape)
    return pl.pallas_call(
        functools.partial(_post_kernel, scale=XATTN_DIM ** -0.5 * LOG2E, final_norm=final_norm),
        grid=(t // TM_PROJ,),
        in_specs=[tok, tok_t(oat.shape[1]), tok_t(obt.shape[1]), cs(g), cs(wga), cs(wgb), cs(wa), cs(wb), cs(wo),
                  cs(xg), cs(xwq),
                  pl.BlockSpec((mem_len, kvmem.shape[1]), lambda i: (i // nseq, 0)), cs(xwo)]
        + [cs(a) for a in ffn] + [cs(fin_g)],
        out_specs=tok,
        out_shape=jax.ShapeDtypeStruct((t, d), F32),
        compiler_params=_params(1),
        name="post_attn",
    )(h, oat, obt, g, wga, wgb, wa, wb, wo, xg, xwq, kvmem, xwo, *ffn, fin_g)


def _pad_heads(w, heads, width, slot):
    k = w.shape[0]
    w = w.reshape(k, heads, width)
    return jnp.pad(w, ((0, 0), (0, 0), (0, slot - width))).reshape(k, heads * slot)


def _rope_tables(seq):
    half = MLA_ROPE // 2
    pos = jnp.arange(seq)
    freqs = ROPE_BASE ** (-jnp.arange(half, dtype=F32) / half)
    ang = freqs[:, None] * pos.astype(F32)[None, :]
    cos, sin = jnp.cos(ang), jnp.sin(ang)
    z_nope = jnp.zeros((MLA_NOPE, seq), F32)
    z_pad = jnp.zeros((HEAD_SLOT - MLA_NOPE - MLA_ROPE, seq), F32)
    return (jnp.concatenate([z_nope, cos, cos, z_pad], axis=0),
            jnp.concatenate([z_nope, -sin, sin, z_pad], axis=0))


def kernel(x, mem, ffn1_norm, ffn1_w_gate, ffn1_w_up, ffn1_w_down, mix_norm, w_in, mla_q_norm, mla_w_q_up, mla_kv_norm, mla_w_kv_up, diff_lambda_q1, diff_lambda_k1, diff_lambda_q2, diff_lambda_k2, diff_sub_norm, rel_bias, w_branch_a, w_branch_b, w_out, xattn_norm, mem_norm, xattn_w_q, xattn_w_kv, xattn_w_o, ffn2_norm, ffn2_w_gate, ffn2_w_up, ffn2_w_down, final_norm):
    b, s, d = x.shape
    depth = ffn1_norm.shape[0]
    mem_len = mem.shape[1]
    t = b * s
    bf = lambda a: a.astype(BF16)
    row = lambda a: a.reshape(1, -1)

    tabs_t = _rope_tables(s)
    kpos = jnp.arange(TK)[:, None]
    qpos = jnp.arange(TQ)[None, :]
    mla_mask = jnp.where(kpos // CHUNK <= qpos // CHUNK, 0.0, NEG_INF).astype(F32)
    whole = lambda a: (a, 0, a.shape[0])
    bias, ffn1_w = _bias_tiles(rel_bias, [whole(ffn1_w_gate[0]), whole(ffn1_w_up[0]), whole(ffn1_w_down[0])])

    h = x.reshape(t, d)
    for l in range(depth):
        w_t = jnp.transpose(w_in[l])
        o_kr = MLA_Q_LORA + MLA_KV_LORA
        o_dq = o_kr + MLA_ROPE
        pieces = [(w_t, 0, o_dq)] + [(w_t, o_dq + n * d, d) for n in range(5)]
        if l > 0:
            ffn1_w = [bf(ffn1_w_gate[l]), bf(ffn1_w_up[l]), bf(ffn1_w_down[l])]
        h, (f2g, f2u, f2d, w_lat, wdq_t, wdk_t, wdv_t, wga_t, wgb_t) = _ffn(
            h, row(ffn1_norm[l]), *ffn1_w,
            [whole(ffn2_w_gate[l]), whole(ffn2_w_up[l]), whole(ffn2_w_down[l])] + pieces)

        w_kr = w_lat[o_kr:]
        zeros = lambda n: jnp.zeros((n, d), BF16)
        pad_l, pad_r = zeros(MLA_NOPE), zeros(HEAD_SLOT - MLA_NOPE - MLA_ROPE)
        w_kr_sw = jnp.concatenate([w_kr[MLA_ROPE // 2:], w_kr[:MLA_ROPE // 2]], axis=0)
        wlat_t = jnp.concatenate([w_lat[:o_kr], pad_l, w_kr, pad_r, pad_l, w_kr_sw, pad_r], axis=0)
        wdv_t = jnp.pad(wdv_t.reshape(DIFF_HEADS, DIFF_V, d),
                        ((0, 0), (0, DIFF_VROWS - DIFF_V), (0, 0))).reshape(-1, d)

        wq = _pad_heads(mla_w_q_up[l], MLA_HEADS, MLA_NOPE + MLA_ROPE, HEAD_SLOT)
        wkv3 = mla_w_kv_up[l].reshape(MLA_KV_LORA, MLA_HEADS, MLA_NOPE + MLA_V)
        wk = _pad_heads(wkv3[..., :MLA_NOPE].reshape(MLA_KV_LORA, -1), MLA_HEADS, MLA_NOPE, HEAD_SLOT)
        wv = _pad_heads(wkv3[..., MLA_NOPE:].reshape(MLA_KV_LORA, -1), MLA_HEADS, MLA_V, MLA_VROWS)

        qt, k, vt, dqt, dk, dvt = _proj(
            h, row(mix_norm[l]), wlat_t, wdk_t, wdq_t, wdv_t,
            row(mla_q_norm[l]), bf(wq.T), row(mla_kv_norm[l]), bf(wk), bf(wv.T),
            tabs_t, b, s)

        oat = _mla_attention(qt, k.reshape(b, s, d), vt, mla_mask)
        lam_init = 0.8 - 0.6 * math.exp(-0.3 * l)
        obt = _diff_attention(dqt, dk.reshape(b, s, d), dvt, bias,
                              row(diff_lambda_q1[l]), row(diff_lambda_k1[l]),
                              row(diff_lambda_q2[l]), row(diff_lambda_k2[l]),
                              diff_sub_norm[l].reshape(-1, 1), lam_init)

        kvmem = _memkv(mem.reshape(b * mem_len, d), row(mem_norm[l]), bf(xattn_w_kv[l]))
        h = _post(h, oat, obt, row(mix_norm[l]), wga_t, wgb_t, bf(w_branch_a[l]), bf(w_branch_b[l]),
                  bf(w_out[l]), row(xattn_norm[l]), bf(xattn_w_q[l]), kvmem, bf(xattn_w_o[l]),
                  (row(ffn2_norm[l]), f2g, f2u, f2d),
                  row(final_norm), l == depth - 1, s, mem_len)
    return h.reshape(b, s, d)
```

```python
import functools
import math

import jax
import jax.numpy as jnp
from jax import lax
from jax.experimental import pallas as pl
from jax.experimental.pallas import tpu as pltpu

F32 = jnp.float32
BF16 = jnp.bfloat16

CHUNK = 64
EPS = 1e-6
NEG_INF = -1e30
MLA_HEADS = 8
MLA_Q_LORA = 384
MLA_KV_LORA = 256
MLA_NOPE = 64
MLA_ROPE = 32
MLA_V = 64
ROPE_BASE = 10000.0
DIFF_HEADS = 8
DIFF_DIM = 64
DIFF_V = 2 * DIFF_DIM
N_BUCKETS = 32
MAX_DISTANCE = 128
XATTN_HEADS = 4
XATTN_DIM = 128
LOG2E = math.log2(math.e)

LANES = 128
HEAD_SLOT = LANES
BF16_ROWS = 16
MXU_TILE = 256
VMEM_LIMIT = 56 * 1024 * 1024

TM_FFN = 512
TM_PROJ = 512
TQ = 512
TK = 512

MLA_VROWS = (MLA_V + 1 + BF16_ROWS - 1) // BF16_ROWS * BF16_ROWS
DIFF_VROWS = (DIFF_V + 1 + BF16_ROWS - 1) // BF16_ROWS * BF16_ROWS

NT_DIMS = (((1,), (1,)), ((), ()))
TN_DIMS = (((0,), (0,)), ((), ()))


def _rms(x, g):
    return x * lax.rsqrt(jnp.mean(x * x, axis=-1, keepdims=True) + EPS) * g


def _const_spec(shape):
    nd = len(shape)
    return pl.BlockSpec(shape, lambda *_: (0,) * nd, pipeline_mode=pl.Buffered(1))


def _params(n_grid):
    return pltpu.CompilerParams(
        dimension_semantics=("arbitrary",) * n_grid, vmem_limit_bytes=VMEM_LIMIT)


def _ffn_splits(dff):
    half = -(-(dff // 2) // MXU_TILE) * MXU_TILE
    return ((0, half), (half, dff))


def _ffn_half_step(x, g_ref, wg_ref, wu_ref, wd_ref):
    nb = _rms(x, g_ref[...]).astype(BF16)
    y = None
    for lo, hi in _ffn_splits(wg_ref.shape[1]):
        g = jnp.dot(nb, wg_ref[:, lo:hi], preferred_element_type=F32)
        u = jnp.dot(nb, wu_ref[:, lo:hi], preferred_element_type=F32)
        h = (g * jax.nn.sigmoid(g) * u).astype(BF16)
        part = jnp.dot(h, wd_ref[lo:hi, :], preferred_element_type=F32)
        y = part if y is None else y + part
    return x + 0.5 * y


def _split_hosted(refs):
    n_cast = (len(refs) - 1) // 2
    return refs[:n_cast], refs[n_cast], refs[n_cast + 1:]


def _cast_bands(srcs, dsts):
    for src, dst in zip(srcs, dsts):
        dst[...] = src[...].astype(BF16)


def _ffn_kernel(x_ref, g_ref, wg_ref, wu_ref, wd_ref, *refs):
    srcs, o_ref, dsts = _split_hosted(refs)
    o_ref[...] = _ffn_half_step(x_ref[...], g_ref, wg_ref, wu_ref, wd_ref)
    _cast_bands(srcs, dsts)


def _cast_band(start, rows, steps):
    band = -(-rows // steps)
    band = -(-band // BF16_ROWS) * BF16_ROWS
    while rows % band or start % band:
        band += BF16_ROWS
    return band


def _cast_specs(to_cast, steps):
    in_specs, out_specs, out_shapes = [], [], []
    for a, start, rows in to_cast:
        band = _cast_band(start, rows, steps)
        first, last = start // band, rows // band - 1
        in_specs.append(pl.BlockSpec(
            (band, a.shape[1]), lambda i, first=first, last=last: (first + jnp.minimum(i, last), 0)))
        out_specs.append(pl.BlockSpec((band, a.shape[1]), lambda i, last=last: (jnp.minimum(i, last), 0)))
        out_shapes.append(jax.ShapeDtypeStruct((rows, a.shape[1]), BF16))
    return in_specs, out_specs, out_shapes


def _ffn(x, norm_g, wg, wu, wd, to_cast):
    t, d = x.shape
    steps = t // TM_FFN
    tok = pl.BlockSpec((TM_FFN, d), lambda i: (i, 0))
    in_specs, out_specs, out_shapes = _cast_specs(to_cast, steps)
    outs = pl.pallas_call(
        _ffn_kernel,
        grid=(steps,),
        in_specs=[tok] + [_const_spec(a.shape) for a in (norm_g, wg, wu, wd)] + in_specs,
        out_specs=[tok] + out_specs,
        out_shape=[jax.ShapeDtypeStruct((t, d), F32)] + out_shapes,
        compiler_params=_params(1),
        name="ffn",
    )(x, norm_g, wg, wu, wd, *[a for a, _, _ in to_cast])
    return outs[0], outs[1:]


def _ones_rows(n_rows, rows_per_head, one_row):
    r = lax.broadcasted_iota(jnp.int32, (n_rows, 1), 0) % rows_per_head
    return jnp.where(r == one_row, 1.0, 0.0)


def _proj_kernel(h_ref, g_ref, wlat_ref, wdk_ref, wdqt_ref, wdvt_ref,
                 qn_ref, wqt_ref, kvn_ref, wk_ref, wvt_ref,
                 ct_ref, st_ref,
                 qt_ref, k_ref, vt_ref, dqt_ref, dk_ref, dvt_ref, *, q_scale, diff_scale):
    u = _rms(h_ref[...], g_ref[...]).astype(BF16)

    lat = lax.dot_general(u, wlat_ref[...], NT_DIMS, preferred_element_type=F32)
    q_lat = lat[:, :MLA_Q_LORA]
    kv_lat = lat[:, MLA_Q_LORA:MLA_Q_LORA + MLA_KV_LORA]
    o = MLA_Q_LORA + MLA_KV_LORA
    kr_a = lat[:, o:o + LANES]
    kr_b = lat[:, o + LANES:o + 2 * LANES]

    qn = _rms(q_lat, qn_ref[...]).astype(BF16)
    kvn = _rms(kv_lat, kvn_ref[...]).astype(BF16)

    qqt = lax.dot_general(wqt_ref[...], qn, NT_DIMS, preferred_element_type=F32)
    ct, st = ct_ref[...], st_ref[...]
    nope_rows = lax.broadcasted_iota(jnp.int32, (HEAD_SLOT, 1), 0) < MLA_NOPE
    cqt = (ct + jnp.where(nope_rows, 1.0, 0.0)) * q_scale
    sqt = st * q_scale
    r0, r1, r2 = MLA_NOPE, MLA_NOPE + MLA_ROPE // 2, MLA_NOPE + MLA_ROPE
    for hd in range(MLA_HEADS):
        rows = slice(hd * HEAD_SLOT, (hd + 1) * HEAD_SLOT)
        qh = qqt[rows]
        qh_sw = jnp.concatenate([qh[:r0], qh[r1:r2], qh[r0:r1], qh[r2:]], axis=0)
        qt_ref[0, rows, :] = (qh * cqt + qh_sw * sqt).astype(BF16)

    kk = jnp.dot(kvn, wk_ref[...], preferred_element_type=F32)
    kr = kr_a * ct.T + kr_b * st.T
    for hd in range(MLA_HEADS):
        sl = slice(hd * HEAD_SLOT, (hd + 1) * HEAD_SLOT)
        k_ref[:, sl] = (kk[:, sl] + kr).astype(BF16)

    vt = lax.dot_general(wvt_ref[...], kvn, NT_DIMS, preferred_element_type=F32)
    vt_ref[0] = (vt + _ones_rows(vt.shape[0], MLA_VROWS, MLA_V)).astype(BF16)

    dqt = lax.dot_general(wdqt_ref[...], u, NT_DIMS, preferred_element_type=F32)
    dqt_ref[0] = (dqt * diff_scale).astype(BF16)
    dk_ref[...] = lax.dot_general(u, wdk_ref[...], NT_DIMS, preferred_element_type=F32).astype(BF16)
    dvt = lax.dot_general(wdvt_ref[...], u, NT_DIMS, preferred_element_type=F32)
    dvt_ref[0] = (dvt + _ones_rows(dvt.shape[0], DIFF_VROWS, DIFF_V)).astype(BF16)


def _proj(h, g, wlat, wdk, wdqt, wdvt, qn, wqt, kvn, wk, wvt, tabs_t, batch, seq):
    t, d = h.shape
    nseq = seq // TM_PROJ
    tok = pl.BlockSpec((TM_PROJ, d), lambda i: (i, 0))
    tok_t = lambda rows: pl.BlockSpec((1, rows, TM_PROJ), lambda i: (i // nseq, 0, i % nseq))
    consts = [g, wlat, wdk, wdqt, wdvt, qn, wqt, kvn, wk, wvt]
    out_t = lambda rows: jax.ShapeDtypeStruct((batch, rows, seq), BF16)
    out_n = jax.ShapeDtypeStruct((t, d), BF16)
    return pl.pallas_call(
        functools.partial(_proj_kernel, q_scale=(MLA_NOPE + MLA_ROPE) ** -0.5 * LOG2E,
                          diff_scale=DIFF_DIM ** -0.5 * LOG2E),
        grid=(t // TM_PROJ,),
        in_specs=[tok] + [_const_spec(c.shape) for c in consts]
        + [pl.BlockSpec((HEAD_SLOT, TM_PROJ), lambda i: (0, i % nseq))] * 2,
        out_specs=[tok_t(wqt.shape[0]), tok, tok_t(wvt.shape[0]),
                   tok_t(wdqt.shape[0]), tok, tok_t(wdvt.shape[0])],
        out_shape=[out_t(wqt.shape[0]), out_n, out_t(wvt.shape[0]),
                   out_t(wdqt.shape[0]), out_n, out_t(wdvt.shape[0])],
        compiler_params=_params(1),
        name="in_proj",
    )(h, *consts, *tabs_t)


def _t5_bucket(rel):
    half = N_BUCKETS // 2
    max_exact = half // 2
    n = jnp.abs(rel)
    n2 = n * n
    assert (MAX_DISTANCE // max_exact) ** 2 == 2 ** (half - max_exact)
    large = max_exact
    for k in range(1, half - max_exact):
        large = large + (n2 >= (max_exact * max_exact) << k).astype(jnp.int32)
    return jnp.where(rel > 0, half, 0) + jnp.where(n < max_exact, n, large)


def _far_distance():
    half = N_BUCKETS // 2
    max_exact = half // 2
    n = max_exact
    while max_exact + sum(n * n >= (max_exact * max_exact) << k
                          for k in range(1, half - max_exact)) < half - 1:
        n += 1
    return n


def _bias_kernel(tab_ref, *refs):
    srcs, o_ref, dsts = _split_hosted(refs)
    _cast_bands(srcs, dsts)
    hd = pl.program_id(0)
    far = _far_distance()
    blk = 2 * CHUNK
    kloc = lax.broadcasted_iota(jnp.int32, (blk, blk), 0)
    qloc = lax.broadcasted_iota(jnp.int32, (blk, blk), 1)
    far_val = tab_ref[N_BUCKETS // 2 - 1, hd]
    for delta in range(2):
        for kb in range(TK // blk):
            for qb in range(TQ // blk):
                off = (kb - qb) * blk - delta * TK
                if delta == 0 and kb > qb:
                    val = jnp.full((blk, blk), NEG_INF, F32)
                elif off + blk - 1 <= -far:
                    val = jnp.zeros((blk, blk), F32)
                else:
                    bucket = _t5_bucket(kloc - qloc + off)
                    val = jnp.zeros((blk, blk), F32)
                    for b in range(N_BUCKETS):
                        val = jnp.where(bucket == b, tab_ref[b, hd], val)
                    val = (val - far_val) * LOG2E
                    if delta == 0 and kb == qb:
                        val = jnp.where(kloc // CHUNK <= qloc // CHUNK, val, NEG_INF)
                o_ref[0, delta, kb * blk:(kb + 1) * blk, qb * blk:(qb + 1) * blk] = val


def _bias_tiles(rel_bias, to_cast):
    in_specs, out_specs, out_shapes = _cast_specs(to_cast, DIFF_HEADS)
    outs = pl.pallas_call(
        _bias_kernel,
        grid=(DIFF_HEADS,),
        in_specs=[pl.BlockSpec(memory_space=pltpu.SMEM)] + in_specs,
        out_specs=[pl.BlockSpec((1, 2, TK, TQ), lambda h: (h, 0, 0, 0))] + out_specs,
        out_shape=[jax.ShapeDtypeStruct((DIFF_HEADS, 2, TK, TQ), F32)] + out_shapes,
        compiler_params=_params(1),
        name="bias_tiles",
    )(rel_bias, *[a for a, _, _ in to_cast])
    return outs[0], outs[1:]


FAR, SUB, DIAG = 0, 1, 2
assert TQ == TK and (TK // 2) % CHUNK == 0


def _scores(k, qt, bias, s_ref, mc_ref, diag):
    if diag:
        h = TK // 2
        parts = [(slice(0, h), slice(0, h)), (slice(0, TK), slice(h, TQ))]
    else:
        parts = [(slice(0, TK), slice(0, TQ))]
    for rows, cols in parts:
        s = jnp.dot(k[rows], qt[:, cols], preferred_element_type=F32)
        if bias is not None:
            s = s + bias[rows, cols]
        s_ref[rows, cols] = s
        mc_ref[:, cols] = jnp.max(s, axis=0, keepdims=True)


def _softmax_pv(s_ref, mc_ref, vt, m_ref, acc_ref, diag, first):
    m_new = mc_ref[...] if first else jnp.maximum(m_ref[...], mc_ref[...])
    if diag:
        h = TK // 2
        parts = [(slice(0, h), slice(0, h)), (slice(0, TK), slice(h, TQ))]
    else:
        parts = [(slice(0, TK), slice(0, TQ))]
    pv = []
    for rows, cols in parts:
        p = jnp.exp2(s_ref[rows, cols] - m_new[:, cols])
        pv.append(jnp.dot(vt[:, rows], p.astype(BF16), preferred_element_type=F32))
    pv = pv[0] if len(pv) == 1 else jnp.concatenate(pv, axis=1)
    acc_ref[...] = pv if first else jnp.exp2(m_ref[...] - m_new) * acc_ref[...] + pv
    m_ref[...] = m_new


FIRST_SLOT = 2
N_SLOTS = 3


def _sweep(i, near, qk, consume, qk_next):
    n_near = len(near)

    def slot_of(n):
        return FIRST_SLOT if n == 0 else (n + 1) % 2

    def run(tiles, own_first, next_kind, from_tile0=True):
        if own_first:
            qk(*tiles[0])
        for n, tile in enumerate(tiles):
            if n + 1 < len(tiles):
                qk(*tiles[n + 1])
            elif next_kind is not None:
                qk_next(next_kind)
            consume(*tile, first=from_tile0 and n == 0)

    def next_kind_after(count):
        return FAR if count + 1 > n_near else near[n_near - count - 1]

    for count in range(1, n_near + 1):
        @pl.when(i + 1 == count)
        def _(count=count):
            tiles = [(n, slot_of(n), kind) for n, kind in enumerate(near[-count:])]
            run(tiles, count <= 2, next_kind_after(count) if count > 1 else None)

    @pl.when(i + 1 > n_near)
    def _():
        n_far = i + 1 - n_near

        @pl.when(n_far == 1)
        def _():
            tiles = [(n, slot_of(n), kind) for n, kind in enumerate((FAR,) + near)]
            run(tiles, n_near <= 1, FAR)

        rest = n_far - 1
        pairs = (rest - 1) // 2

        for left in (1, 2):
            @pl.when(rest == left)
            def _(left=left):
                tiles = [(n, slot_of(n), kind) for n, kind in enumerate((FAR,) * (1 + left) + near)]
                run(tiles, False, FAR)

        @pl.when(pairs >= 1)
        def _():
            def pair(j):
                qk(j + 1, 1, FAR)
                consume(j, 0, FAR, first=False)
                qk(j + 2, 0, FAR)
                consume(j + 1, 1, FAR, first=False)

            for head in (1, 2):
                @pl.when(pairs % 2 == head % 2)
                def _(head=head):
                    qk(1, 0, FAR)
                    consume(0, FIRST_SLOT, FAR, first=True)
                    for n in range(head):
                        pair(2 * n + 1)

            first = 2 - pairs % 2

            def body(t, carry):
                j = 4 * t + 2 * first + 1
                pair(j)
                pair(j + 2)
                return carry

            lax.fori_loop(0, (pairs - first) // 2, body, 0)
            j0 = 2 * pairs + 1
            for left in (1, 2):
                @pl.when(rest - 2 * pairs == left)
                def _(left=left):
                    tiles = [(j0 + n, n % 2, kind) for n, kind in enumerate((FAR,) * left + near)]
                    run(tiles, False, FAR, from_tile0=False)


def _kv_rows(j):
    return pl.ds(pl.multiple_of(j * TK, TK), TK)


def _attn_scratch(streams, v_rows):
    return [pltpu.VMEM((streams, N_SLOTS, TK, TQ), F32),
            pltpu.VMEM((streams, N_SLOTS, 1, TQ), F32),
            pltpu.VMEM((streams, 1, TQ), F32),
            pltpu.VMEM((streams, v_rows, TQ), F32)]


MLA_STREAMS = 2


def _q_cols(i):
    return pl.ds(pl.multiple_of(i * TQ, TQ), TQ)


def _mla_kernel(qt_ref, k_ref, vt_ref, mask_ref, ot_ref, s_ref, mc_ref, m_ref, acc_ref):
    def qk(j, slot, kind, qts):
        bias = mask_ref[...] if kind == DIAG else None
        for st in range(MLA_STREAMS):
            k = k_ref[0, _kv_rows(j), st * HEAD_SLOT:(st + 1) * HEAD_SLOT]
            _scores(k, qts[st], bias, s_ref.at[st, slot], mc_ref.at[st, slot], kind == DIAG)

    def consume(j, slot, kind, first):
        for st in range(MLA_STREAMS):
            vt = vt_ref[0, st * MLA_VROWS:(st + 1) * MLA_VROWS, _kv_rows(j)]
            _softmax_pv(s_ref.at[st, slot], mc_ref.at[st, slot], vt, m_ref.at[st], acc_ref.at[st],
                        kind == DIAG, first)

    def q_tile(i, carry):
        def q_tiles(t):
            return [qt_ref[0, st * HEAD_SLOT:(st + 1) * HEAD_SLOT, _q_cols(t)] for st in range(MLA_STREAMS)]

        qts_next = q_tiles(jnp.minimum(i + 1, n_q - 1))
        _sweep(i, (DIAG,), functools.partial(qk, qts=q_tiles(i)), consume,
               lambda kind: qk(0, FIRST_SLOT, kind, qts_next))
        for st in range(MLA_STREAMS):
            acc = acc_ref[st]
            ot_ref[0, st * MLA_V:(st + 1) * MLA_V, _q_cols(i)] = (
                acc[:MLA_V] / acc[MLA_V:MLA_V + 1]).astype(BF16)
        return carry

    n_q = qt_ref.shape[2] // TQ
    lax.fori_loop(0, n_q, q_tile, 0)


def _mla_attention(qt, k, vt, mask):
    b, _, s = qt.shape
    groups = MLA_HEADS // MLA_STREAMS
    return pl.pallas_call(
        _mla_kernel,
        grid=(b, groups),
        in_specs=[pl.BlockSpec((1, MLA_STREAMS * HEAD_SLOT, s), lambda bi, h: (bi, h, 0)),
                  pl.BlockSpec((1, s, MLA_STREAMS * HEAD_SLOT), lambda bi, h: (bi, 0, h)),
                  pl.BlockSpec((1, MLA_STREAMS * MLA_VROWS, s), lambda bi, h: (bi, h, 0)),
                  _const_spec(mask.shape)],
        out_specs=pl.BlockSpec((1, MLA_STREAMS * MLA_V, s), lambda bi, h: (bi, h, 0)),
        out_shape=jax.ShapeDtypeStruct((b, MLA_HEADS * MLA_V, s), BF16),
        scratch_shapes=_attn_scratch(MLA_STREAMS, MLA_VROWS),
        compiler_params=_params(2),
        name="mla_attn",
    )(qt, k, vt, mask)


def _diff_kernel(qt_ref, k_ref, vt_ref, bias_ref, lq1_ref, lk1_ref, lq2_ref, lk2_ref, sub_ref,
                 ot_ref, s_ref, mc_ref, m_ref, acc_ref, *, lam_init):
    row = lax.broadcasted_iota(jnp.int32, (HEAD_SLOT, TQ), 0)
    zero = jnp.zeros((HEAD_SLOT, TQ), BF16)
    lam = (jnp.exp(jnp.sum(lq1_ref[...] * lk1_ref[...], axis=1, keepdims=True))
           - jnp.exp(jnp.sum(lq2_ref[...] * lk2_ref[...], axis=1, keepdims=True)) + lam_init)

    def qk(j, slot, kind, qts):
        k = k_ref[0, _kv_rows(j), :]
        bias = None if kind == FAR else bias_ref[0, 0 if kind == DIAG else 1]
        for st in range(2):
            _scores(k, qts[st], bias, s_ref.at[st, slot], mc_ref.at[st, slot], kind == DIAG)

    def consume(j, slot, kind, first):
        vt = vt_ref[0, :, _kv_rows(j)]
        for st in range(2):
            _softmax_pv(s_ref.at[st, slot], mc_ref.at[st, slot], vt, m_ref.at[st], acc_ref.at[st],
                        kind == DIAG, first)

    def q_tile(i, carry):
        def q_tiles(t):
            qt = qt_ref[0, :, _q_cols(t)]
            return [jnp.where(row < DIFF_DIM, qt, zero), jnp.where(row >= DIFF_DIM, qt, zero)]

        qts_next = q_tiles(jnp.minimum(i + 1, n_q - 1))
        _sweep(i, (SUB, DIAG), functools.partial(qk, qts=q_tiles(i)), consume,
               lambda kind: qk(0, FIRST_SLOT, kind, qts_next))
        outs = [acc_ref[st, :DIFF_V] / acc_ref[st, DIFF_V:DIFF_V + 1] for st in range(2)]
        o = outs[0] - lam * outs[1]
        o = o * lax.rsqrt(jnp.mean(o * o, axis=0, keepdims=True) + EPS) * sub_ref[...]
        ot_ref[0, :, _q_cols(i)] = (o * (1.0 - lam_init)).astype(BF16)
        return carry

    n_q = qt_ref.shape[2] // TQ
    lax.fori_loop(0, n_q, q_tile, 0)


def _diff_attention(qt, k, vt, bias, lq1, lk1, lq2, lk2, sub_col, lam_init):
    b, _, s = qt.shape
    vec = _const_spec(lq1.shape)
    return pl.pallas_call(
        functools.partial(_diff_kernel, lam_init=lam_init),
        grid=(b, DIFF_HEADS),
        in_specs=[pl.BlockSpec((1, HEAD_SLOT, s), lambda bi, h: (bi, h, 0)),
                  pl.BlockSpec((1, s, HEAD_SLOT), lambda bi, h: (bi, 0, h)),
                  pl.BlockSpec((1, DIFF_VROWS, s), lambda bi, h: (bi, h, 0)),
                  pl.BlockSpec((1, 2, TK, TQ), lambda bi, h: (h, 0, 0, 0)),
                  vec, vec, vec, vec, _const_spec(sub_col.shape)],
        out_specs=pl.BlockSpec((1, DIFF_V, s), lambda bi, h: (bi, h, 0)),
        out_shape=jax.ShapeDtypeStruct((b, DIFF_HEADS * DIFF_V, s), BF16),
        scratch_shapes=_attn_scratch(2, DIFF_VROWS),
        compiler_params=_params(2),
        name="diff_attn",
    )(qt, k, vt, bias, lq1, lk1, lq2, lk2, sub_col)


def _memkv_kernel(m_ref, g_ref, w_ref, o_ref):
    mn = _rms(m_ref[...], g_ref[...]).astype(BF16)
    o_ref[...] = jnp.dot(mn, w_ref[...], preferred_element_type=F32).astype(BF16)


def _memkv(mem2d, g, w):
    rows, d = mem2d.shape
    return pl.pallas_call(
        _memkv_kernel,
        grid=(1,),
        in_specs=[_const_spec(mem2d.shape), _const_spec(g.shape), _const_spec(w.shape)],
        out_specs=_const_spec((rows, w.shape[1])),
        out_shape=jax.ShapeDtypeStruct((rows, w.shape[1]), BF16),
        compiler_params=_params(1),
        name="mem_kv",
    )(mem2d, g, w)


def _post_kernel(h_ref, oat_ref, obt_ref, g_ref, wga_ref, wgb_ref, wa_ref, wb_ref, wo_ref,
                 xg_ref, xwq_ref, kv_ref, xwo_ref, fg_ref, fwg_ref, fwu_ref, fwd_ref, fin_ref,
                 o_ref, *, scale, final_norm):
    h = h_ref[...]
    u = _rms(h, g_ref[...]).astype(BF16)
    gate_a = jax.nn.sigmoid(lax.dot_general(u, wga_ref[...], NT_DIMS, preferred_element_type=F32))
    gate_b = jax.nn.sigmoid(lax.dot_general(u, wgb_ref[...], NT_DIMS, preferred_element_type=F32))
    ya = lax.dot_general(oat_ref[0], wa_ref[...], TN_DIMS, preferred_element_type=F32)
    yb = lax.dot_general(obt_ref[0], wb_ref[...], TN_DIMS, preferred_element_type=F32)
    merged = gate_a * ya + gate_b * yb
    h = h + jnp.dot(merged.astype(BF16), wo_ref[...], preferred_element_type=F32)

    x = _rms(h, xg_ref[...]).astype(BF16)
    q = (jnp.dot(x, xwq_ref[...], preferred_element_type=F32) * scale).astype(BF16)
    kv = kv_ref[...]
    outs = []
    for hd in range(XATTN_HEADS):
        qh = q[:, hd * XATTN_DIM:(hd + 1) * XATTN_DIM]
        kh = kv[:, 2 * hd * XATTN_DIM:(2 * hd + 1) * XATTN_DIM]
        vh = kv[:, (2 * hd + 1) * XATTN_DIM:(2 * hd + 2) * XATTN_DIM]
        s = lax.dot_general(qh, kh, NT_DIMS, preferred_element_type=F32)
        m = jnp.max(s, axis=1, keepdims=True)
        p = jnp.exp2(s - m)
        l = jnp.sum(p, axis=1, keepdims=True)
        oh = jnp.dot(p.astype(BF16), vh, preferred_element_type=F32) / l
        outs.append(oh.astype(BF16))
    o = jnp.concatenate(outs, axis=1)
    h = h + jnp.dot(o, xwo_ref[...], preferred_element_type=F32)

    h = _ffn_half_step(h, fg_ref, fwg_ref, fwu_ref, fwd_ref)
    if final_norm:
        h = _rms(h, fin_ref[...])
    o_ref[...] = h


def _post(h, oat, obt, g, wga, wgb, wa, wb, wo, xg, xwq, kvmem, xwo, ffn, fin_g, final_norm, seq, mem_len):
    t, d = h.shape
    nseq = seq // TM_PROJ
    tok = pl.BlockSpec((TM_PROJ, d), lambda i: (i, 0))
    tok_t = lambda rows: pl.BlockSpec((1, rows, TM_PROJ), lambda i: (i // nseq, 0, i % nseq))
    cs = lambda a: _const_spec(a.shape)
    return pl.pallas_call(
        functools.partial(_post_kernel, scale=XATTN_DIM ** -0.5 * LOG2E, final_norm=final_norm),
        grid=(t // TM_PROJ,),
        in_specs=[tok, tok_t(oat.shape[1]), tok_t(obt.shape[1]), cs(g), cs(wga), cs(wgb), cs(wa), cs(wb), cs(wo),
                  cs(xg), cs(xwq),
                  pl.BlockSpec((mem_len, kvmem.shape[1]), lambda i: (i // nseq, 0)), cs(xwo)]
        + [cs(a) for a in ffn] + [cs(fin_g)],
        out_specs=tok,
        out_shape=jax.ShapeDtypeStruct((t, d), F32),
        compiler_params=_params(1),
        name="post_attn",
    )(h, oat, obt, g, wga, wgb, wa, wb, wo, xg, xwq, kvmem, xwo, *ffn, fin_g)


def _pad_heads(w, heads, width, slot):
    k = w.shape[0]
    w = w.reshape(k, heads, width)
    return jnp.pad(w, ((0, 0), (0, 0), (0, slot - width))).reshape(k, heads * slot)


def _rope_tables(seq):
    half = MLA_ROPE // 2
    pos = jnp.arange(seq)
    freqs = ROPE_BASE ** (-jnp.arange(half, dtype=F32) / half)
    ang = freqs[:, None] * pos.astype(F32)[None, :]
    cos, sin = jnp.cos(ang), jnp.sin(ang)
    z_nope = jnp.zeros((MLA_NOPE, seq), F32)
    z_pad = jnp.zeros((HEAD_SLOT - MLA_NOPE - MLA_ROPE, seq), F32)
    return (jnp.concatenate([z_nope, cos, cos, z_pad], axis=0),
            jnp.concatenate([z_nope, -sin, sin, z_pad], axis=0))


def kernel(x, mem, ffn1_norm, ffn1_w_gate, ffn1_w_up, ffn1_w_down, mix_norm, w_in, mla_q_norm, mla_w_q_up, mla_kv_norm, mla_w_kv_up, diff_lambda_q1, diff_lambda_k1, diff_lambda_q2, diff_lambda_k2, diff_sub_norm, rel_bias, w_branch_a, w_branch_b, w_out, xattn_norm, mem_norm, xattn_w_q, xattn_w_kv, xattn_w_o, ffn2_norm, ffn2_w_gate, ffn2_w_up, ffn2_w_down, final_norm):
    b, s, d = x.shape
    depth = ffn1_norm.shape[0]
    mem_len = mem.shape[1]
    t = b * s
    bf = lambda a: a.astype(BF16)
    row = lambda a: a.reshape(1, -1)

    tabs_t = _rope_tables(s)
    kpos = jnp.arange(TK)[:, None]
    qpos = jnp.arange(TQ)[None, :]
    mla_mask = jnp.where(kpos // CHUNK <= qpos // CHUNK, 0.0, NEG_INF).astype(F32)
    whole = lambda a: (a, 0, a.shape[0])
    bias, ffn1_w = _bias_tiles(rel_bias, [whole(ffn1_w_gate[0]), whole(ffn1_w_up[0]), whole(ffn1_w_down[0])])

    h = x.reshape(t, d)
    for l in range(depth):
        w_t = jnp.transpose(w_in[l])
        o_kr = MLA_Q_LORA + MLA_KV_LORA
        o_dq = o_kr + MLA_ROPE
        pieces = [(w_t, 0, o_dq)] + [(w_t, o_dq + n * d, d) for n in range(5)]
        if l > 0:
            ffn1_w = [bf(ffn1_w_gate[l]), bf(ffn1_w_up[l]), bf(ffn1_w_down[l])]
        post_w = [w_branch_a[l], w_branch_b[l], w_out[l], xattn_w_q[l], xattn_w_kv[l], xattn_w_o[l]]
        h, (f2g, f2u, f2d, w_lat, wdq_t, wdk_t, wdv_t, wga_t, wgb_t, wa, wb, wo, xwq, xwkv, xwo) = _ffn(
            h, row(ffn1_norm[l]), *ffn1_w,
            [whole(ffn2_w_gate[l]), whole(ffn2_w_up[l]), whole(ffn2_w_down[l])] + pieces
            + [whole(a) for a in post_w])

        w_kr = w_lat[o_kr:]
        zeros = lambda n: jnp.zeros((n, d), BF16)
        pad_l, pad_r = zeros(MLA_NOPE), zeros(HEAD_SLOT - MLA_NOPE - MLA_ROPE)
        w_kr_sw = jnp.concatenate([w_kr[MLA_ROPE // 2:], w_kr[:MLA_ROPE // 2]], axis=0)
        wlat_t = jnp.concatenate([w_lat[:o_kr], pad_l, w_kr, pad_r, pad_l, w_kr_sw, pad_r], axis=0)
        wdv_t = jnp.pad(wdv_t.reshape(DIFF_HEADS, DIFF_V, d),
                        ((0, 0), (0, DIFF_VROWS - DIFF_V), (0, 0))).reshape(-1, d)

        wq = _pad_heads(mla_w_q_up[l], MLA_HEADS, MLA_NOPE + MLA_ROPE, HEAD_SLOT)
        wkv3 = mla_w_kv_up[l].reshape(MLA_KV_LORA, MLA_HEADS, MLA_NOPE + MLA_V)
        wk = _pad_heads(wkv3[..., :MLA_NOPE].reshape(MLA_KV_LORA, -1), MLA_HEADS, MLA_NOPE, HEAD_SLOT)
        wv = _pad_heads(wkv3[..., MLA_NOPE:].reshape(MLA_KV_LORA, -1), MLA_HEADS, MLA_V, MLA_VROWS)

        qt, k, vt, dqt, dk, dvt = _proj(
            h, row(mix_norm[l]), wlat_t, wdk_t, wdq_t, wdv_t,
            row(mla_q_norm[l]), bf(wq.T), row(mla_kv_norm[l]), bf(wk), bf(wv.T),
            tabs_t, b, s)

        oat = _mla_attention(qt, k.reshape(b, s, d), vt, mla_mask)
        lam_init = 0.8 - 0.6 * math.exp(-0.3 * l)
        obt = _diff_attention(dqt, dk.reshape(b, s, d), dvt, bias,
                              row(diff_lambda_q1[l]), row(diff_lambda_k1[l]),
                              row(diff_lambda_q2[l]), row(diff_lambda_k2[l]),
                              diff_sub_norm[l].reshape(-1, 1), lam_init)

        kvmem = _memkv(mem.reshape(b * mem_len, d), row(mem_norm[l]), xwkv)
        h = _post(h, oat, obt, row(mix_norm[l]), wga_t, wgb_t, wa, wb,
                  wo, row(xattn_norm[l]), xwq, kvmem, xwo,
                  (row(ffn2_norm[l]), f2g, f2u, f2d),
                  row(final_norm), l == depth - 1, s, mem_len)
    return h.reshape(b, s, d)
```

```python
import functools
import math

import jax
import jax.numpy as jnp
from jax import lax
from jax.experimental import pallas as pl
from jax.experimental.pallas import tpu as pltpu

F32 = jnp.float32
BF16 = jnp.bfloat16

CHUNK = 64
EPS = 1e-6
NEG_INF = -1e30
MLA_HEADS = 8
MLA_Q_LORA = 384
MLA_KV_LORA = 256
MLA_NOPE = 64
MLA_ROPE = 32
MLA_V = 64
ROPE_BASE = 10000.0
DIFF_HEADS = 8
DIFF_DIM = 64
DIFF_V = 2 * DIFF_DIM
N_BUCKETS = 32
MAX_DISTANCE = 128
XATTN_HEADS = 4
XATTN_DIM = 128
LOG2E = math.log2(math.e)

LANES = 128
HEAD_SLOT = LANES
BF16_ROWS = 16
MXU_TILE = 256
VMEM_LIMIT = 56 * 1024 * 1024

TM_FFN = 512
TM_PROJ = 512
TQ = 512
TK = 512

MLA_VROWS = (MLA_V + 1 + BF16_ROWS - 1) // BF16_ROWS * BF16_ROWS
DIFF_VROWS = (DIFF_V + 1 + BF16_ROWS - 1) // BF16_ROWS * BF16_ROWS

NT_DIMS = (((1,), (1,)), ((), ()))
TN_DIMS = (((0,), (0,)), ((), ()))


def _rms(x, g):
    return x * lax.rsqrt(jnp.mean(x * x, axis=-1, keepdims=True) + EPS) * g


def _const_spec(shape):
    nd = len(shape)
    return pl.BlockSpec(shape, lambda *_: (0,) * nd, pipeline_mode=pl.Buffered(1))


def _params(n_grid):
    return pltpu.CompilerParams(
        dimension_semantics=("arbitrary",) * n_grid, vmem_limit_bytes=VMEM_LIMIT)


def _ffn_splits(dff):
    half = -(-(dff // 2) // MXU_TILE) * MXU_TILE
    return ((0, half), (half, dff))


def _ffn_half_step(x, g_ref, wg_ref, wu_ref, wd_ref):
    nb = _rms(x, g_ref[...]).astype(BF16)
    y = None
    for lo, hi in _ffn_splits(wg_ref.shape[1]):
        g = jnp.dot(nb, wg_ref[:, lo:hi], preferred_element_type=F32)
        u = jnp.dot(nb, wu_ref[:, lo:hi], preferred_element_type=F32)
        h = (g * jax.nn.sigmoid(g) * u).astype(BF16)
        part = jnp.dot(h, wd_ref[lo:hi, :], preferred_element_type=F32)
        y = part if y is None else y + part
    return x + 0.5 * y


def _split_hosted(refs):
    n_cast = (len(refs) - 1) // 2
    return refs[:n_cast], refs[n_cast], refs[n_cast + 1:]


def _cast_bands(srcs, dsts):
    for src, dst in zip(srcs, dsts):
        dst[...] = src[...].astype(BF16)


def _ffn_kernel(x_ref, g_ref, wg_ref, wu_ref, wd_ref, *refs):
    srcs, o_ref, dsts = _split_hosted(refs)
    o_ref[...] = _ffn_half_step(x_ref[...], g_ref, wg_ref, wu_ref, wd_ref)
    _cast_bands(srcs, dsts)


def _cast_band(start, rows, steps):
    band = -(-rows // steps)
    band = -(-band // BF16_ROWS) * BF16_ROWS
    while rows % band or start % band:
        band += BF16_ROWS
    return band


def _cast_specs(to_cast, steps):
    in_specs, out_specs, out_shapes = [], [], []
    for a, start, rows in to_cast:
        band = _cast_band(start, rows, steps)
        first, last = start // band, rows // band - 1
        in_specs.append(pl.BlockSpec(
            (band, a.shape[1]), lambda i, first=first, last=last: (first + jnp.minimum(i, last), 0)))
        out_specs.append(pl.BlockSpec((band, a.shape[1]), lambda i, last=last: (jnp.minimum(i, last), 0)))
        out_shapes.append(jax.ShapeDtypeStruct((rows, a.shape[1]), BF16))
    return in_specs, out_specs, out_shapes


def _ffn(x, norm_g, wg, wu, wd, to_cast):
    t, d = x.shape
    steps = t // TM_FFN
    tok = pl.BlockSpec((TM_FFN, d), lambda i: (i, 0))
    in_specs, out_specs, out_shapes = _cast_specs(to_cast, steps)
    outs = pl.pallas_call(
        _ffn_kernel,
        grid=(steps,),
        in_specs=[tok] + [_const_spec(a.shape) for a in (norm_g, wg, wu, wd)] + in_specs,
        out_specs=[tok] + out_specs,
        out_shape=[jax.ShapeDtypeStruct((t, d), F32)] + out_shapes,
        compiler_params=_params(1),
        name="ffn",
    )(x, norm_g, wg, wu, wd, *[a for a, _, _ in to_cast])
    return outs[0], outs[1:]


def _ones_rows(n_rows, rows_per_head, one_row):
    r = lax.broadcasted_iota(jnp.int32, (n_rows, 1), 0) % rows_per_head
    return jnp.where(r == one_row, 1.0, 0.0)


def _proj_kernel(h_ref, g_ref, wlat_ref, wdk_ref, wdqt_ref, wdvt_ref,
                 qn_ref, wqt_ref, kvn_ref, wk_ref, wvt_ref,
                 ct_ref, st_ref,
                 qt_ref, k_ref, vt_ref, dqt_ref, dk_ref, dvt_ref, *, q_scale, diff_scale):
    u = _rms(h_ref[...], g_ref[...]).astype(BF16)

    lat = lax.dot_general(u, wlat_ref[...], NT_DIMS, preferred_element_type=F32)
    q_lat = lat[:, :MLA_Q_LORA]
    kv_lat = lat[:, MLA_Q_LORA:MLA_Q_LORA + MLA_KV_LORA]
    o = MLA_Q_LORA + MLA_KV_LORA
    kr_a = lat[:, o:o + LANES]
    kr_b = lat[:, o + LANES:o + 2 * LANES]

    qn = _rms(q_lat, qn_ref[...]).astype(BF16)
    kvn = _rms(kv_lat, kvn_ref[...]).astype(BF16)

    qqt = lax.dot_general(wqt_ref[...], qn, NT_DIMS, preferred_element_type=F32)
    ct, st = ct_ref[...], st_ref[...]
    nope_rows = lax.broadcasted_iota(jnp.int32, (HEAD_SLOT, 1), 0) < MLA_NOPE
    cqt = (ct + jnp.where(nope_rows, 1.0, 0.0)) * q_scale
    sqt = st * q_scale
    r0, r1, r2 = MLA_NOPE, MLA_NOPE + MLA_ROPE // 2, MLA_NOPE + MLA_ROPE
    for hd in range(MLA_HEADS):
        rows = slice(hd * HEAD_SLOT, (hd + 1) * HEAD_SLOT)
        qh = qqt[rows]
        qh_sw = jnp.concatenate([qh[:r0], qh[r1:r2], qh[r0:r1], qh[r2:]], axis=0)
        qt_ref[0, rows, :] = (qh * cqt + qh_sw * sqt).astype(BF16)

    kk = jnp.dot(kvn, wk_ref[...], preferred_element_type=F32)
    kr = kr_a * ct.T + kr_b * st.T
    for hd in range(MLA_HEADS):
        sl = slice(hd * HEAD_SLOT, (hd + 1) * HEAD_SLOT)
        k_ref[:, sl] = (kk[:, sl] + kr).astype(BF16)

    vt = lax.dot_general(wvt_ref[...], kvn, NT_DIMS, preferred_element_type=F32)
    vt_ref[0] = (vt + _ones_rows(vt.shape[0], MLA_VROWS, MLA_V)).astype(BF16)

    dqt = lax.dot_general(wdqt_ref[...], u, NT_DIMS, preferred_element_type=F32)
    dqt_ref[0] = (dqt * diff_scale).astype(BF16)
    dk_ref[...] = lax.dot_general(u, wdk_ref[...], NT_DIMS, preferred_element_type=F32).astype(BF16)
    dvt = lax.dot_general(wdvt_ref[...], u, NT_DIMS, preferred_element_type=F32)
    dvt_ref[0] = (dvt + _ones_rows(dvt.shape[0], DIFF_VROWS, DIFF_V)).astype(BF16)


def _proj(h, g, wlat, wdk, wdqt, wdvt, qn, wqt, kvn, wk, wvt, tabs_t, batch, seq):
    t, d = h.shape
    nseq = seq // TM_PROJ
    tok = pl.BlockSpec((TM_PROJ, d), lambda i: (i, 0))
    tok_t = lambda rows: pl.BlockSpec((1, rows, TM_PROJ), lambda i: (i // nseq, 0, i % nseq))
    consts = [g, wlat, wdk, wdqt, wdvt, qn, wqt, kvn, wk, wvt]
    out_t = lambda rows: jax.ShapeDtypeStruct((batch, rows, seq), BF16)
    out_n = jax.ShapeDtypeStruct((t, d), BF16)
    return pl.pallas_call(
        functools.partial(_proj_kernel, q_scale=(MLA_NOPE + MLA_ROPE) ** -0.5 * LOG2E,
                          diff_scale=DIFF_DIM ** -0.5 * LOG2E),
        grid=(t // TM_PROJ,),
        in_specs=[tok] + [_const_spec(c.shape) for c in consts]
        + [pl.BlockSpec((HEAD_SLOT, TM_PROJ), lambda i: (0, i % nseq))] * 2,
        out_specs=[tok_t(wqt.shape[0]), tok, tok_t(wvt.shape[0]),
                   tok_t(wdqt.shape[0]), tok, tok_t(wdvt.shape[0])],
        out_shape=[out_t(wqt.shape[0]), out_n, out_t(wvt.shape[0]),
                   out_t(wdqt.shape[0]), out_n, out_t(wdvt.shape[0])],
        compiler_params=_params(1),
        name="in_proj",
    )(h, *consts, *tabs_t)


def _t5_bucket(rel):
    half = N_BUCKETS // 2
    max_exact = half // 2
    n = jnp.abs(rel)
    n2 = n * n
    assert (MAX_DISTANCE // max_exact) ** 2 == 2 ** (half - max_exact)
    large = max_exact
    for k in range(1, half - max_exact):
        large = large + (n2 >= (max_exact * max_exact) << k).astype(jnp.int32)
    return jnp.where(rel > 0, half, 0) + jnp.where(n < max_exact, n, large)


def _far_distance():
    half = N_BUCKETS // 2
    max_exact = half // 2
    n = max_exact
    while max_exact + sum(n * n >= (max_exact * max_exact) << k
                          for k in range(1, half - max_exact)) < half - 1:
        n += 1
    return n


def _bias_kernel(tab_ref, *refs):
    srcs, o_ref, dsts = _split_hosted(refs)
    _cast_bands(srcs, dsts)
    hd = pl.program_id(0)
    far = _far_distance()
    blk = 2 * CHUNK
    kloc = lax.broadcasted_iota(jnp.int32, (blk, blk), 0)
    qloc = lax.broadcasted_iota(jnp.int32, (blk, blk), 1)
    far_val = tab_ref[N_BUCKETS // 2 - 1, hd]
    for delta in range(2):
        for kb in range(TK // blk):
            for qb in range(TQ // blk):
                off = (kb - qb) * blk - delta * TK
                if delta == 0 and kb > qb:
                    val = jnp.full((blk, blk), NEG_INF, F32)
                elif off + blk - 1 <= -far:
                    val = jnp.zeros((blk, blk), F32)
                else:
                    bucket = _t5_bucket(kloc - qloc + off)
                    val = jnp.zeros((blk, blk), F32)
                    for b in range(N_BUCKETS):
                        val = jnp.where(bucket == b, tab_ref[b, hd], val)
                    val = (val - far_val) * LOG2E
                    if delta == 0 and kb == qb:
                        val = jnp.where(kloc // CHUNK <= qloc // CHUNK, val, NEG_INF)
                o_ref[0, delta, kb * blk:(kb + 1) * blk, qb * blk:(qb + 1) * blk] = val


def _bias_tiles(rel_bias, to_cast):
    in_specs, out_specs, out_shapes = _cast_specs(to_cast, DIFF_HEADS)
    outs = pl.pallas_call(
        _bias_kernel,
        grid=(DIFF_HEADS,),
        in_specs=[pl.BlockSpec(memory_space=pltpu.SMEM)] + in_specs,
        out_specs=[pl.BlockSpec((1, 2, TK, TQ), lambda h: (h, 0, 0, 0))] + out_specs,
        out_shape=[jax.ShapeDtypeStruct((DIFF_HEADS, 2, TK, TQ), F32)] + out_shapes,
        compiler_params=_params(1),
        name="bias_tiles",
    )(rel_bias, *[a for a, _, _ in to_cast])
    return outs[0], outs[1:]


FAR, SUB, DIAG = 0, 1, 2
assert TQ == TK and (TK // 2) % CHUNK == 0


def _scores(k, qt, bias, s_ref, mc_ref, diag):
    if diag:
        h = TK // 2
        parts = [(slice(0, h), slice(0, h)), (slice(0, TK), slice(h, TQ))]
    else:
        parts = [(slice(0, TK), slice(0, TQ))]
    for rows, cols in parts:
        s = jnp.dot(k[rows], qt[:, cols], preferred_element_type=F32)
        if bias is not None:
            s = s + bias[rows, cols]
        s_ref[rows, cols] = s
        mc_ref[:, cols] = jnp.max(s, axis=0, keepdims=True)


def _softmax_pv(s_ref, mc_ref, vt, m_ref, acc_ref, diag, first):
    m_new = mc_ref[...] if first else jnp.maximum(m_ref[...], mc_ref[...])
    if diag:
        h = TK // 2
        parts = [(slice(0, h), slice(0, h)), (slice(0, TK), slice(h, TQ))]
    else:
        parts = [(slice(0, TK), slice(0, TQ))]
    pv = []
    for rows, cols in parts:
        p = jnp.exp2(s_ref[rows, cols] - m_new[:, cols])
        pv.append(jnp.dot(vt[:, rows], p.astype(BF16), preferred_element_type=F32))
    pv = pv[0] if len(pv) == 1 else jnp.concatenate(pv, axis=1)
    acc_ref[...] = pv if first else jnp.exp2(m_ref[...] - m_new) * acc_ref[...] + pv
    m_ref[...] = m_new


FIRST_SLOT = 2
N_SLOTS = 3


def _sweep(i, near, qk, consume, qk_next):
    n_near = len(near)

    def slot_of(n):
        return FIRST_SLOT if n == 0 else (n + 1) % 2

    def run(tiles, own_first, next_kind, from_tile0=True):
        if own_first:
            qk(*tiles[0])
        for n, tile in enumerate(tiles):
            if n + 1 < len(tiles):
                qk(*tiles[n + 1])
            elif next_kind is not None:
                qk_next(next_kind)
            consume(*tile, first=from_tile0 and n == 0)

    def next_kind_after(count):
        return FAR if count + 1 > n_near else near[n_near - count - 1]

    for count in range(1, n_near + 1):
        @pl.when(i + 1 == count)
        def _(count=count):
            tiles = [(n, slot_of(n), kind) for n, kind in enumerate(near[-count:])]
            run(tiles, count <= 2, next_kind_after(count) if count > 1 else None)

    @pl.when(i + 1 > n_near)
    def _():
        n_far = i + 1 - n_near

        @pl.when(n_far == 1)
        def _():
            tiles = [(n, slot_of(n), kind) for n, kind in enumerate((FAR,) + near)]
            run(tiles, n_near <= 1, FAR)

        rest = n_far - 1
        pairs = (rest - 1) // 2

        for left in (1, 2):
            @pl.when(rest == left)
            def _(left=left):
                tiles = [(n, slot_of(n), kind) for n, kind in enumerate((FAR,) * (1 + left) + near)]
                run(tiles, False, FAR)

        @pl.when(pairs >= 1)
        def _():
            def pair(j):
                qk(j + 1, 1, FAR)
                consume(j, 0, FAR, first=False)
                qk(j + 2, 0, FAR)
                consume(j + 1, 1, FAR, first=False)

            for head in (1, 2):
                @pl.when(pairs % 2 == head % 2)
                def _(head=head):
                    qk(1, 0, FAR)
                    consume(0, FIRST_SLOT, FAR, first=True)
                    for n in range(head):
                        pair(2 * n + 1)

            first = 2 - pairs % 2

            def body(t, carry):
                j = 4 * t + 2 * first + 1
                pair(j)
                pair(j + 2)
                return carry

            lax.fori_loop(0, (pairs - first) // 2, body, 0)
            j0 = 2 * pairs + 1
            for left in (1, 2):
                @pl.when(rest - 2 * pairs == left)
                def _(left=left):
                    tiles = [(j0 + n, n % 2, kind) for n, kind in enumerate((FAR,) * left + near)]
                    run(tiles, False, FAR, from_tile0=False)


def _kv_rows(j):
    return pl.ds(pl.multiple_of(j * TK, TK), TK)


def _attn_scratch(streams, v_rows):
    return [pltpu.VMEM((streams, N_SLOTS, TK, TQ), F32),
            pltpu.VMEM((streams, N_SLOTS, 1, TQ), F32),
            pltpu.VMEM((streams, 1, TQ), F32),
            pltpu.VMEM((streams, v_rows, TQ), F32)]


MLA_STREAMS = 2


def _q_cols(i):
    return pl.ds(pl.multiple_of(i * TQ, TQ), TQ)


def _mla_kernel(qt_ref, k_ref, vt_ref, mask_ref, ot_ref, s_ref, mc_ref, m_ref, acc_ref):
    def qk(j, slot, kind, qts):
        bias = mask_ref[...] if kind == DIAG else None
        for st in range(MLA_STREAMS):
            k = k_ref[0, _kv_rows(j), st * HEAD_SLOT:(st + 1) * HEAD_SLOT]
            _scores(k, qts[st], bias, s_ref.at[st, slot], mc_ref.at[st, slot], kind == DIAG)

    def consume(j, slot, kind, first):
        for st in range(MLA_STREAMS):
            vt = vt_ref[0, st * MLA_VROWS:(st + 1) * MLA_VROWS, _kv_rows(j)]
            _softmax_pv(s_ref.at[st, slot], mc_ref.at[st, slot], vt, m_ref.at[st], acc_ref.at[st],
                        kind == DIAG, first)

    def q_tile(i, carry):
        def q_tiles(t):
            return [qt_ref[0, st * HEAD_SLOT:(st + 1) * HEAD_SLOT, _q_cols(t)] for st in range(MLA_STREAMS)]

        qts_next = q_tiles(jnp.minimum(i + 1, n_q - 1))
        _sweep(i, (DIAG,), functools.partial(qk, qts=q_tiles(i)), consume,
               lambda kind: qk(0, FIRST_SLOT, kind, qts_next))
        for st in range(MLA_STREAMS):
            acc = acc_ref[st]
            ot_ref[0, st * MLA_V:(st + 1) * MLA_V, _q_cols(i)] = (
                acc[:MLA_V] * (1.0 / acc[MLA_V:MLA_V + 1])).astype(BF16)
        return carry

    n_q = qt_ref.shape[2] // TQ
    lax.fori_loop(0, n_q, q_tile, 0)


def _mla_attention(qt, k, vt, mask):
    b, _, s = qt.shape
    groups = MLA_HEADS // MLA_STREAMS
    return pl.pallas_call(
        _mla_kernel,
        grid=(b, groups),
        in_specs=[pl.BlockSpec((1, MLA_STREAMS * HEAD_SLOT, s), lambda bi, h: (bi, h, 0)),
                  pl.BlockSpec((1, s, MLA_STREAMS * HEAD_SLOT), lambda bi, h: (bi, 0, h)),
                  pl.BlockSpec((1, MLA_STREAMS * MLA_VROWS, s), lambda bi, h: (bi, h, 0)),
                  _const_spec(mask.shape)],
        out_specs=pl.BlockSpec((1, MLA_STREAMS * MLA_V, s), lambda bi, h: (bi, h, 0)),
        out_shape=jax.ShapeDtypeStruct((b, MLA_HEADS * MLA_V, s), BF16),
        scratch_shapes=_attn_scratch(MLA_STREAMS, MLA_VROWS),
        compiler_params=_params(2),
        name="mla_attn",
    )(qt, k, vt, mask)


def _diff_kernel(qt_ref, k_ref, vt_ref, bias_ref, lq1_ref, lk1_ref, lq2_ref, lk2_ref, sub_ref,
                 ot_ref, s_ref, mc_ref, m_ref, acc_ref, *, lam_init):
    row = lax.broadcasted_iota(jnp.int32, (HEAD_SLOT, TQ), 0)
    zero = jnp.zeros((HEAD_SLOT, TQ), BF16)
    lam = (jnp.exp(jnp.sum(lq1_ref[...] * lk1_ref[...], axis=1, keepdims=True))
           - jnp.exp(jnp.sum(lq2_ref[...] * lk2_ref[...], axis=1, keepdims=True)) + lam_init)

    def qk(j, slot, kind, qts):
        k = k_ref[0, _kv_rows(j), :]
        bias = None if kind == FAR else bias_ref[0, 0 if kind == DIAG else 1]
        for st in range(2):
            _scores(k, qts[st], bias, s_ref.at[st, slot], mc_ref.at[st, slot], kind == DIAG)

    def consume(j, slot, kind, first):
        vt = vt_ref[0, :, _kv_rows(j)]
        for st in range(2):
            _softmax_pv(s_ref.at[st, slot], mc_ref.at[st, slot], vt, m_ref.at[st], acc_ref.at[st],
                        kind == DIAG, first)

    def q_tile(i, carry):
        def q_tiles(t):
            qt = qt_ref[0, :, _q_cols(t)]
            return [jnp.where(row < DIFF_DIM, qt, zero), jnp.where(row >= DIFF_DIM, qt, zero)]

        qts_next = q_tiles(jnp.minimum(i + 1, n_q - 1))
        _sweep(i, (SUB, DIAG), functools.partial(qk, qts=q_tiles(i)), consume,
               lambda kind: qk(0, FIRST_SLOT, kind, qts_next))
        outs = [acc_ref[st, :DIFF_V] * (1.0 / acc_ref[st, DIFF_V:DIFF_V + 1]) for st in range(2)]
        o = outs[0] - lam * outs[1]
        o = o * lax.rsqrt(jnp.mean(o * o, axis=0, keepdims=True) + EPS) * sub_ref[...]
        ot_ref[0, :, _q_cols(i)] = (o * (1.0 - lam_init)).astype(BF16)
        return carry

    n_q = qt_ref.shape[2] // TQ
    lax.fori_loop(0, n_q, q_tile, 0)


def _diff_attention(qt, k, vt, bias, lq1, lk1, lq2, lk2, sub_col, lam_init):
    b, _, s = qt.shape
    vec = _const_spec(lq1.shape)
    return pl.pallas_call(
        functools.partial(_diff_kernel, lam_init=lam_init),
        grid=(b, DIFF_HEADS),
        in_specs=[pl.BlockSpec((1, HEAD_SLOT, s), lambda bi, h: (bi, h, 0)),
                  pl.BlockSpec((1, s, HEAD_SLOT), lambda bi, h: (bi, 0, h)),
                  pl.BlockSpec((1, DIFF_VROWS, s), lambda bi, h: (bi, h, 0)),
                  pl.BlockSpec((1, 2, TK, TQ), lambda bi, h: (h, 0, 0, 0)),
                  vec, vec, vec, vec, _const_spec(sub_col.shape)],
        out_specs=pl.BlockSpec((1, DIFF_V, s), lambda bi, h: (bi, h, 0)),
        out_shape=jax.ShapeDtypeStruct((b, DIFF_HEADS * DIFF_V, s), BF16),
        scratch_shapes=_attn_scratch(2, DIFF_VROWS),
        compiler_params=_params(2),
        name="diff_attn",
    )(qt, k, vt, bias, lq1, lk1, lq2, lk2, sub_col)


def _memkv_kernel(m_ref, g_ref, w_ref, o_ref):
    mn = _rms(m_ref[...], g_ref[...]).astype(BF16)
    o_ref[...] = jnp.dot(mn, w_ref[...], preferred_element_type=F32).astype(BF16)


def _memkv(mem2d, g, w):
    rows, d = mem2d.shape
    return pl.pallas_call(
        _memkv_kernel,
        grid=(1,),
        in_specs=[_const_spec(mem2d.shape), _const_spec(g.shape), _const_spec(w.shape)],
        out_specs=_const_spec((rows, w.shape[1])),
        out_shape=jax.ShapeDtypeStruct((rows, w.shape[1]), BF16),
        compiler_params=_params(1),
        name="mem_kv",
    )(mem2d, g, w)


def _post_kernel(h_ref, oat_ref, obt_ref, g_ref, wga_ref, wgb_ref, wa_ref, wb_ref, wo_ref,
                 xg_ref, xwq_ref, kv_ref, xwo_ref, fg_ref, fwg_ref, fwu_ref, fwd_ref, fin_ref,
                 o_ref, *, scale, final_norm):
    h = h_ref[...]
    u = _rms(h, g_ref[...]).astype(BF16)
    gate_a = jax.nn.sigmoid(lax.dot_general(u, wga_ref[...], NT_DIMS, preferred_element_type=F32))
    gate_b = jax.nn.sigmoid(lax.dot_general(u, wgb_ref[...], NT_DIMS, preferred_element_type=F32))
    ya = lax.dot_general(oat_ref[0], wa_ref[...], TN_DIMS, preferred_element_type=F32)
    yb = lax.dot_general(obt_ref[0], wb_ref[...], TN_DIMS, preferred_element_type=F32)
    merged = gate_a * ya + gate_b * yb
    h = h + jnp.dot(merged.astype(BF16), wo_ref[...], preferred_element_type=F32)

    x = _rms(h, xg_ref[...]).astype(BF16)
    q = (jnp.dot(x, xwq_ref[...], preferred_element_type=F32) * scale).astype(BF16)
    kv = kv_ref[...]
    outs = []
    for hd in range(XATTN_HEADS):
        qh = q[:, hd * XATTN_DIM:(hd + 1) * XATTN_DIM]
        kh = kv[:, 2 * hd * XATTN_DIM:(2 * hd + 1) * XATTN_DIM]
        vh = kv[:, (2 * hd + 1) * XATTN_DIM:(2 * hd + 2) * XATTN_DIM]
        s = lax.dot_general(qh, kh, NT_DIMS, preferred_element_type=F32)
        m = jnp.max(s, axis=1, keepdims=True)
        p = jnp.exp2(s - m)
        l = jnp.sum(p, axis=1, keepdims=True)
        oh = jnp.dot(p.astype(BF16), vh, preferred_element_type=F32) / l
        outs.append(oh.astype(BF16))
    o = jnp.concatenate(outs, axis=1)
    h = h + jnp.dot(o, xwo_ref[...], preferred_element_type=F32)

    h = _ffn_half_step(h, fg_ref, fwg_ref, fwu_ref, fwd_ref)
    if final_norm:
        h = _rms(h, fin_ref[...])
    o_ref[...] = h


def _post(h, oat, obt, g, wga, wgb, wa, wb, wo, xg, xwq, kvmem, xwo, ffn, fin_g, final_norm, seq, mem_len):
    t, d = h.shape
    nseq = seq // TM_PROJ
    tok = pl.BlockSpec((TM_PROJ, d), lambda i: (i, 0))
    tok_t = lambda rows: pl.BlockSpec((1, rows, TM_PROJ), lambda i: (i // nseq, 0, i % nseq))
    cs = lambda a: _const_spec(a.shape)
    return pl.pallas_call(
        functools.partial(_post_kernel, scale=XATTN_DIM ** -0.5 * LOG2E, final_norm=final_norm),
        grid=(t // TM_PROJ,),
        in_specs=[tok, tok_t(oat.shape[1]), tok_t(obt.shape[1]), cs(g), cs(wga), cs(wgb), cs(wa), cs(wb), cs(wo),
                  cs(xg), cs(xwq),
                  pl.BlockSpec((mem_len, kvmem.shape[1]), lambda i: (i // nseq, 0)), cs(xwo)]
        + [cs(a) for a in ffn] + [cs(fin_g)],
        out_specs=tok,
        out_shape=jax.ShapeDtypeStruct((t, d), F32),
        compiler_params=_params(1),
        name="post_attn",
    )(h, oat, obt, g, wga, wgb, wa, wb, wo, xg, xwq, kvmem, xwo, *ffn, fin_g)


def _pad_heads(w, heads, width, slot):
    k = w.shape[0]
    w = w.reshape(k, heads, width)
    return jnp.pad(w, ((0, 0), (0, 0), (0, slot - width))).reshape(k, heads * slot)


def _rope_tables(seq):
    half = MLA_ROPE // 2
    pos = jnp.arange(seq)
    freqs = ROPE_BASE ** (-jnp.arange(half, dtype=F32) / half)
    ang = freqs[:, None] * pos.astype(F32)[None, :]
    cos, sin = jnp.cos(ang), jnp.sin(ang)
    z_nope = jnp.zeros((MLA_NOPE, seq), F32)
    z_pad = jnp.zeros((HEAD_SLOT - MLA_NOPE - MLA_ROPE, seq), F32)
    return (jnp.concatenate([z_nope, cos, cos, z_pad], axis=0),
            jnp.concatenate([z_nope, -sin, sin, z_pad], axis=0))


def kernel(x, mem, ffn1_norm, ffn1_w_gate, ffn1_w_up, ffn1_w_down, mix_norm, w_in, mla_q_norm, mla_w_q_up, mla_kv_norm, mla_w_kv_up, diff_lambda_q1, diff_lambda_k1, diff_lambda_q2, diff_lambda_k2, diff_sub_norm, rel_bias, w_branch_a, w_branch_b, w_out, xattn_norm, mem_norm, xattn_w_q, xattn_w_kv, xattn_w_o, ffn2_norm, ffn2_w_gate, ffn2_w_up, ffn2_w_down, final_norm):
    b, s, d = x.shape
    depth = ffn1_norm.shape[0]
    mem_len = mem.shape[1]
    t = b * s
    bf = lambda a: a.astype(BF16)
    row = lambda a: a.reshape(1, -1)

    tabs_t = _rope_tables(s)
    kpos = jnp.arange(TK)[:, None]
    qpos = jnp.arange(TQ)[None, :]
    mla_mask = jnp.where(kpos // CHUNK <= qpos // CHUNK, 0.0, NEG_INF).astype(F32)
    whole = lambda a: (a, 0, a.shape[0])
    bias, ffn1_w = _bias_tiles(rel_bias, [whole(ffn1_w_gate[0]), whole(ffn1_w_up[0]), whole(ffn1_w_down[0])])

    h = x.reshape(t, d)
    for l in range(depth):
        w_t = jnp.transpose(w_in[l])
        o_kr = MLA_Q_LORA + MLA_KV_LORA
        o_dq = o_kr + MLA_ROPE
        pieces = [(w_t, 0, o_dq)] + [(w_t, o_dq + n * d, d) for n in range(5)]
        if l > 0:
            ffn1_w = [bf(ffn1_w_gate[l]), bf(ffn1_w_up[l]), bf(ffn1_w_down[l])]
        h, (f2g, f2u, f2d, w_lat, wdq_t, wdk_t, wdv_t, wga_t, wgb_t) = _ffn(
            h, row(ffn1_norm[l]), *ffn1_w,
            [whole(ffn2_w_gate[l]), whole(ffn2_w_up[l]), whole(ffn2_w_down[l])] + pieces)

        w_kr = w_lat[o_kr:]
        zeros = lambda n: jnp.zeros((n, d), BF16)
        pad_l, pad_r = zeros(MLA_NOPE), zeros(HEAD_SLOT - MLA_NOPE - MLA_ROPE)
        w_kr_sw = jnp.concatenate([w_kr[MLA_ROPE // 2:], w_kr[:MLA_ROPE // 2]], axis=0)
        wlat_t = jnp.concatenate([w_lat[:o_kr], pad_l, w_kr, pad_r, pad_l, w_kr_sw, pad_r], axis=0)
        wdv_t = jnp.pad(wdv_t.reshape(DIFF_HEADS, DIFF_V, d),
                        ((0, 0), (0, DIFF_VROWS - DIFF_V), (0, 0))).reshape(-1, d)

        wq = _pad_heads(mla_w_q_up[l], MLA_HEADS, MLA_NOPE + MLA_ROPE, HEAD_SLOT)
        wkv3 = mla_w_kv_up[l].reshape(MLA_KV_LORA, MLA_HEADS, MLA_NOPE + MLA_V)
        wk = _pad_heads(wkv3[..., :MLA_NOPE].reshape(MLA_KV_LORA, -1), MLA_HEADS, MLA_NOPE, HEAD_SLOT)
        wv = _pad_heads(wkv3[..., MLA_NOPE:].reshape(MLA_KV_LORA, -1), MLA_HEADS, MLA_V, MLA_VROWS)

        qt, k, vt, dqt, dk, dvt = _proj(
            h, row(mix_norm[l]), wlat_t, wdk_t, wdq_t, wdv_t,
            row(mla_q_norm[l]), bf(wq.T), row(mla_kv_norm[l]), bf(wk), bf(wv.T),
            tabs_t, b, s)

        oat = _mla_attention(qt, k.reshape(b, s, d), vt, mla_mask)
        lam_init = 0.8 - 0.6 * math.exp(-0.3 * l)
        obt = _diff_attention(dqt, dk.reshape(b, s, d), dvt, bias,
                              row(diff_lambda_q1[l]), row(diff_lambda_k1[l]),
                              row(diff_lambda_q2[l]), row(diff_lambda_k2[l]),
                              diff_sub_norm[l].reshape(-1, 1), lam_init)

        kvmem = _memkv(mem.reshape(b * mem_len, d), row(mem_norm[l]), bf(xattn_w_kv[l]))
        h = _post(h, oat, obt, row(mix_norm[l]), wga_t, wgb_t, bf(w_branch_a[l]), bf(w_branch_b[l]),
                  bf(w_out[l]), row(xattn_norm[l]), bf(xattn_w_q[l]), kvmem, bf(xattn_w_o[l]),
                  (row(ffn2_norm[l]), f2g, f2u, f2d),
                  row(final_norm), l == depth - 1, s, mem_len)
    return h.reshape(b, s, d)
```

```python
import functools
import math

import jax
import jax.numpy as jnp
from jax import lax
from jax.experimental import pallas as pl
from jax.experimental.pallas import tpu as pltpu

F32 = jnp.float32
BF16 = jnp.bfloat16

CHUNK = 64
EPS = 1e-6
NEG_INF = -1e30
MLA_HEADS = 8
MLA_Q_LORA = 384
MLA_KV_LORA = 256
MLA_NOPE = 64
MLA_ROPE = 32
MLA_V = 64
ROPE_BASE = 10000.0
DIFF_HEADS = 8
DIFF_DIM = 64
DIFF_V = 2 * DIFF_DIM
N_BUCKETS = 32
MAX_DISTANCE = 128
XATTN_HEADS = 4
XATTN_DIM = 128
LOG2E = math.log2(math.e)

LANES = 128
HEAD_SLOT = LANES
BF16_ROWS = 16
MXU_TILE = 256
VMEM_LIMIT = 56 * 1024 * 1024

TM_FFN = 512
TM_PROJ = 512
TQ = 512
TK = 512

MLA_VROWS = (MLA_V + 1 + BF16_ROWS - 1) // BF16_ROWS * BF16_ROWS
DIFF_VROWS = (DIFF_V + 1 + BF16_ROWS - 1) // BF16_ROWS * BF16_ROWS

NT_DIMS = (((1,), (1,)), ((), ()))
TN_DIMS = (((0,), (0,)), ((), ()))


def _rms(x, g):
    return x * lax.rsqrt(jnp.mean(x * x, axis=-1, keepdims=True) + EPS) * g


def _const_spec(shape):
    nd = len(shape)
    return pl.BlockSpec(shape, lambda *_: (0,) * nd, pipeline_mode=pl.Buffered(1))


def _params(n_grid):
    return pltpu.CompilerParams(
        dimension_semantics=("arbitrary",) * n_grid, vmem_limit_bytes=VMEM_LIMIT)


def _ffn_splits(dff):
    half = -(-(dff // 2) // MXU_TILE) * MXU_TILE
    return ((0, half), (half, dff))


def _ffn_half_step(x, g_ref, wg_ref, wu_ref, wd_ref):
    nb = _rms(x, g_ref[...]).astype(BF16)
    y = None
    for lo, hi in _ffn_splits(wg_ref.shape[1]):
        g = jnp.dot(nb, wg_ref[:, lo:hi], preferred_element_type=F32)
        u = jnp.dot(nb, wu_ref[:, lo:hi], preferred_element_type=F32)
        h = (g * jax.nn.sigmoid(g) * u).astype(BF16)
        part = jnp.dot(h, wd_ref[lo:hi, :], preferred_element_type=F32)
        y = part if y is None else y + part
    return x + 0.5 * y


def _split_hosted(refs):
    n_cast = (len(refs) - 1) // 2
    return refs[:n_cast], refs[n_cast], refs[n_cast + 1:]


def _cast_bands(srcs, dsts):
    for src, dst in zip(srcs, dsts):
        dst[...] = src[...].astype(BF16)


def _ffn_kernel(x_ref, g_ref, wg_ref, wu_ref, wd_ref, *refs):
    srcs, o_ref, dsts = _split_hosted(refs)
    o_ref[...] = _ffn_half_step(x_ref[...], g_ref, wg_ref, wu_ref, wd_ref)
    _cast_bands(srcs, dsts)


def _cast_band(start, rows, steps):
    band = -(-rows // steps)
    band = -(-band // BF16_ROWS) * BF16_ROWS
    while rows % band or start % band:
        band += BF16_ROWS
    return band


def _cast_specs(to_cast, steps):
    in_specs, out_specs, out_shapes = [], [], []
    for a, start, rows in to_cast:
        band = _cast_band(start, rows, steps)
        first, last = start // band, rows // band - 1
        in_specs.append(pl.BlockSpec(
            (band, a.shape[1]), lambda i, first=first, last=last: (first + jnp.minimum(i, last), 0)))
        out_specs.append(pl.BlockSpec((band, a.shape[1]), lambda i, last=last: (jnp.minimum(i, last), 0)))
        out_shapes.append(jax.ShapeDtypeStruct((rows, a.shape[1]), BF16))
    return in_specs, out_specs, out_shapes


def _ffn(x, norm_g, wg, wu, wd, to_cast):
    t, d = x.shape
    steps = t // TM_FFN
    tok = pl.BlockSpec((TM_FFN, d), lambda i: (i, 0))
    in_specs, out_specs, out_shapes = _cast_specs(to_cast, steps)
    outs = pl.pallas_call(
        _ffn_kernel,
        grid=(steps,),
        in_specs=[tok] + [_const_spec(a.shape) for a in (norm_g, wg, wu, wd)] + in_specs,
        out_specs=[tok] + out_specs,
        out_shape=[jax.ShapeDtypeStruct((t, d), F32)] + out_shapes,
        compiler_params=_params(1),
        name="ffn",
    )(x, norm_g, wg, wu, wd, *[a for a, _, _ in to_cast])
    return outs[0], outs[1:]


def _ones_rows(n_rows, rows_per_head, one_row):
    r = lax.broadcasted_iota(jnp.int32, (n_rows, 1), 0) % rows_per_head
    return jnp.where(r == one_row, 1.0, 0.0)


def _proj_kernel(h_ref, g_ref, wlat_ref, wdk_ref, wdqt_ref, wdvt_ref,
                 qn_ref, wqt_ref, kvn_ref, wk_ref, wvt_ref,
                 ct_ref, st_ref,
                 qt_ref, k_ref, vt_ref, dqt_ref, dk_ref, dvt_ref, *, q_scale, diff_scale):
    u = _rms(h_ref[...], g_ref[...]).astype(BF16)

    lat = lax.dot_general(u, wlat_ref[...], NT_DIMS, preferred_element_type=F32)
    q_lat = lat[:, :MLA_Q_LORA]
    kv_lat = lat[:, MLA_Q_LORA:MLA_Q_LORA + MLA_KV_LORA]
    o = MLA_Q_LORA + MLA_KV_LORA
    kr_a = lat[:, o:o + LANES]
    kr_b = lat[:, o + LANES:o + 2 * LANES]

    qn = _rms(q_lat, qn_ref[...]).astype(BF16)
    kvn = _rms(kv_lat, kvn_ref[...]).astype(BF16)

    qqt = lax.dot_general(wqt_ref[...], qn, NT_DIMS, preferred_element_type=F32)
    ct, st = ct_ref[...], st_ref[...]
    nope_rows = lax.broadcasted_iota(jnp.int32, (HEAD_SLOT, 1), 0) < MLA_NOPE
    cqt = (ct + jnp.where(nope_rows, 1.0, 0.0)) * q_scale
    sqt = st * q_scale
    r0, r1, r2 = MLA_NOPE, MLA_NOPE + MLA_ROPE // 2, MLA_NOPE + MLA_ROPE
    for hd in range(MLA_HEADS):
        rows = slice(hd * HEAD_SLOT, (hd + 1) * HEAD_SLOT)
        qh = qqt[rows]
        qh_sw = jnp.concatenate([qh[:r0], qh[r1:r2], qh[r0:r1], qh[r2:]], axis=0)
        qt_ref[0, rows, :] = (qh * cqt + qh_sw * sqt).astype(BF16)

    kk = jnp.dot(kvn, wk_ref[...], preferred_element_type=F32)
    kr = kr_a * ct.T + kr_b * st.T
    for hd in range(MLA_HEADS):
        sl = slice(hd * HEAD_SLOT, (hd + 1) * HEAD_SLOT)
        k_ref[:, sl] = (kk[:, sl] + kr).astype(BF16)

    vt = lax.dot_general(wvt_ref[...], kvn, NT_DIMS, preferred_element_type=F32)
    vt_ref[0] = (vt + _ones_rows(vt.shape[0], MLA_VROWS, MLA_V)).astype(BF16)

    dqt = lax.dot_general(wdqt_ref[...], u, NT_DIMS, preferred_element_type=F32)
    dqt_ref[0] = (dqt * diff_scale).astype(BF16)
    dk_ref[...] = lax.dot_general(u, wdk_ref[...], NT_DIMS, preferred_element_type=F32).astype(BF16)
    dvt = lax.dot_general(wdvt_ref[...], u, NT_DIMS, preferred_element_type=F32)
    dvt_ref[0] = (dvt + _ones_rows(dvt.shape[0], DIFF_VROWS, DIFF_V)).astype(BF16)


def _proj(h, g, wlat, wdk, wdqt, wdvt, qn, wqt, kvn, wk, wvt, tabs_t, batch, seq):
    t, d = h.shape
    nseq = seq // TM_PROJ
    tok = pl.BlockSpec((TM_PROJ, d), lambda i: (i, 0))
    tok_t = lambda rows: pl.BlockSpec((1, rows, TM_PROJ), lambda i: (i // nseq, 0, i % nseq))
    consts = [g, wlat, wdk, wdqt, wdvt, qn, wqt, kvn, wk, wvt]
    out_t = lambda rows: jax.ShapeDtypeStruct((batch, rows, seq), BF16)
    out_n = jax.ShapeDtypeStruct((t, d), BF16)
    return pl.pallas_call(
        functools.partial(_proj_kernel, q_scale=(MLA_NOPE + MLA_ROPE) ** -0.5 * LOG2E,
                          diff_scale=DIFF_DIM ** -0.5 * LOG2E),
        grid=(t // TM_PROJ,),
        in_specs=[tok] + [_const_spec(c.shape) for c in consts]
        + [pl.BlockSpec((HEAD_SLOT, TM_PROJ), lambda i: (0, i % nseq))] * 2,
        out_specs=[tok_t(wqt.shape[0]), tok, tok_t(wvt.shape[0]),
                   tok_t(wdqt.shape[0]), tok, tok_t(wdvt.shape[0])],
        out_shape=[out_t(wqt.shape[0]), out_n, out_t(wvt.shape[0]),
                   out_t(wdqt.shape[0]), out_n, out_t(wdvt.shape[0])],
        compiler_params=_params(1),
        name="in_proj",
    )(h, *consts, *tabs_t)


def _t5_bucket(rel):
    half = N_BUCKETS // 2
    max_exact = half // 2
    n = jnp.abs(rel)
    n2 = n * n
    assert (MAX_DISTANCE // max_exact) ** 2 == 2 ** (half - max_exact)
    large = max_exact
    for k in range(1, half - max_exact):
        large = large + (n2 >= (max_exact * max_exact) << k).astype(jnp.int32)
    return jnp.where(rel > 0, half, 0) + jnp.where(n < max_exact, n, large)


def _far_distance():
    half = N_BUCKETS // 2
    max_exact = half // 2
    n = max_exact
    while max_exact + sum(n * n >= (max_exact * max_exact) << k
                          for k in range(1, half - max_exact)) < half - 1:
        n += 1
    return n


def _bias_kernel(tab_ref, *refs):
    srcs, o_ref, dsts = _split_hosted(refs)
    _cast_bands(srcs, dsts)
    hd = pl.program_id(0)
    far = _far_distance()
    blk = 2 * CHUNK
    kloc = lax.broadcasted_iota(jnp.int32, (blk, blk), 0)
    qloc = lax.broadcasted_iota(jnp.int32, (blk, blk), 1)
    far_val = tab_ref[N_BUCKETS // 2 - 1, hd]
    for delta in range(2):
        for kb in range(TK // blk):
            for qb in range(TQ // blk):
                off = (kb - qb) * blk - delta * TK
                if delta == 0 and kb > qb:
                    val = jnp.full((blk, blk), NEG_INF, F32)
                elif off + blk - 1 <= -far:
                    val = jnp.zeros((blk, blk), F32)
                else:
                    bucket = _t5_bucket(kloc - qloc + off)
                    val = jnp.zeros((blk, blk), F32)
                    for b in range(N_BUCKETS):
                        val = jnp.where(bucket == b, tab_ref[b, hd], val)
                    val = (val - far_val) * LOG2E
                    if delta == 0 and kb == qb:
                        val = jnp.where(kloc // CHUNK <= qloc // CHUNK, val, NEG_INF)
                o_ref[0, delta, kb * blk:(kb + 1) * blk, qb * blk:(qb + 1) * blk] = val


def _bias_tiles(rel_bias, to_cast):
    in_specs, out_specs, out_shapes = _cast_specs(to_cast, DIFF_HEADS)
    outs = pl.pallas_call(
        _bias_kernel,
        grid=(DIFF_HEADS,),
        in_specs=[pl.BlockSpec(memory_space=pltpu.SMEM)] + in_specs,
        out_specs=[pl.BlockSpec((1, 2, TK, TQ), lambda h: (h, 0, 0, 0))] + out_specs,
        out_shape=[jax.ShapeDtypeStruct((DIFF_HEADS, 2, TK, TQ), F32)] + out_shapes,
        compiler_params=_params(1),
        name="bias_tiles",
    )(rel_bias, *[a for a, _, _ in to_cast])
    return outs[0], outs[1:]


FAR, SUB, DIAG = 0, 1, 2
assert TQ == TK and (TK // 2) % CHUNK == 0


def _scores(k, qt, bias, s_ref, mc_ref, diag):
    if diag:
        h = TK // 2
        parts = [(slice(0, h), slice(0, h)), (slice(0, TK), slice(h, TQ))]
    else:
        parts = [(slice(0, TK), slice(0, TQ))]
    for rows, cols in parts:
        s = jnp.dot(k[rows], qt[:, cols], preferred_element_type=F32)
        if bias is not None:
            s = s + bias[rows, cols]
        s_ref[rows, cols] = s
        mc_ref[:, cols] = jnp.max(s, axis=0, keepdims=True)


def _softmax_pv(s_ref, mc_ref, vt, m_ref, acc_ref, diag, first):
    m_new = mc_ref[...] if first else jnp.maximum(m_ref[...], mc_ref[...])
    if diag:
        h = TK // 2
        parts = [(slice(0, h), slice(0, h)), (slice(0, TK), slice(h, TQ))]
    else:
        parts = [(slice(0, TK), slice(0, TQ))]
    pv = []
    for rows, cols in parts:
        p = jnp.exp2(s_ref[rows, cols] - m_new[:, cols])
        pv.append(jnp.dot(vt[:, rows], p.astype(BF16), preferred_element_type=F32))
    pv = pv[0] if len(pv) == 1 else jnp.concatenate(pv, axis=1)
    acc_ref[...] = pv if first else jnp.exp2(m_ref[...] - m_new) * acc_ref[...] + pv
    m_ref[...] = m_new


FIRST_SLOT = 2
N_SLOTS = 3


def _sweep(i, near, qk, consume, qk_next):
    n_near = len(near)

    def slot_of(n):
        return FIRST_SLOT if n == 0 else (n + 1) % 2

    def run(tiles, own_first, next_kind, from_tile0=True):
        if own_first:
            qk(*tiles[0])
        for n, tile in enumerate(tiles):
            if n + 1 < len(tiles):
                qk(*tiles[n + 1])
            elif next_kind is not None:
                qk_next(next_kind)
            consume(*tile, first=from_tile0 and n == 0)

    def next_kind_after(count):
        return FAR if count + 1 > n_near else near[n_near - count - 1]

    for count in range(1, n_near + 1):
        @pl.when(i + 1 == count)
        def _(count=count):
            tiles = [(n, slot_of(n), kind) for n, kind in enumerate(near[-count:])]
            run(tiles, count <= 2, next_kind_after(count) if count > 1 else None)

    @pl.when(i + 1 > n_near)
    def _():
        n_far = i + 1 - n_near

        @pl.when(n_far == 1)
        def _():
            tiles = [(n, slot_of(n), kind) for n, kind in enumerate((FAR,) + near)]
            run(tiles, n_near <= 1, FAR)

        rest = n_far - 1
        pairs = (rest - 1) // 2

        for left in (1, 2):
            @pl.when(rest == left)
            def _(left=left):
                tiles = [(n, slot_of(n), kind) for n, kind in enumerate((FAR,) * (1 + left) + near)]
                run(tiles, False, FAR)

        @pl.when(pairs >= 1)
        def _():
            def pair(j):
                qk(j + 1, 1, FAR)
                consume(j, 0, FAR, first=False)
                qk(j + 2, 0, FAR)
                consume(j + 1, 1, FAR, first=False)

            for head in (1, 2):
                @pl.when(pairs % 2 == head % 2)
                def _(head=head):
                    qk(1, 0, FAR)
                    consume(0, FIRST_SLOT, FAR, first=True)
                    for n in range(head):
                        pair(2 * n + 1)

            first = 2 - pairs % 2

            def body(t, carry):
                j = 4 * t + 2 * first + 1
                pair(j)
                pair(j + 2)
                return carry

            lax.fori_loop(0, (pairs - first) // 2, body, 0)
            j0 = 2 * pairs + 1
            for left in (1, 2):
                @pl.when(rest - 2 * pairs == left)
                def _(left=left):
                    tiles = [(j0 + n, n % 2, kind) for n, kind in enumerate((FAR,) * left + near)]
                    run(tiles, False, FAR, from_tile0=False)


def _kv_rows(j):
    return pl.ds(pl.multiple_of(j * TK, TK), TK)


def _attn_scratch(streams, v_rows, acc_sets=()):
    return [pltpu.VMEM((streams, N_SLOTS, TK, TQ), F32),
            pltpu.VMEM((streams, N_SLOTS, 1, TQ), F32),
            pltpu.VMEM((streams, 1, TQ), F32),
            pltpu.VMEM(acc_sets + (streams, v_rows, TQ), F32)]


MLA_STREAMS = 2


def _q_cols(i):
    return pl.ds(pl.multiple_of(i * TQ, TQ), TQ)


def _mla_kernel(qt_ref, k_ref, vt_ref, mask_ref, ot_ref, s_ref, mc_ref, m_ref, acc_ref):
    def qk(j, slot, kind, qts):
        bias = mask_ref[...] if kind == DIAG else None
        for st in range(MLA_STREAMS):
            k = k_ref[0, _kv_rows(j), st * HEAD_SLOT:(st + 1) * HEAD_SLOT]
            _scores(k, qts[st], bias, s_ref.at[st, slot], mc_ref.at[st, slot], kind == DIAG)

    def consume(j, slot, kind, first):
        for st in range(MLA_STREAMS):
            vt = vt_ref[0, st * MLA_VROWS:(st + 1) * MLA_VROWS, _kv_rows(j)]
            _softmax_pv(s_ref.at[st, slot], mc_ref.at[st, slot], vt, m_ref.at[st], acc_ref.at[st],
                        kind == DIAG, first)

    def q_tile(i, carry):
        def q_tiles(t):
            return [qt_ref[0, st * HEAD_SLOT:(st + 1) * HEAD_SLOT, _q_cols(t)] for st in range(MLA_STREAMS)]

        qts_next = q_tiles(jnp.minimum(i + 1, n_q - 1))
        _sweep(i, (DIAG,), functools.partial(qk, qts=q_tiles(i)), consume,
               lambda kind: qk(0, FIRST_SLOT, kind, qts_next))
        for st in range(MLA_STREAMS):
            acc = acc_ref[st]
            ot_ref[0, st * MLA_V:(st + 1) * MLA_V, _q_cols(i)] = (
                acc[:MLA_V] * (1.0 / acc[MLA_V:MLA_V + 1])).astype(BF16)
        return carry

    n_q = qt_ref.shape[2] // TQ
    lax.fori_loop(0, n_q, q_tile, 0)


def _mla_attention(qt, k, vt, mask):
    b, _, s = qt.shape
    groups = MLA_HEADS // MLA_STREAMS
    return pl.pallas_call(
        _mla_kernel,
        grid=(b, groups),
        in_specs=[pl.BlockSpec((1, MLA_STREAMS * HEAD_SLOT, s), lambda bi, h: (bi, h, 0)),
                  pl.BlockSpec((1, s, MLA_STREAMS * HEAD_SLOT), lambda bi, h: (bi, 0, h)),
                  pl.BlockSpec((1, MLA_STREAMS * MLA_VROWS, s), lambda bi, h: (bi, h, 0)),
                  _const_spec(mask.shape)],
        out_specs=pl.BlockSpec((1, MLA_STREAMS * MLA_V, s), lambda bi, h: (bi, h, 0)),
        out_shape=jax.ShapeDtypeStruct((b, MLA_HEADS * MLA_V, s), BF16),
        scratch_shapes=_attn_scratch(MLA_STREAMS, MLA_VROWS),
        compiler_params=_params(2),
        name="mla_attn",
    )(qt, k, vt, mask)


def _diff_kernel(qt_ref, k_ref, vt_ref, bias_ref, lq1_ref, lk1_ref, lq2_ref, lk2_ref, sub_ref,
                 ot_ref, s_ref, mc_ref, m_ref, acc_ref, *, lam_init):
    row = lax.broadcasted_iota(jnp.int32, (HEAD_SLOT, TQ), 0)
    zero = jnp.zeros((HEAD_SLOT, TQ), BF16)
    lam = (jnp.exp(jnp.sum(lq1_ref[...] * lk1_ref[...], axis=1, keepdims=True))
           - jnp.exp(jnp.sum(lq2_ref[...] * lk2_ref[...], axis=1, keepdims=True)) + lam_init)

    def qk(j, slot, kind, qts):
        k = k_ref[0, _kv_rows(j), :]
        bias = None if kind == FAR else bias_ref[0, 0 if kind == DIAG else 1]
        for st in range(2):
            _scores(k, qts[st], bias, s_ref.at[st, slot], mc_ref.at[st, slot], kind == DIAG)

    def finalize(t, par):
        outs = [acc_ref[par, st, :DIFF_V] * (1.0 / acc_ref[par, st, DIFF_V:DIFF_V + 1]) for st in range(2)]
        o = outs[0] - lam * outs[1]
        o = o * lax.rsqrt(jnp.mean(o * o, axis=0, keepdims=True) + EPS) * sub_ref[...]
        ot_ref[0, :, _q_cols(t)] = (o * (1.0 - lam_init)).astype(BF16)

    def consume(j, slot, kind, first, i, par):
        if first:
            finalize(jnp.maximum(i - 1, 0), 1 - par)
        vt = vt_ref[0, :, _kv_rows(j)]
        for st in range(2):
            _softmax_pv(s_ref.at[st, slot], mc_ref.at[st, slot], vt, m_ref.at[st], acc_ref.at[par, st],
                        kind == DIAG, first)

    def q_tiles(t):
        qt = qt_ref[0, :, _q_cols(t)]
        return [jnp.where(row < DIFF_DIM, qt, zero), jnp.where(row >= DIFF_DIM, qt, zero)]

    def q_tile_pair(t, carry):
        for par in range(2):
            i = 2 * t + par
            qts_next = q_tiles(jnp.minimum(i + 1, n_q - 1))
            _sweep(i, (SUB, DIAG), functools.partial(qk, qts=q_tiles(i)),
                   functools.partial(consume, i=i, par=par),
                   lambda kind, qts_next=qts_next: qk(0, FIRST_SLOT, kind, qts_next))
        return carry

    n_q = qt_ref.shape[2] // TQ
    assert n_q % 2 == 0
    acc_ref[1] = jnp.ones(acc_ref.shape[1:], F32)
    lax.fori_loop(0, n_q // 2, q_tile_pair, 0)
    finalize(n_q - 1, 1)


def _diff_attention(qt, k, vt, bias, lq1, lk1, lq2, lk2, sub_col, lam_init):
    b, _, s = qt.shape
    vec = _const_spec(lq1.shape)
    return pl.pallas_call(
        functools.partial(_diff_kernel, lam_init=lam_init),
        grid=(b, DIFF_HEADS),
        in_specs=[pl.BlockSpec((1, HEAD_SLOT, s), lambda bi, h: (bi, h, 0)),
                  pl.BlockSpec((1, s, HEAD_SLOT), lambda bi, h: (bi, 0, h)),
                  pl.BlockSpec((1, DIFF_VROWS, s), lambda bi, h: (bi, h, 0)),
                  pl.BlockSpec((1, 2, TK, TQ), lambda bi, h: (h, 0, 0, 0)),
                  vec, vec, vec, vec, _const_spec(sub_col.shape)],
        out_specs=pl.BlockSpec((1, DIFF_V, s), lambda bi, h: (bi, h, 0)),
        out_shape=jax.ShapeDtypeStruct((b, DIFF_HEADS * DIFF_V, s), BF16),
        scratch_shapes=_attn_scratch(2, DIFF_VROWS, acc_sets=(2,)),
        compiler_params=_params(2),
        name="diff_attn",
    )(qt, k, vt, bias, lq1, lk1, lq2, lk2, sub_col)


def _memkv_kernel(m_ref, g_ref, w_ref, o_ref):
    mn = _rms(m_ref[...], g_ref[...]).astype(BF16)
    o_ref[...] = jnp.dot(mn, w_ref[...], preferred_element_type=F32).astype(BF16)


def _memkv(mem2d, g, w):
    rows, d = mem2d.shape
    return pl.pallas_call(
        _memkv_kernel,
        grid=(1,),
        in_specs=[_const_spec(mem2d.shape), _const_spec(g.shape), _const_spec(w.shape)],
        out_specs=_const_spec((rows, w.shape[1])),
        out_shape=jax.ShapeDtypeStruct((rows, w.shape[1]), BF16),
        compiler_params=_params(1),
        name="mem_kv",
    )(mem2d, g, w)


def _post_kernel(h_ref, oat_ref, obt_ref, g_ref, wga_ref, wgb_ref, wa_ref, wb_ref, wo_ref,
                 xg_ref, xwq_ref, kv_ref, xwo_ref, fg_ref, fwg_ref, fwu_ref, fwd_ref, fin_ref,
                 o_ref, *, scale, final_norm):
    h = h_ref[...]
    u = _rms(h, g_ref[...]).astype(BF16)
    gate_a = jax.nn.sigmoid(lax.dot_general(u, wga_ref[...], NT_DIMS, preferred_element_type=F32))
    gate_b = jax.nn.sigmoid(lax.dot_general(u, wgb_ref[...], NT_DIMS, preferred_element_type=F32))
    ya = lax.dot_general(oat_ref[0], wa_ref[...], TN_DIMS, preferred_element_type=F32)
    yb = lax.dot_general(obt_ref[0], wb_ref[...], TN_DIMS, preferred_element_type=F32)
    merged = gate_a * ya + gate_b * yb
    h = h + jnp.dot(merged.astype(BF16), wo_ref[...], preferred_element_type=F32)

    x = _rms(h, xg_ref[...]).astype(BF16)
    q = (jnp.dot(x, xwq_ref[...], preferred_element_type=F32) * scale).astype(BF16)
    kv = kv_ref[...]
    outs = []
    for hd in range(XATTN_HEADS):
        qh = q[:, hd * XATTN_DIM:(hd + 1) * XATTN_DIM]
        kh = kv[:, 2 * hd * XATTN_DIM:(2 * hd + 1) * XATTN_DIM]
        vh = kv[:, (2 * hd + 1) * XATTN_DIM:(2 * hd + 2) * XATTN_DIM]
        s = lax.dot_general(qh, kh, NT_DIMS, preferred_element_type=F32)
        m = jnp.max(s, axis=1, keepdims=True)
        p = jnp.exp2(s - m)
        l = jnp.sum(p, axis=1, keepdims=True)
        oh = jnp.dot(p.astype(BF16), vh, preferred_element_type=F32) / l
        outs.append(oh.astype(BF16))
    o = jnp.concatenate(outs, axis=1)
    h = h + jnp.dot(o, xwo_ref[...], preferred_element_type=F32)

    h = _ffn_half_step(h, fg_ref, fwg_ref, fwu_ref, fwd_ref)
    if final_norm:
        h = _rms(h, fin_ref[...])
    o_ref[...] = h


def _post(h, oat, obt, g, wga, wgb, wa, wb, wo, xg, xwq, kvmem, xwo, ffn, fin_g, final_norm, seq, mem_len):
    t, d = h.shape
    nseq = seq // TM_PROJ
    tok = pl.BlockSpec((TM_PROJ, d), lambda i: (i, 0))
    tok_t = lambda rows: pl.BlockSpec((1, rows, TM_PROJ), lambda i: (i // nseq, 0, i % nseq))
    cs = lambda a: _const_spec(a.shape)
    return pl.pallas_call(
        functools.partial(_post_kernel, scale=XATTN_DIM ** -0.5 * LOG2E, final_norm=final_norm),
        grid=(t // TM_PROJ,),
        in_specs=[tok, tok_t(oat.shape[1]), tok_t(obt.shape[1]), cs(g), cs(wga), cs(wgb), cs(wa), cs(wb), cs(wo),
                  cs(xg), cs(xwq),
                  pl.BlockSpec((mem_len, kvmem.shape[1]), lambda i: (i // nseq, 0)), cs(xwo)]
        + [cs(a) for a in ffn] + [cs(fin_g)],
        out_specs=tok,
        out_shape=jax.ShapeDtypeStruct((t, d), F32),
        compiler_params=_params(1),
        name="post_attn",
    )(h, oat, obt, g, wga, wgb, wa, wb, wo, xg, xwq, kvmem, xwo, *ffn, fin_g)


def _pad_heads(w, heads, width, slot):
    k = w.shape[0]
    w = w.reshape(k, heads, width)
    return jnp.pad(w, ((0, 0), (0, 0), (0, slot - width))).reshape(k, heads * slot)


def _rope_tables(seq):
    half = MLA_ROPE // 2
    pos = jnp.arange(seq)
    freqs = ROPE_BASE ** (-jnp.arange(half, dtype=F32) / half)
    ang = freqs[:, None] * pos.astype(F32)[None, :]
    cos, sin = jnp.cos(ang), jnp.sin(ang)
    z_nope = jnp.zeros((MLA_NOPE, seq), F32)
    z_pad = jnp.zeros((HEAD_SLOT - MLA_NOPE - MLA_ROPE, seq), F32)
    return (jnp.concatenate([z_nope, cos, cos, z_pad], axis=0),
            jnp.concatenate([z_nope, -sin, sin, z_pad], axis=0))


def kernel(x, mem, ffn1_norm, ffn1_w_gate, ffn1_w_up, ffn1_w_down, mix_norm, w_in, mla_q_norm, mla_w_q_up, mla_kv_norm, mla_w_kv_up, diff_lambda_q1, diff_lambda_k1, diff_lambda_q2, diff_lambda_k2, diff_sub_norm, rel_bias, w_branch_a, w_branch_b, w_out, xattn_norm, mem_norm, xattn_w_q, xattn_w_kv, xattn_w_o, ffn2_norm, ffn2_w_gate, ffn2_w_up, ffn2_w_down, final_norm):
    b, s, d = x.shape
    depth = ffn1_norm.shape[0]
    mem_len = mem.shape[1]
    t = b * s
    bf = lambda a: a.astype(BF16)
    row = lambda a: a.reshape(1, -1)

    tabs_t = _rope_tables(s)
    kpos = jnp.arange(TK)[:, None]
    qpos = jnp.arange(TQ)[None, :]
    mla_mask = jnp.where(kpos // CHUNK <= qpos // CHUNK, 0.0, NEG_INF).astype(F32)
    whole = lambda a: (a, 0, a.shape[0])
    bias, ffn1_w = _bias_tiles(rel_bias, [whole(ffn1_w_gate[0]), whole(ffn1_w_up[0]), whole(ffn1_w_down[0])])

    h = x.reshape(t, d)
    for l in range(depth):
        w_t = jnp.transpose(w_in[l])
        o_kr = MLA_Q_LORA + MLA_KV_LORA
        o_dq = o_kr + MLA_ROPE
        pieces = [(w_t, 0, o_dq)] + [(w_t, o_dq + n * d, d) for n in range(5)]
        if l > 0:
            ffn1_w = [bf(ffn1_w_gate[l]), bf(ffn1_w_up[l]), bf(ffn1_w_down[l])]
        h, (f2g, f2u, f2d, w_lat, wdq_t, wdk_t, wdv_t, wga_t, wgb_t) = _ffn(
            h, row(ffn1_norm[l]), *ffn1_w,
            [whole(ffn2_w_gate[l]), whole(ffn2_w_up[l]), whole(ffn2_w_down[l])] + pieces)

        w_kr = w_lat[o_kr:]
        zeros = lambda n: jnp.zeros((n, d), BF16)
        pad_l, pad_r = zeros(MLA_NOPE), zeros(HEAD_SLOT - MLA_NOPE - MLA_ROPE)
        w_kr_sw = jnp.concatenate([w_kr[MLA_ROPE // 2:], w_kr[:MLA_ROPE // 2]], axis=0)
        wlat_t = jnp.concatenate([w_lat[:o_kr], pad_l, w_kr, pad_r, pad_l, w_kr_sw, pad_r], axis=0)
        wdv_t = jnp.pad(wdv_t.reshape(DIFF_HEADS, DIFF_V, d),
                        ((0, 0), (0, DIFF_VROWS - DIFF_V), (0, 0))).reshape(-1, d)

        wq = _pad_heads(mla_w_q_up[l], MLA_HEADS, MLA_NOPE + MLA_ROPE, HEAD_SLOT)
        wkv3 = mla_w_kv_up[l].reshape(MLA_KV_LORA, MLA_HEADS, MLA_NOPE + MLA_V)
        wk = _pad_heads(wkv3[..., :MLA_NOPE].reshape(MLA_KV_LORA, -1), MLA_HEADS, MLA_NOPE, HEAD_SLOT)
        wv = _pad_heads(wkv3[..., MLA_NOPE:].reshape(MLA_KV_LORA, -1), MLA_HEADS, MLA_V, MLA_VROWS)

        qt, k, vt, dqt, dk, dvt = _proj(
            h, row(mix_norm[l]), wlat_t, wdk_t, wdq_t, wdv_t,
            row(mla_q_norm[l]), bf(wq.T), row(mla_kv_norm[l]), bf(wk), bf(wv.T),
            tabs_t, b, s)

        oat = _mla_attention(qt, k.reshape(b, s, d), vt, mla_mask)
        lam_init = 0.8 - 0.6 * math.exp(-0.3 * l)
        obt = _diff_attention(dqt, dk.reshape(b, s, d), dvt, bias,
                              row(diff_lambda_q1[l]), row(diff_lambda_k1[l]),
                              row(diff_lambda_q2[l]), row(diff_lambda_k2[l]),
                              diff_sub_norm[l].reshape(-1, 1), lam_init)

        kvmem = _memkv(mem.reshape(b * mem_len, d), row(mem_norm[l]), bf(xattn_w_kv[l]))
        h = _post(h, oat, obt, row(mix_norm[l]), wga_t, wgb_t, bf(w_branch_a[l]), bf(w_branch_b[l]),
                  bf(w_out[l]), row(xattn_norm[l]), bf(xattn_w_q[l]), kvmem, bf(xattn_w_o[l]),
                  (row(ffn2_norm[l]), f2g, f2u, f2d),
                  row(final_norm), l == depth - 1, s, mem_len)
    return h.reshape(b, s, d)
```

```python
import functools
import math

import jax
import jax.numpy as jnp
from jax import lax
from jax.experimental import pallas as pl
from jax.experimental.pallas import tpu as pltpu

F32 = jnp.float32
BF16 = jnp.bfloat16

CHUNK = 64
EPS = 1e-6
NEG_INF = -1e30
MLA_HEADS = 8
MLA_Q_LORA = 384
MLA_KV_LORA = 256
MLA_NOPE = 64
MLA_ROPE = 32
MLA_V = 64
ROPE_BASE = 10000.0
DIFF_HEADS = 8
DIFF_DIM = 64
DIFF_V = 2 * DIFF_DIM
N_BUCKETS = 32
MAX_DISTANCE = 128
XATTN_HEADS = 4
XATTN_DIM = 128
LOG2E = math.log2(math.e)

LANES = 128
HEAD_SLOT = LANES
BF16_ROWS = 16
MXU_TILE = 256
VMEM_LIMIT = 56 * 1024 * 1024

TM_FFN = 512
TM_PROJ = 512
TQ = 512
TK = 512

MLA_VROWS = (MLA_V + 1 + BF16_ROWS - 1) // BF16_ROWS * BF16_ROWS
DIFF_VROWS = (DIFF_V + 1 + BF16_ROWS - 1) // BF16_ROWS * BF16_ROWS

NT_DIMS = (((1,), (1,)), ((), ()))
TN_DIMS = (((0,), (0,)), ((), ()))


def _rms(x, g):
    return x * lax.rsqrt(jnp.mean(x * x, axis=-1, keepdims=True) + EPS) * g


def _const_spec(shape):
    nd = len(shape)
    return pl.BlockSpec(shape, lambda *_: (0,) * nd, pipeline_mode=pl.Buffered(1))


def _params(n_grid):
    return pltpu.CompilerParams(
        dimension_semantics=("arbitrary",) * n_grid, vmem_limit_bytes=VMEM_LIMIT)


def _ffn_splits(dff):
    half = -(-(dff // 2) // MXU_TILE) * MXU_TILE
    return ((0, half), (half, dff))


def _ffn_half_step(x, g_ref, wg_ref, wu_ref, wd_ref):
    nb = _rms(x, g_ref[...]).astype(BF16)
    y = None
    for lo, hi in _ffn_splits(wg_ref.shape[1]):
        g = jnp.dot(nb, wg_ref[:, lo:hi], preferred_element_type=F32)
        u = jnp.dot(nb, wu_ref[:, lo:hi], preferred_element_type=F32)
        h = (g * jax.nn.sigmoid(g) * u).astype(BF16)
        part = jnp.dot(h, wd_ref[lo:hi, :], preferred_element_type=F32)
        y = part if y is None else y + part
    return x + 0.5 * y


def _split_hosted(refs):
    n_cast = (len(refs) - 1) // 2
    return refs[:n_cast], refs[n_cast], refs[n_cast + 1:]


def _cast_bands(srcs, dsts):
    for src, dst in zip(srcs, dsts):
        dst[...] = src[...].astype(BF16)


def _ffn_kernel(x_ref, g_ref, wg_ref, wu_ref, wd_ref, *refs):
    srcs, o_ref, dsts = _split_hosted(refs)
    o_ref[...] = _ffn_half_step(x_ref[...], g_ref, wg_ref, wu_ref, wd_ref)
    _cast_bands(srcs, dsts)


def _cast_band(start, rows, steps):
    band = -(-rows // steps)
    band = -(-band // BF16_ROWS) * BF16_ROWS
    while rows % band or start % band:
        band += BF16_ROWS
    return band


def _cast_specs(to_cast, steps):
    in_specs, out_specs, out_shapes = [], [], []
    for a, start, rows in to_cast:
        band = _cast_band(start, rows, steps)
        first, last = start // band, rows // band - 1
        in_specs.append(pl.BlockSpec(
            (band, a.shape[1]), lambda i, first=first, last=last: (first + jnp.minimum(i, last), 0)))
        out_specs.append(pl.BlockSpec((band, a.shape[1]), lambda i, last=last: (jnp.minimum(i, last), 0)))
        out_shapes.append(jax.ShapeDtypeStruct((rows, a.shape[1]), BF16))
    return in_specs, out_specs, out_shapes


def _ffn(x, norm_g, wg, wu, wd, to_cast):
    t, d = x.shape
    steps = t // TM_FFN
    tok = pl.BlockSpec((TM_FFN, d), lambda i: (i, 0))
    in_specs, out_specs, out_shapes = _cast_specs(to_cast, steps)
    outs = pl.pallas_call(
        _ffn_kernel,
        grid=(steps,),
        in_specs=[tok] + [_const_spec(a.shape) for a in (norm_g, wg, wu, wd)] + in_specs,
        out_specs=[tok] + out_specs,
        out_shape=[jax.ShapeDtypeStruct((t, d), F32)] + out_shapes,
        compiler_params=_params(1),
        name="ffn",
    )(x, norm_g, wg, wu, wd, *[a for a, _, _ in to_cast])
    return outs[0], outs[1:]


def _ones_rows(n_rows, rows_per_head, one_row):
    r = lax.broadcasted_iota(jnp.int32, (n_rows, 1), 0) % rows_per_head
    return jnp.where(r == one_row, 1.0, 0.0)


def _proj_kernel(h_ref, g_ref, wlat_ref, wdk_ref, wdqt_ref, wdvt_ref,
                 qn_ref, wqt_ref, kvn_ref, wk_ref, wvt_ref,
                 ct_ref, st_ref,
                 qt_ref, k_ref, vt_ref, dqt_ref, dk_ref, dvt_ref, *, q_scale, diff_scale):
    u = _rms(h_ref[...], g_ref[...]).astype(BF16)

    lat = lax.dot_general(u, wlat_ref[...], NT_DIMS, preferred_element_type=F32)
    q_lat = lat[:, :MLA_Q_LORA]
    kv_lat = lat[:, MLA_Q_LORA:MLA_Q_LORA + MLA_KV_LORA]
    o = MLA_Q_LORA + MLA_KV_LORA
    kr_a = lat[:, o:o + LANES]
    kr_b = lat[:, o + LANES:o + 2 * LANES]

    qn = _rms(q_lat, qn_ref[...]).astype(BF16)
    kvn = _rms(kv_lat, kvn_ref[...]).astype(BF16)

    qqt = lax.dot_general(wqt_ref[...], qn, NT_DIMS, preferred_element_type=F32)
    ct, st = ct_ref[...], st_ref[...]
    nope_rows = lax.broadcasted_iota(jnp.int32, (HEAD_SLOT, 1), 0) < MLA_NOPE
    cqt = (ct + jnp.where(nope_rows, 1.0, 0.0)) * q_scale
    sqt = st * q_scale
    r0, r1, r2 = MLA_NOPE, MLA_NOPE + MLA_ROPE // 2, MLA_NOPE + MLA_ROPE
    for hd in range(MLA_HEADS):
        rows = slice(hd * HEAD_SLOT, (hd + 1) * HEAD_SLOT)
        qh = qqt[rows]
        qh_sw = jnp.concatenate([qh[:r0], qh[r1:r2], qh[r0:r1], qh[r2:]], axis=0)
        qt_ref[0, rows, :] = (qh * cqt + qh_sw * sqt).astype(BF16)

    kk = jnp.dot(kvn, wk_ref[...], preferred_element_type=F32)
    kr = kr_a * ct.T + kr_b * st.T
    for hd in range(MLA_HEADS):
        sl = slice(hd * HEAD_SLOT, (hd + 1) * HEAD_SLOT)
        k_ref[:, sl] = (kk[:, sl] + kr).astype(BF16)

    vt = lax.dot_general(wvt_ref[...], kvn, NT_DIMS, preferred_element_type=F32)
    vt_ref[0] = (vt + _ones_rows(vt.shape[0], MLA_VROWS, MLA_V)).astype(BF16)

    dqt = lax.dot_general(wdqt_ref[...], u, NT_DIMS, preferred_element_type=F32)
    dqt_ref[0] = (dqt * diff_scale).astype(BF16)
    dk_ref[...] = lax.dot_general(u, wdk_ref[...], NT_DIMS, preferred_element_type=F32).astype(BF16)
    dvt = lax.dot_general(wdvt_ref[...], u, NT_DIMS, preferred_element_type=F32)
    dvt_ref[0] = (dvt + _ones_rows(dvt.shape[0], DIFF_VROWS, DIFF_V)).astype(BF16)


def _proj(h, g, wlat, wdk, wdqt, wdvt, qn, wqt, kvn, wk, wvt, tabs_t, batch, seq):
    t, d = h.shape
    nseq = seq // TM_PROJ
    tok = pl.BlockSpec((TM_PROJ, d), lambda i: (i, 0))
    tok_t = lambda rows: pl.BlockSpec((1, rows, TM_PROJ), lambda i: (i // nseq, 0, i % nseq))
    consts = [g, wlat, wdk, wdqt, wdvt, qn, wqt, kvn, wk, wvt]
    out_t = lambda rows: jax.ShapeDtypeStruct((batch, rows, seq), BF16)
    out_n = jax.ShapeDtypeStruct((t, d), BF16)
    return pl.pallas_call(
        functools.partial(_proj_kernel, q_scale=(MLA_NOPE + MLA_ROPE) ** -0.5 * LOG2E,
                          diff_scale=DIFF_DIM ** -0.5 * LOG2E),
        grid=(t // TM_PROJ,),
        in_specs=[tok] + [_const_spec(c.shape) for c in consts]
        + [pl.BlockSpec((HEAD_SLOT, TM_PROJ), lambda i: (0, i % nseq))] * 2,
        out_specs=[tok_t(wqt.shape[0]), tok, tok_t(wvt.shape[0]),
                   tok_t(wdqt.shape[0]), tok, tok_t(wdvt.shape[0])],
        out_shape=[out_t(wqt.shape[0]), out_n, out_t(wvt.shape[0]),
                   out_t(wdqt.shape[0]), out_n, out_t(wdvt.shape[0])],
        compiler_params=_params(1),
        name="in_proj",
    )(h, *consts, *tabs_t)


def _t5_bucket(rel):
    half = N_BUCKETS // 2
    max_exact = half // 2
    n = jnp.abs(rel)
    n2 = n * n
    assert (MAX_DISTANCE // max_exact) ** 2 == 2 ** (half - max_exact)
    large = max_exact
    for k in range(1, half - max_exact):
        large = large + (n2 >= (max_exact * max_exact) << k).astype(jnp.int32)
    return jnp.where(rel > 0, half, 0) + jnp.where(n < max_exact, n, large)


def _far_distance():
    half = N_BUCKETS // 2
    max_exact = half // 2
    n = max_exact
    while max_exact + sum(n * n >= (max_exact * max_exact) << k
                          for k in range(1, half - max_exact)) < half - 1:
        n += 1
    return n


def _bias_kernel(tab_ref, *refs):
    srcs, o_ref, dsts = _split_hosted(refs)
    _cast_bands(srcs, dsts)
    hd = pl.program_id(0)
    far = _far_distance()
    blk = 2 * CHUNK
    kloc = lax.broadcasted_iota(jnp.int32, (blk, blk), 0)
    qloc = lax.broadcasted_iota(jnp.int32, (blk, blk), 1)
    far_val = tab_ref[N_BUCKETS // 2 - 1, hd]
    for delta in range(2):
        for kb in range(TK // blk):
            for qb in range(TQ // blk):
                off = (kb - qb) * blk - delta * TK
                if delta == 0 and kb > qb:
                    val = jnp.full((blk, blk), NEG_INF, F32)
                elif off + blk - 1 <= -far:
                    val = jnp.zeros((blk, blk), F32)
                else:
                    bucket = _t5_bucket(kloc - qloc + off)
                    val = jnp.zeros((blk, blk), F32)
                    for b in range(N_BUCKETS):
                        val = jnp.where(bucket == b, tab_ref[b, hd], val)
                    val = (val - far_val) * LOG2E
                    if delta == 0 and kb == qb:
                        val = jnp.where(kloc // CHUNK <= qloc // CHUNK, val, NEG_INF)
                o_ref[0, delta, kb * blk:(kb + 1) * blk, qb * blk:(qb + 1) * blk] = val


def _bias_tiles(rel_bias, to_cast):
    in_specs, out_specs, out_shapes = _cast_specs(to_cast, DIFF_HEADS)
    outs = pl.pallas_call(
        _bias_kernel,
        grid=(DIFF_HEADS,),
        in_specs=[pl.BlockSpec(memory_space=pltpu.SMEM)] + in_specs,
        out_specs=[pl.BlockSpec((1, 2, TK, TQ), lambda h: (h, 0, 0, 0))] + out_specs,
        out_shape=[jax.ShapeDtypeStruct((DIFF_HEADS, 2, TK, TQ), F32)] + out_shapes,
        compiler_params=_params(1),
        name="bias_tiles",
    )(rel_bias, *[a for a, _, _ in to_cast])
    return outs[0], outs[1:]


FAR, SUB, DIAG = 0, 1, 2
assert TQ == TK and (TK // 2) % CHUNK == 0


def _scores(k, qt, bias, s_ref, mc_ref, diag):
    if diag:
        h = TK // 2
        parts = [(slice(0, h), slice(0, h)), (slice(0, TK), slice(h, TQ))]
    else:
        parts = [(slice(0, TK), slice(0, TQ))]
    for rows, cols in parts:
        s = jnp.dot(k[rows], qt[:, cols], preferred_element_type=F32)
        if bias is not None:
            s = s + bias[rows, cols]
        s_ref[rows, cols] = s
        mc_ref[:, cols] = jnp.max(s, axis=0, keepdims=True)


def _softmax_pv(s_ref, mc_ref, vt, m_ref, acc_ref, diag, first):
    m_new = mc_ref[...] if first else jnp.maximum(m_ref[...], mc_ref[...])
    if diag:
        h = TK // 2
        parts = [(slice(0, h), slice(0, h)), (slice(0, TK), slice(h, TQ))]
    else:
        parts = [(slice(0, TK), slice(0, TQ))]
    pv = []
    for rows, cols in parts:
        p = jnp.exp2(s_ref[rows, cols] - m_new[:, cols])
        pv.append(jnp.dot(vt[:, rows], p.astype(BF16), preferred_element_type=F32))
    pv = pv[0] if len(pv) == 1 else jnp.concatenate(pv, axis=1)
    acc_ref[...] = pv if first else jnp.exp2(m_ref[...] - m_new) * acc_ref[...] + pv
    m_ref[...] = m_new


FIRST_SLOT = 2
N_SLOTS = 3


def _sweep(i, n_q, near, qk, consume, qk_next):
    def kinds_of(count):
        return ((FAR,) * count + near)[-count:]

    count = i + 1
    tiles = [(n, FIRST_SLOT if n == 0 else (n + 1) % 2, kind) for n, kind in enumerate(kinds_of(count))]
    if i <= 1:
        qk(*tiles[0])
    prefetch = 1 < count < n_q
    for n, tile in enumerate(tiles):
        if n + 1 < count:
            qk(*tiles[n + 1])
        elif prefetch:
            qk_next(kinds_of(count + 1)[0])
        consume(*tile, first=n == 0)


def _kv_rows(j):
    return pl.ds(j * TK, TK)


def _attn_scratch(streams, v_rows):
    return [pltpu.VMEM((streams, N_SLOTS, TK, TQ), F32),
            pltpu.VMEM((streams, N_SLOTS, 1, TQ), F32),
            pltpu.VMEM((streams, 1, TQ), F32),
            pltpu.VMEM((streams, v_rows, TQ), F32)]


MLA_STREAMS = 2


def _q_cols(i):
    return pl.ds(i * TQ, TQ)


def _per_query_tile(n_q, q_tile):
    def step(i, carry):
        for c in range(n_q):
            pl.when(i == c)(functools.partial(q_tile, c))
        return carry

    lax.fori_loop(0, n_q, step, 0)


def _mla_kernel(qt_ref, k_ref, vt_ref, mask_ref, ot_ref, s_ref, mc_ref, m_ref, acc_ref):
    def qk(j, slot, kind, qts):
        bias = mask_ref[...] if kind == DIAG else None
        for st in range(MLA_STREAMS):
            k = k_ref[0, _kv_rows(j), st * HEAD_SLOT:(st + 1) * HEAD_SLOT]
            _scores(k, qts[st], bias, s_ref.at[st, slot], mc_ref.at[st, slot], kind == DIAG)

    def consume(j, slot, kind, first):
        for st in range(MLA_STREAMS):
            vt = vt_ref[0, st * MLA_VROWS:(st + 1) * MLA_VROWS, _kv_rows(j)]
            _softmax_pv(s_ref.at[st, slot], mc_ref.at[st, slot], vt, m_ref.at[st], acc_ref.at[st],
                        kind == DIAG, first)

    def q_tiles(t):
        return [qt_ref[0, st * HEAD_SLOT:(st + 1) * HEAD_SLOT, _q_cols(t)] for st in range(MLA_STREAMS)]

    def q_tile(i):
        _sweep(i, n_q, (DIAG,), functools.partial(qk, qts=q_tiles(i)), consume,
               lambda kind: qk(0, FIRST_SLOT, kind, q_tiles(i + 1)))
        for st in range(MLA_STREAMS):
            acc = acc_ref[st]
            ot_ref[0, st * MLA_V:(st + 1) * MLA_V, _q_cols(i)] = (
                acc[:MLA_V] * (1.0 / acc[MLA_V:MLA_V + 1])).astype(BF16)

    n_q = qt_ref.shape[2] // TQ
    _per_query_tile(n_q, q_tile)


def _mla_attention(qt, k, vt, mask):
    b, _, s = qt.shape
    groups = MLA_HEADS // MLA_STREAMS
    return pl.pallas_call(
        _mla_kernel,
        grid=(b, groups),
        in_specs=[pl.BlockSpec((1, MLA_STREAMS * HEAD_SLOT, s), lambda bi, h: (bi, h, 0)),
                  pl.BlockSpec((1, s, MLA_STREAMS * HEAD_SLOT), lambda bi, h: (bi, 0, h)),
                  pl.BlockSpec((1, MLA_STREAMS * MLA_VROWS, s), lambda bi, h: (bi, h, 0)),
                  _const_spec(mask.shape)],
        out_specs=pl.BlockSpec((1, MLA_STREAMS * MLA_V, s), lambda bi, h: (bi, h, 0)),
        out_shape=jax.ShapeDtypeStruct((b, MLA_HEADS * MLA_V, s), BF16),
        scratch_shapes=_attn_scratch(MLA_STREAMS, MLA_VROWS),
        compiler_params=_params(2),
        name="mla_attn",
    )(qt, k, vt, mask)


def _diff_kernel(qt_ref, k_ref, vt_ref, bias_ref, lq1_ref, lk1_ref, lq2_ref, lk2_ref, sub_ref,
                 ot_ref, s_ref, mc_ref, m_ref, acc_ref, *, lam_init):
    row = lax.broadcasted_iota(jnp.int32, (HEAD_SLOT, TQ), 0)
    zero = jnp.zeros((HEAD_SLOT, TQ), BF16)
    lam = (jnp.exp(jnp.sum(lq1_ref[...] * lk1_ref[...], axis=1, keepdims=True))
           - jnp.exp(jnp.sum(lq2_ref[...] * lk2_ref[...], axis=1, keepdims=True)) + lam_init)

    def qk(j, slot, kind, qts):
        k = k_ref[0, _kv_rows(j), :]
        bias = None if kind == FAR else bias_ref[0, 0 if kind == DIAG else 1]
        for st in range(2):
            _scores(k, qts[st], bias, s_ref.at[st, slot], mc_ref.at[st, slot], kind == DIAG)

    def consume(j, slot, kind, first):
        vt = vt_ref[0, :, _kv_rows(j)]
        for st in range(2):
            _softmax_pv(s_ref.at[st, slot], mc_ref.at[st, slot], vt, m_ref.at[st], acc_ref.at[st],
                        kind == DIAG, first)

    def q_tiles(t):
        qt = qt_ref[0, :, _q_cols(t)]
        return [jnp.where(row < DIFF_DIM, qt, zero), jnp.where(row >= DIFF_DIM, qt, zero)]

    def q_tile(i):
        _sweep(i, n_q, (SUB, DIAG), functools.partial(qk, qts=q_tiles(i)), consume,
               lambda kind: qk(0, FIRST_SLOT, kind, q_tiles(i + 1)))
        outs = [acc_ref[st, :DIFF_V] * (1.0 / acc_ref[st, DIFF_V:DIFF_V + 1]) for st in range(2)]
        o = outs[0] - lam * outs[1]
        o = o * lax.rsqrt(jnp.mean(o * o, axis=0, keepdims=True) + EPS) * sub_ref[...]
        ot_ref[0, :, _q_cols(i)] = (o * (1.0 - lam_init)).astype(BF16)

    n_q = qt_ref.shape[2] // TQ
    _per_query_tile(n_q, q_tile)


def _diff_attention(qt, k, vt, bias, lq1, lk1, lq2, lk2, sub_col, lam_init):
    b, _, s = qt.shape
    vec = _const_spec(lq1.shape)
    return pl.pallas_call(
        functools.partial(_diff_kernel, lam_init=lam_init),
        grid=(b, DIFF_HEADS),
        in_specs=[pl.BlockSpec((1, HEAD_SLOT, s), lambda bi, h: (bi, h, 0)),
                  pl.BlockSpec((1, s, HEAD_SLOT), lambda bi, h: (bi, 0, h)),
                  pl.BlockSpec((1, DIFF_VROWS, s), lambda bi, h: (bi, h, 0)),
                  pl.BlockSpec((1, 2, TK, TQ), lambda bi, h: (h, 0, 0, 0)),
                  vec, vec, vec, vec, _const_spec(sub_col.shape)],
        out_specs=pl.BlockSpec((1, DIFF_V, s), lambda bi, h: (bi, h, 0)),
        out_shape=jax.ShapeDtypeStruct((b, DIFF_HEADS * DIFF_V, s), BF16),
        scratch_shapes=_attn_scratch(2, DIFF_VROWS),
        compiler_params=_params(2),
        name="diff_attn",
    )(qt, k, vt, bias, lq1, lk1, lq2, lk2, sub_col)


def _memkv_kernel(m_ref, g_ref, w_ref, o_ref):
    mn = _rms(m_ref[...], g_ref[...]).astype(BF16)
    o_ref[...] = jnp.dot(mn, w_ref[...], preferred_element_type=F32).astype(BF16)


def _memkv(mem2d, g, w):
    rows, d = mem2d.shape
    return pl.pallas_call(
        _memkv_kernel,
        grid=(1,),
        in_specs=[_const_spec(mem2d.shape), _const_spec(g.shape), _const_spec(w.shape)],
        out_specs=_const_spec((rows, w.shape[1])),
        out_shape=jax.ShapeDtypeStruct((rows, w.shape[1]), BF16),
        compiler_params=_params(1),
        name="mem_kv",
    )(mem2d, g, w)


def _post_kernel(h_ref, oat_ref, obt_ref, g_ref, wga_ref, wgb_ref, wa_ref, wb_ref, wo_ref,
                 xg_ref, xwq_ref, kv_ref, xwo_ref, fg_ref, fwg_ref, fwu_ref, fwd_ref, fin_ref,
                 o_ref, *, scale, final_norm):
    h = h_ref[...]
    u = _rms(h, g_ref[...]).astype(BF16)
    gate_a = jax.nn.sigmoid(lax.dot_general(u, wga_ref[...], NT_DIMS, preferred_element_type=F32))
    gate_b = jax.nn.sigmoid(lax.dot_general(u, wgb_ref[...], NT_DIMS, preferred_element_type=F32))
    ya = lax.dot_general(oat_ref[0], wa_ref[...], TN_DIMS, preferred_element_type=F32)
    yb = lax.dot_general(obt_ref[0], wb_ref[...], TN_DIMS, preferred_element_type=F32)
    merged = gate_a * ya + gate_b * yb
    h = h + jnp.dot(merged.astype(BF16), wo_ref[...], preferred_element_type=F32)

    x = _rms(h, xg_ref[...]).astype(BF16)
    q = (jnp.dot(x, xwq_ref[...], preferred_element_type=F32) * scale).astype(BF16)
    kv = kv_ref[...]
    outs = []
    for hd in range(XATTN_HEADS):
        qh = q[:, hd * XATTN_DIM:(hd + 1) * XATTN_DIM]
        kh = kv[:, 2 * hd * XATTN_DIM:(2 * hd + 1) * XATTN_DIM]
        vh = kv[:, (2 * hd + 1) * XATTN_DIM:(2 * hd + 2) * XATTN_DIM]
        s = lax.dot_general(qh, kh, NT_DIMS, preferred_element_type=F32)
        m = jnp.max(s, axis=1, keepdims=True)
        p = jnp.exp2(s - m)
        l = jnp.sum(p, axis=1, keepdims=True)
        oh = jnp.dot(p.astype(BF16), vh, preferred_element_type=F32) / l
        outs.append(oh.astype(BF16))
    o = jnp.concatenate(outs, axis=1)
    h = h + jnp.dot(o, xwo_ref[...], preferred_element_type=F32)

    h = _ffn_half_step(h, fg_ref, fwg_ref, fwu_ref, fwd_ref)
    if final_norm:
        h = _rms(h, fin_ref[...])
    o_ref[...] = h


def _post(h, oat, obt, g, wga, wgb, wa, wb, wo, xg, xwq, kvmem, xwo, ffn, fin_g, final_norm, seq, mem_len):
    t, d = h.shape
    nseq = seq // TM_PROJ
    tok = pl.BlockSpec((TM_PROJ, d), lambda i: (i, 0))
    tok_t = lambda rows: pl.BlockSpec((1, rows, TM_PROJ), lambda i: (i // nseq, 0, i % nseq))
    cs = lambda a: _const_spec(a.shape)
    return pl.pallas_call(
        functools.partial(_post_kernel, scale=XATTN_DIM ** -0.5 * LOG2E, final_norm=final_norm),
        grid=(t // TM_PROJ,),
        in_specs=[tok, tok_t(oat.shape[1]), tok_t(obt.shape[1]), cs(g), cs(wga), cs(wgb), cs(wa), cs(wb), cs(wo),
                  cs(xg), cs(xwq),
                  pl.BlockSpec((mem_len, kvmem.shape[1]), lambda i: (i // nseq, 0)), cs(xwo)]
        + [cs(a) for a in ffn] + [cs(fin_g)],
        out_specs=tok,
        out_shape=jax.ShapeDtypeStruct((t, d), F32),
        compiler_params=_params(1),
        name="post_attn",
    )(h, oat, obt, g, wga, wgb, wa, wb, wo, xg, xwq, kvmem, xwo, *ffn, fin_g)


def _pad_heads(w, heads, width, slot):
    k = w.shape[0]
    w = w.reshape(k, heads, width)
    return jnp.pad(w, ((0, 0), (0, 0), (0, slot - width))).reshape(k, heads * slot)


def _rope_tables(seq):
    half = MLA_ROPE // 2
    pos = jnp.arange(seq)
    freqs = ROPE_BASE ** (-jnp.arange(half, dtype=F32) / half)
    ang = freqs[:, None] * pos.astype(F32)[None, :]
    cos, sin = jnp.cos(ang), jnp.sin(ang)
    z_nope = jnp.zeros((MLA_NOPE, seq), F32)
    z_pad = jnp.zeros((HEAD_SLOT - MLA_NOPE - MLA_ROPE, seq), F32)
    return (jnp.concatenate([z_nope, cos, cos, z_pad], axis=0),
            jnp.concatenate([z_nope, -sin, sin, z_pad], axis=0))


def kernel(x, mem, ffn1_norm, ffn1_w_gate, ffn1_w_up, ffn1_w_down, mix_norm, w_in, mla_q_norm, mla_w_q_up, mla_kv_norm, mla_w_kv_up, diff_lambda_q1, diff_lambda_k1, diff_lambda_q2, diff_lambda_k2, diff_sub_norm, rel_bias, w_branch_a, w_branch_b, w_out, xattn_norm, mem_norm, xattn_w_q, xattn_w_kv, xattn_w_o, ffn2_norm, ffn2_w_gate, ffn2_w_up, ffn2_w_down, final_norm):
    b, s, d = x.shape
    depth = ffn1_norm.shape[0]
    mem_len = mem.shape[1]
    t = b * s
    bf = lambda a: a.astype(BF16)
    row = lambda a: a.reshape(1, -1)

    tabs_t = _rope_tables(s)
    kpos = jnp.arange(TK)[:, None]
    qpos = jnp.arange(TQ)[None, :]
    mla_mask = jnp.where(kpos // CHUNK <= qpos // CHUNK, 0.0, NEG_INF).astype(F32)
    whole = lambda a: (a, 0, a.shape[0])
    bias, ffn1_w = _bias_tiles(rel_bias, [whole(ffn1_w_gate[0]), whole(ffn1_w_up[0]), whole(ffn1_w_down[0])])

    h = x.reshape(t, d)
    for l in range(depth):
        w_t = jnp.transpose(w_in[l])
        o_kr = MLA_Q_LORA + MLA_KV_LORA
        o_dq = o_kr + MLA_ROPE
        pieces = [(w_t, 0, o_dq)] + [(w_t, o_dq + n * d, d) for n in range(5)]
        if l > 0:
            ffn1_w = [bf(ffn1_w_gate[l]), bf(ffn1_w_up[l]), bf(ffn1_w_down[l])]
        h, (f2g, f2u, f2d, w_lat, wdq_t, wdk_t, wdv_t, wga_t, wgb_t) = _ffn(
            h, row(ffn1_norm[l]), *ffn1_w,
            [whole(ffn2_w_gate[l]), whole(ffn2_w_up[l]), whole(ffn2_w_down[l])] + pieces)

        w_kr = w_lat[o_kr:]
        zeros = lambda n: jnp.zeros((n, d), BF16)
        pad_l, pad_r = zeros(MLA_NOPE), zeros(HEAD_SLOT - MLA_NOPE - MLA_ROPE)
        w_kr_sw = jnp.concatenate([w_kr[MLA_ROPE // 2:], w_kr[:MLA_ROPE // 2]], axis=0)
        wlat_t = jnp.concatenate([w_lat[:o_kr], pad_l, w_kr, pad_r, pad_l, w_kr_sw, pad_r], axis=0)
        wdv_t = jnp.pad(wdv_t.reshape(DIFF_HEADS, DIFF_V, d),
                        ((0, 0), (0, DIFF_VROWS - DIFF_V), (0, 0))).reshape(-1, d)

        wq = _pad_heads(mla_w_q_up[l], MLA_HEADS, MLA_NOPE + MLA_ROPE, HEAD_SLOT)
        wkv3 = mla_w_kv_up[l].reshape(MLA_KV_LORA, MLA_HEADS, MLA_NOPE + MLA_V)
        wk = _pad_heads(wkv3[..., :MLA_NOPE].reshape(MLA_KV_LORA, -1), MLA_HEADS, MLA_NOPE, HEAD_SLOT)
        wv = _pad_heads(wkv3[..., MLA_NOPE:].reshape(MLA_KV_LORA, -1), MLA_HEADS, MLA_V, MLA_VROWS)

        qt, k, vt, dqt, dk, dvt = _proj(
            h, row(mix_norm[l]), wlat_t, wdk_t, wdq_t, wdv_t,
            row(mla_q_norm[l]), bf(wq.T), row(mla_kv_norm[l]), bf(wk), bf(wv.T),
            tabs_t, b, s)

        oat = _mla_attention(qt, k.reshape(b, s, d), vt, mla_mask)
        lam_init = 0.8 - 0.6 * math.exp(-0.3 * l)
        obt = _diff_attention(dqt, dk.reshape(b, s, d), dvt, bias,
                              row(diff_lambda_q1[l]), row(diff_lambda_k1[l]),
                              row(diff_lambda_q2[l]), row(diff_lambda_k2[l]),
                              diff_sub_norm[l].reshape(-1, 1), lam_init)

        kvmem = _memkv(mem.reshape(b * mem_len, d), row(mem_norm[l]), bf(xattn_w_kv[l]))
        h = _post(h, oat, obt, row(mix_norm[l]), wga_t, wgb_t, bf(w_branch_a[l]), bf(w_branch_b[l]),
                  bf(w_out[l]), row(xattn_norm[l]), bf(xattn_w_q[l]), kvmem, bf(xattn_w_o[l]),
                  (row(ffn2_norm[l]), f2g, f2u, f2d),
                  row(final_norm), l == depth - 1, s, mem_len)
    return h.reshape(b, s, d)
```

```python
import functools
import math

import jax
import jax.numpy as jnp
from jax import lax
from jax.experimental import pallas as pl
from jax.experimental.pallas import tpu as pltpu

F32 = jnp.float32
BF16 = jnp.bfloat16

CHUNK = 64
EPS = 1e-6
NEG_INF = -1e30
MLA_HEADS = 8
MLA_Q_LORA = 384
MLA_KV_LORA = 256
MLA_NOPE = 64
MLA_ROPE = 32
MLA_V = 64
ROPE_BASE = 10000.0
DIFF_HEADS = 8
DIFF_DIM = 64
DIFF_V = 2 * DIFF_DIM
N_BUCKETS = 32
MAX_DISTANCE = 128
XATTN_HEADS = 4
XATTN_DIM = 128
LOG2E = math.log2(math.e)

LANES = 128
HEAD_SLOT = LANES
BF16_ROWS = 16
MXU_TILE = 256
VMEM_LIMIT = 56 * 1024 * 1024

TM_FFN = 512
TM_PROJ = 512
TQ = 512
TK = 512

MLA_VROWS = (MLA_V + 1 + BF16_ROWS - 1) // BF16_ROWS * BF16_ROWS
DIFF_VROWS = (DIFF_V + 1 + BF16_ROWS - 1) // BF16_ROWS * BF16_ROWS

NT_DIMS = (((1,), (1,)), ((), ()))
TN_DIMS = (((0,), (0,)), ((), ()))


def _rms(x, g):
    return x * lax.rsqrt(jnp.mean(x * x, axis=-1, keepdims=True) + EPS) * g


def _const_spec(shape):
    nd = len(shape)
    return pl.BlockSpec(shape, lambda *_: (0,) * nd, pipeline_mode=pl.Buffered(1))


def _params(n_grid):
    return pltpu.CompilerParams(
        dimension_semantics=("arbitrary",) * n_grid, vmem_limit_bytes=VMEM_LIMIT)


def _ffn_splits(dff):
    half = -(-(dff // 2) // MXU_TILE) * MXU_TILE
    return ((0, half), (half, dff))


def _ffn_half_step(x, g_ref, wg_ref, wu_ref, wd_ref):
    nb = _rms(x, g_ref[...]).astype(BF16)
    y = None
    for lo, hi in _ffn_splits(wg_ref.shape[1]):
        g = jnp.dot(nb, wg_ref[:, lo:hi], preferred_element_type=F32)
        u = jnp.dot(nb, wu_ref[:, lo:hi], preferred_element_type=F32)
        h = (g * jax.nn.sigmoid(g) * u).astype(BF16)
        part = jnp.dot(h, wd_ref[lo:hi, :], preferred_element_type=F32)
        y = part if y is None else y + part
    return x + 0.5 * y


def _split_hosted(refs):
    n_cast = (len(refs) - 1) // 2
    return refs[:n_cast], refs[n_cast], refs[n_cast + 1:]


def _cast_bands(srcs, dsts):
    for src, dst in zip(srcs, dsts):
        dst[...] = src[...].astype(BF16)


def _ffn_kernel(x_ref, g_ref, wg_ref, wu_ref, wd_ref, *refs):
    srcs, o_ref, dsts = _split_hosted(refs)
    o_ref[...] = _ffn_half_step(x_ref[...], g_ref, wg_ref, wu_ref, wd_ref)
    _cast_bands(srcs, dsts)


def _cast_band(start, rows, steps):
    band = -(-rows // steps)
    band = -(-band // BF16_ROWS) * BF16_ROWS
    while rows % band or start % band:
        band += BF16_ROWS
    return band


def _cast_specs(to_cast, steps):
    in_specs, out_specs, out_shapes = [], [], []
    for a, start, rows in to_cast:
        band = _cast_band(start, rows, steps)
        first, last = start // band, rows // band - 1
        in_specs.append(pl.BlockSpec(
            (band, a.shape[1]), lambda i, first=first, last=last: (first + jnp.minimum(i, last), 0)))
        out_specs.append(pl.BlockSpec((band, a.shape[1]), lambda i, last=last: (jnp.minimum(i, last), 0)))
        out_shapes.append(jax.ShapeDtypeStruct((rows, a.shape[1]), BF16))
    return in_specs, out_specs, out_shapes


def _ffn(x, norm_g, wg, wu, wd, to_cast):
    t, d = x.shape
    steps = t // TM_FFN
    tok = pl.BlockSpec((TM_FFN, d), lambda i: (i, 0))
    in_specs, out_specs, out_shapes = _cast_specs(to_cast, steps)
    outs = pl.pallas_call(
        _ffn_kernel,
        grid=(steps,),
        in_specs=[tok] + [_const_spec(a.shape) for a in (norm_g, wg, wu, wd)] + in_specs,
        out_specs=[tok] + out_specs,
        out_shape=[jax.ShapeDtypeStruct((t, d), F32)] + out_shapes,
        compiler_params=_params(1),
        name="ffn",
    )(x, norm_g, wg, wu, wd, *[a for a, _, _ in to_cast])
    return outs[0], outs[1:]


def _ones_rows(n_rows, rows_per_head, one_row):
    r = lax.broadcasted_iota(jnp.int32, (n_rows, 1), 0) % rows_per_head
    return jnp.where(r == one_row, 1.0, 0.0)


def _proj_kernel(h_ref, g_ref, wlat_ref, wdk_ref, wdqt_ref, wdvt_ref,
                 qn_ref, wqt_ref, kvn_ref, wk_ref, wvt_ref,
                 ct_ref, st_ref,
                 qt_ref, k_ref, vt_ref, dqt_ref, dk_ref, dvt_ref, *, q_scale, diff_scale):
    u = _rms(h_ref[...], g_ref[...]).astype(BF16)

    lat = lax.dot_general(u, wlat_ref[...], NT_DIMS, preferred_element_type=F32)
    q_lat = lat[:, :MLA_Q_LORA]
    kv_lat = lat[:, MLA_Q_LORA:MLA_Q_LORA + MLA_KV_LORA]
    o = MLA_Q_LORA + MLA_KV_LORA
    kr_a = lat[:, o:o + LANES]
    kr_b = lat[:, o + LANES:o + 2 * LANES]

    qn = _rms(q_lat, qn_ref[...]).astype(BF16)
    kvn = _rms(kv_lat, kvn_ref[...]).astype(BF16)

    qqt = lax.dot_general(wqt_ref[...], qn, NT_DIMS, preferred_element_type=F32)
    ct, st = ct_ref[...], st_ref[...]
    nope_rows = lax.broadcasted_iota(jnp.int32, (HEAD_SLOT, 1), 0) < MLA_NOPE
    cqt = (ct + jnp.where(nope_rows, 1.0, 0.0)) * q_scale
    sqt = st * q_scale
    r0, r1, r2 = MLA_NOPE, MLA_NOPE + MLA_ROPE // 2, MLA_NOPE + MLA_ROPE
    for hd in range(MLA_HEADS):
        rows = slice(hd * HEAD_SLOT, (hd + 1) * HEAD_SLOT)
        qh = qqt[rows]
        qh_sw = jnp.concatenate([qh[:r0], qh[r1:r2], qh[r0:r1], qh[r2:]], axis=0)
        qt_ref[0, rows, :] = (qh * cqt + qh_sw * sqt).astype(BF16)

    kk = jnp.dot(kvn, wk_ref[...], preferred_element_type=F32)
    kr = kr_a * ct.T + kr_b * st.T
    for hd in range(MLA_HEADS):
        sl = slice(hd * HEAD_SLOT, (hd + 1) * HEAD_SLOT)
        k_ref[:, sl] = (kk[:, sl] + kr).astype(BF16)

    vt = lax.dot_general(wvt_ref[...], kvn, NT_DIMS, preferred_element_type=F32)
    vt_ref[0] = (vt + _ones_rows(vt.shape[0], MLA_VROWS, MLA_V)).astype(BF16)

    dqt = lax.dot_general(wdqt_ref[...], u, NT_DIMS, preferred_element_type=F32)
    dqt_ref[0] = (dqt * diff_scale).astype(BF16)
    dk_ref[...] = lax.dot_general(u, wdk_ref[...], NT_DIMS, preferred_element_type=F32).astype(BF16)
    dvt = lax.dot_general(wdvt_ref[...], u, NT_DIMS, preferred_element_type=F32)
    dvt_ref[0] = (dvt + _ones_rows(dvt.shape[0], DIFF_VROWS, DIFF_V)).astype(BF16)


def _proj(h, g, wlat, wdk, wdqt, wdvt, qn, wqt, kvn, wk, wvt, tabs_t, batch, seq):
    t, d = h.shape
    nseq = seq // TM_PROJ
    tok = pl.BlockSpec((TM_PROJ, d), lambda i: (i, 0))
    tok_t = lambda rows: pl.BlockSpec((1, rows, TM_PROJ), lambda i: (i // nseq, 0, i % nseq))
    consts = [g, wlat, wdk, wdqt, wdvt, qn, wqt, kvn, wk, wvt]
    out_t = lambda rows: jax.ShapeDtypeStruct((batch, rows, seq), BF16)
    out_n = jax.ShapeDtypeStruct((t, d), BF16)
    return pl.pallas_call(
        functools.partial(_proj_kernel, q_scale=(MLA_NOPE + MLA_ROPE) ** -0.5 * LOG2E,
                          diff_scale=DIFF_DIM ** -0.5 * LOG2E),
        grid=(t // TM_PROJ,),
        in_specs=[tok] + [_const_spec(c.shape) for c in consts]
        + [pl.BlockSpec((HEAD_SLOT, TM_PROJ), lambda i: (0, i % nseq))] * 2,
        out_specs=[tok_t(wqt.shape[0]), tok, tok_t(wvt.shape[0]),
                   tok_t(wdqt.shape[0]), tok, tok_t(wdvt.shape[0])],
        out_shape=[out_t(wqt.shape[0]), out_n, out_t(wvt.shape[0]),
                   out_t(wdqt.shape[0]), out_n, out_t(wdvt.shape[0])],
        compiler_params=_params(1),
        name="in_proj",
    )(h, *consts, *tabs_t)


def _t5_bucket(rel):
    half = N_BUCKETS // 2
    max_exact = half // 2
    n = jnp.abs(rel)
    n2 = n * n
    assert (MAX_DISTANCE // max_exact) ** 2 == 2 ** (half - max_exact)
    large = max_exact
    for k in range(1, half - max_exact):
        large = large + (n2 >= (max_exact * max_exact) << k).astype(jnp.int32)
    return jnp.where(rel > 0, half, 0) + jnp.where(n < max_exact, n, large)


def _far_distance():
    half = N_BUCKETS // 2
    max_exact = half // 2
    n = max_exact
    while max_exact + sum(n * n >= (max_exact * max_exact) << k
                          for k in range(1, half - max_exact)) < half - 1:
        n += 1
    return n


def _bias_kernel(tab_ref, *refs):
    srcs, o_ref, dsts = _split_hosted(refs)
    _cast_bands(srcs, dsts)
    hd = pl.program_id(0)
    far = _far_distance()
    blk = 2 * CHUNK
    kloc = lax.broadcasted_iota(jnp.int32, (blk, blk), 0)
    qloc = lax.broadcasted_iota(jnp.int32, (blk, blk), 1)
    far_val = tab_ref[N_BUCKETS // 2 - 1, hd]
    for delta in range(2):
        for kb in range(TK // blk):
            for qb in range(TQ // blk):
                off = (kb - qb) * blk - delta * TK
                if delta == 0 and kb > qb:
                    val = jnp.full((blk, blk), NEG_INF, F32)
                elif off + blk - 1 <= -far:
                    val = jnp.zeros((blk, blk), F32)
                else:
                    bucket = _t5_bucket(kloc - qloc + off)
                    val = jnp.zeros((blk, blk), F32)
                    for b in range(N_BUCKETS):
                        val = jnp.where(bucket == b, tab_ref[b, hd], val)
                    val = (val - far_val) * LOG2E
                    if delta == 0 and kb == qb:
                        val = jnp.where(kloc // CHUNK <= qloc // CHUNK, val, NEG_INF)
                o_ref[0, delta, kb * blk:(kb + 1) * blk, qb * blk:(qb + 1) * blk] = val


def _bias_tiles(rel_bias, to_cast):
    in_specs, out_specs, out_shapes = _cast_specs(to_cast, DIFF_HEADS)
    outs = pl.pallas_call(
        _bias_kernel,
        grid=(DIFF_HEADS,),
        in_specs=[pl.BlockSpec(memory_space=pltpu.SMEM)] + in_specs,
        out_specs=[pl.BlockSpec((1, 2, TK, TQ), lambda h: (h, 0, 0, 0))] + out_specs,
        out_shape=[jax.ShapeDtypeStruct((DIFF_HEADS, 2, TK, TQ), F32)] + out_shapes,
        compiler_params=_params(1),
        name="bias_tiles",
    )(rel_bias, *[a for a, _, _ in to_cast])
    return outs[0], outs[1:]


FAR, SUB, DIAG = 0, 1, 2
assert TQ == TK and (TK // 2) % CHUNK == 0


def _scores(k, qt, bias, s_ref, mc_ref, diag):
    if diag:
        h = TK // 2
        parts = [(slice(0, h), slice(0, h)), (slice(0, TK), slice(h, TQ))]
    else:
        parts = [(slice(0, TK), slice(0, TQ))]
    for rows, cols in parts:
        s = jnp.dot(k[rows], qt[:, cols], preferred_element_type=F32)
        if bias is not None:
            s = s + bias[rows, cols]
        s_ref[rows, cols] = s
        mc_ref[:, cols] = jnp.max(s, axis=0, keepdims=True)


def _softmax_pv(s_ref, mc_ref, vt, m_ref, acc_ref, diag, first):
    m_new = mc_ref[...] if first else jnp.maximum(m_ref[...], mc_ref[...])
    if diag:
        h = TK // 2
        parts = [(slice(0, h), slice(0, h)), (slice(0, TK), slice(h, TQ))]
    else:
        parts = [(slice(0, TK), slice(0, TQ // 2)), (slice(0, TK), slice(TQ // 2, TQ))]
    pv = []
    for rows, cols in parts:
        p = jnp.exp2(s_ref[rows, cols] - m_new[:, cols])
        pv.append(jnp.dot(vt[:, rows], p.astype(BF16), preferred_element_type=F32))
    pv = pv[0] if len(pv) == 1 else jnp.concatenate(pv, axis=1)
    acc_ref[...] = pv if first else jnp.exp2(m_ref[...] - m_new) * acc_ref[...] + pv
    m_ref[...] = m_new


FIRST_SLOT = 2
N_SLOTS = 3


def _sweep(i, near, qk, consume, qk_next):
    n_near = len(near)

    def slot_of(n):
        return FIRST_SLOT if n == 0 else (n + 1) % 2

    def run(tiles, own_first, next_kind, from_tile0=True):
        if own_first:
            qk(*tiles[0])
        for n, tile in enumerate(tiles):
            if n + 1 < len(tiles):
                qk(*tiles[n + 1])
            elif next_kind is not None:
                qk_next(next_kind)
            consume(*tile, first=from_tile0 and n == 0)

    def next_kind_after(count):
        return FAR if count + 1 > n_near else near[n_near - count - 1]

    for count in range(1, n_near + 1):
        @pl.when(i + 1 == count)
        def _(count=count):
            tiles = [(n, slot_of(n), kind) for n, kind in enumerate(near[-count:])]
            run(tiles, count <= 2, next_kind_after(count) if count > 1 else None)

    @pl.when(i + 1 > n_near)
    def _():
        n_far = i + 1 - n_near

        @pl.when(n_far == 1)
        def _():
            tiles = [(n, slot_of(n), kind) for n, kind in enumerate((FAR,) + near)]
            run(tiles, n_near <= 1, FAR)

        rest = n_far - 1
        pairs = (rest - 1) // 2

        for left in (1, 2):
            @pl.when(rest == left)
            def _(left=left):
                tiles = [(n, slot_of(n), kind) for n, kind in enumerate((FAR,) * (1 + left) + near)]
                run(tiles, False, FAR)

        @pl.when(pairs >= 1)
        def _():
            def pair(j):
                qk(j + 1, 1, FAR)
                consume(j, 0, FAR, first=False)
                qk(j + 2, 0, FAR)
                consume(j + 1, 1, FAR, first=False)

            for head in (1, 2):
                @pl.when(pairs % 2 == head % 2)
                def _(head=head):
                    qk(1, 0, FAR)
                    consume(0, FIRST_SLOT, FAR, first=True)
                    for n in range(head):
                        pair(2 * n + 1)

            first = 2 - pairs % 2

            def body(t, carry):
                j = 4 * t + 2 * first + 1
                pair(j)
                pair(j + 2)
                return carry

            lax.fori_loop(0, (pairs - first) // 2, body, 0)
            j0 = 2 * pairs + 1
            for left in (1, 2):
                @pl.when(rest - 2 * pairs == left)
                def _(left=left):
                    tiles = [(j0 + n, n % 2, kind) for n, kind in enumerate((FAR,) * left + near)]
                    run(tiles, False, FAR, from_tile0=False)


def _kv_rows(j):
    return pl.ds(pl.multiple_of(j * TK, TK), TK)


def _attn_scratch(streams, v_rows):
    return [pltpu.VMEM((streams, N_SLOTS, TK, TQ), F32),
            pltpu.VMEM((streams, N_SLOTS, 1, TQ), F32),
            pltpu.VMEM((streams, 1, TQ), F32),
            pltpu.VMEM((streams, v_rows, TQ), F32)]


MLA_STREAMS = 2


def _q_cols(i):
    return pl.ds(pl.multiple_of(i * TQ, TQ), TQ)


def _mla_kernel(qt_ref, k_ref, vt_ref, mask_ref, ot_ref, s_ref, mc_ref, m_ref, acc_ref):
    def qk(j, slot, kind, qts):
        bias = mask_ref[...] if kind == DIAG else None
        for st in range(MLA_STREAMS):
            k = k_ref[0, _kv_rows(j), st * HEAD_SLOT:(st + 1) * HEAD_SLOT]
            _scores(k, qts[st], bias, s_ref.at[st, slot], mc_ref.at[st, slot], kind == DIAG)

    def consume(j, slot, kind, first):
        for st in range(MLA_STREAMS):
            vt = vt_ref[0, st * MLA_VROWS:(st + 1) * MLA_VROWS, _kv_rows(j)]
            _softmax_pv(s_ref.at[st, slot], mc_ref.at[st, slot], vt, m_ref.at[st], acc_ref.at[st],
                        kind == DIAG, first)

    def q_tile(i, carry):
        def q_tiles(t):
            return [qt_ref[0, st * HEAD_SLOT:(st + 1) * HEAD_SLOT, _q_cols(t)] for st in range(MLA_STREAMS)]

        qts_next = q_tiles(jnp.minimum(i + 1, n_q - 1))
        _sweep(i, (DIAG,), functools.partial(qk, qts=q_tiles(i)), consume,
               lambda kind: qk(0, FIRST_SLOT, kind, qts_next))
        for st in range(MLA_STREAMS):
            acc = acc_ref[st]
            ot_ref[0, st * MLA_V:(st + 1) * MLA_V, _q_cols(i)] = (
                acc[:MLA_V] * (1.0 / acc[MLA_V:MLA_V + 1])).astype(BF16)
        return carry

    n_q = qt_ref.shape[2] // TQ
    lax.fori_loop(0, n_q, q_tile, 0)


def _mla_attention(qt, k, vt, mask):
    b, _, s = qt.shape
    groups = MLA_HEADS // MLA_STREAMS
    return pl.pallas_call(
        _mla_kernel,
        grid=(b, groups),
        in_specs=[pl.BlockSpec((1, MLA_STREAMS * HEAD_SLOT, s), lambda bi, h: (bi, h, 0)),
                  pl.BlockSpec((1, s, MLA_STREAMS * HEAD_SLOT), lambda bi, h: (bi, 0, h)),
                  pl.BlockSpec((1, MLA_STREAMS * MLA_VROWS, s), lambda bi, h: (bi, h, 0)),
                  _const_spec(mask.shape)],
        out_specs=pl.BlockSpec((1, MLA_STREAMS * MLA_V, s), lambda bi, h: (bi, h, 0)),
        out_shape=jax.ShapeDtypeStruct((b, MLA_HEADS * MLA_V, s), BF16),
        scratch_shapes=_attn_scratch(MLA_STREAMS, MLA_VROWS),
        compiler_params=_params(2),
        name="mla_attn",
    )(qt, k, vt, mask)


def _diff_kernel(qt_ref, k_ref, vt_ref, bias_ref, lq1_ref, lk1_ref, lq2_ref, lk2_ref, sub_ref,
                 ot_ref, s_ref, mc_ref, m_ref, acc_ref, *, lam_init):
    row = lax.broadcasted_iota(jnp.int32, (HEAD_SLOT, TQ), 0)
    zero = jnp.zeros((HEAD_SLOT, TQ), BF16)
    lam = (jnp.exp(jnp.sum(lq1_ref[...] * lk1_ref[...], axis=1, keepdims=True))
           - jnp.exp(jnp.sum(lq2_ref[...] * lk2_ref[...], axis=1, keepdims=True)) + lam_init)

    def qk(j, slot, kind, qts):
        k = k_ref[0, _kv_rows(j), :]
        bias = None if kind == FAR else bias_ref[0, 0 if kind == DIAG else 1]
        for st in range(2):
            _scores(k, qts[st], bias, s_ref.at[st, slot], mc_ref.at[st, slot], kind == DIAG)

    def consume(j, slot, kind, first):
        vt = vt_ref[0, :, _kv_rows(j)]
        for st in range(2):
            _softmax_pv(s_ref.at[st, slot], mc_ref.at[st, slot], vt, m_ref.at[st], acc_ref.at[st],
                        kind == DIAG, first)

    def q_tile(i, carry):
        def q_tiles(t):
            qt = qt_ref[0, :, _q_cols(t)]
            return [jnp.where(row < DIFF_DIM, qt, zero), jnp.where(row >= DIFF_DIM, qt, zero)]

        qts_next = q_tiles(jnp.minimum(i + 1, n_q - 1))
        _sweep(i, (SUB, DIAG), functools.partial(qk, qts=q_tiles(i)), consume,
               lambda kind: qk(0, FIRST_SLOT, kind, qts_next))
        outs = [acc_ref[st, :DIFF_V] * (1.0 / acc_ref[st, DIFF_V:DIFF_V + 1]) for st in range(2)]
        o = outs[0] - lam * outs[1]
        o = o * lax.rsqrt(jnp.mean(o * o, axis=0, keepdims=True) + EPS) * sub_ref[...]
        ot_ref[0, :, _q_cols(i)] = (o * (1.0 - lam_init)).astype(BF16)
        return carry

    n_q = qt_ref.shape[2] // TQ
    lax.fori_loop(0, n_q, q_tile, 0)


def _diff_attention(qt, k, vt, bias, lq1, lk1, lq2, lk2, sub_col, lam_init):
    b, _, s = qt.shape
    vec = _const_spec(lq1.shape)
    return pl.pallas_call(
        functools.partial(_diff_kernel, lam_init=lam_init),
        grid=(b, DIFF_HEADS),
        in_specs=[pl.BlockSpec((1, HEAD_SLOT, s), lambda bi, h: (bi, h, 0)),
                  pl.BlockSpec((1, s, HEAD_SLOT), lambda bi, h: (bi, 0, h)),
                  pl.BlockSpec((1, DIFF_VROWS, s), lambda bi, h: (bi, h, 0)),
                  pl.BlockSpec((1, 2, TK, TQ), lambda bi, h: (h, 0, 0, 0)),
                  vec, vec, vec, vec, _const_spec(sub_col.shape)],
        out_specs=pl.BlockSpec((1, DIFF_V, s), lambda bi, h: (bi, h, 0)),
        out_shape=jax.ShapeDtypeStruct((b, DIFF_HEADS * DIFF_V, s), BF16),
        scratch_shapes=_attn_scratch(2, DIFF_VROWS),
        compiler_params=_params(2),
        name="diff_attn",
    )(qt, k, vt, bias, lq1, lk1, lq2, lk2, sub_col)


def _memkv_kernel(m_ref, g_ref, w_ref, o_ref):
    mn = _rms(m_ref[...], g_ref[...]).astype(BF16)
    o_ref[...] = jnp.dot(mn, w_ref[...], preferred_element_type=F32).astype(BF16)


def _memkv(mem2d, g, w):
    rows, d = mem2d.shape
    return pl.pallas_call(
        _memkv_kernel,
        grid=(1,),
        in_specs=[_const_spec(mem2d.shape), _const_spec(g.shape), _const_spec(w.shape)],
        out_specs=_const_spec((rows, w.shape[1])),
        out_shape=jax.ShapeDtypeStruct((rows, w.shape[1]), BF16),
        compiler_params=_params(1),
        name="mem_kv",
    )(mem2d, g, w)


def _post_kernel(h_ref, oat_ref, obt_ref, g_ref, wga_ref, wgb_ref, wa_ref, wb_ref, wo_ref,
                 xg_ref, xwq_ref, kv_ref, xwo_ref, fg_ref, fwg_ref, fwu_ref, fwd_ref, fin_ref,
                 o_ref, *, scale, final_norm):
    h = h_ref[...]
    u = _rms(h, g_ref[...]).astype(BF16)
    gate_a = jax.nn.sigmoid(lax.dot_general(u, wga_ref[...], NT_DIMS, preferred_element_type=F32))
    gate_b = jax.nn.sigmoid(lax.dot_general(u, wgb_ref[...], NT_DIMS, preferred_element_type=F32))
    ya = lax.dot_general(oat_ref[0], wa_ref[...], TN_DIMS, preferred_element_type=F32)
    yb = lax.dot_general(obt_ref[0], wb_ref[...], TN_DIMS, preferred_element_type=F32)
    merged = gate_a * ya + gate_b * yb
    h = h + jnp.dot(merged.astype(BF16), wo_ref[...], preferred_element_type=F32)

    x = _rms(h, xg_ref[...]).astype(BF16)
    q = (jnp.dot(x, xwq_ref[...], preferred_element_type=F32) * scale).astype(BF16)
    kv = kv_ref[...]
    outs = []
    for hd in range(XATTN_HEADS):
        qh = q[:, hd * XATTN_DIM:(hd + 1) * XATTN_DIM]
        kh = kv[:, 2 * hd * XATTN_DIM:(2 * hd + 1) * XATTN_DIM]
        vh = kv[:, (2 * hd + 1) * XATTN_DIM:(2 * hd + 2) * XATTN_DIM]
        s = lax.dot_general(qh, kh, NT_DIMS, preferred_element_type=F32)
        m = jnp.max(s, axis=1, keepdims=True)
        p = jnp.exp2(s - m)
        l = jnp.sum(p, axis=1, keepdims=True)
        oh = jnp.dot(p.astype(BF16), vh, preferred_element_type=F32) / l
        outs.append(oh.astype(BF16))
    o = jnp.concatenate(outs, axis=1)
    h = h + jnp.dot(o, xwo_ref[...], preferred_element_type=F32)

    h = _ffn_half_step(h, fg_ref, fwg_ref, fwu_ref, fwd_ref)
    if final_norm:
        h = _rms(h, fin_ref[...])
    o_ref[...] = h


def _post(h, oat, obt, g, wga, wgb, wa, wb, wo, xg, xwq, kvmem, xwo, ffn, fin_g, final_norm, seq, mem_len):
    t, d = h.shape
    nseq = seq // TM_PROJ
    tok = pl.BlockSpec((TM_PROJ, d), lambda i: (i, 0))
    tok_t = lambda rows: pl.BlockSpec((1, rows, TM_PROJ), lambda i: (i // nseq, 0, i % nseq))
    cs = lambda a: _const_spec(a.shape)
    return pl.pallas_call(
        functools.partial(_post_kernel, scale=XATTN_DIM ** -0.5 * LOG2E, final_norm=final_norm),
        grid=(t // TM_PROJ,),
        in_specs=[tok, tok_t(oat.shape[1]), tok_t(obt.shape[1]), cs(g), cs(wga), cs(wgb), cs(wa), cs(wb), cs(wo),
                  cs(xg), cs(xwq),
                  pl.BlockSpec((mem_len, kvmem.shape[1]), lambda i: (i // nseq, 0)), cs(xwo)]
        + [cs(a) for a in ffn] + [cs(fin_g)],
        out_specs=tok,
        out_shape=jax.ShapeDtypeStruct((t, d), F32),
        compiler_params=_params(1),
        name="post_attn",
    )(h, oat, obt, g, wga, wgb, wa, wb, wo, xg, xwq, kvmem, xwo, *ffn, fin_g)


def _pad_heads(w, heads, width, slot):
    k = w.shape[0]
    w = w.reshape(k, heads, width)
    return jnp.pad(w, ((0, 0), (0, 0), (0, slot - width))).reshape(k, heads * slot)


def _rope_tables(seq):
    half = MLA_ROPE // 2
    pos = jnp.arange(seq)
    freqs = ROPE_BASE ** (-jnp.arange(half, dtype=F32) / half)
    ang = freqs[:, None] * pos.astype(F32)[None, :]
    cos, sin = jnp.cos(ang), jnp.sin(ang)
    z_nope = jnp.zeros((MLA_NOPE, seq), F32)
    z_pad = jnp.zeros((HEAD_SLOT - MLA_NOPE - MLA_ROPE, seq), F32)
    return (jnp.concatenate([z_nope, cos, cos, z_pad], axis=0),
            jnp.concatenate([z_nope, -sin, sin, z_pad], axis=0))


def kernel(x, mem, ffn1_norm, ffn1_w_gate, ffn1_w_up, ffn1_w_down, mix_norm, w_in, mla_q_norm, mla_w_q_up, mla_kv_norm, mla_w_kv_up, diff_lambda_q1, diff_lambda_k1, diff_lambda_q2, diff_lambda_k2, diff_sub_norm, rel_bias, w_branch_a, w_branch_b, w_out, xattn_norm, mem_norm, xattn_w_q, xattn_w_kv, xattn_w_o, ffn2_norm, ffn2_w_gate, ffn2_w_up, ffn2_w_down, final_norm):
    b, s, d = x.shape
    depth = ffn1_norm.shape[0]
    mem_len = mem.shape[1]
    t = b * s
    bf = lambda a: a.astype(BF16)
    row = lambda a: a.reshape(1, -1)

    tabs_t = _rope_tables(s)
    kpos = jnp.arange(TK)[:, None]
    qpos = jnp.arange(TQ)[None, :]
    mla_mask = jnp.where(kpos // CHUNK <= qpos // CHUNK, 0.0, NEG_INF).astype(F32)
    whole = lambda a: (a, 0, a.shape[0])
    bias, ffn1_w = _bias_tiles(rel_bias, [whole(ffn1_w_gate[0]), whole(ffn1_w_up[0]), whole(ffn1_w_down[0])])

    h = x.reshape(t, d)
    for l in range(depth):
        w_t = jnp.transpose(w_in[l])
        o_kr = MLA_Q_LORA + MLA_KV_LORA
        o_dq = o_kr + MLA_ROPE
        pieces = [(w_t, 0, o_dq)] + [(w_t, o_dq + n * d, d) for n in range(5)]
        if l > 0:
            ffn1_w = [bf(ffn1_w_gate[l]), bf(ffn1_w_up[l]), bf(ffn1_w_down[l])]
        h, (f2g, f2u, f2d, w_lat, wdq_t, wdk_t, wdv_t, wga_t, wgb_t) = _ffn(
            h, row(ffn1_norm[l]), *ffn1_w,
            [whole(ffn2_w_gate[l]), whole(ffn2_w_up[l]), whole(ffn2_w_down[l])] + pieces)

        w_kr = w_lat[o_kr:]
        zeros = lambda n: jnp.zeros((n, d), BF16)
        pad_l, pad_r = zeros(MLA_NOPE), zeros(HEAD_SLOT - MLA_NOPE - MLA_ROPE)
        w_kr_sw = jnp.concatenate([w_kr[MLA_ROPE // 2:], w_kr[:MLA_ROPE // 2]], axis=0)
        wlat_t = jnp.concatenate([w_lat[:o_kr], pad_l, w_kr, pad_r, pad_l, w_kr_sw, pad_r], axis=0)
        wdv_t = jnp.pad(wdv_t.reshape(DIFF_HEADS, DIFF_V, d),
                        ((0, 0), (0, DIFF_VROWS - DIFF_V), (0, 0))).reshape(-1, d)

        wq = _pad_heads(mla_w_q_up[l], MLA_HEADS, MLA_NOPE + MLA_ROPE, HEAD_SLOT)
        wkv3 = mla_w_kv_up[l].reshape(MLA_KV_LORA, MLA_HEADS, MLA_NOPE + MLA_V)
        wk = _pad_heads(wkv3[..., :MLA_NOPE].reshape(MLA_KV_LORA, -1), MLA_HEADS, MLA_NOPE, HEAD_SLOT)
        wv = _pad_heads(wkv3[..., MLA_NOPE:].reshape(MLA_KV_LORA, -1), MLA_HEADS, MLA_V, MLA_VROWS)

        qt, k, vt, dqt, dk, dvt = _proj(
            h, row(mix_norm[l]), wlat_t, wdk_t, wdq_t, wdv_t,
            row(mla_q_norm[l]), bf(wq.T), row(mla_kv_norm[l]), bf(wk), bf(wv.T),
            tabs_t, b, s)

        oat = _mla_attention(qt, k.reshape(b, s, d), vt, mla_mask)
        lam_init = 0.8 - 0.6 * math.exp(-0.3 * l)
        obt = _diff_attention(dqt, dk.reshape(b, s, d), dvt, bias,
                              row(diff_lambda_q1[l]), row(diff_lambda_k1[l]),
                              row(diff_lambda_q2[l]), row(diff_lambda_k2[l]),
                              diff_sub_norm[l].reshape(-1, 1), lam_init)

        kvmem = _memkv(mem.reshape(b * mem_len, d), row(mem_norm[l]), bf(xattn_w_kv[l]))
        h = _post(h, oat, obt, row(mix_norm[l]), wga_t, wgb_t, bf(w_branch_a[l]), bf(w_branch_b[l]),
                  bf(w_out[l]), row(xattn_norm[l]), bf(xattn_w_q[l]), kvmem, bf(xattn_w_o[l]),
                  (row(ffn2_norm[l]), f2g, f2u, f2d),
                  row(final_norm), l == depth - 1, s, mem_len)
    return h.reshape(b, s, d)
```
